```python
import functools
import math
import jax
import jax.numpy as jnp
from jax import lax
import numpy as np

D_MODEL = 1024
BATCH = 4
SEQ = 8192
DEPTH = 1
DEC_BATCH = 8
DEC_SEQ = 64
PAST_LEN = 1024

CHUNK = 64
HEAD_DIM = 64
N_HEADS_A = 8
N_HEADS_IDX = 4
IDX_DIM = 64
TOPK_MAX = 256
IDX_QBLK = CHUNK
IDX_SCALE = (IDX_DIM * N_HEADS_IDX) ** -0.5
N_HEADS_B = 8
BAND_CHUNKS = 8
REL_BACK = 128
REL_SIZE = REL_BACK + CHUNK
T5_BUCKETS = 32
T5_MAX_DIST = 128
D_FF = 2816
LN_EPS = 1e-5
NEG_INF = -1e30
WIDTH_A = N_HEADS_A * HEAD_DIM
WIDTH_B = N_HEADS_B * HEAD_DIM
IN_SIZES = (WIDTH_A, WIDTH_A, WIDTH_A, N_HEADS_IDX * IDX_DIM, IDX_DIM, N_HEADS_IDX,
            WIDTH_B, WIDTH_B, WIDTH_B, D_MODEL, D_MODEL)
N_IN = sum(IN_SIZES)

kernel_name = 'chunk_stream_dsa_band_hybrid'


def _layer_norm(x, g, b):
    x32 = x.astype(jnp.float32)
    mu = x32.mean(-1, keepdims=True)
    var = jnp.square(x32 - mu).mean(-1, keepdims=True)
    y = (x32 - mu) * lax.rsqrt(var + LN_EPS) * g.astype(jnp.float32) + b.astype(jnp.float32)
    return y.astype(x.dtype)


def _swiglu(x, wi, wo):
    a, u = jnp.split(x @ wi, 2, axis=-1)
    return (jax.nn.silu(a) * u) @ wo


def _t5_bucket(rel):
    half = T5_BUCKETS // 2
    exact = half // 2
    n = jnp.abs(rel)
    log_ratio = jnp.log(jnp.maximum(n, 1).astype(jnp.float32) / exact) / math.log(T5_MAX_DIST / exact)
    large = jnp.minimum(exact + (log_ratio * (half - exact)).astype(jnp.int32), half - 1)
    return (rel > 0).astype(jnp.int32) * half + jnp.where(n < exact, n, large)


def _split_in(h, w_in):
    b, t = h.shape[:2]
    z = h @ w_in
    offs, acc = [], 0
    for s in IN_SIZES[:-1]:
        acc += s
        offs.append(acc)
    qa, ka, va, qi, ki, wi, qb, kb, vb, ga, gb = jnp.split(z, offs, axis=-1)
    heads = lambda a, n: a.reshape(b, t, n, -1)
    return (heads(qa, N_HEADS_A), heads(ka, N_HEADS_A), heads(va, N_HEADS_A),
            heads(qi, N_HEADS_IDX), ki, wi,
            heads(qb, N_HEADS_B), heads(kb, N_HEADS_B), heads(vb, N_HEADS_B), ga, gb)


def _dsa_attend(q, qi, wi, q_pos, k, v, ki, k_pos, n_sel, t5_bias):
    f32 = jnp.float32
    dots = jnp.einsum('bqhd,bsd->bqhs', qi, ki).astype(f32)
    score = jnp.einsum('bqh,bqhs->bqs', wi.astype(f32), jax.nn.relu(dots)) * IDX_SCALE
    q_chunk = q_pos // CHUNK
    adm = (k_pos // CHUNK)[None, :] <= q_chunk[:, None]
    score = jnp.where(adm[None], score, -jnp.inf)
    _, sel = lax.top_k(score, n_sel)
    sel_pos = k_pos[sel]
    valid = (sel_pos // CHUNK) <= q_chunk[None, :, None]
    kg = jax.vmap(lambda kb_, ib: kb_[ib])(k, sel)
    vg = jax.vmap(lambda vb_, ib: vb_[ib])(v, sel)
    bias = t5_bias.T[_t5_bucket(sel_pos - q_pos[None, :, None])]
    logits = jnp.einsum('bqhd,bqkhd->bqkh', q, kg).astype(f32) * (HEAD_DIM ** -0.5) + bias.astype(f32)
    logits = jnp.where(valid[..., None], logits, NEG_INF)
    p = jax.nn.softmax(logits, axis=2).astype(v.dtype)
    return jnp.einsum('bqkh,bqkhd->bqhd', p, vg)


def _dsa_prompt(q, qi, wi, k, v, ki, t5_bias):
    b, t = q.shape[:2]
    nb = t // IDX_QBLK
    n_sel = min(TOPK_MAX, t // 4)
    pos = jnp.arange(t, dtype=jnp.int32)
    blk = lambda a: jnp.moveaxis(a.reshape((b, nb, IDX_QBLK) + a.shape[2:]), 1, 0)

    def one(args):
        qb, qib, wib, pb = args
        return _dsa_attend(qb, qib, wib, pb, k, v, ki, pos, n_sel, t5_bias)

    out = lax.map(one, (blk(q), blk(qi), blk(wi), pos.reshape(nb, IDX_QBLK)))
    return jnp.moveaxis(out, 0, 1).reshape(b, t, N_HEADS_A, HEAD_DIM)


def _band_attend(q, q_pos, k, v, k_pos, rel_bias):
    f32 = jnp.float32
    qc = q_pos // CHUNK
    kc = k_pos // CHUNK
    mask = ((k_pos >= 0)[None, :] & (kc[None, :] <= qc[:, None])
            & (kc[None, :] >= qc[:, None] - BAND_CHUNKS))
    ridx = jnp.clip(k_pos[None, :] - q_pos[:, None], -REL_BACK, CHUNK - 1) + REL_BACK
    bias = rel_bias[:, ridx]
    logits = jnp.einsum('bqhd,bkhd->bhqk', q, k).astype(f32) * (HEAD_DIM ** -0.5) + bias[None].astype(f32)
    logits = jnp.where(mask[None, None], logits, NEG_INF)
    p = jax.nn.softmax(logits, axis=-1).astype(v.dtype)
    return jnp.einsum('bhqk,bkhd->bqhd', p, v)


def _band_prompt(q, k, v, rel_bias):
    b, t = q.shape[:2]
    nc = t // CHUNK
    pad = BAND_CHUNKS * CHUNK
    kp = jnp.pad(k, ((0, 0), (pad, 0), (0, 0), (0, 0)))
    vp = jnp.pad(v, ((0, 0), (pad, 0), (0, 0), (0, 0)))
    qc = jnp.moveaxis(q.reshape(b, nc, CHUNK, N_HEADS_B, HEAD_DIM), 1, 0)

    def one(args):
        c, qb = args
        start = c * CHUNK
        kb = lax.dynamic_slice_in_dim(kp, start, pad + CHUNK, axis=1)
        vb = lax.dynamic_slice_in_dim(vp, start, pad + CHUNK, axis=1)
        q_pos = start + jnp.arange(CHUNK, dtype=jnp.int32)
        k_pos = start - pad + jnp.arange(pad + CHUNK, dtype=jnp.int32)
        return _band_attend(qb, q_pos, kb, vb, k_pos, rel_bias)

    out = lax.map(one, (jnp.arange(nc, dtype=jnp.int32), qc))
    return jnp.moveaxis(out, 0, 1).reshape(b, t, N_HEADS_B, HEAD_DIM)


def _prompt_mixer(h, w_in, t5_bias, rel_bias):
    qa, ka, va, qi, ki, wi, qb, kb, vb, ga, gb = _split_in(h, w_in)
    oa = _dsa_prompt(qa, qi, wi, ka, va, ki, t5_bias)
    ob = _band_prompt(qb, kb, vb, rel_bias)
    keep = min(BAND_CHUNKS * CHUNK, h.shape[1])
    return oa, ob, ga, gb, (ka, va, ki, kb[:, -keep:], vb[:, -keep:])


def _sample_mixer(h, w_in, t5_bias, rel_bias, c_k_a, c_v_a, c_kidx_a, c_k_b, c_v_b):
    qa, ka, va, qi, ki, wi, qb, kb, vb, ga, gb = _split_in(h, w_in)
    past, n = c_k_a.shape[1], h.shape[1]
    q_pos = past + jnp.arange(n, dtype=jnp.int32)
    ka_all = jnp.concatenate([c_k_a.astype(ka.dtype), ka], axis=1)
    va_all = jnp.concatenate([c_v_a.astype(va.dtype), va], axis=1)
    ki_all = jnp.concatenate([c_kidx_a.astype(ki.dtype), ki], axis=1)
    ka_pos = jnp.arange(past + n, dtype=jnp.int32)
    n_sel = min(TOPK_MAX, (past + n) // 4)
    oa = _dsa_attend(qa, qi, wi, q_pos, ka_all, va_all, ki_all, ka_pos, n_sel, t5_bias)
    band = c_k_b.shape[1]
    kb_all = jnp.concatenate([c_k_b.astype(kb.dtype), kb], axis=1)
    vb_all = jnp.concatenate([c_v_b.astype(vb.dtype), vb], axis=1)
    kb_pos = (past - band) + jnp.arange(band + n, dtype=jnp.int32)
    ob = _band_attend(qb, q_pos, kb_all, vb_all, kb_pos, rel_bias)
    return oa, ob, ga, gb, (ka, va, ki, kb_all[:, -band:], vb_all[:, -band:])


def _layer(x, mixer, ln1_g, ln1_b, ffn1_wi, ffn1_wo, ln2_g, ln2_b, w_branch_a, w_branch_b,
           w_out, ln3_g, ln3_b, ffn2_wi, ffn2_wo):
    alpha = (2.0 * DEPTH) ** 0.25
    h = _layer_norm(alpha * x + 0.5 * _swiglu(x, ffn1_wi, ffn1_wo), ln1_g, ln1_b)
    oa, ob, ga, gb, states = mixer(h)
    b, t = h.shape[:2]
    ya = oa.reshape(b, t, WIDTH_A) @ w_branch_a
    yb = ob.reshape(b, t, WIDTH_B) @ w_branch_b
    mix = (jax.nn.sigmoid(ga) * ya + jax.nn.sigmoid(gb) * yb) @ w_out
    h = _layer_norm(alpha * h + mix, ln2_g, ln2_b)
    h = _layer_norm(alpha * h + 0.5 * _swiglu(h, ffn2_wi, ffn2_wo), ln3_g, ln3_b)
    return h, states


def setup_inputs(seed: int = 0) -> dict:
    key = jax.random.key(seed)
    ks = jax.random.split(key, 24)
    f32 = jnp.float32
    beta = (8.0 * DEPTH) ** -0.25
    band = min(BAND_CHUNKS * CHUNK, PAST_LEN)

    def nrm(k, shape, scale):
        return jax.random.normal(k, shape, f32) * scale

    col_scale = jnp.concatenate([jnp.full((s,), beta if i in (2, 8) else 1.0, f32)
                                 for i, s in enumerate(IN_SIZES)])
    return {
        'x_prompt': nrm(ks[0], (BATCH, SEQ, D_MODEL), 1.0),
        'x_sample': nrm(ks[1], (DEC_BATCH, DEC_SEQ, D_MODEL), 1.0),
        'cache_k_a': nrm(ks[2], (DEPTH, DEC_BATCH, PAST_LEN, N_HEADS_A, HEAD_DIM), 1.0),
        'cache_v_a': nrm(ks[3], (DEPTH, DEC_BATCH, PAST_LEN, N_HEADS_A, HEAD_DIM), 1.0),
        'cache_kidx_a': nrm(ks[4], (DEPTH, DEC_BATCH, PAST_LEN, IDX_DIM), 1.0),
        'cache_k_b': nrm(ks[5], (DEPTH, DEC_BATCH, band, N_HEADS_B, HEAD_DIM), 1.0),
        'cache_v_b': nrm(ks[6], (DEPTH, DEC_BATCH, band, N_HEADS_B, HEAD_DIM), 1.0),
        't5_bias': nrm(ks[7], (N_HEADS_A, T5_BUCKETS), 0.5),
        'ln1_g': 1.0 + nrm(ks[8], (DEPTH, D_MODEL), 0.05),
        'ln1_b': nrm(ks[9], (DEPTH, D_MODEL), 0.02),
        'ffn1_wi': nrm(ks[10], (DEPTH, D_MODEL, 2 * D_FF), D_MODEL ** -0.5),
        'ffn1_wo': nrm(ks[11], (DEPTH, D_FF, D_MODEL), beta * D_FF ** -0.5),
        'ln2_g': 1.0 + nrm(ks[12], (DEPTH, D_MODEL), 0.05),
        'ln2_b': nrm(ks[13], (DEPTH, D_MODEL), 0.02),
        'w_in': nrm(ks[14], (DEPTH, D_MODEL, N_IN), D_MODEL ** -0.5) * col_scale,
        'rel_bias_b': nrm(ks[15], (DEPTH, N_HEADS_B, REL_SIZE), 0.5),
        'w_branch_a': nrm(ks[16], (DEPTH, WIDTH_A, D_MODEL), beta * WIDTH_A ** -0.5),
        'w_branch_b': nrm(ks[17], (DEPTH, WIDTH_B, D_MODEL), beta * WIDTH_B ** -0.5),
        'w_out': nrm(ks[18], (DEPTH, D_MODEL, D_MODEL), beta * D_MODEL ** -0.5),
        'ln3_g': 1.0 + nrm(ks[19], (DEPTH, D_MODEL), 0.05),
        'ln3_b': nrm(ks[20], (DEPTH, D_MODEL), 0.02),
        'ffn2_wi': nrm(ks[21], (DEPTH, D_MODEL, 2 * D_FF), D_MODEL ** -0.5),
        'ffn2_wo': nrm(ks[22], (DEPTH, D_FF, D_MODEL), beta * D_FF ** -0.5),
    }


def reference(x_prompt, x_sample, cache_k_a, cache_v_a, cache_kidx_a, cache_k_b, cache_v_b,
              t5_bias, ln1_g, ln1_b, ffn1_wi, ffn1_wo, ln2_g, ln2_b, w_in, rel_bias_b,
              w_branch_a, w_branch_b, w_out, ln3_g, ln3_b, ffn2_wi, ffn2_wo):
    y_prompt, y_sample = x_prompt, x_sample
    st_p, st_s = [], []
    for l in range(DEPTH):
        shared = (ln1_g[l], ln1_b[l], ffn1_wi[l], ffn1_wo[l], ln2_g[l], ln2_b[l],
                  w_branch_a[l], w_branch_b[l], w_out[l], ln3_g[l], ln3_b[l],
                  ffn2_wi[l], ffn2_wo[l])
        mix_p = functools.partial(_prompt_mixer, w_in=w_in[l], t5_bias=t5_bias,
                                  rel_bias=rel_bias_b[l])
        mix_s = functools.partial(_sample_mixer, w_in=w_in[l], t5_bias=t5_bias,
                                  rel_bias=rel_bias_b[l], c_k_a=cache_k_a[l],
                                  c_v_a=cache_v_a[l], c_kidx_a=cache_kidx_a[l],
                                  c_k_b=cache_k_b[l], c_v_b=cache_v_b[l])
        y_prompt, sp = _layer(y_prompt, mix_p, *shared)
        y_sample, ss = _layer(y_sample, mix_s, *shared)
        st_p.append(sp)
        st_s.append(ss)
    new_k_a_prompt = jnp.stack([s[0] for s in st_p])
    new_v_a_prompt = jnp.stack([s[1] for s in st_p])
    new_kidx_a_prompt = jnp.stack([s[2] for s in st_p])
    new_k_b_prompt = jnp.stack([s[3] for s in st_p])
    new_v_b_prompt = jnp.stack([s[4] for s in st_p])
    new_k_a_sample = jnp.stack([s[0] for s in st_s])
    new_v_a_sample = jnp.stack([s[1] for s in st_s])
    new_kidx_a_sample = jnp.stack([s[2] for s in st_s])
    new_k_b_sample = jnp.stack([s[3] for s in st_s])
    new_v_b_sample = jnp.stack([s[4] for s in st_s])
    return (y_prompt, y_sample, new_k_a_prompt, new_v_a_prompt, new_kidx_a_prompt,
            new_k_b_prompt, new_v_b_prompt, new_k_a_sample, new_v_a_sample,
            new_kidx_a_sample, new_k_b_sample, new_v_b_sample)
```

```python
import functools
import math

import jax
import jax.numpy as jnp
from jax import lax
from jax.experimental import pallas as pl
from jax.experimental.pallas import tpu as pltpu

F32 = jnp.float32
BF16 = jnp.bfloat16
I32 = jnp.int32

CHUNK = 64
CHUNK_SHIFT = 6
HEAD_DIM = 64
N_HEADS = 8
WIDTH = N_HEADS * HEAD_DIM
N_HEADS_IDX = 4
IDX_DIM = 64
TOPK_MAX = 256
IDX_SCALE = (IDX_DIM * N_HEADS_IDX) ** -0.5
BAND_CHUNKS = 8
REL_BACK = 128
T5_BUCKETS = 32
T5_MAX_DIST = 128
LN_EPS = 1e-5
NEG_INF = -1e30
INT_MIN = -2 ** 31

ATT_TILE = 256
TOKEN_TILE = 512
FF_CHUNK = 256
VMEM_LIMIT_MB = 56


def _cparams(sem):
    return pltpu.CompilerParams(dimension_semantics=sem,
                                vmem_limit_bytes=VMEM_LIMIT_MB * 1024 * 1024)


def _const_spec(shape):
    nd = len(shape)
    return pl.BlockSpec(shape, lambda *_: (0,) * nd, pipeline_mode=pl.Buffered(1))


def _tree_sum(xs):
    xs = list(xs)
    while len(xs) > 1:
        nxt = [xs[a] + xs[a + 1] for a in range(0, len(xs) - 1, 2)]
        if len(xs) % 2:
            nxt.append(xs[-1])
        xs = nxt
    return xs[0]


def _layer_norm_rows(y, g, b):
    mu = jnp.mean(y, axis=-1, keepdims=True)
    d = y - mu
    var = jnp.mean(d * d, axis=-1, keepdims=True)
    return d * lax.rsqrt(var + LN_EPS) * g + b


def _ffn_ln_kernel(x_ref, wi_ref, wo_ref, g_ref, b_ref, o_ref, *, alpha, dff):
    x = x_ref[...]
    xb = x.astype(BF16)
    acc = jnp.zeros(x.shape, F32)
    for c in range(dff // FF_CHUNK):
        lo = c * FF_CHUNK
        a = jnp.dot(xb, wi_ref[:, lo:lo + FF_CHUNK], preferred_element_type=F32)
        u = jnp.dot(xb, wi_ref[:, dff + lo:dff + lo + FF_CHUNK], preferred_element_type=F32)
        hm = (a * jax.nn.sigmoid(a)) * u
        acc = acc + jnp.dot(hm.astype(BF16), wo_ref[lo:lo + FF_CHUNK, :],
                            preferred_element_type=F32)
    y = alpha * x + 0.5 * acc
    o_ref[...] = _layer_norm_rows(y, g_ref[...], b_ref[...])


def _ffn_ln(x, wi, wo, g, b, alpha):
    n, d = x.shape
    dff = wo.shape[0]
    tm = TOKEN_TILE
    return pl.pallas_call(
        functools.partial(_ffn_ln_kernel, alpha=alpha, dff=dff),
        out_shape=jax.ShapeDtypeStruct((n, d), F32),
        grid=(n // tm,),
        in_specs=[pl.BlockSpec((tm, d), lambda i: (i, 0)),
                  _const_spec((d, 2 * dff)), _const_spec((dff, d)),
                  _const_spec((1, d)), _const_spec((1, d))],
        out_specs=pl.BlockSpec((tm, d), lambda i: (i, 0)),
        compiler_params=_cparams(("parallel",)),
        name="ffn_ln",
    )(x, wi, wo, g.reshape(1, d), b.reshape(1, d))


_C_QA, _C_KA, _C_VA, _C_QI, _C_KI, _C_WI, _C_QB, _C_KB, _C_VB, _C_END = (
    0, 512, 1024, 1536, 1792, 1920, 2048, 2560, 3072, 3584)


def _inproj_kernel(h_ref, w_ref, qaT_ref, ka_ref, kab_ref, va_ref, vaT_ref, qiT_ref, ki_ref,
                   kk_ref, wiT_ref, qbT_ref, kbb_ref, vbT_ref, kbl_ref, vbl_ref, *, per):
    hb = h_ref[...].astype(BF16)

    def proj(c0, c1):
        return jnp.dot(hb, w_ref[:, c0:c1], preferred_element_type=F32)

    def put_tiles(ref, zt):
        for c in range(ref.shape[0]):
            ref[c] = zt[:, c * ATT_TILE:(c + 1) * ATT_TILE].astype(ref.dtype)

    scale = HEAD_DIM ** -0.5
    qaT_ref[...] = (proj(_C_QA, _C_KA) * scale).T.astype(BF16)
    ka = proj(_C_KA, _C_VA)
    ka_ref[...] = ka
    kab_ref[...] = ka.astype(BF16)
    va = proj(_C_VA, _C_QI)
    va_ref[...] = va
    put_tiles(vaT_ref, va.T)
    qiT_ref[...] = proj(_C_QI, _C_KI).T.astype(BF16)
    kk = proj(_C_KI, _C_WI)
    ki_ref[...] = kk[:, :IDX_DIM]
    kk_ref[...] = kk.astype(BF16)
    wiT_ref[...] = proj(_C_WI, _C_QB).T[:8, :]
    qbT_ref[...] = (proj(_C_QB, _C_KB) * scale).T.astype(BF16)
    kb = proj(_C_KB, _C_VB)
    kbb_ref[...] = kb.astype(BF16)
    vb = proj(_C_VB, _C_END)
    put_tiles(vbT_ref, vb.T)

    @pl.when(pl.program_id(0) % per == per - 1)
    def _():
        kbl_ref[...] = kb
        vbl_ref[...] = vb


def _in_proj(h, w, per):
    n, d = h.shape
    tm = TOKEN_TILE
    nt = n // tm
    sub = tm // ATT_TILE
    row = lambda width: pl.BlockSpec((tm, width), lambda i: (i, 0))
    col = lambda rows: pl.BlockSpec((rows, tm), lambda i: (0, i))
    til = pl.BlockSpec((sub, WIDTH, ATT_TILE), lambda i: (i, 0, 0))
    last = pl.BlockSpec((tm, WIDTH), lambda i: (i // per, 0))
    sds = jax.ShapeDtypeStruct
    out_shape = (
        sds((WIDTH, n), BF16),
        sds((n, WIDTH), F32), sds((n, WIDTH), BF16),
        sds((n, WIDTH), F32), sds((n // ATT_TILE, WIDTH, ATT_TILE), BF16),
        sds((N_HEADS_IDX * IDX_DIM, n), BF16),
        sds((n, IDX_DIM), F32), sds((n, 128), BF16),
        sds((8, n), F32),
        sds((WIDTH, n), BF16),
        sds((n, WIDTH), BF16),
        sds((n // ATT_TILE, WIDTH, ATT_TILE), BF16),
        sds((nt // per * tm, WIDTH), F32), sds((nt // per * tm, WIDTH), F32),
    )
    out_specs = (col(WIDTH), row(WIDTH), row(WIDTH), row(WIDTH), til,
                 col(N_HEADS_IDX * IDX_DIM), row(IDX_DIM), row(128), col(8),
                 col(WIDTH), row(WIDTH), til, last, last)
    return pl.pallas_call(
        functools.partial(_inproj_kernel, per=per),
        out_shape=out_shape,
        grid=(nt,),
        in_specs=[pl.BlockSpec((tm, d), lambda i: (i, 0)), _const_spec((d, _C_END))],
        out_specs=out_specs,
        compiler_params=_cparams(("arbitrary",)),
        name="in_proj",
    )(h, w)


def _idx_mask_kernel(qiT_ref, wiT_ref, kk_ref, mask_ref, st_ref, *, tq, qpos0, n_valid, n_sel):
    tk = ATT_TILE
    nkt_total = st_ref.shape[0] // tk
    q0 = qpos0 + pl.program_id(1) * tq
    n_keys = jnp.minimum(q0 + tq, n_valid)
    nt = jnp.minimum(lax.shift_right_logical(n_keys + tk - 1, int(math.log2(tk))), nkt_total)

    lane = lax.broadcasted_iota(I32, (8, tq), 1)
    q_chunk = lax.shift_right_logical(q0 + lane, CHUNK_SHIFT)
    n_adm = jnp.minimum((q_chunk + 1) * CHUNK, n_valid)
    k_row = jnp.minimum(n_adm, n_sel)

    zeros = jnp.zeros((128 - IDX_DIM, tq), BF16)
    qh = [jnp.concatenate([qiT_ref[h * IDX_DIM:(h + 1) * IDX_DIM, :], zeros], axis=0)
          for h in range(N_HEADS_IDX)]
    wh = [wiT_ref[h:h + 1, :] for h in range(N_HEADS_IDX)]

    def score_tile(j, carry):
        r0 = pl.multiple_of(j * tk, tk)
        kt = kk_ref[pl.ds(r0, tk), :]
        sc = None
        for h in range(N_HEADS_IDX):
            d = jnp.dot(kt, qh[h], preferred_element_type=F32)
            term = wh[h] * jnp.maximum(d, 0.0)
            sc = term if sc is None else sc + term
        sc = sc * IDX_SCALE
        bits = lax.bitcast_convert_type(sc, I32)
        bits = jnp.where(bits == INT_MIN, 0, bits)
        key = bits ^ (lax.shift_right_arithmetic(bits, 31) & 0x7FFFFFFF)
        kpos = r0 + lax.broadcasted_iota(I32, (tk, tq), 0)
        st_ref[pl.ds(r0, tk), :] = jnp.where(kpos < n_adm[0:1, :], key, INT_MIN)
        return carry

    lax.fori_loop(0, nt, score_tile, 0)

    def count(pred, thr8):
        def body(j, acc):
            r0 = pl.multiple_of(j * tk, tk)
            blk = st_ref[pl.ds(r0, tk), :]
            parts = [jnp.where(pred(blk[r * 8:(r + 1) * 8, :], thr8), 1, 0)
                     for r in range(tk // 8)]
            return acc + _tree_sum(parts)
        acc = lax.fori_loop(0, nt, body, jnp.zeros((8, tq), I32))
        return jnp.broadcast_to(jnp.sum(acc, axis=0, keepdims=True), (8, tq))

    ge = lambda a, b: a >= b
    gt = lambda a, b: a > b

    def bisect(k, t8):
        cand = t8 + lax.shift_left(jnp.int32(1), 31 - k)
        return jnp.where(count(ge, cand) >= k_row, cand, t8)

    t8 = lax.fori_loop(0, 32, bisect, jnp.full((8, tq), INT_MIN, I32))
    n_gt = count(gt, t8)
    t_row = t8[0:1, :]
    ties_wanted = (k_row - n_gt)[0:1, :].astype(F32)

    tri = jnp.where(lax.broadcasted_iota(I32, (tk, tk), 0) >= lax.broadcasted_iota(I32, (tk, tk), 1),
                    1.0, 0.0).astype(BF16)

    def final_tile(j, seen):
        r0 = pl.multiple_of(j * tk, tk)
        blk = st_ref[pl.ds(r0, tk), :]
        eq = jnp.where(blk == t_row, 1.0, 0.0)
        incl = jnp.dot(tri, eq.astype(BF16), preferred_element_type=F32)
        rank = seen + incl - eq
        take_tie = jnp.where(rank < ties_wanted, eq, 0.0)
        sel = jnp.where(blk > t_row, 1.0, take_tie)
        mask_ref[pl.ds(r0, tk), :] = sel.astype(jnp.int8)
        return seen + incl[tk - 1:tk, :]

    lax.fori_loop(0, nt, final_tile, jnp.zeros((1, tq), F32))

    def zero_tile(j, carry):
        r0 = pl.multiple_of(j * tk, tk)
        mask_ref[pl.ds(r0, tk), :] = jnp.zeros((tk, tq), jnp.int8)
        return carry

    lax.fori_loop(nt, nkt_total, zero_tile, 0)


def _idx_mask(qiT, wiT, kk, *, nb, tq, qpos0, n_valid, n_sel):
    tkp = kk.shape[1]
    nq = qiT.shape[1] // nb // tq
    return pl.pallas_call(
        functools.partial(_idx_mask_kernel, tq=tq, qpos0=qpos0, n_valid=n_valid, n_sel=n_sel),
        out_shape=jax.ShapeDtypeStruct((nb, tkp, nq * tq), jnp.int8),
        grid=(nb, nq),
        in_specs=[pl.BlockSpec((N_HEADS_IDX * IDX_DIM, tq), lambda b, i: (0, b * nq + i)),
                  pl.BlockSpec((8, tq), lambda b, i: (0, b * nq + i)),
                  pl.BlockSpec((None, tkp, 128), lambda b, i: (b, 0, 0))],
        out_specs=pl.BlockSpec((None, tkp, tq), lambda b, i: (b, 0, i)),
        scratch_shapes=[pltpu.VMEM((tkp, tq), I32)],
        compiler_params=_cparams(("parallel", "parallel")),
        name="idx_mask",
    )(qiT, wiT, kk)


def _load_qsel(qT_ref, qsel_ref, tq):
    zeros = jnp.zeros((HEAD_DIM, tq), BF16)
    for h in range(N_HEADS):
        blk = qT_ref[h * HEAD_DIM:(h + 1) * HEAD_DIM, :]
        qsel_ref[h] = jnp.concatenate([blk, zeros] if h % 2 == 0 else [zeros, blk], axis=0)


def _init_state(m_ref, l_ref, acc_ref):
    m_ref[...] = jnp.full(m_ref.shape, NEG_INF, F32)
    l_ref[...] = jnp.zeros(l_ref.shape, F32)
    acc_ref[...] = jnp.zeros(acc_ref.shape, F32)


def _heads_tile(k_slab, v_rows, qsel_ref, m_ref, l_ref, acc_ref, keep, bias):
    for h in range(N_HEADS):
        s = jnp.dot(k_slab(h // 2), qsel_ref[h], preferred_element_type=F32)
        if bias is not None:
            s = s + bias(h)
        if keep is not None:
            s = jnp.where(keep, s, NEG_INF)
        m_old = m_ref[h:h + 1, :]
        m_new = jnp.maximum(m_old, jnp.max(s, axis=0, keepdims=True))
        alpha = jnp.exp(m_old - m_new)
        p = jnp.exp(s - m_new)
        l_ref[h:h + 1, :] = alpha * l_ref[h:h + 1, :] + jnp.sum(p, axis=0, keepdims=True)
        m_ref[h:h + 1, :] = m_new
        pv = jnp.dot(v_rows(h), p.astype(BF16), preferred_element_type=F32)
        rows = slice(h * HEAD_DIM, (h + 1) * HEAD_DIM)
        acc_ref[rows, :] = alpha * acc_ref[rows, :] + pv


def _store_out(o_ref, l_ref, acc_ref):
    outs = [acc_ref[h * HEAD_DIM:(h + 1) * HEAD_DIM, :] / l_ref[h:h + 1, :] for h in range(N_HEADS)]
    o_ref[...] = jnp.concatenate(outs, axis=0).T.astype(o_ref.dtype)


def _dsa_kernel(qT_ref, k_ref, vT_ref, mask_ref, bias_ref, o_ref, qsel_ref, m_ref, l_ref, acc_ref,
                *, tq, qpos0):
    tk = ATT_TILE
    jd = lax.shift_right_logical(qpos0 + pl.program_id(1) * tq, int(math.log2(tk)))
    _load_qsel(qT_ref, qsel_ref, tq)
    _init_state(m_ref, l_ref, acc_ref)

    def tile(j, near):
        r0 = pl.multiple_of(j * tk, tk)
        keep = mask_ref[pl.ds(r0, tk), :].astype(I32) != 0
        _heads_tile(lambda g: k_ref[pl.ds(r0, tk), g * 128:(g + 1) * 128],
                    lambda h: vT_ref[j, h * HEAD_DIM:(h + 1) * HEAD_DIM, :],
                    qsel_ref, m_ref, l_ref, acc_ref, keep,
                    None if near is None else (lambda h: bias_ref[near, h]))

    def far(j, carry):
        tile(j, None)
        return carry

    lax.fori_loop(0, jnp.maximum(jd - 1, 0), far, 0)

    @pl.when(jd >= 1)
    def _():
        tile(jd - 1, 0)

    tile(jd, 1)
    _store_out(o_ref, l_ref, acc_ref)


def _dsa_attn(qT, k, vT, maskT, bias, *, nb, tq, qpos0):
    tkp = k.shape[1]
    nkt = tkp // ATT_TILE
    nq = qT.shape[1] // nb // tq
    return pl.pallas_call(
        functools.partial(_dsa_kernel, tq=tq, qpos0=qpos0),
        out_shape=jax.ShapeDtypeStruct((nb * nq * tq, WIDTH), BF16),
        grid=(nb, nq),
        in_specs=[pl.BlockSpec((WIDTH, tq), lambda b, i: (0, b * nq + i)),
                  pl.BlockSpec((None, tkp, WIDTH), lambda b, i: (b, 0, 0),
                               pipeline_mode=pl.Buffered(1)),
                  pl.BlockSpec((nkt, WIDTH, ATT_TILE), lambda b, i: (b, 0, 0),
                               pipeline_mode=pl.Buffered(1)),
                  pl.BlockSpec((None, tkp, tq), lambda b, i: (b, 0, i)),
                  _const_spec(bias.shape)],
        out_specs=pl.BlockSpec((tq, WIDTH), lambda b, i: (b * nq + i, 0)),
        scratch_shapes=[pltpu.VMEM((N_HEADS, 128, tq), BF16), pltpu.VMEM((N_HEADS, tq), F32),
                        pltpu.VMEM((N_HEADS, tq), F32), pltpu.VMEM((WIDTH, tq), F32)],
        compiler_params=_cparams(("parallel", "parallel")),
        name="dsa_attn",
    )(qT, k, vT, maskT, bias)


_BAND_TILES = BAND_CHUNKS * CHUNK // ATT_TILE + 1


def _band_kernel(qT_ref, *refs, tq, off):
    k_refs = refs[:_BAND_TILES]
    v_refs = refs[_BAND_TILES:2 * _BAND_TILES]
    bias_ref, o_ref, qsel_ref, m_ref, l_ref, acc_ref = refs[2 * _BAND_TILES:]
    _load_qsel(qT_ref, qsel_ref, tq)
    _init_state(m_ref, l_ref, acc_ref)
    for w in range(_BAND_TILES):
        @pl.when(pl.program_id(1) + off + w >= 0)
        def _(w=w):
            _heads_tile(lambda g: k_refs[w][:, g * 128:(g + 1) * 128],
                        lambda h: v_refs[w][0, h * HEAD_DIM:(h + 1) * HEAD_DIM, :],
                        qsel_ref, m_ref, l_ref, acc_ref, None, lambda h: bias_ref[w, h])
    _store_out(o_ref, l_ref, acc_ref)


def _band_attn(qT, k, vT, bias, *, nb, tq, off):
    nkt = k.shape[1] // ATT_TILE
    nq = qT.shape[1] // nb // tq
    tile_of = lambda i, w: jnp.maximum(i + off + w, 0)
    k_specs = [pl.BlockSpec((None, ATT_TILE, WIDTH), lambda b, i, w=w: (b, tile_of(i, w), 0))
               for w in range(_BAND_TILES)]
    v_specs = [pl.BlockSpec((1, WIDTH, ATT_TILE), lambda b, i, w=w: (b * nkt + tile_of(i, w), 0, 0))
               for w in range(_BAND_TILES)]
    return pl.pallas_call(
        functools.partial(_band_kernel, tq=tq, off=off),
        out_shape=jax.ShapeDtypeStruct((nb * nq * tq, WIDTH), BF16),
        grid=(nb, nq),
        in_specs=[pl.BlockSpec((WIDTH, tq), lambda b, i: (0, b * nq + i))] + k_specs + v_specs
                 + [_const_spec(bias.shape)],
        out_specs=pl.BlockSpec((tq, WIDTH), lambda b, i: (b * nq + i, 0)),
        scratch_shapes=[pltpu.VMEM((N_HEADS, 128, tq), BF16), pltpu.VMEM((N_HEADS, tq), F32),
                        pltpu.VMEM((N_HEADS, tq), F32), pltpu.VMEM((WIDTH, tq), F32)],
        compiler_params=_cparams(("parallel", "parallel")),
        name="band_attn",
    )(qT, *([k] * _BAND_TILES), *([vT] * _BAND_TILES), bias)


def _mix_ln_kernel(h_ref, oa_ref, ob_ref, wg_ref, wba_ref, wbb_ref, wo_ref, g_ref, b_ref, o_ref,
                   *, alpha):
    h = h_ref[...]
    hb = h.astype(BF16)
    d = h.shape[1]
    ga = jnp.dot(hb, wg_ref[:, :d], preferred_element_type=F32)
    gb = jnp.dot(hb, wg_ref[:, d:], preferred_element_type=F32)
    ya = jnp.dot(oa_ref[...], wba_ref[...], preferred_element_type=F32)
    yb = jnp.dot(ob_ref[...], wbb_ref[...], preferred_element_type=F32)
    gated = jax.nn.sigmoid(ga) * ya + jax.nn.sigmoid(gb) * yb
    mix = jnp.dot(gated.astype(BF16), wo_ref[...], preferred_element_type=F32)
    o_ref[...] = _layer_norm_rows(alpha * h + mix, g_ref[...], b_ref[...])


def _mix_ln(h, oa, ob, wg, wba, wbb, wo, g, b, alpha):
    n, d = h.shape
    tm = TOKEN_TILE
    row = lambda width: pl.BlockSpec((tm, width), lambda i: (i, 0))
    return pl.pallas_call(
        functools.partial(_mix_ln_kernel, alpha=alpha),
        out_shape=jax.ShapeDtypeStruct((n, d), F32),
        grid=(n // tm,),
        in_specs=[row(d), row(WIDTH), row(WIDTH), _const_spec(wg.shape), _const_spec(wba.shape),
                  _const_spec(wbb.shape), _const_spec(wo.shape), _const_spec((1, d)),
                  _const_spec((1, d))],
        out_specs=row(d),
        compiler_params=_cparams(("parallel",)),
        name="mix_ln",
    )(h, oa, ob, wg, wba, wbb, wo, g.reshape(1, d), b.reshape(1, d))


def _t5_bucket(rel):
    half = T5_BUCKETS // 2
    exact = half // 2
    n = jnp.abs(rel)
    log_ratio = jnp.log(jnp.maximum(n, 1).astype(F32) / exact) / math.log(T5_MAX_DIST / exact)
    large = jnp.minimum(exact + (log_ratio * (half - exact)).astype(I32), half - 1)
    return (rel > 0).astype(I32) * half + jnp.where(n < exact, n, large)


def _rel_grid(n_tiles, first, tq):
    w = jnp.arange(n_tiles, dtype=I32)[:, None, None]
    j = jnp.arange(ATT_TILE, dtype=I32)[None, :, None]
    i = jnp.arange(tq, dtype=I32)[None, None, :]
    return (w + first) * ATT_TILE + j - i


def _dsa_bias_table(t5_bias, tq):
    assert T5_MAX_DIST <= ATT_TILE + 1
    rel = _rel_grid(2, -1, tq)
    far = t5_bias[:, _t5_bucket(jnp.int32(-(ATT_TILE + 1)))]
    return jnp.moveaxis(t5_bias[:, _t5_bucket(rel)], 0, 1) - far[None, :, None, None]


def _band_bias_table(rel_bias, tq):
    first = 1 - _BAND_TILES
    rel = _rel_grid(_BAND_TILES, first, tq)
    w = jnp.arange(_BAND_TILES, dtype=I32)[:, None, None]
    j = jnp.arange(ATT_TILE, dtype=I32)[None, :, None]
    i = jnp.arange(tq, dtype=I32)[None, None, :]
    dchunk = ((w + first) * ATT_TILE + j) // CHUNK - i // CHUNK
    ok = (dchunk <= 0) & (dchunk >= -BAND_CHUNKS)
    ridx = jnp.clip(rel, -REL_BACK, CHUNK - 1) + REL_BACK
    bias = jnp.moveaxis(rel_bias[:, ridx], 0, 1)
    return jnp.where(ok[:, None], bias, NEG_INF)


def _regroup_w_in(w_in):
    sizes = (WIDTH, WIDTH, WIDTH, N_HEADS_IDX * IDX_DIM, IDX_DIM, N_HEADS_IDX,
             WIDTH, WIDTH, WIDTH, w_in.shape[0], w_in.shape[0])
    offs = [0]
    for s in sizes:
        offs.append(offs[-1] + s)
    qa, ka, va, qi, ki, wi, qb, kb, vb, ga, gb = (w_in[:, offs[t]:offs[t + 1]] for t in range(11))
    wi_pad = jnp.pad(wi, ((0, 0), (0, 128 - N_HEADS_IDX)))
    w_proj = jnp.concatenate([qa, ka, va, qi, ki, ki, wi_pad, qb, kb, vb], axis=1).astype(BF16)
    w_gate = jnp.concatenate([ga, gb], axis=1).astype(BF16)
    return w_proj, w_gate


def _key_tiles_T(v):
    nb, tkp, wd = v.shape
    return v.reshape(nb, tkp // ATT_TILE, ATT_TILE, wd).transpose(0, 1, 3, 2).reshape(-1, wd, ATT_TILE)


def _pad_keys(x, tkp):
    return jnp.pad(x, ((0, 0), (0, tkp - x.shape[1]), (0, 0)))


def kernel(x_prompt, x_sample, cache_k_a, cache_v_a, cache_kidx_a, cache_k_b, cache_v_b, t5_bias,
           ln1_g, ln1_b, ffn1_wi, ffn1_wo, ln2_g, ln2_b, w_in, rel_bias_b, w_branch_a, w_branch_b,
           w_out, ln3_g, ln3_b, ffn2_wi, ffn2_wo):
    depth = ln1_g.shape[0]
    alpha = (2.0 * depth) ** 0.25
    nbp, seq, d = x_prompt.shape
    nbs, dec, _ = x_sample.shape
    past = cache_k_a.shape[2]
    band = cache_k_b.shape[2]
    keep = min(BAND_CHUNKS * CHUNK, seq)
    tq_p, tq_s = ATT_TILE, 128
    assert seq % TOKEN_TILE == 0 and keep == TOKEN_TILE and dec <= tq_s and past % ATT_TILE == 0
    assert band == BAND_CHUNKS * CHUNK and (nbs * tq_s) % TOKEN_TILE == 0

    yp = x_prompt.reshape(nbp * seq, d)
    ys = jnp.pad(x_sample, ((0, 0), (0, tq_s - dec), (0, 0))).reshape(nbs * tq_s, d)
    dsa_bias_p = _dsa_bias_table(t5_bias, tq_p)
    dsa_bias_s = dsa_bias_p[..., :tq_s]
    n_sel_p = min(TOPK_MAX, seq // 4)
    n_sel_s = min(TOPK_MAX, (past + dec) // 4)
    tk_s = -(-(past + tq_s) // ATT_TILE) * ATT_TILE
    tkb_s = _BAND_TILES * ATT_TILE

    st_p, st_s = [], []
    for l in range(depth):
        w_proj, w_gate = _regroup_w_in(w_in[l])
        wi1, wo1 = ffn1_wi[l].astype(BF16), ffn1_wo[l].astype(BF16)
        wi2, wo2 = ffn2_wi[l].astype(BF16), ffn2_wo[l].astype(BF16)
        wba, wbb, wo = (w_branch_a[l].astype(BF16), w_branch_b[l].astype(BF16),
                        w_out[l].astype(BF16))
        band_bias_p = _band_bias_table(rel_bias_b[l], tq_p)
        band_bias_s = band_bias_p[..., :tq_s]

        h = _ffn_ln(yp, wi1, wo1, ln1_g[l], ln1_b[l], alpha)
        (qaT, ka, kab, va, vaT, qiT, ki, kk, wiT, qbT, kbb, vbT, kbl, vbl) = _in_proj(
            h, w_proj, seq // TOKEN_TILE)
        maskT = _idx_mask(qiT, wiT, kk.reshape(nbp, seq, 128), nb=nbp, tq=tq_p, qpos0=0,
                          n_valid=seq, n_sel=n_sel_p)
        oa = _dsa_attn(qaT, kab.reshape(nbp, seq, WIDTH), vaT, maskT, dsa_bias_p,
                       nb=nbp, tq=tq_p, qpos0=0)
        ob = _band_attn(qbT, kbb.reshape(nbp, seq, WIDTH), vbT, band_bias_p,
                        nb=nbp, tq=tq_p, off=1 - _BAND_TILES)
        h2 = _mix_ln(h, oa, ob, w_gate, wba, wbb, wo, ln2_g[l], ln2_b[l], alpha)
        yp = _ffn_ln(h2, wi2, wo2, ln3_g[l], ln3_b[l], alpha)
        st_p.append((ka.reshape(nbp, seq, N_HEADS, HEAD_DIM), va.reshape(nbp, seq, N_HEADS, HEAD_DIM),
                     ki.reshape(nbp, seq, IDX_DIM), kbl.reshape(nbp, keep, N_HEADS, HEAD_DIM),
                     vbl.reshape(nbp, keep, N_HEADS, HEAD_DIM)))

        h = _ffn_ln(ys, wi1, wo1, ln1_g[l], ln1_b[l], alpha)
        (qaT, ka, _, va, _, qiT, ki, _, wiT, qbT, _, _, kbl, vbl) = _in_proj(h, w_proj, 1)
        new = lambda a: a.reshape(nbs, tq_s, -1)[:, :dec]
        ka_n, va_n, ki_n, kb_n, vb_n = new(ka), new(va), new(ki), new(kbl), new(vbl)
        ka_all = jnp.concatenate([cache_k_a[l].reshape(nbs, past, WIDTH), ka_n], axis=1)
        va_all = jnp.concatenate([cache_v_a[l].reshape(nbs, past, WIDTH), va_n], axis=1)
        ki_all = jnp.concatenate([cache_kidx_a[l], ki_n], axis=1)
        kb_all = jnp.concatenate([cache_k_b[l].reshape(nbs, band, WIDTH), kb_n], axis=1)
        vb_all = jnp.concatenate([cache_v_b[l].reshape(nbs, band, WIDTH), vb_n], axis=1)
        kk_s = _pad_keys(jnp.concatenate([ki_all, ki_all], axis=2).astype(BF16), tk_s)
        maskT = _idx_mask(qiT, wiT, kk_s, nb=nbs, tq=tq_s, qpos0=past, n_valid=past + dec,
                          n_sel=n_sel_s)
        oa = _dsa_attn(qaT, _pad_keys(ka_all.astype(BF16), tk_s),
                       _key_tiles_T(_pad_keys(va_all.astype(BF16), tk_s)), maskT, dsa_bias_s,
                       nb=nbs, tq=tq_s, qpos0=past)
        ob = _band_attn(qbT, _pad_keys(kb_all.astype(BF16), tkb_s),
                        _key_tiles_T(_pad_keys(vb_all.astype(BF16), tkb_s)), band_bias_s,
                        nb=nbs, tq=tq_s, off=0)
        h2 = _mix_ln(h, oa, ob, w_gate, wba, wbb, wo, ln2_g[l], ln2_b[l], alpha)
        ys = _ffn_ln(h2, wi2, wo2, ln3_g[l], ln3_b[l], alpha)
        heads = lambda a: a.reshape(nbs, -1, N_HEADS, HEAD_DIM)
        st_s.append((heads(ka_n), heads(va_n), ki_n, heads(kb_all[:, -band:]),
                     heads(vb_all[:, -band:])))

    y_prompt = yp.reshape(nbp, seq, d)
    y_sample = ys.reshape(nbs, tq_s, d)[:, :dec]
    stack = lambda sts, t: jnp.stack([s[t] for s in sts])
    return (y_prompt, y_sample,
            stack(st_p, 0), stack(st_p, 1), stack(st_p, 2), stack(st_p, 3), stack(st_p, 4),
            stack(st_s, 0), stack(st_s, 1), stack(st_s, 2), stack(st_s, 3), stack(st_s, 4))
```

```python
import functools
import math

import jax
import jax.numpy as jnp
from jax import lax
from jax.experimental import pallas as pl
from jax.experimental.pallas import tpu as pltpu

F32 = jnp.float32
BF16 = jnp.bfloat16
I32 = jnp.int32

CHUNK = 64
CHUNK_SHIFT = 6
HEAD_DIM = 64
N_HEADS = 8
WIDTH = N_HEADS * HEAD_DIM
N_HEADS_IDX = 4
IDX_DIM = 64
TOPK_MAX = 256
IDX_SCALE = (IDX_DIM * N_HEADS_IDX) ** -0.5
BAND_CHUNKS = 8
REL_BACK = 128
T5_BUCKETS = 32
T5_MAX_DIST = 128
LN_EPS = 1e-5
NEG_INF = -1e30
INT_MIN = -2 ** 31

ATT_TILE = 256
TOKEN_TILE = 512
FF_CHUNK = 256
VMEM_LIMIT_MB = 56


def _cparams(sem):
    return pltpu.CompilerParams(dimension_semantics=sem,
                                vmem_limit_bytes=VMEM_LIMIT_MB * 1024 * 1024)


def _const_spec(shape):
    nd = len(shape)
    return pl.BlockSpec(shape, lambda *_: (0,) * nd, pipeline_mode=pl.Buffered(1))


def _tree_sum(xs):
    xs = list(xs)
    while len(xs) > 1:
        nxt = [xs[a] + xs[a + 1] for a in range(0, len(xs) - 1, 2)]
        if len(xs) % 2:
            nxt.append(xs[-1])
        xs = nxt
    return xs[0]


def _layer_norm_rows(y, g, b):
    mu = jnp.mean(y, axis=-1, keepdims=True)
    d = y - mu
    var = jnp.mean(d * d, axis=-1, keepdims=True)
    return d * lax.rsqrt(var + LN_EPS) * g + b


def _ffn_ln_kernel(x_ref, wi_ref, wo_ref, g_ref, b_ref, o_ref, *, alpha, dff):
    x = x_ref[...]
    xb = x.astype(BF16)
    acc = jnp.zeros(x.shape, F32)
    for c in range(dff // FF_CHUNK):
        lo = c * FF_CHUNK
        a = jnp.dot(xb, wi_ref[:, lo:lo + FF_CHUNK], preferred_element_type=F32)
        u = jnp.dot(xb, wi_ref[:, dff + lo:dff + lo + FF_CHUNK], preferred_element_type=F32)
        hm = (a * jax.nn.sigmoid(a)) * u
        acc = acc + jnp.dot(hm.astype(BF16), wo_ref[lo:lo + FF_CHUNK, :],
                            preferred_element_type=F32)
    y = alpha * x + 0.5 * acc
    o_ref[...] = _layer_norm_rows(y, g_ref[...], b_ref[...])


def _ffn_ln(x, wi, wo, g, b, alpha):
    n, d = x.shape
    dff = wo.shape[0]
    tm = TOKEN_TILE
    return pl.pallas_call(
        functools.partial(_ffn_ln_kernel, alpha=alpha, dff=dff),
        out_shape=jax.ShapeDtypeStruct((n, d), F32),
        grid=(n // tm,),
        in_specs=[pl.BlockSpec((tm, d), lambda i: (i, 0)),
                  _const_spec((d, 2 * dff)), _const_spec((dff, d)),
                  _const_spec((1, d)), _const_spec((1, d))],
        out_specs=pl.BlockSpec((tm, d), lambda i: (i, 0)),
        compiler_params=_cparams(("parallel",)),
        name="ffn_ln",
    )(x, wi, wo, g.reshape(1, d), b.reshape(1, d))


_C_QA, _C_KA, _C_VA, _C_QI, _C_KI, _C_WI, _C_QB, _C_KB, _C_VB, _C_END = (
    0, 512, 1024, 1536, 1792, 1920, 2048, 2560, 3072, 3584)


def _inproj_kernel(h_ref, w_ref, qaT_ref, ka_ref, kab_ref, va_ref, vaT_ref, qiT_ref, ki_ref,
                   kk_ref, wiT_ref, qbT_ref, kbb_ref, vbT_ref, kbl_ref, vbl_ref, *, per):
    hb = h_ref[...].astype(BF16)

    def proj(c0, c1):
        return jnp.dot(hb, w_ref[:, c0:c1], preferred_element_type=F32)

    def put_tiles(ref, zt):
        for c in range(ref.shape[0]):
            ref[c] = zt[:, c * ATT_TILE:(c + 1) * ATT_TILE].astype(ref.dtype)

    scale = HEAD_DIM ** -0.5
    qaT_ref[...] = (proj(_C_QA, _C_KA) * scale).T.astype(BF16)
    ka = proj(_C_KA, _C_VA)
    ka_ref[...] = ka
    kab_ref[...] = ka.astype(BF16)
    va = proj(_C_VA, _C_QI)
    va_ref[...] = va
    put_tiles(vaT_ref, va.T)
    qiT_ref[...] = proj(_C_QI, _C_KI).T.astype(BF16)
    kk = proj(_C_KI, _C_WI)
    ki_ref[...] = kk[:, :IDX_DIM]
    kk_ref[...] = kk.astype(BF16)
    wiT_ref[...] = proj(_C_WI, _C_QB).T[:8, :]
    qbT_ref[...] = (proj(_C_QB, _C_KB) * scale).T.astype(BF16)
    kb = proj(_C_KB, _C_VB)
    kbb_ref[...] = kb.astype(BF16)
    vb = proj(_C_VB, _C_END)
    put_tiles(vbT_ref, vb.T)

    @pl.when(pl.program_id(0) % per == per - 1)
    def _():
        kbl_ref[...] = kb
        vbl_ref[...] = vb


def _in_proj(h, w, per):
    n, d = h.shape
    tm = TOKEN_TILE
    nt = n // tm
    sub = tm // ATT_TILE
    row = lambda width: pl.BlockSpec((tm, width), lambda i: (i, 0))
    col = lambda rows: pl.BlockSpec((rows, tm), lambda i: (0, i))
    til = pl.BlockSpec((sub, WIDTH, ATT_TILE), lambda i: (i, 0, 0))
    last = pl.BlockSpec((tm, WIDTH), lambda i: (i // per, 0))
    sds = jax.ShapeDtypeStruct
    out_shape = (
        sds((WIDTH, n), BF16),
        sds((n, WIDTH), F32), sds((n, WIDTH), BF16),
        sds((n, WIDTH), F32), sds((n // ATT_TILE, WIDTH, ATT_TILE), BF16),
        sds((N_HEADS_IDX * IDX_DIM, n), BF16),
        sds((n, IDX_DIM), F32), sds((n, 128), BF16),
        sds((8, n), F32),
        sds((WIDTH, n), BF16),
        sds((n, WIDTH), BF16),
        sds((n // ATT_TILE, WIDTH, ATT_TILE), BF16),
        sds((nt // per * tm, WIDTH), F32), sds((nt // per * tm, WIDTH), F32),
    )
    out_specs = (col(WIDTH), row(WIDTH), row(WIDTH), row(WIDTH), til,
                 col(N_HEADS_IDX * IDX_DIM), row(IDX_DIM), row(128), col(8),
                 col(WIDTH), row(WIDTH), til, last, last)
    return pl.pallas_call(
        functools.partial(_inproj_kernel, per=per),
        out_shape=out_shape,
        grid=(nt,),
        in_specs=[pl.BlockSpec((tm, d), lambda i: (i, 0)), _const_spec((d, _C_END))],
        out_specs=out_specs,
        compiler_params=_cparams(("arbitrary",)),
        name="in_proj",
    )(h, w)


def _idx_mask_kernel(qiT_ref, wiT_ref, kk_ref, mask_ref, st_ref, *, tq, qpos0, n_valid, n_sel):
    tk = ATT_TILE
    nkt_total = st_ref.shape[0] // tk
    q0 = qpos0 + pl.program_id(1) * tq
    n_keys = jnp.minimum(q0 + tq, n_valid)
    nt = jnp.minimum(lax.shift_right_logical(n_keys + tk - 1, int(math.log2(tk))), nkt_total)

    lane = lax.broadcasted_iota(I32, (8, tq), 1)
    q_chunk = lax.shift_right_logical(q0 + lane, CHUNK_SHIFT)
    n_adm = jnp.minimum((q_chunk + 1) * CHUNK, n_valid)
    k_row = jnp.minimum(n_adm, n_sel)

    zeros = jnp.zeros((128 - IDX_DIM, tq), BF16)
    qh = [jnp.concatenate([qiT_ref[h * IDX_DIM:(h + 1) * IDX_DIM, :], zeros], axis=0)
          for h in range(N_HEADS_IDX)]
    wh = [wiT_ref[h:h + 1, :] for h in range(N_HEADS_IDX)]

    def score_tile(j, carry):
        r0 = pl.multiple_of(j * tk, tk)
        kt = kk_ref[pl.ds(r0, tk), :]
        sc = None
        for h in range(N_HEADS_IDX):
            d = jnp.dot(kt, qh[h], preferred_element_type=F32)
            term = wh[h] * jnp.maximum(d, 0.0)
            sc = term if sc is None else sc + term
        sc = sc * IDX_SCALE
        bits = lax.bitcast_convert_type(sc, I32)
        bits = jnp.where(bits == INT_MIN, 0, bits)
        key = bits ^ (lax.shift_right_arithmetic(bits, 31) & 0x7FFFFFFF)
        kpos = r0 + lax.broadcasted_iota(I32, (tk, tq), 0)
        st_ref[pl.ds(r0, tk), :] = jnp.where(kpos < n_adm[0:1, :], key, INT_MIN)
        return carry

    lax.fori_loop(0, nt, score_tile, 0)

    def count(pred, thr8):
        def body(j, acc):
            r0 = pl.multiple_of(j * tk, tk)
            blk = st_ref[pl.ds(r0, tk), :]
            parts = [jnp.where(pred(blk[r * 8:(r + 1) * 8, :], thr8), 1, 0)
                     for r in range(tk // 8)]
            return acc + _tree_sum(parts)
        acc = lax.fori_loop(0, nt, body, jnp.zeros((8, tq), I32))
        return jnp.broadcast_to(jnp.sum(acc, axis=0, keepdims=True), (8, tq))

    ge = lambda a, b: a >= b
    gt = lambda a, b: a > b

    def bisect(k, t8):
        cand = t8 + lax.shift_left(jnp.int32(1), 31 - k)
        return jnp.where(count(ge, cand) >= k_row, cand, t8)

    t8 = lax.fori_loop(0, 32, bisect, jnp.full((8, tq), INT_MIN, I32))
    n_gt = count(gt, t8)
    t_row = t8[0:1, :]
    ties_wanted = (k_row - n_gt)[0:1, :].astype(F32)

    tri = jnp.where(lax.broadcasted_iota(I32, (tk, tk), 0) >= lax.broadcasted_iota(I32, (tk, tk), 1),
                    1.0, 0.0).astype(BF16)

    def final_tile(j, seen):
        r0 = pl.multiple_of(j * tk, tk)
        blk = st_ref[pl.ds(r0, tk), :]
        eq = jnp.where(blk == t_row, 1.0, 0.0)
        incl = jnp.dot(tri, eq.astype(BF16), preferred_element_type=F32)
        rank = seen + incl - eq
        take_tie = jnp.where(rank < ties_wanted, eq, 0.0)
        sel = jnp.where(blk > t_row, 1.0, take_tie)
        mask_ref[pl.ds(r0, tk), :] = sel.astype(jnp.int8)
        return seen + incl[tk - 1:tk, :]

    lax.fori_loop(0, nt, final_tile, jnp.zeros((1, tq), F32))

    def zero_tile(j, carry):
        r0 = pl.multiple_of(j * tk, tk)
        mask_ref[pl.ds(r0, tk), :] = jnp.zeros((tk, tq), jnp.int8)
        return carry

    lax.fori_loop(nt, nkt_total, zero_tile, 0)


def _idx_mask(qiT, wiT, kk, *, nb, tq, qpos0, n_valid, n_sel):
    tkp = kk.shape[1]
    nq = qiT.shape[1] // nb // tq
    return pl.pallas_call(
        functools.partial(_idx_mask_kernel, tq=tq, qpos0=qpos0, n_valid=n_valid, n_sel=n_sel),
        out_shape=jax.ShapeDtypeStruct((nb, tkp, nq * tq), jnp.int8),
        grid=(nb, nq),
        in_specs=[pl.BlockSpec((N_HEADS_IDX * IDX_DIM, tq), lambda b, i: (0, b * nq + i)),
                  pl.BlockSpec((8, tq), lambda b, i: (0, b * nq + i)),
                  pl.BlockSpec((None, tkp, 128), lambda b, i: (b, 0, 0))],
        out_specs=pl.BlockSpec((None, tkp, tq), lambda b, i: (b, 0, i)),
        scratch_shapes=[pltpu.VMEM((tkp, tq), I32)],
        compiler_params=_cparams(("parallel", "parallel")),
        name="idx_mask",
    )(qiT, wiT, kk)


def _load_qsel(qT_ref, qsel_ref, tq):
    zeros = jnp.zeros((HEAD_DIM, tq), BF16)
    for h in range(N_HEADS):
        blk = qT_ref[h * HEAD_DIM:(h + 1) * HEAD_DIM, :]
        qsel_ref[h] = jnp.concatenate([blk, zeros] if h % 2 == 0 else [zeros, blk], axis=0)


def _init_state(m_ref, l_ref, acc_ref):
    m_ref[...] = jnp.full(m_ref.shape, NEG_INF, F32)
    l_ref[...] = jnp.zeros(l_ref.shape, F32)
    acc_ref[...] = jnp.zeros(acc_ref.shape, F32)


def _scores(k_slab, qsel_ref, s_ref, h):
    s_ref[h] = jnp.dot(k_slab(h // 2), qsel_ref[h], preferred_element_type=F32)


def _heads_tile(v_rows, next_k_slab, qsel_ref, s_ref, m_ref, l_ref, acc_ref, addend):
    for h in range(N_HEADS):
        s = s_ref[h]
        for a in addend(h):
            s = s + a
        if next_k_slab is not None:
            _scores(next_k_slab, qsel_ref, s_ref, h)
        m_old = m_ref[h:h + 1, :]
        m_new = jnp.maximum(m_old, jnp.max(s, axis=0, keepdims=True))
        alpha = jnp.exp(m_old - m_new)
        p = jnp.exp(s - m_new)
        l_ref[h:h + 1, :] = alpha * l_ref[h:h + 1, :] + jnp.sum(p, axis=0, keepdims=True)
        m_ref[h:h + 1, :] = m_new
        pv = jnp.dot(v_rows(h), p.astype(BF16), preferred_element_type=F32)
        rows = slice(h * HEAD_DIM, (h + 1) * HEAD_DIM)
        acc_ref[rows, :] = alpha * acc_ref[rows, :] + pv


def _attn_scratch(tq):
    return [pltpu.VMEM((N_HEADS, 128, tq), BF16), pltpu.VMEM((N_HEADS, ATT_TILE, tq), F32),
            pltpu.VMEM((N_HEADS, tq), F32), pltpu.VMEM((N_HEADS, tq), F32),
            pltpu.VMEM((WIDTH, tq), F32)]


def _store_out(o_ref, l_ref, acc_ref):
    outs = [acc_ref[h * HEAD_DIM:(h + 1) * HEAD_DIM, :] / l_ref[h:h + 1, :] for h in range(N_HEADS)]
    o_ref[...] = jnp.concatenate(outs, axis=0).T.astype(o_ref.dtype)


def _dsa_kernel(qT_ref, k_ref, vT_ref, mask_ref, bias_ref, o_ref, qsel_ref, s_ref, m_ref, l_ref,
                acc_ref, *, tq, qpos0):
    tk = ATT_TILE
    jd = lax.shift_right_logical(qpos0 + pl.program_id(1) * tq, int(math.log2(tk)))
    _load_qsel(qT_ref, qsel_ref, tq)
    _init_state(m_ref, l_ref, acc_ref)

    def k_slab(j):
        r0 = pl.multiple_of(j * tk, tk)
        return lambda g: k_ref[pl.ds(r0, tk), g * 128:(g + 1) * 128]

    def tile(j, near, last=False):
        r0 = pl.multiple_of(j * tk, tk)
        unselected = jnp.where(mask_ref[pl.ds(r0, tk), :].astype(I32) != 0, 0.0, NEG_INF)
        addend = ((lambda h: (unselected,)) if near is None
                  else (lambda h: (unselected, bias_ref[near, h])))
        _heads_tile(lambda h: vT_ref[j, h * HEAD_DIM:(h + 1) * HEAD_DIM, :],
                    None if last else k_slab(j + 1),
                    qsel_ref, s_ref, m_ref, l_ref, acc_ref, addend)

    for h in range(N_HEADS):
        _scores(k_slab(0), qsel_ref, s_ref, h)

    def far(j, carry):
        tile(j, None)
        return carry

    lax.fori_loop(0, jnp.maximum(jd - 1, 0), far, 0)

    @pl.when(jd >= 1)
    def _():
        tile(jd - 1, 0)

    tile(jd, 1, last=True)
    _store_out(o_ref, l_ref, acc_ref)


def _dsa_attn(qT, k, vT, maskT, bias, *, nb, tq, qpos0):
    tkp = k.shape[1]
    nkt = tkp // ATT_TILE
    nq = qT.shape[1] // nb // tq
    return pl.pallas_call(
        functools.partial(_dsa_kernel, tq=tq, qpos0=qpos0),
        out_shape=jax.ShapeDtypeStruct((nb * nq * tq, WIDTH), BF16),
        grid=(nb, nq),
        in_specs=[pl.BlockSpec((WIDTH, tq), lambda b, i: (0, b * nq + i)),
                  pl.BlockSpec((None, tkp, WIDTH), lambda b, i: (b, 0, 0),
                               pipeline_mode=pl.Buffered(1)),
                  pl.BlockSpec((nkt, WIDTH, ATT_TILE), lambda b, i: (b, 0, 0),
                               pipeline_mode=pl.Buffered(1)),
                  pl.BlockSpec((None, tkp, tq), lambda b, i: (b, 0, i)),
                  _const_spec(bias.shape)],
        out_specs=pl.BlockSpec((tq, WIDTH), lambda b, i: (b * nq + i, 0)),
        scratch_shapes=_attn_scratch(tq),
        compiler_params=_cparams(("parallel", "parallel")),
        name="dsa_attn",
    )(qT, k, vT, maskT, bias)


_BAND_TILES = BAND_CHUNKS * CHUNK // ATT_TILE + 1


def _band_kernel(qT_ref, *refs, tq, off):
    k_refs = refs[:_BAND_TILES]
    v_refs = refs[_BAND_TILES:2 * _BAND_TILES]
    bias_ref, o_ref, qsel_ref, s_ref, m_ref, l_ref, acc_ref = refs[2 * _BAND_TILES:]
    _load_qsel(qT_ref, qsel_ref, tq)
    _init_state(m_ref, l_ref, acc_ref)
    k_slab = lambda w: (lambda g: k_refs[w][:, g * 128:(g + 1) * 128])
    for h in range(N_HEADS):
        _scores(k_slab(0), qsel_ref, s_ref, h)
    for w in range(_BAND_TILES):
        entry = jnp.where(pl.program_id(1) + off + w >= 0, w, _BAND_TILES)
        _heads_tile(lambda h: v_refs[w][0, h * HEAD_DIM:(h + 1) * HEAD_DIM, :],
                    None if w == _BAND_TILES - 1 else k_slab(w + 1),
                    qsel_ref, s_ref, m_ref, l_ref, acc_ref, lambda h: (bias_ref[entry, h],))
    _store_out(o_ref, l_ref, acc_ref)


def _band_attn(qT, k, vT, bias, *, nb, tq, off):
    nkt = k.shape[1] // ATT_TILE
    nq = qT.shape[1] // nb // tq
    tile_of = lambda i, w: jnp.maximum(i + off + w, 0)
    k_specs = [pl.BlockSpec((None, ATT_TILE, WIDTH), lambda b, i, w=w: (b, tile_of(i, w), 0))
               for w in range(_BAND_TILES)]
    v_specs = [pl.BlockSpec((1, WIDTH, ATT_TILE), lambda b, i, w=w: (b * nkt + tile_of(i, w), 0, 0))
               for w in range(_BAND_TILES)]
    return pl.pallas_call(
        functools.partial(_band_kernel, tq=tq, off=off),
        out_shape=jax.ShapeDtypeStruct((nb * nq * tq, WIDTH), BF16),
        grid=(nb, nq),
        in_specs=[pl.BlockSpec((WIDTH, tq), lambda b, i: (0, b * nq + i))] + k_specs + v_specs
                 + [_const_spec(bias.shape)],
        out_specs=pl.BlockSpec((tq, WIDTH), lambda b, i: (b * nq + i, 0)),
        scratch_shapes=_attn_scratch(tq),
        compiler_params=_cparams(("parallel", "parallel")),
        name="band_attn",
    )(qT, *([k] * _BAND_TILES), *([vT] * _BAND_TILES), bias)


def _mix_ln_kernel(h_ref, oa_ref, ob_ref, wg_ref, wba_ref, wbb_ref, wo_ref, g_ref, b_ref, o_ref,
                   *, alpha):
    h = h_ref[...]
    hb = h.astype(BF16)
    d = h.shape[1]
    ga = jnp.dot(hb, wg_ref[:, :d], preferred_element_type=F32)
    gb = jnp.dot(hb, wg_ref[:, d:], preferred_element_type=F32)
    ya = jnp.dot(oa_ref[...], wba_ref[...], preferred_element_type=F32)
    yb = jnp.dot(ob_ref[...], wbb_ref[...], preferred_element_type=F32)
    gated = jax.nn.sigmoid(ga) * ya + jax.nn.sigmoid(gb) * yb
    mix = jnp.dot(gated.astype(BF16), wo_ref[...], preferred_element_type=F32)
    o_ref[...] = _layer_norm_rows(alpha * h + mix, g_ref[...], b_ref[...])


def _mix_ln(h, oa, ob, wg, wba, wbb, wo, g, b, alpha):
    n, d = h.shape
    tm = TOKEN_TILE
    row = lambda width: pl.BlockSpec((tm, width), lambda i: (i, 0))
    return pl.pallas_call(
        functools.partial(_mix_ln_kernel, alpha=alpha),
        out_shape=jax.ShapeDtypeStruct((n, d), F32),
        grid=(n // tm,),
        in_specs=[row(d), row(WIDTH), row(WIDTH), _const_spec(wg.shape), _const_spec(wba.shape),
                  _const_spec(wbb.shape), _const_spec(wo.shape), _const_spec((1, d)),
                  _const_spec((1, d))],
        out_specs=row(d),
        compiler_params=_cparams(("parallel",)),
        name="mix_ln",
    )(h, oa, ob, wg, wba, wbb, wo, g.reshape(1, d), b.reshape(1, d))


def _t5_bucket(rel):
    half = T5_BUCKETS // 2
    exact = half // 2
    n = jnp.abs(rel)
    log_ratio = jnp.log(jnp.maximum(n, 1).astype(F32) / exact) / math.log(T5_MAX_DIST / exact)
    large = jnp.minimum(exact + (log_ratio * (half - exact)).astype(I32), half - 1)
    return (rel > 0).astype(I32) * half + jnp.where(n < exact, n, large)


def _rel_line(n_tiles, first, tq):
    period = ATT_TILE + tq
    y = jnp.arange(period, dtype=I32)[None, :]
    c = (jnp.arange(n_tiles, dtype=I32)[:, None] + first) * ATT_TILE
    return jnp.where(y < tq, c - y, c + period - y)


def _toeplitz(v, tq):
    period = v.shape[-1]
    flat = jnp.tile(v, (1,) * (v.ndim - 1) + (ATT_TILE,))[..., :ATT_TILE * (period - 1)]
    return flat.reshape(v.shape[:-1] + (ATT_TILE, period - 1))[..., :tq]


def _dsa_bias_table(t5_bias, tq):
    assert T5_MAX_DIST <= ATT_TILE + 1
    far = t5_bias[:, _t5_bucket(jnp.int32(-(ATT_TILE + 1)))]
    line = t5_bias[:, _t5_bucket(_rel_line(2, -1, tq))] - far[:, None, None]
    return _toeplitz(jnp.moveaxis(line, 0, 1), tq)


def _band_bias_table(rel_bias, tq):
    first = 1 - _BAND_TILES
    ridx = jnp.clip(_rel_line(_BAND_TILES, first, tq), -REL_BACK, CHUNK - 1) + REL_BACK
    bias = _toeplitz(jnp.moveaxis(rel_bias[:, ridx], 0, 1), tq)
    w = jnp.arange(_BAND_TILES, dtype=I32)[:, None, None]
    j = jnp.arange(ATT_TILE, dtype=I32)[None, :, None]
    i = jnp.arange(tq, dtype=I32)[None, None, :]
    dchunk = ((w + first) * ATT_TILE + j) // CHUNK - i // CHUNK
    ok = (dchunk <= 0) & (dchunk >= -BAND_CHUNKS)
    table = jnp.where(ok[:, None], bias, NEG_INF)
    return jnp.concatenate([table, jnp.full_like(table[:1], NEG_INF)], axis=0)


def _regroup_w_in(w_in):
    sizes = (WIDTH, WIDTH, WIDTH, N_HEADS_IDX * IDX_DIM, IDX_DIM, N_HEADS_IDX,
             WIDTH, WIDTH, WIDTH, w_in.shape[0], w_in.shape[0])
    offs = [0]
    for s in sizes:
        offs.append(offs[-1] + s)
    qa, ka, va, qi, ki, wi, qb, kb, vb, ga, gb = (w_in[:, offs[t]:offs[t + 1]] for t in range(11))
    wi_pad = jnp.pad(wi, ((0, 0), (0, 128 - N_HEADS_IDX)))
    w_proj = jnp.concatenate([qa, ka, va, qi, ki, ki, wi_pad, qb, kb, vb], axis=1).astype(BF16)
    w_gate = jnp.concatenate([ga, gb], axis=1).astype(BF16)
    return w_proj, w_gate


def _key_tiles_T(v):
    nb, tkp, wd = v.shape
    return v.reshape(nb, tkp // ATT_TILE, ATT_TILE, wd).transpose(0, 1, 3, 2).reshape(-1, wd, ATT_TILE)


def _pad_keys(x, tkp):
    return jnp.pad(x, ((0, 0), (0, tkp - x.shape[1]), (0, 0)))


def kernel(x_prompt, x_sample, cache_k_a, cache_v_a, cache_kidx_a, cache_k_b, cache_v_b, t5_bias,
           ln1_g, ln1_b, ffn1_wi, ffn1_wo, ln2_g, ln2_b, w_in, rel_bias_b, w_branch_a, w_branch_b,
           w_out, ln3_g, ln3_b, ffn2_wi, ffn2_wo):
    depth = ln1_g.shape[0]
    alpha = (2.0 * depth) ** 0.25
    nbp, seq, d = x_prompt.shape
    nbs, dec, _ = x_sample.shape
    past = cache_k_a.shape[2]
    band = cache_k_b.shape[2]
    keep = min(BAND_CHUNKS * CHUNK, seq)
    tq_p, tq_s = ATT_TILE, 128
    assert seq % TOKEN_TILE == 0 and keep == TOKEN_TILE and dec <= tq_s and past % ATT_TILE == 0
    assert band == BAND_CHUNKS * CHUNK and (nbs * tq_s) % TOKEN_TILE == 0

    yp = x_prompt.reshape(nbp * seq, d)
    ys = jnp.pad(x_sample, ((0, 0), (0, tq_s - dec), (0, 0))).reshape(nbs * tq_s, d)
    dsa_bias_p = _dsa_bias_table(t5_bias, tq_p)
    dsa_bias_s = dsa_bias_p[..., :tq_s]
    n_sel_p = min(TOPK_MAX, seq // 4)
    n_sel_s = min(TOPK_MAX, (past + dec) // 4)
    tk_s = -(-(past + tq_s) // ATT_TILE) * ATT_TILE
    tkb_s = _BAND_TILES * ATT_TILE

    st_p, st_s = [], []
    for l in range(depth):
        w_proj, w_gate = _regroup_w_in(w_in[l])
        wi1, wo1 = ffn1_wi[l].astype(BF16), ffn1_wo[l].astype(BF16)
        wi2, wo2 = ffn2_wi[l].astype(BF16), ffn2_wo[l].astype(BF16)
        wba, wbb, wo = (w_branch_a[l].astype(BF16), w_branch_b[l].astype(BF16),
                        w_out[l].astype(BF16))
        band_bias_p = _band_bias_table(rel_bias_b[l], tq_p)
        band_bias_s = band_bias_p[..., :tq_s]

        h = _ffn_ln(yp, wi1, wo1, ln1_g[l], ln1_b[l], alpha)
        (qaT, ka, kab, va, vaT, qiT, ki, kk, wiT, qbT, kbb, vbT, kbl, vbl) = _in_proj(
            h, w_proj, seq // TOKEN_TILE)
        maskT = _idx_mask(qiT, wiT, kk.reshape(nbp, seq, 128), nb=nbp, tq=tq_p, qpos0=0,
                          n_valid=seq, n_sel=n_sel_p)
        oa = _dsa_attn(qaT, kab.reshape(nbp, seq, WIDTH), vaT, maskT, dsa_bias_p,
                       nb=nbp, tq=tq_p, qpos0=0)
        ob = _band_attn(qbT, kbb.reshape(nbp, seq, WIDTH), vbT, band_bias_p,
                        nb=nbp, tq=tq_p, off=1 - _BAND_TILES)
        h2 = _mix_ln(h, oa, ob, w_gate, wba, wbb, wo, ln2_g[l], ln2_b[l], alpha)
        yp = _ffn_ln(h2, wi2, wo2, ln3_g[l], ln3_b[l], alpha)
        st_p.append((ka.reshape(nbp, seq, N_HEADS, HEAD_DIM), va.reshape(nbp, seq, N_HEADS, HEAD_DIM),
                     ki.reshape(nbp, seq, IDX_DIM), kbl.reshape(nbp, keep, N_HEADS, HEAD_DIM),
                     vbl.reshape(nbp, keep, N_HEADS, HEAD_DIM)))

        h = _ffn_ln(ys, wi1, wo1, ln1_g[l], ln1_b[l], alpha)
        (qaT, ka, _, va, _, qiT, ki, _, wiT, qbT, _, _, kbl, vbl) = _in_proj(h, w_proj, 1)
        new = lambda a: a.reshape(nbs, tq_s, -1)[:, :dec]
        ka_n, va_n, ki_n, kb_n, vb_n = new(ka), new(va), new(ki), new(kbl), new(vbl)
        ka_all = jnp.concatenate([cache_k_a[l].reshape(nbs, past, WIDTH), ka_n], axis=1)
        va_all = jnp.concatenate([cache_v_a[l].reshape(nbs, past, WIDTH), va_n], axis=1)
        ki_all = jnp.concatenate([cache_kidx_a[l], ki_n], axis=1)
        kb_all = jnp.concatenate([cache_k_b[l].reshape(nbs, band, WIDTH), kb_n], axis=1)
        vb_all = jnp.concatenate([cache_v_b[l].reshape(nbs, band, WIDTH), vb_n], axis=1)
        kk_s = _pad_keys(jnp.concatenate([ki_all, ki_all], axis=2).astype(BF16), tk_s)
        maskT = _idx_mask(qiT, wiT, kk_s, nb=nbs, tq=tq_s, qpos0=past, n_valid=past + dec,
                          n_sel=n_sel_s)
        oa = _dsa_attn(qaT, _pad_keys(ka_all.astype(BF16), tk_s),
                       _key_tiles_T(_pad_keys(va_all.astype(BF16), tk_s)), maskT, dsa_bias_s,
                       nb=nbs, tq=tq_s, qpos0=past)
        ob = _band_attn(qbT, _pad_keys(kb_all.astype(BF16), tkb_s),
                        _key_tiles_T(_pad_keys(vb_all.astype(BF16), tkb_s)), band_bias_s,
                        nb=nbs, tq=tq_s, off=0)
        h2 = _mix_ln(h, oa, ob, w_gate, wba, wbb, wo, ln2_g[l], ln2_b[l], alpha)
        ys = _ffn_ln(h2, wi2, wo2, ln3_g[l], ln3_b[l], alpha)
        heads = lambda a: a.reshape(nbs, -1, N_HEADS, HEAD_DIM)
        st_s.append((heads(ka_n), heads(va_n), ki_n, heads(kb_all[:, -band:]),
                     heads(vb_all[:, -band:])))

    y_prompt = yp.reshape(nbp, seq, d)
    y_sample = ys.reshape(nbs, tq_s, d)[:, :dec]
    stack = lambda sts, t: jnp.stack([s[t] for s in sts])
    return (y_prompt, y_sample,
            stack(st_p, 0), stack(st_p, 1), stack(st_p, 2), stack(st_p, 3), stack(st_p, 4),
            stack(st_s, 0), stack(st_s, 1), stack(st_s, 2), stack(st_s, 3), stack(st_s, 4))
```

```python
import functools
import math

import jax
import jax.numpy as jnp
from jax import lax
from jax.experimental import pallas as pl
from jax.experimental.pallas import tpu as pltpu

F32 = jnp.float32
BF16 = jnp.bfloat16
I32 = jnp.int32

CHUNK = 64
CHUNK_SHIFT = 6
HEAD_DIM = 64
N_HEADS = 8
WIDTH = N_HEADS * HEAD_DIM
N_HEADS_IDX = 4
IDX_DIM = 64
TOPK_MAX = 256
IDX_SCALE = (IDX_DIM * N_HEADS_IDX) ** -0.5
BAND_CHUNKS = 8
REL_BACK = 128
T5_BUCKETS = 32
T5_MAX_DIST = 128
LN_EPS = 1e-5
NEG_INF = -1e30
INT_MIN = -2 ** 31

ATT_TILE = 256
TOKEN_TILE = 512
FF_CHUNK = 256
VMEM_LIMIT_MB = 56


def _cparams(sem):
    return pltpu.CompilerParams(dimension_semantics=sem,
                                vmem_limit_bytes=VMEM_LIMIT_MB * 1024 * 1024)


def _const_spec(shape):
    nd = len(shape)
    return pl.BlockSpec(shape, lambda *_: (0,) * nd, pipeline_mode=pl.Buffered(1))


def _tree_sum(xs):
    xs = list(xs)
    while len(xs) > 1:
        nxt = [xs[a] + xs[a + 1] for a in range(0, len(xs) - 1, 2)]
        if len(xs) % 2:
            nxt.append(xs[-1])
        xs = nxt
    return xs[0]


def _layer_norm_rows(y, g, b):
    mu = jnp.mean(y, axis=-1, keepdims=True)
    d = y - mu
    var = jnp.mean(d * d, axis=-1, keepdims=True)
    return d * lax.rsqrt(var + LN_EPS) * g + b


def _ffn_ln_kernel(x_ref, wi_ref, wo_ref, g_ref, b_ref, o_ref, *, alpha, dff):
    x = x_ref[...]
    xb = x.astype(BF16)
    acc = jnp.zeros(x.shape, F32)
    for c in range(dff // FF_CHUNK):
        lo = c * FF_CHUNK
        a = jnp.dot(xb, wi_ref[:, lo:lo + FF_CHUNK], preferred_element_type=F32)
        u = jnp.dot(xb, wi_ref[:, dff + lo:dff + lo + FF_CHUNK], preferred_element_type=F32)
        hm = (a * jax.nn.sigmoid(a)) * u
        acc = acc + jnp.dot(hm.astype(BF16), wo_ref[lo:lo + FF_CHUNK, :],
                            preferred_element_type=F32)
    y = alpha * x + 0.5 * acc
    o_ref[...] = _layer_norm_rows(y, g_ref[...], b_ref[...])


def _ffn_ln(x, wi, wo, g, b, alpha):
    n, d = x.shape
    dff = wo.shape[0]
    tm = TOKEN_TILE
    return pl.pallas_call(
        functools.partial(_ffn_ln_kernel, alpha=alpha, dff=dff),
        out_shape=jax.ShapeDtypeStruct((n, d), F32),
        grid=(n // tm,),
        in_specs=[pl.BlockSpec((tm, d), lambda i: (i, 0)),
                  _const_spec((d, 2 * dff)), _const_spec((dff, d)),
                  _const_spec((1, d)), _const_spec((1, d))],
        out_specs=pl.BlockSpec((tm, d), lambda i: (i, 0)),
        compiler_params=_cparams(("parallel",)),
        name="ffn_ln",
    )(x, wi, wo, g.reshape(1, d), b.reshape(1, d))


_C_QA, _C_KA, _C_VA, _C_QI, _C_KI, _C_WI, _C_QB, _C_KB, _C_VB, _C_END = (
    0, 512, 1024, 1536, 1792, 1920, 2048, 2560, 3072, 3584)


def _inproj_kernel(h_ref, w_ref, qaT_ref, ka_ref, kab_ref, va_ref, vaT_ref, qiT_ref, ki_ref,
                   kk_ref, wiT_ref, qbT_ref, kbb_ref, vbT_ref, kbl_ref, vbl_ref, *, per):
    hb = h_ref[...].astype(BF16)

    def proj(c0, c1):
        return jnp.dot(hb, w_ref[:, c0:c1], preferred_element_type=F32)

    def put_tiles(ref, zt):
        for c in range(ref.shape[0]):
            ref[c] = zt[:, c * ATT_TILE:(c + 1) * ATT_TILE].astype(ref.dtype)

    scale = HEAD_DIM ** -0.5
    qaT_ref[...] = (proj(_C_QA, _C_KA) * scale).T.astype(BF16)
    ka = proj(_C_KA, _C_VA)
    ka_ref[...] = ka
    kab_ref[...] = ka.astype(BF16)
    va = proj(_C_VA, _C_QI)
    va_ref[...] = va
    put_tiles(vaT_ref, va.T)
    qiT_ref[...] = proj(_C_QI, _C_KI).T.astype(BF16)
    kk = proj(_C_KI, _C_WI)
    ki_ref[...] = kk[:, :IDX_DIM]
    kk_ref[...] = kk.astype(BF16)
    wiT_ref[...] = proj(_C_WI, _C_QB).T[:8, :]
    qbT_ref[...] = (proj(_C_QB, _C_KB) * scale).T.astype(BF16)
    kb = proj(_C_KB, _C_VB)
    kbb_ref[...] = kb.astype(BF16)
    vb = proj(_C_VB, _C_END)
    put_tiles(vbT_ref, vb.T)

    @pl.when(pl.program_id(0) % per == per - 1)
    def _():
        kbl_ref[...] = kb
        vbl_ref[...] = vb


def _in_proj(h, w, per):
    n, d = h.shape
    tm = TOKEN_TILE
    nt = n // tm
    sub = tm // ATT_TILE
    row = lambda width: pl.BlockSpec((tm, width), lambda i: (i, 0))
    col = lambda rows: pl.BlockSpec((rows, tm), lambda i: (0, i))
    til = pl.BlockSpec((sub, WIDTH, ATT_TILE), lambda i: (i, 0, 0))
    last = pl.BlockSpec((tm, WIDTH), lambda i: (i // per, 0))
    sds = jax.ShapeDtypeStruct
    out_shape = (
        sds((WIDTH, n), BF16),
        sds((n, WIDTH), F32), sds((n, WIDTH), BF16),
        sds((n, WIDTH), F32), sds((n // ATT_TILE, WIDTH, ATT_TILE), BF16),
        sds((N_HEADS_IDX * IDX_DIM, n), BF16),
        sds((n, IDX_DIM), F32), sds((n, 128), BF16),
        sds((8, n), F32),
        sds((WIDTH, n), BF16),
        sds((n, WIDTH), BF16),
        sds((n // ATT_TILE, WIDTH, ATT_TILE), BF16),
        sds((nt // per * tm, WIDTH), F32), sds((nt // per * tm, WIDTH), F32),
    )
    out_specs = (col(WIDTH), row(WIDTH), row(WIDTH), row(WIDTH), til,
                 col(N_HEADS_IDX * IDX_DIM), row(IDX_DIM), row(128), col(8),
                 col(WIDTH), row(WIDTH), til, last, last)
    return pl.pallas_call(
        functools.partial(_inproj_kernel, per=per),
        out_shape=out_shape,
        grid=(nt,),
        in_specs=[pl.BlockSpec((tm, d), lambda i: (i, 0)), _const_spec((d, _C_END))],
        out_specs=out_specs,
        compiler_params=_cparams(("arbitrary",)),
        name="in_proj",
    )(h, w)


_MAX_PASSES = 72


def _flip(bits):
    return bits ^ (lax.shift_right_arithmetic(bits, 31) & 0x7FFFFFFF)


def _idx_mask_kernel(qiT_ref, wiT_ref, kk_ref, mask_ref, st_ref, gmax_ref, *, tq, qpos0, n_valid,
                     n_sel):
    tk = ATT_TILE
    nkt_total = st_ref.shape[0] // tk
    q0 = qpos0 + pl.program_id(1) * tq
    n_keys = jnp.minimum(q0 + tq, n_valid)
    nt = jnp.minimum(lax.shift_right_logical(n_keys + tk - 1, int(math.log2(tk))), nkt_total)

    lane = lax.broadcasted_iota(I32, (8, tq), 1)
    q_chunk = lax.shift_right_logical(q0 + lane, CHUNK_SHIFT)
    n_adm = jnp.minimum((q_chunk + 1) * CHUNK, n_valid)
    k_row = jnp.minimum(n_adm, n_sel)

    zeros = jnp.zeros((128 - IDX_DIM, tq), BF16)
    qh = [jnp.concatenate([qiT_ref[h * IDX_DIM:(h + 1) * IDX_DIM, :], zeros], axis=0)
          for h in range(N_HEADS_IDX)]
    wh = [wiT_ref[h:h + 1, :] for h in range(N_HEADS_IDX)]

    def score_tile(j, carry):
        r0 = pl.multiple_of(j * tk, tk)
        kt = kk_ref[pl.ds(r0, tk), :]
        sc = None
        for h in range(N_HEADS_IDX):
            d = jnp.dot(kt, qh[h], preferred_element_type=F32)
            term = wh[h] * jnp.maximum(d, 0.0)
            sc = term if sc is None else sc + term
        sc = sc * IDX_SCALE
        bits = lax.bitcast_convert_type(sc, I32)
        bits = jnp.where(bits == INT_MIN, 0, bits)
        key = _flip(bits)
        kpos = r0 + lax.broadcasted_iota(I32, (tk, tq), 0)
        key = jnp.where(kpos < n_adm[0:1, :], key, INT_MIN)
        st_ref[pl.ds(r0, tk), :] = key
        gmax_ref[...] = jnp.maximum(gmax_ref[...], key)
        return carry

    gmax_ref[...] = jnp.full((tk, tq), INT_MIN, I32)
    lax.fori_loop(0, nt, score_tile, 0)

    def count_ge(thr8):
        def body(j, acc):
            r0 = pl.multiple_of(j * tk, tk)
            blk = st_ref[pl.ds(r0, tk), :]
            parts = [jnp.where(blk[r * 8:(r + 1) * 8, :] >= thr8, 1, 0) for r in range(tk // 8)]
            return acc + _tree_sum(parts)
        acc = lax.fori_loop(0, nt, body, jnp.zeros((8, tq), I32))
        return jnp.broadcast_to(jnp.sum(acc, axis=0, keepdims=True), (8, tq))

    gmax = gmax_ref[...]
    probe0 = jnp.broadcast_to(jnp.min(gmax, axis=0, keepdims=True), (8, tq))
    top = jnp.broadcast_to(jnp.max(gmax, axis=0, keepdims=True), (8, tq))
    as_f32 = lambda key: lax.bitcast_convert_type(_flip(key), F32)

    def search_body(st):
        it, lo, hi, c_lo, c_hi, _ = st
        mid = (lo & hi) + lax.shift_right_arithmetic(lo ^ hi, 1)
        f_lo, f_hi = as_f32(lo), as_f32(hi)
        frac = ((c_lo - k_row).astype(F32) + 0.5) / (c_lo - c_hi).astype(F32)
        f_t = f_lo + (f_hi - f_lo) * frac
        guess = jnp.clip(_flip(lax.bitcast_convert_type(f_t, I32)), lo + 1, hi - 1)
        usable = jnp.where(f_t >= f_lo, jnp.where(f_t <= f_hi, 1, 0), 0)
        cand = jnp.where((it & 1) == 0, jnp.where(usable == 1, guess, mid), mid)
        cand = jnp.where(it == 0, probe0, cand)
        c = count_ge(cand)
        up = c >= k_row
        lo, c_lo = jnp.where(up, cand, lo), jnp.where(up, c, c_lo)
        hi, c_hi = jnp.where(up, hi, cand), jnp.where(up, c_hi, c)
        open_ = jnp.where(c_lo == k_row, 0.0, jnp.where(hi - lo == 1, 0.0, 1.0))
        return it + 1, lo, hi, c_lo, c_hi, jnp.max(open_)

    init = (jnp.int32(0), jnp.full((8, tq), INT_MIN, I32), jnp.minimum(top, 2 ** 31 - 2) + 1,
            jnp.full((8, tq), nt * tk, I32), jnp.zeros((8, tq), I32), jnp.float32(1.0))
    _, lo, _, _, c_hi, _ = lax.while_loop(lambda st: (st[0] < _MAX_PASSES) & (st[5] > 0.0),
                                          search_body, init)
    t_row = lo[0:1, :]
    ties_wanted = (k_row - c_hi)[0:1, :].astype(F32)

    tri = jnp.where(lax.broadcasted_iota(I32, (tk, tk), 0) >= lax.broadcasted_iota(I32, (tk, tk), 1),
                    1.0, 0.0).astype(BF16)

    def final_tile(j, seen):
        r0 = pl.multiple_of(j * tk, tk)
        blk = st_ref[pl.ds(r0, tk), :]
        eq = jnp.where(blk == t_row, 1.0, 0.0)
        incl = jnp.dot(tri, eq.astype(BF16), preferred_element_type=F32)
        rank = seen + incl - eq
        take_tie = jnp.where(rank < ties_wanted, eq, 0.0)
        sel = jnp.where(blk > t_row, 1.0, take_tie)
        mask_ref[pl.ds(r0, tk), :] = sel.astype(jnp.int8)
        return seen + incl[tk - 1:tk, :]

    lax.fori_loop(0, nt, final_tile, jnp.zeros((1, tq), F32))

    def zero_tile(j, carry):
        r0 = pl.multiple_of(j * tk, tk)
        mask_ref[pl.ds(r0, tk), :] = jnp.zeros((tk, tq), jnp.int8)
        return carry

    lax.fori_loop(nt, nkt_total, zero_tile, 0)


def _idx_mask(qiT, wiT, kk, *, nb, tq, qpos0, n_valid, n_sel):
    tkp = kk.shape[1]
    nq = qiT.shape[1] // nb // tq
    return pl.pallas_call(
        functools.partial(_idx_mask_kernel, tq=tq, qpos0=qpos0, n_valid=n_valid, n_sel=n_sel),
        out_shape=jax.ShapeDtypeStruct((nb, tkp, nq * tq), jnp.int8),
        grid=(nb, nq),
        in_specs=[pl.BlockSpec((N_HEADS_IDX * IDX_DIM, tq), lambda b, i: (0, b * nq + i)),
                  pl.BlockSpec((8, tq), lambda b, i: (0, b * nq + i)),
                  pl.BlockSpec((None, tkp, 128), lambda b, i: (b, 0, 0))],
        out_specs=pl.BlockSpec((None, tkp, tq), lambda b, i: (b, 0, i)),
        scratch_shapes=[pltpu.VMEM((tkp, tq), I32), pltpu.VMEM((ATT_TILE, tq), I32)],
        compiler_params=_cparams(("parallel", "parallel")),
        name="idx_mask",
    )(qiT, wiT, kk)


def _load_qsel(qT_ref, qsel_ref, tq):
    zeros = jnp.zeros((HEAD_DIM, tq), BF16)
    for h in range(N_HEADS):
        blk = qT_ref[h * HEAD_DIM:(h + 1) * HEAD_DIM, :]
        qsel_ref[h] = jnp.concatenate([blk, zeros] if h % 2 == 0 else [zeros, blk], axis=0)


def _init_state(m_ref, l_ref, acc_ref):
    m_ref[...] = jnp.full(m_ref.shape, NEG_INF, F32)
    l_ref[...] = jnp.zeros(l_ref.shape, F32)
    acc_ref[...] = jnp.zeros(acc_ref.shape, F32)


def _scores(k_slab, qsel_ref, s_ref, h):
    s_ref[h] = jnp.dot(k_slab(h // 2), qsel_ref[h], preferred_element_type=F32)


def _heads_tile(v_rows, next_k_slab, qsel_ref, s_ref, m_ref, l_ref, acc_ref, addend):
    for h in range(N_HEADS):
        s = s_ref[h]
        for a in addend(h):
            s = s + a
        if next_k_slab is not None:
            _scores(next_k_slab, qsel_ref, s_ref, h)
        m_old = m_ref[h:h + 1, :]
        m_new = jnp.maximum(m_old, jnp.max(s, axis=0, keepdims=True))
        alpha = jnp.exp(m_old - m_new)
        p = jnp.exp(s - m_new)
        l_ref[h:h + 1, :] = alpha * l_ref[h:h + 1, :] + jnp.sum(p, axis=0, keepdims=True)
        m_ref[h:h + 1, :] = m_new
        pv = jnp.dot(v_rows(h), p.astype(BF16), preferred_element_type=F32)
        rows = slice(h * HEAD_DIM, (h + 1) * HEAD_DIM)
        acc_ref[rows, :] = alpha * acc_ref[rows, :] + pv


def _attn_scratch(tq):
    return [pltpu.VMEM((N_HEADS, 128, tq), BF16), pltpu.VMEM((N_HEADS, ATT_TILE, tq), F32),
            pltpu.VMEM((N_HEADS, tq), F32), pltpu.VMEM((N_HEADS, tq), F32),
            pltpu.VMEM((WIDTH, tq), F32)]


def _store_out(o_ref, l_ref, acc_ref):
    outs = [acc_ref[h * HEAD_DIM:(h + 1) * HEAD_DIM, :] / l_ref[h:h + 1, :] for h in range(N_HEADS)]
    o_ref[...] = jnp.concatenate(outs, axis=0).T.astype(o_ref.dtype)


def _dsa_kernel(qT_ref, k_ref, vT_ref, mask_ref, bias_ref, o_ref, qsel_ref, s_ref, m_ref, l_ref,
                acc_ref, *, tq, qpos0):
    tk = ATT_TILE
    jd = lax.shift_right_logical(qpos0 + pl.program_id(1) * tq, int(math.log2(tk)))
    _load_qsel(qT_ref, qsel_ref, tq)
    _init_state(m_ref, l_ref, acc_ref)

    def k_slab(j):
        r0 = pl.multiple_of(j * tk, tk)
        return lambda g: k_ref[pl.ds(r0, tk), g * 128:(g + 1) * 128]

    def tile(j, near, last=False):
        r0 = pl.multiple_of(j * tk, tk)
        unselected = jnp.where(mask_ref[pl.ds(r0, tk), :].astype(I32) != 0, 0.0, NEG_INF)
        addend = ((lambda h: (unselected,)) if near is None
                  else (lambda h: (unselected, bias_ref[near, h])))
        _heads_tile(lambda h: vT_ref[j, h * HEAD_DIM:(h + 1) * HEAD_DIM, :],
                    None if last else k_slab(j + 1),
                    qsel_ref, s_ref, m_ref, l_ref, acc_ref, addend)

    for h in range(N_HEADS):
        _scores(k_slab(0), qsel_ref, s_ref, h)

    def far(j, carry):
        tile(j, None)
        return carry

    lax.fori_loop(0, jnp.maximum(jd - 1, 0), far, 0)

    @pl.when(jd >= 1)
    def _():
        tile(jd - 1, 0)

    tile(jd, 1, last=True)
    _store_out(o_ref, l_ref, acc_ref)


def _dsa_attn(qT, k, vT, maskT, bias, *, nb, tq, qpos0):
    tkp = k.shape[1]
    nkt = tkp // ATT_TILE
    nq = qT.shape[1] // nb // tq
    return pl.pallas_call(
        functools.partial(_dsa_kernel, tq=tq, qpos0=qpos0),
        out_shape=jax.ShapeDtypeStruct((nb * nq * tq, WIDTH), BF16),
        grid=(nb, nq),
        in_specs=[pl.BlockSpec((WIDTH, tq), lambda b, i: (0, b * nq + i)),
                  pl.BlockSpec((None, tkp, WIDTH), lambda b, i: (b, 0, 0),
                               pipeline_mode=pl.Buffered(1)),
                  pl.BlockSpec((nkt, WIDTH, ATT_TILE), lambda b, i: (b, 0, 0),
                               pipeline_mode=pl.Buffered(1)),
                  pl.BlockSpec((None, tkp, tq), lambda b, i: (b, 0, i)),
                  _const_spec(bias.shape)],
        out_specs=pl.BlockSpec((tq, WIDTH), lambda b, i: (b * nq + i, 0)),
        scratch_shapes=_attn_scratch(tq),
        compiler_params=_cparams(("parallel", "parallel")),
        name="dsa_attn",
    )(qT, k, vT, maskT, bias)


_BAND_TILES = BAND_CHUNKS * CHUNK // ATT_TILE + 1


def _band_kernel(qT_ref, *refs, tq, off):
    k_refs = refs[:_BAND_TILES]
    v_refs = refs[_BAND_TILES:2 * _BAND_TILES]
    bias_ref, o_ref, qsel_ref, s_ref, m_ref, l_ref, acc_ref = refs[2 * _BAND_TILES:]
    _load_qsel(qT_ref, qsel_ref, tq)
    _init_state(m_ref, l_ref, acc_ref)
    k_slab = lambda w: (lambda g: k_refs[w][:, g * 128:(g + 1) * 128])
    for h in range(N_HEADS):
        _scores(k_slab(0), qsel_ref, s_ref, h)
    for w in range(_BAND_TILES):
        entry = jnp.where(pl.program_id(1) + off + w >= 0, w, _BAND_TILES)
        _heads_tile(lambda h: v_refs[w][0, h * HEAD_DIM:(h + 1) * HEAD_DIM, :],
                    None if w == _BAND_TILES - 1 else k_slab(w + 1),
                    qsel_ref, s_ref, m_ref, l_ref, acc_ref, lambda h: (bias_ref[entry, h],))
    _store_out(o_ref, l_ref, acc_ref)


def _band_attn(qT, k, vT, bias, *, nb, tq, off):
    nkt = k.shape[1] // ATT_TILE
    nq = qT.shape[1] // nb // tq
    tile_of = lambda i, w: jnp.maximum(i + off + w, 0)
    k_specs = [pl.BlockSpec((None, ATT_TILE, WIDTH), lambda b, i, w=w: (b, tile_of(i, w), 0))
               for w in range(_BAND_TILES)]
    v_specs = [pl.BlockSpec((1, WIDTH, ATT_TILE), lambda b, i, w=w: (b * nkt + tile_of(i, w), 0, 0))
               for w in range(_BAND_TILES)]
    return pl.pallas_call(
        functools.partial(_band_kernel, tq=tq, off=off),
        out_shape=jax.ShapeDtypeStruct((nb * nq * tq, WIDTH), BF16),
        grid=(nb, nq),
        in_specs=[pl.BlockSpec((WIDTH, tq), lambda b, i: (0, b * nq + i))] + k_specs + v_specs
                 + [_const_spec(bias.shape)],
        out_specs=pl.BlockSpec((tq, WIDTH), lambda b, i: (b * nq + i, 0)),
        scratch_shapes=_attn_scratch(tq),
        compiler_params=_cparams(("parallel", "parallel")),
        name="band_attn",
    )(qT, *([k] * _BAND_TILES), *([vT] * _BAND_TILES), bias)


def _mix_ln_kernel(h_ref, oa_ref, ob_ref, wg_ref, wba_ref, wbb_ref, wo_ref, g_ref, b_ref, o_ref,
                   *, alpha):
    h = h_ref[...]
    hb = h.astype(BF16)
    d = h.shape[1]
    ga = jnp.dot(hb, wg_ref[:, :d], preferred_element_type=F32)
    gb = jnp.dot(hb, wg_ref[:, d:], preferred_element_type=F32)
    ya = jnp.dot(oa_ref[...], wba_ref[...], preferred_element_type=F32)
    yb = jnp.dot(ob_ref[...], wbb_ref[...], preferred_element_type=F32)
    gated = jax.nn.sigmoid(ga) * ya + jax.nn.sigmoid(gb) * yb
    mix = jnp.dot(gated.astype(BF16), wo_ref[...], preferred_element_type=F32)
    o_ref[...] = _layer_norm_rows(alpha * h + mix, g_ref[...], b_ref[...])


def _mix_ln(h, oa, ob, wg, wba, wbb, wo, g, b, alpha):
    n, d = h.shape
    tm = TOKEN_TILE
    row = lambda width: pl.BlockSpec((tm, width), lambda i: (i, 0))
    return pl.pallas_call(
        functools.partial(_mix_ln_kernel, alpha=alpha),
        out_shape=jax.ShapeDtypeStruct((n, d), F32),
        grid=(n // tm,),
        in_specs=[row(d), row(WIDTH), row(WIDTH), _const_spec(wg.shape), _const_spec(wba.shape),
                  _const_spec(wbb.shape), _const_spec(wo.shape), _const_spec((1, d)),
                  _const_spec((1, d))],
        out_specs=row(d),
        compiler_params=_cparams(("parallel",)),
        name="mix_ln",
    )(h, oa, ob, wg, wba, wbb, wo, g.reshape(1, d), b.reshape(1, d))


def _t5_bucket(rel):
    half = T5_BUCKETS // 2
    exact = half // 2
    n = jnp.abs(rel)
    log_ratio = jnp.log(jnp.maximum(n, 1).astype(F32) / exact) / math.log(T5_MAX_DIST / exact)
    large = jnp.minimum(exact + (log_ratio * (half - exact)).astype(I32), half - 1)
    return (rel > 0).astype(I32) * half + jnp.where(n < exact, n, large)


def _rel_line(n_tiles, first, tq):
    period = ATT_TILE + tq
    y = jnp.arange(period, dtype=I32)[None, :]
    c = (jnp.arange(n_tiles, dtype=I32)[:, None] + first) * ATT_TILE
    return jnp.where(y < tq, c - y, c + period - y)


def _toeplitz(v, tq):
    period = v.shape[-1]
    flat = jnp.tile(v, (1,) * (v.ndim - 1) + (ATT_TILE,))[..., :ATT_TILE * (period - 1)]
    return flat.reshape(v.shape[:-1] + (ATT_TILE, period - 1))[..., :tq]


def _dsa_bias_table(t5_bias, tq):
    assert T5_MAX_DIST <= ATT_TILE + 1
    far = t5_bias[:, _t5_bucket(jnp.int32(-(ATT_TILE + 1)))]
    line = t5_bias[:, _t5_bucket(_rel_line(2, -1, tq))] - far[:, None, None]
    return _toeplitz(jnp.moveaxis(line, 0, 1), tq)


def _band_bias_table(rel_bias, tq):
    first = 1 - _BAND_TILES
    ridx = jnp.clip(_rel_line(_BAND_TILES, first, tq), -REL_BACK, CHUNK - 1) + REL_BACK
    bias = _toeplitz(jnp.moveaxis(rel_bias[:, ridx], 0, 1), tq)
    w = jnp.arange(_BAND_TILES, dtype=I32)[:, None, None]
    j = jnp.arange(ATT_TILE, dtype=I32)[None, :, None]
    i = jnp.arange(tq, dtype=I32)[None, None, :]
    dchunk = ((w + first) * ATT_TILE + j) // CHUNK - i // CHUNK
    ok = (dchunk <= 0) & (dchunk >= -BAND_CHUNKS)
    table = jnp.where(ok[:, None], bias, NEG_INF)
    return jnp.concatenate([table, jnp.full_like(table[:1], NEG_INF)], axis=0)


def _regroup_w_in(w_in):
    sizes = (WIDTH, WIDTH, WIDTH, N_HEADS_IDX * IDX_DIM, IDX_DIM, N_HEADS_IDX,
             WIDTH, WIDTH, WIDTH, w_in.shape[0], w_in.shape[0])
    offs = [0]
    for s in sizes:
        offs.append(offs[-1] + s)
    qa, ka, va, qi, ki, wi, qb, kb, vb, ga, gb = (w_in[:, offs[t]:offs[t + 1]] for t in range(11))
    wi_pad = jnp.pad(wi, ((0, 0), (0, 128 - N_HEADS_IDX)))
    w_proj = jnp.concatenate([qa, ka, va, qi, ki, ki, wi_pad, qb, kb, vb], axis=1).astype(BF16)
    w_gate = jnp.concatenate([ga, gb], axis=1).astype(BF16)
    return w_proj, w_gate


def _key_tiles_T(v):
    nb, tkp, wd = v.shape
    return v.reshape(nb, tkp // ATT_TILE, ATT_TILE, wd).transpose(0, 1, 3, 2).reshape(-1, wd, ATT_TILE)


def _pad_keys(x, tkp):
    return jnp.pad(x, ((0, 0), (0, tkp - x.shape[1]), (0, 0)))


def kernel(x_prompt, x_sample, cache_k_a, cache_v_a, cache_kidx_a, cache_k_b, cache_v_b, t5_bias,
           ln1_g, ln1_b, ffn1_wi, ffn1_wo, ln2_g, ln2_b, w_in, rel_bias_b, w_branch_a, w_branch_b,
           w_out, ln3_g, ln3_b, ffn2_wi, ffn2_wo):
    depth = ln1_g.shape[0]
    alpha = (2.0 * depth) ** 0.25
    nbp, seq, d = x_prompt.shape
    nbs, dec, _ = x_sample.shape
    past = cache_k_a.shape[2]
    band = cache_k_b.shape[2]
    keep = min(BAND_CHUNKS * CHUNK, seq)
    tq_p, tq_s = ATT_TILE, 128
    assert seq % TOKEN_TILE == 0 and keep == TOKEN_TILE and dec <= tq_s and past % ATT_TILE == 0
    assert band == BAND_CHUNKS * CHUNK and (nbs * tq_s) % TOKEN_TILE == 0

    yp = x_prompt.reshape(nbp * seq, d)
    ys = jnp.pad(x_sample, ((0, 0), (0, tq_s - dec), (0, 0))).reshape(nbs * tq_s, d)
    dsa_bias_p = _dsa_bias_table(t5_bias, tq_p)
    dsa_bias_s = dsa_bias_p[..., :tq_s]
    n_sel_p = min(TOPK_MAX, seq // 4)
    n_sel_s = min(TOPK_MAX, (past + dec) // 4)
    tk_s = -(-(past + tq_s) // ATT_TILE) * ATT_TILE
    tkb_s = _BAND_TILES * ATT_TILE

    st_p, st_s = [], []
    for l in range(depth):
        w_proj, w_gate = _regroup_w_in(w_in[l])
        wi1, wo1 = ffn1_wi[l].astype(BF16), ffn1_wo[l].astype(BF16)
        wi2, wo2 = ffn2_wi[l].astype(BF16), ffn2_wo[l].astype(BF16)
        wba, wbb, wo = (w_branch_a[l].astype(BF16), w_branch_b[l].astype(BF16),
                        w_out[l].astype(BF16))
        band_bias_p = _band_bias_table(rel_bias_b[l], tq_p)
        band_bias_s = band_bias_p[..., :tq_s]

        h = _ffn_ln(yp, wi1, wo1, ln1_g[l], ln1_b[l], alpha)
        (qaT, ka, kab, va, vaT, qiT, ki, kk, wiT, qbT, kbb, vbT, kbl, vbl) = _in_proj(
            h, w_proj, seq // TOKEN_TILE)
        maskT = _idx_mask(qiT, wiT, kk.reshape(nbp, seq, 128), nb=nbp, tq=tq_p, qpos0=0,
                          n_valid=seq, n_sel=n_sel_p)
        oa = _dsa_attn(qaT, kab.reshape(nbp, seq, WIDTH), vaT, maskT, dsa_bias_p,
                       nb=nbp, tq=tq_p, qpos0=0)
        ob = _band_attn(qbT, kbb.reshape(nbp, seq, WIDTH), vbT, band_bias_p,
                        nb=nbp, tq=tq_p, off=1 - _BAND_TILES)
        h2 = _mix_ln(h, oa, ob, w_gate, wba, wbb, wo, ln2_g[l], ln2_b[l], alpha)
        yp = _ffn_ln(h2, wi2, wo2, ln3_g[l], ln3_b[l], alpha)
        st_p.append((ka.reshape(nbp, seq, N_HEADS, HEAD_DIM), va.reshape(nbp, seq, N_HEADS, HEAD_DIM),
                     ki.reshape(nbp, seq, IDX_DIM), kbl.reshape(nbp, keep, N_HEADS, HEAD_DIM),
                     vbl.reshape(nbp, keep, N_HEADS, HEAD_DIM)))

        h = _ffn_ln(ys, wi1, wo1, ln1_g[l], ln1_b[l], alpha)
        (qaT, ka, _, va, _, qiT, ki, _, wiT, qbT, _, _, kbl, vbl) = _in_proj(h, w_proj, 1)
        new = lambda a: a.reshape(nbs, tq_s, -1)[:, :dec]
        ka_n, va_n, ki_n, kb_n, vb_n = new(ka), new(va), new(ki), new(kbl), new(vbl)
        ka_all = jnp.concatenate([cache_k_a[l].reshape(nbs, past, WIDTH), ka_n], axis=1)
        va_all = jnp.concatenate([cache_v_a[l].reshape(nbs, past, WIDTH), va_n], axis=1)
        ki_all = jnp.concatenate([cache_kidx_a[l], ki_n], axis=1)
        kb_all = jnp.concatenate([cache_k_b[l].reshape(nbs, band, WIDTH), kb_n], axis=1)
        vb_all = jnp.concatenate([cache_v_b[l].reshape(nbs, band, WIDTH), vb_n], axis=1)
        kk_s = _pad_keys(jnp.concatenate([ki_all, ki_all], axis=2).astype(BF16), tk_s)
        maskT = _idx_mask(qiT, wiT, kk_s, nb=nbs, tq=tq_s, qpos0=past, n_valid=past + dec,
                          n_sel=n_sel_s)
        oa = _dsa_attn(qaT, _pad_keys(ka_all.astype(BF16), tk_s),
                       _key_tiles_T(_pad_keys(va_all.astype(BF16), tk_s)), maskT, dsa_bias_s,
                       nb=nbs, tq=tq_s, qpos0=past)
        ob = _band_attn(qbT, _pad_keys(kb_all.astype(BF16), tkb_s),
                        _key_tiles_T(_pad_keys(vb_all.astype(BF16), tkb_s)), band_bias_s,
                        nb=nbs, tq=tq_s, off=0)
        h2 = _mix_ln(h, oa, ob, w_gate, wba, wbb, wo, ln2_g[l], ln2_b[l], alpha)
        ys = _ffn_ln(h2, wi2, wo2, ln3_g[l], ln3_b[l], alpha)
        heads = lambda a: a.reshape(nbs, -1, N_HEADS, HEAD_DIM)
        st_s.append((heads(ka_n), heads(va_n), ki_n, heads(kb_all[:, -band:]),
                     heads(vb_all[:, -band:])))

    y_prompt = yp.reshape(nbp, seq, d)
    y_sample = ys.reshape(nbs, tq_s, d)[:, :dec]
    stack = lambda sts, t: jnp.stack([s[t] for s in sts])
    return (y_prompt, y_sample,
            stack(st_p, 0), stack(st_p, 1), stack(st_p, 2), stack(st_p, 3), stack(st_p, 4),
            stack(st_s, 0), stack(st_s, 1), stack(st_s, 2), stack(st_s, 3), stack(st_s, 4))
```

```python
import functools
import math

import jax
import jax.numpy as jnp
from jax import lax
from jax.experimental import pallas as pl
from jax.experimental.pallas import tpu as pltpu

F32 = jnp.float32
BF16 = jnp.bfloat16
I32 = jnp.int32
I16 = jnp.int16

CHUNK = 64
CHUNK_SHIFT = 6
HEAD_DIM = 64
N_HEADS = 8
WIDTH = N_HEADS * HEAD_DIM
N_HEADS_IDX = 4
IDX_DIM = 64
TOPK_MAX = 256
IDX_SCALE = (IDX_DIM * N_HEADS_IDX) ** -0.5
BAND_CHUNKS = 8
REL_BACK = 128
T5_BUCKETS = 32
T5_MAX_DIST = 128
LN_EPS = 1e-5
NEG_INF = -1e30
INT_MIN = -2 ** 31

ATT_TILE = 256
TOKEN_TILE = 512
FF_CHUNK = 256
VMEM_LIMIT_MB = 56


def _cparams(sem):
    return pltpu.CompilerParams(dimension_semantics=sem,
                                vmem_limit_bytes=VMEM_LIMIT_MB * 1024 * 1024)


def _const_spec(shape):
    nd = len(shape)
    return pl.BlockSpec(shape, lambda *_: (0,) * nd, pipeline_mode=pl.Buffered(1))


def _tree_sum(xs):
    xs = list(xs)
    while len(xs) > 1:
        nxt = [xs[a] + xs[a + 1] for a in range(0, len(xs) - 1, 2)]
        if len(xs) % 2:
            nxt.append(xs[-1])
        xs = nxt
    return xs[0]


def _layer_norm_rows(y, g, b):
    mu = jnp.mean(y, axis=-1, keepdims=True)
    d = y - mu
    var = jnp.mean(d * d, axis=-1, keepdims=True)
    return d * lax.rsqrt(var + LN_EPS) * g + b


def _ffn_ln_kernel(x_ref, wi_ref, wo_ref, g_ref, b_ref, o_ref, *, alpha, dff):
    x = x_ref[...]
    xb = x.astype(BF16)
    acc = jnp.zeros(x.shape, F32)
    for c in range(dff // FF_CHUNK):
        lo = c * FF_CHUNK
        a = jnp.dot(xb, wi_ref[:, lo:lo + FF_CHUNK], preferred_element_type=F32)
        u = jnp.dot(xb, wi_ref[:, dff + lo:dff + lo + FF_CHUNK], preferred_element_type=F32)
        hm = (a * jax.nn.sigmoid(a)) * u
        acc = acc + jnp.dot(hm.astype(BF16), wo_ref[lo:lo + FF_CHUNK, :],
                            preferred_element_type=F32)
    y = alpha * x + 0.5 * acc
    o_ref[...] = _layer_norm_rows(y, g_ref[...], b_ref[...])


def _ffn_ln(x, wi, wo, g, b, alpha):
    n, d = x.shape
    dff = wo.shape[0]
    tm = TOKEN_TILE
    return pl.pallas_call(
        functools.partial(_ffn_ln_kernel, alpha=alpha, dff=dff),
        out_shape=jax.ShapeDtypeStruct((n, d), F32),
        grid=(n // tm,),
        in_specs=[pl.BlockSpec((tm, d), lambda i: (i, 0)),
                  _const_spec((d, 2 * dff)), _const_spec((dff, d)),
                  _const_spec((1, d)), _const_spec((1, d))],
        out_specs=pl.BlockSpec((tm, d), lambda i: (i, 0)),
        compiler_params=_cparams(("parallel",)),
        name="ffn_ln",
    )(x, wi, wo, g.reshape(1, d), b.reshape(1, d))


_C_QA, _C_KA, _C_VA, _C_QI, _C_KI, _C_WI, _C_QB, _C_KB, _C_VB, _C_END = (
    0, 512, 1024, 1536, 1792, 1920, 2048, 2560, 3072, 3584)


def _inproj_kernel(h_ref, w_ref, qaT_ref, ka_ref, kab_ref, va_ref, vaT_ref, qiT_ref, ki_ref,
                   kk_ref, wiT_ref, qbT_ref, kbb_ref, vbT_ref, kbl_ref, vbl_ref, *, per):
    hb = h_ref[...].astype(BF16)

    def proj(c0, c1):
        return jnp.dot(hb, w_ref[:, c0:c1], preferred_element_type=F32)

    def put_tiles(ref, zt):
        for c in range(ref.shape[0]):
            ref[c] = zt[:, c * ATT_TILE:(c + 1) * ATT_TILE].astype(ref.dtype)

    scale = HEAD_DIM ** -0.5
    qaT_ref[...] = (proj(_C_QA, _C_KA) * scale).T.astype(BF16)
    ka = proj(_C_KA, _C_VA)
    ka_ref[...] = ka
    kab_ref[...] = ka.astype(BF16)
    va = proj(_C_VA, _C_QI)
    va_ref[...] = va
    put_tiles(vaT_ref, va.T)
    qiT_ref[...] = proj(_C_QI, _C_KI).T.astype(BF16)
    kk = proj(_C_KI, _C_WI)
    ki_ref[...] = kk[:, :IDX_DIM]
    kk_ref[...] = kk.astype(BF16)
    wiT_ref[...] = proj(_C_WI, _C_QB).T[:8, :]
    qbT_ref[...] = (proj(_C_QB, _C_KB) * scale).T.astype(BF16)
    kb = proj(_C_KB, _C_VB)
    kbb_ref[...] = kb.astype(BF16)
    vb = proj(_C_VB, _C_END)
    put_tiles(vbT_ref, vb.T)

    @pl.when(pl.program_id(0) % per == per - 1)
    def _():
        kbl_ref[...] = kb
        vbl_ref[...] = vb


def _in_proj(h, w, per):
    n, d = h.shape
    tm = TOKEN_TILE
    nt = n // tm
    sub = tm // ATT_TILE
    row = lambda width: pl.BlockSpec((tm, width), lambda i: (i, 0))
    col = lambda rows: pl.BlockSpec((rows, tm), lambda i: (0, i))
    til = pl.BlockSpec((sub, WIDTH, ATT_TILE), lambda i: (i, 0, 0))
    last = pl.BlockSpec((tm, WIDTH), lambda i: (i // per, 0))
    sds = jax.ShapeDtypeStruct
    out_shape = (
        sds((WIDTH, n), BF16),
        sds((n, WIDTH), F32), sds((n, WIDTH), BF16),
        sds((n, WIDTH), F32), sds((n // ATT_TILE, WIDTH, ATT_TILE), BF16),
        sds((N_HEADS_IDX * IDX_DIM, n), BF16),
        sds((n, IDX_DIM), F32), sds((n, 128), BF16),
        sds((8, n), F32),
        sds((WIDTH, n), BF16),
        sds((n, WIDTH), BF16),
        sds((n // ATT_TILE, WIDTH, ATT_TILE), BF16),
        sds((nt // per * tm, WIDTH), F32), sds((nt // per * tm, WIDTH), F32),
    )
    out_specs = (col(WIDTH), row(WIDTH), row(WIDTH), row(WIDTH), til,
                 col(N_HEADS_IDX * IDX_DIM), row(IDX_DIM), row(128), col(8),
                 col(WIDTH), row(WIDTH), til, last, last)
    return pl.pallas_call(
        functools.partial(_inproj_kernel, per=per),
        out_shape=out_shape,
        grid=(nt,),
        in_specs=[pl.BlockSpec((tm, d), lambda i: (i, 0)), _const_spec((d, _C_END))],
        out_specs=out_specs,
        compiler_params=_cparams(("arbitrary",)),
        name="in_proj",
    )(h, w)


def _flip(bits):
    return bits ^ (lax.shift_right_arithmetic(bits, 31) & 0x7FFFFFFF)


def _idx_mask_kernel(qiT_ref, wiT_ref, kk_ref, mask_ref, st_ref, hi_ref, lo_ref, *, tq, qpos0,
                     n_valid, n_sel):
    tk = ATT_TILE
    nkt_total = st_ref.shape[0] // tk
    q0 = qpos0 + pl.program_id(1) * tq
    n_keys = jnp.minimum(q0 + tq, n_valid)
    nt = jnp.minimum(lax.shift_right_logical(n_keys + tk - 1, int(math.log2(tk))), nkt_total)

    lane = lax.broadcasted_iota(I32, (8, tq), 1)
    q_chunk = lax.shift_right_logical(q0 + lane, CHUNK_SHIFT)
    n_adm = jnp.minimum((q_chunk + 1) * CHUNK, n_valid)
    k_row = jnp.minimum(n_adm, n_sel)

    zeros = jnp.zeros((128 - IDX_DIM, tq), BF16)
    qh = [jnp.concatenate([qiT_ref[h * IDX_DIM:(h + 1) * IDX_DIM, :], zeros], axis=0)
          for h in range(N_HEADS_IDX)]
    wh = [wiT_ref[h:h + 1, :] for h in range(N_HEADS_IDX)]

    def score_tile(j, carry):
        r0 = pl.multiple_of(j * tk, tk)
        kt = kk_ref[pl.ds(r0, tk), :]
        sc = None
        for h in range(N_HEADS_IDX):
            d = jnp.dot(kt, qh[h], preferred_element_type=F32)
            term = wh[h] * jnp.maximum(d, 0.0)
            sc = term if sc is None else sc + term
        sc = sc * IDX_SCALE
        bits = lax.bitcast_convert_type(sc, I32)
        bits = jnp.where(bits == INT_MIN, 0, bits)
        key = _flip(bits)
        kpos = r0 + lax.broadcasted_iota(I32, (tk, tq), 0)
        key = jnp.where(kpos < n_adm[0:1, :], key, INT_MIN)
        st_ref[pl.ds(r0, tk), :] = key
        hi_ref[pl.ds(r0, tk), :] = lax.shift_right_arithmetic(key, 16).astype(I16)
        lo_ref[pl.ds(r0, tk), :] = ((key & 0xFFFF) - 2 ** 15).astype(I16)
        return carry

    lax.fori_loop(0, nt, score_tile, 0)

    def count16(ref, pred, thr8):
        thr = jnp.concatenate([thr8, thr8], axis=0).astype(I16)

        def body(j, acc):
            r0 = pl.multiple_of(j * tk, tk)
            blk = ref[pl.ds(r0, tk), :]
            parts = [jnp.where(pred(blk[r * 16:(r + 1) * 16, :], thr), jnp.int16(1), jnp.int16(0))
                     for r in range(tk // 16)]
            return acc + _tree_sum(parts)

        acc = lax.fori_loop(0, nt, body, jnp.zeros((16, tq), I16))
        return jnp.broadcast_to(jnp.sum(acc.astype(I32), axis=0, keepdims=True), (8, tq))

    ge = lambda a, b: a >= b
    gt = lambda a, b: a > b

    def kth_largest16(ref, want):
        def step(k, t8):
            cand = t8 + lax.shift_left(jnp.int32(1), 15 - k)
            return jnp.where(count16(ref, ge, cand) >= want, cand, t8)
        return lax.fori_loop(0, 16, step, jnp.full((8, tq), -2 ** 15, I32))

    t_hi = kth_largest16(hi_ref, k_row)
    n_above = count16(hi_ref, gt, t_hi)
    t_hi16 = jnp.concatenate([t_hi, t_hi], axis=0).astype(I16)

    def park(j, carry):
        r0 = pl.multiple_of(j * tk, tk)
        for r in range(tk // 16):
            rows = pl.ds(r0 + r * 16, 16)
            lo_ref[rows, :] = jnp.where(hi_ref[rows, :] == t_hi16, lo_ref[rows, :],
                                        jnp.int16(-2 ** 15))
        return carry

    lax.fori_loop(0, nt, park, 0)
    t_lo = kth_largest16(lo_ref, k_row - n_above)
    n_gt = n_above + count16(lo_ref, gt, t_lo)
    t_row = (t_hi * 65536 + (t_lo + 2 ** 15))[0:1, :]
    ties_wanted = (k_row - n_gt)[0:1, :].astype(F32)

    tri = jnp.where(lax.broadcasted_iota(I32, (tk, tk), 0) >= lax.broadcasted_iota(I32, (tk, tk), 1),
                    1.0, 0.0).astype(BF16)

    def final_tile(j, seen):
        r0 = pl.multiple_of(j * tk, tk)
        blk = st_ref[pl.ds(r0, tk), :]
        eq = jnp.where(blk == t_row, 1.0, 0.0)
        incl = jnp.dot(tri, eq.astype(BF16), preferred_element_type=F32)
        rank = seen + incl - eq
        take_tie = jnp.where(rank < ties_wanted, eq, 0.0)
        sel = jnp.where(blk > t_row, 1.0, take_tie)
        mask_ref[pl.ds(r0, tk), :] = sel.astype(jnp.int8)
        return seen + incl[tk - 1:tk, :]

    lax.fori_loop(0, nt, final_tile, jnp.zeros((1, tq), F32))

    def zero_tile(j, carry):
        r0 = pl.multiple_of(j * tk, tk)
        mask_ref[pl.ds(r0, tk), :] = jnp.zeros((tk, tq), jnp.int8)
        return carry

    lax.fori_loop(nt, nkt_total, zero_tile, 0)


def _idx_mask(qiT, wiT, kk, *, nb, tq, qpos0, n_valid, n_sel):
    tkp = kk.shape[1]
    nq = qiT.shape[1] // nb // tq
    return pl.pallas_call(
        functools.partial(_idx_mask_kernel, tq=tq, qpos0=qpos0, n_valid=n_valid, n_sel=n_sel),
        out_shape=jax.ShapeDtypeStruct((nb, tkp, nq * tq), jnp.int8),
        grid=(nb, nq),
        in_specs=[pl.BlockSpec((N_HEADS_IDX * IDX_DIM, tq), lambda b, i: (0, b * nq + i)),
                  pl.BlockSpec((8, tq), lambda b, i: (0, b * nq + i)),
                  pl.BlockSpec((None, tkp, 128), lambda b, i: (b, 0, 0))],
        out_specs=pl.BlockSpec((None, tkp, tq), lambda b, i: (b, 0, i)),
        scratch_shapes=[pltpu.VMEM((tkp, tq), I32), pltpu.VMEM((tkp, tq), I16),
                        pltpu.VMEM((tkp, tq), I16)],
        compiler_params=_cparams(("parallel", "parallel")),
        name="idx_mask",
    )(qiT, wiT, kk)


def _load_qsel(qT_ref, qsel_ref, tq):
    zeros = jnp.zeros((HEAD_DIM, tq), BF16)
    for h in range(N_HEADS):
        blk = qT_ref[h * HEAD_DIM:(h + 1) * HEAD_DIM, :]
        qsel_ref[h] = jnp.concatenate([blk, zeros] if h % 2 == 0 else [zeros, blk], axis=0)


def _init_state(m_ref, l_ref, acc_ref):
    m_ref[...] = jnp.full(m_ref.shape, NEG_INF, F32)
    l_ref[...] = jnp.zeros(l_ref.shape, F32)
    acc_ref[...] = jnp.zeros(acc_ref.shape, F32)


def _scores(k_slab, qsel_ref, s_ref, h):
    s_ref[h] = jnp.dot(k_slab(h // 2), qsel_ref[h], preferred_element_type=F32)


def _heads_tile(v_rows, next_k_slab, qsel_ref, s_ref, m_ref, l_ref, acc_ref, addend):
    for h in range(N_HEADS):
        s = s_ref[h]
        for a in addend(h):
            s = s + a
        if next_k_slab is not None:
            _scores(next_k_slab, qsel_ref, s_ref, h)
        m_old = m_ref[h:h + 1, :]
        m_new = jnp.maximum(m_old, jnp.max(s, axis=0, keepdims=True))
        alpha = jnp.exp(m_old - m_new)
        p = jnp.exp(s - m_new)
        l_ref[h:h + 1, :] = alpha * l_ref[h:h + 1, :] + jnp.sum(p, axis=0, keepdims=True)
        m_ref[h:h + 1, :] = m_new
        pv = jnp.dot(v_rows(h), p.astype(BF16), preferred_element_type=F32)
        rows = slice(h * HEAD_DIM, (h + 1) * HEAD_DIM)
        acc_ref[rows, :] = alpha * acc_ref[rows, :] + pv


def _attn_scratch(tq):
    return [pltpu.VMEM((N_HEADS, 128, tq), BF16), pltpu.VMEM((N_HEADS, ATT_TILE, tq), F32),
            pltpu.VMEM((N_HEADS, tq), F32), pltpu.VMEM((N_HEADS, tq), F32),
            pltpu.VMEM((WIDTH, tq), F32)]


def _store_out(o_ref, l_ref, acc_ref):
    outs = [acc_ref[h * HEAD_DIM:(h + 1) * HEAD_DIM, :] / l_ref[h:h + 1, :] for h in range(N_HEADS)]
    o_ref[...] = jnp.concatenate(outs, axis=0).T.astype(o_ref.dtype)


def _dsa_kernel(qT_ref, k_ref, vT_ref, mask_ref, bias_ref, o_ref, qsel_ref, s_ref, m_ref, l_ref,
                acc_ref, *, tq, qpos0):
    tk = ATT_TILE
    jd = lax.shift_right_logical(qpos0 + pl.program_id(1) * tq, int(math.log2(tk)))
    _load_qsel(qT_ref, qsel_ref, tq)
    _init_state(m_ref, l_ref, acc_ref)

    def k_slab(j):
        r0 = pl.multiple_of(j * tk, tk)
        return lambda g: k_ref[pl.ds(r0, tk), g * 128:(g + 1) * 128]

    def tile(j, near, last=False):
        r0 = pl.multiple_of(j * tk, tk)
        unselected = jnp.where(mask_ref[pl.ds(r0, tk), :].astype(I32) != 0, 0.0, NEG_INF)
        addend = ((lambda h: (unselected,)) if near is None
                  else (lambda h: (unselected, bias_ref[near, h])))
        _heads_tile(lambda h: vT_ref[j, h * HEAD_DIM:(h + 1) * HEAD_DIM, :],
                    None if last else k_slab(j + 1),
                    qsel_ref, s_ref, m_ref, l_ref, acc_ref, addend)

    for h in range(N_HEADS):
        _scores(k_slab(0), qsel_ref, s_ref, h)

    def far(j, carry):
        tile(j, None)
        return carry

    lax.fori_loop(0, jnp.maximum(jd - 1, 0), far, 0)

    @pl.when(jd >= 1)
    def _():
        tile(jd - 1, 0)

    tile(jd, 1, last=True)
    _store_out(o_ref, l_ref, acc_ref)


def _dsa_attn(qT, k, vT, maskT, bias, *, nb, tq, qpos0):
    tkp = k.shape[1]
    nkt = tkp // ATT_TILE
    nq = qT.shape[1] // nb // tq
    return pl.pallas_call(
        functools.partial(_dsa_kernel, tq=tq, qpos0=qpos0),
        out_shape=jax.ShapeDtypeStruct((nb * nq * tq, WIDTH), BF16),
        grid=(nb, nq),
        in_specs=[pl.BlockSpec((WIDTH, tq), lambda b, i: (0, b * nq + i)),
                  pl.BlockSpec((None, tkp, WIDTH), lambda b, i: (b, 0, 0),
                               pipeline_mode=pl.Buffered(1)),
                  pl.BlockSpec((nkt, WIDTH, ATT_TILE), lambda b, i: (b, 0, 0),
                               pipeline_mode=pl.Buffered(1)),
                  pl.BlockSpec((None, tkp, tq), lambda b, i: (b, 0, i)),
                  _const_spec(bias.shape)],
        out_specs=pl.BlockSpec((tq, WIDTH), lambda b, i: (b * nq + i, 0)),
        scratch_shapes=_attn_scratch(tq),
        compiler_params=_cparams(("parallel", "parallel")),
        name="dsa_attn",
    )(qT, k, vT, maskT, bias)


_BAND_TILES = BAND_CHUNKS * CHUNK // ATT_TILE + 1


def _band_kernel(qT_ref, *refs, tq, off):
    k_refs = refs[:_BAND_TILES]
    v_refs = refs[_BAND_TILES:2 * _BAND_TILES]
    bias_ref, o_ref, qsel_ref, s_ref, m_ref, l_ref, acc_ref = refs[2 * _BAND_TILES:]
    _load_qsel(qT_ref, qsel_ref, tq)
    _init_state(m_ref, l_ref, acc_ref)
    k_slab = lambda w: (lambda g: k_refs[w][:, g * 128:(g + 1) * 128])
    for h in range(N_HEADS):
        _scores(k_slab(0), qsel_ref, s_ref, h)
    for w in range(_BAND_TILES):
        entry = jnp.where(pl.program_id(1) + off + w >= 0, w, _BAND_TILES)
        _heads_tile(lambda h: v_refs[w][0, h * HEAD_DIM:(h + 1) * HEAD_DIM, :],
                    None if w == _BAND_TILES - 1 else k_slab(w + 1),
                    qsel_ref, s_ref, m_ref, l_ref, acc_ref, lambda h: (bias_ref[entry, h],))
    _store_out(o_ref, l_ref, acc_ref)


def _band_attn(qT, k, vT, bias, *, nb, tq, off):
    nkt = k.shape[1] // ATT_TILE
    nq = qT.shape[1] // nb // tq
    tile_of = lambda i, w: jnp.maximum(i + off + w, 0)
    k_specs = [pl.BlockSpec((None, ATT_TILE, WIDTH), lambda b, i, w=w: (b, tile_of(i, w), 0))
               for w in range(_BAND_TILES)]
    v_specs = [pl.BlockSpec((1, WIDTH, ATT_TILE), lambda b, i, w=w: (b * nkt + tile_of(i, w), 0, 0))
               for w in range(_BAND_TILES)]
    return pl.pallas_call(
        functools.partial(_band_kernel, tq=tq, off=off),
        out_shape=jax.ShapeDtypeStruct((nb * nq * tq, WIDTH), BF16),
        grid=(nb, nq),
        in_specs=[pl.BlockSpec((WIDTH, tq), lambda b, i: (0, b * nq + i))] + k_specs + v_specs
                 + [_const_spec(bias.shape)],
        out_specs=pl.BlockSpec((tq, WIDTH), lambda b, i: (b * nq + i, 0)),
        scratch_shapes=_attn_scratch(tq),
        compiler_params=_cparams(("parallel", "parallel")),
        name="band_attn",
    )(qT, *([k] * _BAND_TILES), *([vT] * _BAND_TILES), bias)


def _mix_ln_kernel(h_ref, oa_ref, ob_ref, wg_ref, wba_ref, wbb_ref, wo_ref, g_ref, b_ref, o_ref,
                   *, alpha):
    h = h_ref[...]
    hb = h.astype(BF16)
    d = h.shape[1]
    ga = jnp.dot(hb, wg_ref[:, :d], preferred_element_type=F32)
    gb = jnp.dot(hb, wg_ref[:, d:], preferred_element_type=F32)
    ya = jnp.dot(oa_ref[...], wba_ref[...], preferred_element_type=F32)
    yb = jnp.dot(ob_ref[...], wbb_ref[...], preferred_element_type=F32)
    gated = jax.nn.sigmoid(ga) * ya + jax.nn.sigmoid(gb) * yb
    mix = jnp.dot(gated.astype(BF16), wo_ref[...], preferred_element_type=F32)
    o_ref[...] = _layer_norm_rows(alpha * h + mix, g_ref[...], b_ref[...])


def _mix_ln(h, oa, ob, wg, wba, wbb, wo, g, b, alpha):
    n, d = h.shape
    tm = TOKEN_TILE
    row = lambda width: pl.BlockSpec((tm, width), lambda i: (i, 0))
    return pl.pallas_call(
        functools.partial(_mix_ln_kernel, alpha=alpha),
        out_shape=jax.ShapeDtypeStruct((n, d), F32),
        grid=(n // tm,),
        in_specs=[row(d), row(WIDTH), row(WIDTH), _const_spec(wg.shape), _const_spec(wba.shape),
                  _const_spec(wbb.shape), _const_spec(wo.shape), _const_spec((1, d)),
                  _const_spec((1, d))],
        out_specs=row(d),
        compiler_params=_cparams(("parallel",)),
        name="mix_ln",
    )(h, oa, ob, wg, wba, wbb, wo, g.reshape(1, d), b.reshape(1, d))


def _t5_bucket(rel):
    half = T5_BUCKETS // 2
    exact = half // 2
    n = jnp.abs(rel)
    log_ratio = jnp.log(jnp.maximum(n, 1).astype(F32) / exact) / math.log(T5_MAX_DIST / exact)
    large = jnp.minimum(exact + (log_ratio * (half - exact)).astype(I32), half - 1)
    return (rel > 0).astype(I32) * half + jnp.where(n < exact, n, large)


def _rel_line(n_tiles, first, tq):
    period = ATT_TILE + tq
    y = jnp.arange(period, dtype=I32)[None, :]
    c = (jnp.arange(n_tiles, dtype=I32)[:, None] + first) * ATT_TILE
    return jnp.where(y < tq, c - y, c + period - y)


def _toeplitz(v, tq):
    period = v.shape[-1]
    flat = jnp.tile(v, (1,) * (v.ndim - 1) + (ATT_TILE,))[..., :ATT_TILE * (period - 1)]
    return flat.reshape(v.shape[:-1] + (ATT_TILE, period - 1))[..., :tq]


def _dsa_bias_table(t5_bias, tq):
    assert T5_MAX_DIST <= ATT_TILE + 1
    far = t5_bias[:, _t5_bucket(jnp.int32(-(ATT_TILE + 1)))]
    line = t5_bias[:, _t5_bucket(_rel_line(2, -1, tq))] - far[:, None, None]
    return _toeplitz(jnp.moveaxis(line, 0, 1), tq)


def _band_bias_table(rel_bias, tq):
    first = 1 - _BAND_TILES
    ridx = jnp.clip(_rel_line(_BAND_TILES, first, tq), -REL_BACK, CHUNK - 1) + REL_BACK
    bias = _toeplitz(jnp.moveaxis(rel_bias[:, ridx], 0, 1), tq)
    w = jnp.arange(_BAND_TILES, dtype=I32)[:, None, None]
    j = jnp.arange(ATT_TILE, dtype=I32)[None, :, None]
    i = jnp.arange(tq, dtype=I32)[None, None, :]
    dchunk = ((w + first) * ATT_TILE + j) // CHUNK - i // CHUNK
    ok = (dchunk <= 0) & (dchunk >= -BAND_CHUNKS)
    table = jnp.where(ok[:, None], bias, NEG_INF)
    return jnp.concatenate([table, jnp.full_like(table[:1], NEG_INF)], axis=0)


def _regroup_w_in(w_in):
    sizes = (WIDTH, WIDTH, WIDTH, N_HEADS_IDX * IDX_DIM, IDX_DIM, N_HEADS_IDX,
             WIDTH, WIDTH, WIDTH, w_in.shape[0], w_in.shape[0])
    offs = [0]
    for s in sizes:
        offs.append(offs[-1] + s)
    qa, ka, va, qi, ki, wi, qb, kb, vb, ga, gb = (w_in[:, offs[t]:offs[t + 1]] for t in range(11))
    wi_pad = jnp.pad(wi, ((0, 0), (0, 128 - N_HEADS_IDX)))
    w_proj = jnp.concatenate([qa, ka, va, qi, ki, ki, wi_pad, qb, kb, vb], axis=1).astype(BF16)
    w_gate = jnp.concatenate([ga, gb], axis=1).astype(BF16)
    return w_proj, w_gate


def _key_tiles_T(v):
    nb, tkp, wd = v.shape
    return v.reshape(nb, tkp // ATT_TILE, ATT_TILE, wd).transpose(0, 1, 3, 2).reshape(-1, wd, ATT_TILE)


def _pad_keys(x, tkp):
    return jnp.pad(x, ((0, 0), (0, tkp - x.shape[1]), (0, 0)))


def kernel(x_prompt, x_sample, cache_k_a, cache_v_a, cache_kidx_a, cache_k_b, cache_v_b, t5_bias,
           ln1_g, ln1_b, ffn1_wi, ffn1_wo, ln2_g, ln2_b, w_in, rel_bias_b, w_branch_a, w_branch_b,
           w_out, ln3_g, ln3_b, ffn2_wi, ffn2_wo):
    depth = ln1_g.shape[0]
    alpha = (2.0 * depth) ** 0.25
    nbp, seq, d = x_prompt.shape
    nbs, dec, _ = x_sample.shape
    past = cache_k_a.shape[2]
    band = cache_k_b.shape[2]
    keep = min(BAND_CHUNKS * CHUNK, seq)
    tq_p, tq_s = ATT_TILE, 128
    assert seq % TOKEN_TILE == 0 and keep == TOKEN_TILE and dec <= tq_s and past % ATT_TILE == 0
    assert band == BAND_CHUNKS * CHUNK and (nbs * tq_s) % TOKEN_TILE == 0

    yp = x_prompt.reshape(nbp * seq, d)
    ys = jnp.pad(x_sample, ((0, 0), (0, tq_s - dec), (0, 0))).reshape(nbs * tq_s, d)
    dsa_bias_p = _dsa_bias_table(t5_bias, tq_p)
    dsa_bias_s = dsa_bias_p[..., :tq_s]
    n_sel_p = min(TOPK_MAX, seq // 4)
    n_sel_s = min(TOPK_MAX, (past + dec) // 4)
    tk_s = -(-(past + tq_s) // ATT_TILE) * ATT_TILE
    tkb_s = _BAND_TILES * ATT_TILE

    st_p, st_s = [], []
    for l in range(depth):
        w_proj, w_gate = _regroup_w_in(w_in[l])
        wi1, wo1 = ffn1_wi[l].astype(BF16), ffn1_wo[l].astype(BF16)
        wi2, wo2 = ffn2_wi[l].astype(BF16), ffn2_wo[l].astype(BF16)
        wba, wbb, wo = (w_branch_a[l].astype(BF16), w_branch_b[l].astype(BF16),
                        w_out[l].astype(BF16))
        band_bias_p = _band_bias_table(rel_bias_b[l], tq_p)
        band_bias_s = band_bias_p[..., :tq_s]

        h = _ffn_ln(yp, wi1, wo1, ln1_g[l], ln1_b[l], alpha)
        (qaT, ka, kab, va, vaT, qiT, ki, kk, wiT, qbT, kbb, vbT, kbl, vbl) = _in_proj(
            h, w_proj, seq // TOKEN_TILE)
        maskT = _idx_mask(qiT, wiT, kk.reshape(nbp, seq, 128), nb=nbp, tq=tq_p, qpos0=0,
                          n_valid=seq, n_sel=n_sel_p)
        oa = _dsa_attn(qaT, kab.reshape(nbp, seq, WIDTH), vaT, maskT, dsa_bias_p,
                       nb=nbp, tq=tq_p, qpos0=0)
        ob = _band_attn(qbT, kbb.reshape(nbp, seq, WIDTH), vbT, band_bias_p,
                        nb=nbp, tq=tq_p, off=1 - _BAND_TILES)
        h2 = _mix_ln(h, oa, ob, w_gate, wba, wbb, wo, ln2_g[l], ln2_b[l], alpha)
        yp = _ffn_ln(h2, wi2, wo2, ln3_g[l], ln3_b[l], alpha)
        st_p.append((ka.reshape(nbp, seq, N_HEADS, HEAD_DIM), va.reshape(nbp, seq, N_HEADS, HEAD_DIM),
                     ki.reshape(nbp, seq, IDX_DIM), kbl.reshape(nbp, keep, N_HEADS, HEAD_DIM),
                     vbl.reshape(nbp, keep, N_HEADS, HEAD_DIM)))

        h = _ffn_ln(ys, wi1, wo1, ln1_g[l], ln1_b[l], alpha)
        (qaT, ka, _, va, _, qiT, ki, _, wiT, qbT, _, _, kbl, vbl) = _in_proj(h, w_proj, 1)
        new = lambda a: a.reshape(nbs, tq_s, -1)[:, :dec]
        ka_n, va_n, ki_n, kb_n, vb_n = new(ka), new(va), new(ki), new(kbl), new(vbl)
        ka_all = jnp.concatenate([cache_k_a[l].reshape(nbs, past, WIDTH), ka_n], axis=1)
        va_all = jnp.concatenate([cache_v_a[l].reshape(nbs, past, WIDTH), va_n], axis=1)
        ki_all = jnp.concatenate([cache_kidx_a[l], ki_n], axis=1)
        kb_all = jnp.concatenate([cache_k_b[l].reshape(nbs, band, WIDTH), kb_n], axis=1)
        vb_all = jnp.concatenate([cache_v_b[l].reshape(nbs, band, WIDTH), vb_n], axis=1)
        kk_s = _pad_keys(jnp.concatenate([ki_all, ki_all], axis=2).astype(BF16), tk_s)
        maskT = _idx_mask(qiT, wiT, kk_s, nb=nbs, tq=tq_s, qpos0=past, n_valid=past + dec,
                          n_sel=n_sel_s)
        oa = _dsa_attn(qaT, _pad_keys(ka_all.astype(BF16), tk_s),
                       _key_tiles_T(_pad_keys(va_all.astype(BF16), tk_s)), maskT, dsa_bias_s,
                       nb=nbs, tq=tq_s, qpos0=past)
        ob = _band_attn(qbT, _pad_keys(kb_all.astype(BF16), tkb_s),
                        _key_tiles_T(_pad_keys(vb_all.astype(BF16), tkb_s)), band_bias_s,
                        nb=nbs, tq=tq_s, off=0)
        h2 = _mix_ln(h, oa, ob, w_gate, wba, wbb, wo, ln2_g[l], ln2_b[l], alpha)
        ys = _ffn_ln(h2, wi2, wo2, ln3_g[l], ln3_b[l], alpha)
        heads = lambda a: a.reshape(nbs, -1, N_HEADS, HEAD_DIM)
        st_s.append((heads(ka_n), heads(va_n), ki_n, heads(kb_all[:, -band:]),
                     heads(vb_all[:, -band:])))

    y_prompt = yp.reshape(nbp, seq, d)
    y_sample = ys.reshape(nbs, tq_s, d)[:, :dec]
    stack = lambda sts, t: jnp.stack([s[t] for s in sts])
    return (y_prompt, y_sample,
            stack(st_p, 0), stack(st_p, 1), stack(st_p, 2), stack(st_p, 3), stack(st_p, 4),
            stack(st_s, 0), stack(st_s, 1), stack(st_s, 2), stack(st_s, 3), stack(st_s, 4))
```

```python
import functools
import math

import jax
import jax.numpy as jnp
from jax import lax
from jax.experimental import pallas as pl
from jax.experimental.pallas import tpu as pltpu

F32 = jnp.float32
BF16 = jnp.bfloat16
I32 = jnp.int32
I16 = jnp.int16

CHUNK = 64
CHUNK_SHIFT = 6
HEAD_DIM = 64
N_HEADS = 8
WIDTH = N_HEADS * HEAD_DIM
N_HEADS_IDX = 4
IDX_DIM = 64
TOPK_MAX = 256
IDX_SCALE = (IDX_DIM * N_HEADS_IDX) ** -0.5
BAND_CHUNKS = 8
REL_BACK = 128
T5_BUCKETS = 32
T5_MAX_DIST = 128
LN_EPS = 1e-5
NEG_INF = -1e30
INT_MIN = -2 ** 31

ATT_TILE = 256
TOKEN_TILE = 512
FF_CHUNK = 256
VMEM_LIMIT_MB = 56


def _cparams(sem):
    return pltpu.CompilerParams(dimension_semantics=sem,
                                vmem_limit_bytes=VMEM_LIMIT_MB * 1024 * 1024)


def _const_spec(shape):
    nd = len(shape)
    return pl.BlockSpec(shape, lambda *_: (0,) * nd, pipeline_mode=pl.Buffered(1))


def _tree_sum(xs):
    xs = list(xs)
    while len(xs) > 1:
        nxt = [xs[a] + xs[a + 1] for a in range(0, len(xs) - 1, 2)]
        if len(xs) % 2:
            nxt.append(xs[-1])
        xs = nxt
    return xs[0]


def _layer_norm_rows(y, g, b):
    mu = jnp.mean(y, axis=-1, keepdims=True)
    d = y - mu
    var = jnp.mean(d * d, axis=-1, keepdims=True)
    return d * lax.rsqrt(var + LN_EPS) * g + b


def _ffn_ln_kernel(x_ref, wi_ref, wo_ref, g_ref, b_ref, o_ref, *, alpha, dff):
    x = x_ref[...]
    xb = x.astype(BF16)
    acc = jnp.zeros(x.shape, F32)
    for c in range(dff // FF_CHUNK):
        lo = c * FF_CHUNK
        a = jnp.dot(xb, wi_ref[:, lo:lo + FF_CHUNK], preferred_element_type=F32)
        u = jnp.dot(xb, wi_ref[:, dff + lo:dff + lo + FF_CHUNK], preferred_element_type=F32)
        hm = (a * jax.nn.sigmoid(a)) * u
        acc = acc + jnp.dot(hm.astype(BF16), wo_ref[lo:lo + FF_CHUNK, :],
                            preferred_element_type=F32)
    y = alpha * x + 0.5 * acc
    o_ref[...] = _layer_norm_rows(y, g_ref[...], b_ref[...])


def _ffn_ln(x, wi, wo, g, b, alpha):
    n, d = x.shape
    dff = wo.shape[0]
    tm = TOKEN_TILE
    return pl.pallas_call(
        functools.partial(_ffn_ln_kernel, alpha=alpha, dff=dff),
        out_shape=jax.ShapeDtypeStruct((n, d), F32),
        grid=(n // tm,),
        in_specs=[pl.BlockSpec((tm, d), lambda i: (i, 0)),
                  _const_spec((d, 2 * dff)), _const_spec((dff, d)),
                  _const_spec((1, d)), _const_spec((1, d))],
        out_specs=pl.BlockSpec((tm, d), lambda i: (i, 0)),
        compiler_params=_cparams(("parallel",)),
        name="ffn_ln",
    )(x, wi, wo, g.reshape(1, d), b.reshape(1, d))


_C_QA, _C_KA, _C_VA, _C_QI, _C_KI, _C_WI, _C_QB, _C_KB, _C_VB, _C_END = (
    0, 512, 1024, 1536, 1792, 1920, 2048, 2560, 3072, 3584)


def _inproj_kernel(h_ref, w_ref, qaT_ref, ka_ref, kab_ref, va_ref, vaT_ref, qiT_ref, ki_ref,
                   kk_ref, wiT_ref, qbT_ref, kbb_ref, vbT_ref, kbl_ref, vbl_ref, *, per):
    hb = h_ref[...].astype(BF16)

    def proj(c0, c1):
        return jnp.dot(hb, w_ref[:, c0:c1], preferred_element_type=F32)

    def put_tiles(ref, zt):
        for c in range(ref.shape[0]):
            ref[c] = zt[:, c * ATT_TILE:(c + 1) * ATT_TILE].astype(ref.dtype)

    scale = HEAD_DIM ** -0.5
    qaT_ref[...] = (proj(_C_QA, _C_KA) * scale).T.astype(BF16)
    ka = proj(_C_KA, _C_VA)
    ka_ref[...] = ka
    kab_ref[...] = ka.astype(BF16)
    va = proj(_C_VA, _C_QI)
    va_ref[...] = va
    put_tiles(vaT_ref, va.T)
    qiT_ref[...] = proj(_C_QI, _C_KI).T.astype(BF16)
    kk = proj(_C_KI, _C_WI)
    ki_ref[...] = kk[:, :IDX_DIM]
    kk_ref[...] = kk.astype(BF16)
    wiT_ref[...] = proj(_C_WI, _C_QB).T[:8, :]
    qbT_ref[...] = (proj(_C_QB, _C_KB) * scale).T.astype(BF16)
    kb = proj(_C_KB, _C_VB)
    kbb_ref[...] = kb.astype(BF16)
    vb = proj(_C_VB, _C_END)
    put_tiles(vbT_ref, vb.T)

    @pl.when(pl.program_id(0) % per == per - 1)
    def _():
        kbl_ref[...] = kb
        vbl_ref[...] = vb


def _in_proj(h, w, per):
    n, d = h.shape
    tm = TOKEN_TILE
    nt = n // tm
    sub = tm // ATT_TILE
    row = lambda width: pl.BlockSpec((tm, width), lambda i: (i, 0))
    col = lambda rows: pl.BlockSpec((rows, tm), lambda i: (0, i))
    til = pl.BlockSpec((sub, WIDTH, ATT_TILE), lambda i: (i, 0, 0))
    last = pl.BlockSpec((tm, WIDTH), lambda i: (i // per, 0))
    sds = jax.ShapeDtypeStruct
    out_shape = (
        sds((WIDTH, n), BF16),
        sds((n, WIDTH), F32), sds((n, WIDTH), BF16),
        sds((n, WIDTH), F32), sds((n // ATT_TILE, WIDTH, ATT_TILE), BF16),
        sds((N_HEADS_IDX * IDX_DIM, n), BF16),
        sds((n, IDX_DIM), F32), sds((n, 128), BF16),
        sds((8, n), F32),
        sds((WIDTH, n), BF16),
        sds((n, WIDTH), BF16),
        sds((n // ATT_TILE, WIDTH, ATT_TILE), BF16),
        sds((nt // per * tm, WIDTH), F32), sds((nt // per * tm, WIDTH), F32),
    )
    out_specs = (col(WIDTH), row(WIDTH), row(WIDTH), row(WIDTH), til,
                 col(N_HEADS_IDX * IDX_DIM), row(IDX_DIM), row(128), col(8),
                 col(WIDTH), row(WIDTH), til, last, last)
    return pl.pallas_call(
        functools.partial(_inproj_kernel, per=per),
        out_shape=out_shape,
        grid=(nt,),
        in_specs=[pl.BlockSpec((tm, d), lambda i: (i, 0)), _const_spec((d, _C_END))],
        out_specs=out_specs,
        compiler_params=_cparams(("arbitrary",)),
        name="in_proj",
    )(h, w)


def _flip(bits):
    return bits ^ (lax.shift_right_arithmetic(bits, 31) & 0x7FFFFFFF)


def _idx_mask_kernel(qiT_ref, wiT_ref, kk_ref, mask_ref, st_ref, hi_ref, lo_ref, d_ref, incl_ref,
                     *, tq, qpos0, n_valid, n_sel):
    tk = ATT_TILE
    nkt_total = st_ref.shape[0] // tk
    q0 = qpos0 + pl.program_id(1) * tq
    n_keys = jnp.minimum(q0 + tq, n_valid)
    nt = jnp.minimum(lax.shift_right_logical(n_keys + tk - 1, int(math.log2(tk))), nkt_total)

    lane = lax.broadcasted_iota(I32, (8, tq), 1)
    q_chunk = lax.shift_right_logical(q0 + lane, CHUNK_SHIFT)
    n_adm = jnp.minimum((q_chunk + 1) * CHUNK, n_valid)
    k_row = jnp.minimum(n_adm, n_sel)

    zeros = jnp.zeros((128 - IDX_DIM, tq), BF16)
    qh = [jnp.concatenate([qiT_ref[h * IDX_DIM:(h + 1) * IDX_DIM, :], zeros], axis=0)
          for h in range(N_HEADS_IDX)]
    scale_exact = math.frexp(IDX_SCALE)[0] == 0.5
    wh = [wiT_ref[h:h + 1, :] * IDX_SCALE if scale_exact else wiT_ref[h:h + 1, :]
          for h in range(N_HEADS_IDX)]

    def dots(j, h):
        r0 = pl.multiple_of(j * tk, tk)
        d_ref[h] = jnp.dot(kk_ref[pl.ds(r0, tk), :], qh[h], preferred_element_type=F32)

    def score_tile(j, last):
        r0 = pl.multiple_of(j * tk, tk)
        sc = None
        for h in range(N_HEADS_IDX):
            d = d_ref[h]
            if not last:
                dots(j + 1, h)
            term = wh[h] * jnp.maximum(d, 0.0)
            sc = term if sc is None else sc + term
        if not scale_exact:
            sc = sc * IDX_SCALE
        bits = lax.bitcast_convert_type(sc, I32)
        sign = lax.shift_right_arithmetic(bits, 31)
        key = (bits ^ (sign & 0x7FFFFFFF)) - sign
        if last:
            kpos = r0 + lax.broadcasted_iota(I32, (tk, tq), 0)
            key = jnp.where(kpos < n_adm[0:1, :], key, INT_MIN)
        st_ref[pl.ds(r0, tk), :] = key
        hi_ref[pl.ds(r0, tk), :] = lax.shift_right_arithmetic(key, 16).astype(I16)
        lo_ref[pl.ds(r0, tk), :] = (key ^ 0x8000).astype(I16)

    for h in range(N_HEADS_IDX):
        dots(0, h)

    def score_body(j, carry):
        score_tile(j, False)
        return carry

    lax.fori_loop(0, nt - 1, score_body, 0)
    score_tile(nt - 1, True)

    def count16(ref, pred, thr8):
        thr = jnp.concatenate([thr8, thr8], axis=0).astype(I16)

        def body(j, acc):
            r0 = pl.multiple_of(j * tk, tk)
            blk = ref[pl.ds(r0, tk), :]
            parts = [jnp.where(pred(blk[r * 16:(r + 1) * 16, :], thr), jnp.int16(1), jnp.int16(0))
                     for r in range(tk // 16)]
            return acc + _tree_sum(parts)

        acc = lax.fori_loop(0, nt, body, jnp.zeros((16, tq), I16))
        return jnp.broadcast_to(jnp.sum(acc.astype(I32), axis=0, keepdims=True), (8, tq))

    ge = lambda a, b: a >= b
    gt = lambda a, b: a > b

    def kth_largest16(ref, want):
        def step(k, t8):
            cand = t8 + lax.shift_left(jnp.int32(1), 15 - k)
            return jnp.where(count16(ref, ge, cand) >= want, cand, t8)
        return lax.fori_loop(0, 16, step, jnp.full((8, tq), -2 ** 15, I32))

    t_hi = kth_largest16(hi_ref, k_row)
    n_above = count16(hi_ref, gt, t_hi)
    t_hi16 = jnp.concatenate([t_hi, t_hi], axis=0).astype(I16)

    def park(j, carry):
        r0 = pl.multiple_of(j * tk, tk)
        hi, lo = hi_ref[pl.ds(r0, tk), :], lo_ref[pl.ds(r0, tk), :]
        lo_ref[pl.ds(r0, tk), :] = jnp.concatenate(
            [jnp.where(hi[r * 16:(r + 1) * 16, :] == t_hi16, lo[r * 16:(r + 1) * 16, :],
                       jnp.int16(-2 ** 15)) for r in range(tk // 16)], axis=0)
        return carry

    lax.fori_loop(0, nt, park, 0)
    t_lo = kth_largest16(lo_ref, k_row - n_above)
    n_gt = n_above + count16(lo_ref, gt, t_lo)
    t_row = (t_hi * 65536 + (t_lo + 2 ** 15))[0:1, :]
    ties_wanted = (k_row - n_gt)[0:1, :].astype(F32)

    tri = jnp.where(lax.broadcasted_iota(I32, (tk, tk), 0) >= lax.broadcasted_iota(I32, (tk, tk), 1),
                    1.0, 0.0).astype(BF16)

    def tie_prefix(j):
        r0 = pl.multiple_of(j * tk, tk)
        eq = jnp.where(st_ref[pl.ds(r0, tk), :] == t_row, 1.0, 0.0)
        incl_ref[...] = jnp.dot(tri, eq.astype(BF16), preferred_element_type=F32)

    def final_tile(j, seen, last):
        r0 = pl.multiple_of(j * tk, tk)
        incl = incl_ref[...]
        if not last:
            tie_prefix(j + 1)
        rank = seen + incl
        bar = t_row + jnp.where(rank <= ties_wanted, 0, 1)
        sel = jnp.where(st_ref[pl.ds(r0, tk), :] >= bar, 1, 0)
        mask_ref[pl.ds(r0, tk), :] = sel.astype(jnp.int8)
        return seen + incl[tk - 1:tk, :]

    tie_prefix(0)
    seen = lax.fori_loop(0, nt - 1, lambda j, seen: final_tile(j, seen, False),
                         jnp.zeros((1, tq), F32))
    final_tile(nt - 1, seen, True)

    def zero_tile(j, carry):
        r0 = pl.multiple_of(j * tk, tk)
        mask_ref[pl.ds(r0, tk), :] = jnp.zeros((tk, tq), jnp.int8)
        return carry

    lax.fori_loop(nt, nkt_total, zero_tile, 0)


def _idx_mask(qiT, wiT, kk, *, nb, tq, qpos0, n_valid, n_sel):
    tkp = kk.shape[1]
    nq = qiT.shape[1] // nb // tq
    return pl.pallas_call(
        functools.partial(_idx_mask_kernel, tq=tq, qpos0=qpos0, n_valid=n_valid, n_sel=n_sel),
        out_shape=jax.ShapeDtypeStruct((nb, tkp, nq * tq), jnp.int8),
        grid=(nb, nq),
        in_specs=[pl.BlockSpec((N_HEADS_IDX * IDX_DIM, tq), lambda b, i: (0, b * nq + i)),
                  pl.BlockSpec((8, tq), lambda b, i: (0, b * nq + i)),
                  pl.BlockSpec((None, tkp, 128), lambda b, i: (b, 0, 0))],
        out_specs=pl.BlockSpec((None, tkp, tq), lambda b, i: (b, 0, i)),
        scratch_shapes=[pltpu.VMEM((tkp, tq), I32), pltpu.VMEM((tkp, tq), I16),
                        pltpu.VMEM((tkp, tq), I16), pltpu.VMEM((N_HEADS_IDX, ATT_TILE, tq), F32),
                        pltpu.VMEM((ATT_TILE, tq), F32)],
        compiler_params=_cparams(("parallel", "parallel")),
        name="idx_mask",
    )(qiT, wiT, kk)


def _load_qsel(qT_ref, qsel_ref, tq):
    zeros = jnp.zeros((HEAD_DIM, tq), BF16)
    for h in range(N_HEADS):
        blk = qT_ref[h * HEAD_DIM:(h + 1) * HEAD_DIM, :]
        qsel_ref[h] = jnp.concatenate([blk, zeros] if h % 2 == 0 else [zeros, blk], axis=0)


def _init_state(m_ref, l_ref, acc_ref):
    m_ref[...] = jnp.full(m_ref.shape, NEG_INF, F32)
    l_ref[...] = jnp.zeros(l_ref.shape, F32)
    acc_ref[...] = jnp.zeros(acc_ref.shape, F32)


def _scores(k_slab, qsel_ref, s_ref, h):
    s_ref[h] = jnp.dot(k_slab(h // 2), qsel_ref[h], preferred_element_type=F32)


def _heads_tile(v_rows, next_k_slab, qsel_ref, s_ref, m_ref, l_ref, acc_ref, addend):
    for h in range(N_HEADS):
        s = s_ref[h]
        for a in addend(h):
            s = s + a
        if next_k_slab is not None:
            _scores(next_k_slab, qsel_ref, s_ref, h)
        m_old = m_ref[h:h + 1, :]
        m_new = jnp.maximum(m_old, jnp.max(s, axis=0, keepdims=True))
        alpha = jnp.exp(m_old - m_new)
        p = jnp.exp(s - m_new)
        l_ref[h:h + 1, :] = alpha * l_ref[h:h + 1, :] + jnp.sum(p, axis=0, keepdims=True)
        m_ref[h:h + 1, :] = m_new
        pv = jnp.dot(v_rows(h), p.astype(BF16), preferred_element_type=F32)
        rows = slice(h * HEAD_DIM, (h + 1) * HEAD_DIM)
        acc_ref[rows, :] = alpha * acc_ref[rows, :] + pv


def _attn_scratch(tq):
    return [pltpu.VMEM((N_HEADS, 128, tq), BF16), pltpu.VMEM((N_HEADS, ATT_TILE, tq), F32),
            pltpu.VMEM((N_HEADS, tq), F32), pltpu.VMEM((N_HEADS, tq), F32),
            pltpu.VMEM((WIDTH, tq), F32)]


def _store_out(o_ref, l_ref, acc_ref):
    outs = [acc_ref[h * HEAD_DIM:(h + 1) * HEAD_DIM, :] / l_ref[h:h + 1, :] for h in range(N_HEADS)]
    o_ref[...] = jnp.concatenate(outs, axis=0).T.astype(o_ref.dtype)


def _dsa_kernel(qT_ref, k_ref, vT_ref, mask_ref, bias_ref, o_ref, qsel_ref, s_ref, m_ref, l_ref,
                acc_ref, *, tq, qpos0):
    tk = ATT_TILE
    jd = lax.shift_right_logical(qpos0 + pl.program_id(1) * tq, int(math.log2(tk)))
    _load_qsel(qT_ref, qsel_ref, tq)
    _init_state(m_ref, l_ref, acc_ref)

    def k_slab(j):
        r0 = pl.multiple_of(j * tk, tk)
        return lambda g: k_ref[pl.ds(r0, tk), g * 128:(g + 1) * 128]

    def tile(j, near, last=False):
        r0 = pl.multiple_of(j * tk, tk)
        unselected = jnp.where(mask_ref[pl.ds(r0, tk), :].astype(I32) != 0, 0.0, NEG_INF)
        addend = ((lambda h: (unselected,)) if near is None
                  else (lambda h: (unselected, bias_ref[near, h])))
        _heads_tile(lambda h: vT_ref[j, h * HEAD_DIM:(h + 1) * HEAD_DIM, :],
                    None if last else k_slab(j + 1),
                    qsel_ref, s_ref, m_ref, l_ref, acc_ref, addend)

    for h in range(N_HEADS):
        _scores(k_slab(0), qsel_ref, s_ref, h)

    def far(j, carry):
        tile(j, None)
        return carry

    lax.fori_loop(0, jnp.maximum(jd - 1, 0), far, 0)

    @pl.when(jd >= 1)
    def _():
        tile(jd - 1, 0)

    tile(jd, 1, last=True)
    _store_out(o_ref, l_ref, acc_ref)


def _dsa_attn(qT, k, vT, maskT, bias, *, nb, tq, qpos0):
    tkp = k.shape[1]
    nkt = tkp // ATT_TILE
    nq = qT.shape[1] // nb // tq
    return pl.pallas_call(
        functools.partial(_dsa_kernel, tq=tq, qpos0=qpos0),
        out_shape=jax.ShapeDtypeStruct((nb * nq * tq, WIDTH), BF16),
        grid=(nb, nq),
        in_specs=[pl.BlockSpec((WIDTH, tq), lambda b, i: (0, b * nq + i)),
                  pl.BlockSpec((None, tkp, WIDTH), lambda b, i: (b, 0, 0),
                               pipeline_mode=pl.Buffered(1)),
                  pl.BlockSpec((nkt, WIDTH, ATT_TILE), lambda b, i: (b, 0, 0),
                               pipeline_mode=pl.Buffered(1)),
                  pl.BlockSpec((None, tkp, tq), lambda b, i: (b, 0, i)),
                  _const_spec(bias.shape)],
        out_specs=pl.BlockSpec((tq, WIDTH), lambda b, i: (b * nq + i, 0)),
        scratch_shapes=_attn_scratch(tq),
        compiler_params=_cparams(("parallel", "parallel")),
        name="dsa_attn",
    )(qT, k, vT, maskT, bias)


_BAND_TILES = BAND_CHUNKS * CHUNK // ATT_TILE + 1


def _band_kernel(qT_ref, *refs, tq, off):
    k_refs = refs[:_BAND_TILES]
    v_refs = refs[_BAND_TILES:2 * _BAND_TILES]
    bias_ref, o_ref, qsel_ref, s_ref, m_ref, l_ref, acc_ref = refs[2 * _BAND_TILES:]
    _load_qsel(qT_ref, qsel_ref, tq)
    _init_state(m_ref, l_ref, acc_ref)
    k_slab = lambda w: (lambda g: k_refs[w][:, g * 128:(g + 1) * 128])
    for h in range(N_HEADS):
        _scores(k_slab(0), qsel_ref, s_ref, h)
    for w in range(_BAND_TILES):
        entry = jnp.where(pl.program_id(1) + off + w >= 0, w, _BAND_TILES)
        _heads_tile(lambda h: v_refs[w][0, h * HEAD_DIM:(h + 1) * HEAD_DIM, :],
                    None if w == _BAND_TILES - 1 else k_slab(w + 1),
                    qsel_ref, s_ref, m_ref, l_ref, acc_ref, lambda h: (bias_ref[entry, h],))
    _store_out(o_ref, l_ref, acc_ref)


def _band_attn(qT, k, vT, bias, *, nb, tq, off):
    nkt = k.shape[1] // ATT_TILE
    nq = qT.shape[1] // nb // tq
    tile_of = lambda i, w: jnp.maximum(i + off + w, 0)
    k_specs = [pl.BlockSpec((None, ATT_TILE, WIDTH), lambda b, i, w=w: (b, tile_of(i, w), 0))
               for w in range(_BAND_TILES)]
    v_specs = [pl.BlockSpec((1, WIDTH, ATT_TILE), lambda b, i, w=w: (b * nkt + tile_of(i, w), 0, 0))
               for w in range(_BAND_TILES)]
    return pl.pallas_call(
        functools.partial(_band_kernel, tq=tq, off=off),
        out_shape=jax.ShapeDtypeStruct((nb * nq * tq, WIDTH), BF16),
        grid=(nb, nq),
        in_specs=[pl.BlockSpec((WIDTH, tq), lambda b, i: (0, b * nq + i))] + k_specs + v_specs
                 + [_const_spec(bias.shape)],
        out_specs=pl.BlockSpec((tq, WIDTH), lambda b, i: (b * nq + i, 0)),
        scratch_shapes=_attn_scratch(tq),
        compiler_params=_cparams(("parallel", "parallel")),
        name="band_attn",
    )(qT, *([k] * _BAND_TILES), *([vT] * _BAND_TILES), bias)


def _mix_ln_kernel(h_ref, oa_ref, ob_ref, wg_ref, wba_ref, wbb_ref, wo_ref, g_ref, b_ref, o_ref,
                   *, alpha):
    h = h_ref[...]
    hb = h.astype(BF16)
    d = h.shape[1]
    ga = jnp.dot(hb, wg_ref[:, :d], preferred_element_type=F32)
    gb = jnp.dot(hb, wg_ref[:, d:], preferred_element_type=F32)
    ya = jnp.dot(oa_ref[...], wba_ref[...], preferred_element_type=F32)
    yb = jnp.dot(ob_ref[...], wbb_ref[...], preferred_element_type=F32)
    gated = jax.nn.sigmoid(ga) * ya + jax.nn.sigmoid(gb) * yb
    mix = jnp.dot(gated.astype(BF16), wo_ref[...], preferred_element_type=F32)
    o_ref[...] = _layer_norm_rows(alpha * h + mix, g_ref[...], b_ref[...])


def _mix_ln(h, oa, ob, wg, wba, wbb, wo, g, b, alpha):
    n, d = h.shape
    tm = TOKEN_TILE
    row = lambda width: pl.BlockSpec((tm, width), lambda i: (i, 0))
    return pl.pallas_call(
        functools.partial(_mix_ln_kernel, alpha=alpha),
        out_shape=jax.ShapeDtypeStruct((n, d), F32),
        grid=(n // tm,),
        in_specs=[row(d), row(WIDTH), row(WIDTH), _const_spec(wg.shape), _const_spec(wba.shape),
                  _const_spec(wbb.shape), _const_spec(wo.shape), _const_spec((1, d)),
                  _const_spec((1, d))],
        out_specs=row(d),
        compiler_params=_cparams(("parallel",)),
        name="mix_ln",
    )(h, oa, ob, wg, wba, wbb, wo, g.reshape(1, d), b.reshape(1, d))


def _t5_bucket(rel):
    half = T5_BUCKETS // 2
    exact = half // 2
    n = jnp.abs(rel)
    log_ratio = jnp.log(jnp.maximum(n, 1).astype(F32) / exact) / math.log(T5_MAX_DIST / exact)
    large = jnp.minimum(exact + (log_ratio * (half - exact)).astype(I32), half - 1)
    return (rel > 0).astype(I32) * half + jnp.where(n < exact, n, large)


def _rel_line(n_tiles, first, tq):
    period = ATT_TILE + tq
    y = jnp.arange(period, dtype=I32)[None, :]
    c = (jnp.arange(n_tiles, dtype=I32)[:, None] + first) * ATT_TILE
    return jnp.where(y < tq, c - y, c + period - y)


def _toeplitz(v, tq):
    period = v.shape[-1]
    flat = jnp.tile(v, (1,) * (v.ndim - 1) + (ATT_TILE,))[..., :ATT_TILE * (period - 1)]
    return flat.reshape(v.shape[:-1] + (ATT_TILE, period - 1))[..., :tq]


def _dsa_bias_table(t5_bias, tq):
    assert T5_MAX_DIST <= ATT_TILE + 1
    far = t5_bias[:, _t5_bucket(jnp.int32(-(ATT_TILE + 1)))]
    line = t5_bias[:, _t5_bucket(_rel_line(2, -1, tq))] - far[:, None, None]
    return _toeplitz(jnp.moveaxis(line, 0, 1), tq)


def _band_bias_table(rel_bias, tq):
    first = 1 - _BAND_TILES
    ridx = jnp.clip(_rel_line(_BAND_TILES, first, tq), -REL_BACK, CHUNK - 1) + REL_BACK
    bias = _toeplitz(jnp.moveaxis(rel_bias[:, ridx], 0, 1), tq)
    w = jnp.arange(_BAND_TILES, dtype=I32)[:, None, None]
    j = jnp.arange(ATT_TILE, dtype=I32)[None, :, None]
    i = jnp.arange(tq, dtype=I32)[None, None, :]
    dchunk = ((w + first) * ATT_TILE + j) // CHUNK - i // CHUNK
    ok = (dchunk <= 0) & (dchunk >= -BAND_CHUNKS)
    table = jnp.where(ok[:, None], bias, NEG_INF)
    return jnp.concatenate([table, jnp.full_like(table[:1], NEG_INF)], axis=0)


def _regroup_w_in(w_in):
    sizes = (WIDTH, WIDTH, WIDTH, N_HEADS_IDX * IDX_DIM, IDX_DIM, N_HEADS_IDX,
             WIDTH, WIDTH, WIDTH, w_in.shape[0], w_in.shape[0])
    offs = [0]
    for s in sizes:
        offs.append(offs[-1] + s)
    qa, ka, va, qi, ki, wi, qb, kb, vb, ga, gb = (w_in[:, offs[t]:offs[t + 1]] for t in range(11))
    wi_pad = jnp.pad(wi, ((0, 0), (0, 128 - N_HEADS_IDX)))
    w_proj = jnp.concatenate([qa, ka, va, qi, ki, ki, wi_pad, qb, kb, vb], axis=1).astype(BF16)
    w_gate = jnp.concatenate([ga, gb], axis=1).astype(BF16)
    return w_proj, w_gate


def _key_tiles_T(v):
    nb, tkp, wd = v.shape
    return v.reshape(nb, tkp // ATT_TILE, ATT_TILE, wd).transpose(0, 1, 3, 2).reshape(-1, wd, ATT_TILE)


def _pad_keys(x, tkp):
    return jnp.pad(x, ((0, 0), (0, tkp - x.shape[1]), (0, 0)))


def kernel(x_prompt, x_sample, cache_k_a, cache_v_a, cache_kidx_a, cache_k_b, cache_v_b, t5_bias,
           ln1_g, ln1_b, ffn1_wi, ffn1_wo, ln2_g, ln2_b, w_in, rel_bias_b, w_branch_a, w_branch_b,
           w_out, ln3_g, ln3_b, ffn2_wi, ffn2_wo):
    depth = ln1_g.shape[0]
    alpha = (2.0 * depth) ** 0.25
    nbp, seq, d = x_prompt.shape
    nbs, dec, _ = x_sample.shape
    past = cache_k_a.shape[2]
    band = cache_k_b.shape[2]
    keep = min(BAND_CHUNKS * CHUNK, seq)
    tq_p, tq_s = ATT_TILE, 128
    assert seq % TOKEN_TILE == 0 and keep == TOKEN_TILE and dec <= tq_s and past % ATT_TILE == 0
    assert band == BAND_CHUNKS * CHUNK and (nbs * tq_s) % TOKEN_TILE == 0

    yp = x_prompt.reshape(nbp * seq, d)
    ys = jnp.pad(x_sample, ((0, 0), (0, tq_s - dec), (0, 0))).reshape(nbs * tq_s, d)
    dsa_bias_p = _dsa_bias_table(t5_bias, tq_p)
    dsa_bias_s = dsa_bias_p[..., :tq_s]
    n_sel_p = min(TOPK_MAX, seq // 4)
    n_sel_s = min(TOPK_MAX, (past + dec) // 4)
    tk_s = -(-(past + tq_s) // ATT_TILE) * ATT_TILE
    tkb_s = _BAND_TILES * ATT_TILE

    st_p, st_s = [], []
    for l in range(depth):
        w_proj, w_gate = _regroup_w_in(w_in[l])
        wi1, wo1 = ffn1_wi[l].astype(BF16), ffn1_wo[l].astype(BF16)
        wi2, wo2 = ffn2_wi[l].astype(BF16), ffn2_wo[l].astype(BF16)
        wba, wbb, wo = (w_branch_a[l].astype(BF16), w_branch_b[l].astype(BF16),
                        w_out[l].astype(BF16))
        band_bias_p = _band_bias_table(rel_bias_b[l], tq_p)
        band_bias_s = band_bias_p[..., :tq_s]

        h = _ffn_ln(yp, wi1, wo1, ln1_g[l], ln1_b[l], alpha)
        (qaT, ka, kab, va, vaT, qiT, ki, kk, wiT, qbT, kbb, vbT, kbl, vbl) = _in_proj(
            h, w_proj, seq // TOKEN_TILE)
        maskT = _idx_mask(qiT, wiT, kk.reshape(nbp, seq, 128), nb=nbp, tq=tq_p, qpos0=0,
                          n_valid=seq, n_sel=n_sel_p)
        oa = _dsa_attn(qaT, kab.reshape(nbp, seq, WIDTH), vaT, maskT, dsa_bias_p,
                       nb=nbp, tq=tq_p, qpos0=0)
        ob = _band_attn(qbT, kbb.reshape(nbp, seq, WIDTH), vbT, band_bias_p,
                        nb=nbp, tq=tq_p, off=1 - _BAND_TILES)
        h2 = _mix_ln(h, oa, ob, w_gate, wba, wbb, wo, ln2_g[l], ln2_b[l], alpha)
        yp = _ffn_ln(h2, wi2, wo2, ln3_g[l], ln3_b[l], alpha)
        st_p.append((ka.reshape(nbp, seq, N_HEADS, HEAD_DIM), va.reshape(nbp, seq, N_HEADS, HEAD_DIM),
                     ki.reshape(nbp, seq, IDX_DIM), kbl.reshape(nbp, keep, N_HEADS, HEAD_DIM),
                     vbl.reshape(nbp, keep, N_HEADS, HEAD_DIM)))

        h = _ffn_ln(ys, wi1, wo1, ln1_g[l], ln1_b[l], alpha)
        (qaT, ka, _, va, _, qiT, ki, _, wiT, qbT, _, _, kbl, vbl) = _in_proj(h, w_proj, 1)
        new = lambda a: a.reshape(nbs, tq_s, -1)[:, :dec]
        ka_n, va_n, ki_n, kb_n, vb_n = new(ka), new(va), new(ki), new(kbl), new(vbl)
        ka_all = jnp.concatenate([cache_k_a[l].reshape(nbs, past, WIDTH), ka_n], axis=1)
        va_all = jnp.concatenate([cache_v_a[l].reshape(nbs, past, WIDTH), va_n], axis=1)
        ki_all = jnp.concatenate([cache_kidx_a[l], ki_n], axis=1)
        kb_all = jnp.concatenate([cache_k_b[l].reshape(nbs, band, WIDTH), kb_n], axis=1)
        vb_all = jnp.concatenate([cache_v_b[l].reshape(nbs, band, WIDTH), vb_n], axis=1)
        kk_s = _pad_keys(jnp.concatenate([ki_all, ki_all], axis=2).astype(BF16), tk_s)
        maskT = _idx_mask(qiT, wiT, kk_s, nb=nbs, tq=tq_s, qpos0=past, n_valid=past + dec,
                          n_sel=n_sel_s)
        oa = _dsa_attn(qaT, _pad_keys(ka_all.astype(BF16), tk_s),
                       _key_tiles_T(_pad_keys(va_all.astype(BF16), tk_s)), maskT, dsa_bias_s,
                       nb=nbs, tq=tq_s, qpos0=past)
        ob = _band_attn(qbT, _pad_keys(kb_all.astype(BF16), tkb_s),
                        _key_tiles_T(_pad_keys(vb_all.astype(BF16), tkb_s)), band_bias_s,
                        nb=nbs, tq=tq_s, off=0)
        h2 = _mix_ln(h, oa, ob, w_gate, wba, wbb, wo, ln2_g[l], ln2_b[l], alpha)
        ys = _ffn_ln(h2, wi2, wo2, ln3_g[l], ln3_b[l], alpha)
        heads = lambda a: a.reshape(nbs, -1, N_HEADS, HEAD_DIM)
        st_s.append((heads(ka_n), heads(va_n), ki_n, heads(kb_all[:, -band:]),
                     heads(vb_all[:, -band:])))

    y_prompt = yp.reshape(nbp, seq, d)
    y_sample = ys.reshape(nbs, tq_s, d)[:, :dec]
    stack = lambda sts, t: jnp.stack([s[t] for s in sts])
    return (y_prompt, y_sample,
            stack(st_p, 0), stack(st_p, 1), stack(st_p, 2), stack(st_p, 3), stack(st_p, 4),
            stack(st_s, 0), stack(st_s, 1), stack(st_s, 2), stack(st_s, 3), stack(st_s, 4))
```

```python
import functools
import math

import jax
import jax.numpy as jnp
from jax import lax
from jax.experimental import pallas as pl
from jax.experimental.pallas import tpu as pltpu

F32 = jnp.float32
BF16 = jnp.bfloat16
I32 = jnp.int32
I16 = jnp.int16

CHUNK = 64
CHUNK_SHIFT = 6
HEAD_DIM = 64
N_HEADS = 8
WIDTH = N_HEADS * HEAD_DIM
N_HEADS_IDX = 4
IDX_DIM = 64
TOPK_MAX = 256
IDX_SCALE = (IDX_DIM * N_HEADS_IDX) ** -0.5
BAND_CHUNKS = 8
REL_BACK = 128
T5_BUCKETS = 32
T5_MAX_DIST = 128
LN_EPS = 1e-5
NEG_INF = -1e30
LOG2E = math.log2(math.e)
V_ROWS = HEAD_DIM + 16
INT_MIN = -2 ** 31

ATT_TILE = 256
TOKEN_TILE = 512
FF_CHUNK = 256
VMEM_LIMIT_MB = 56


def _cparams(sem):
    return pltpu.CompilerParams(dimension_semantics=sem,
                                vmem_limit_bytes=VMEM_LIMIT_MB * 1024 * 1024)


def _const_spec(shape):
    nd = len(shape)
    return pl.BlockSpec(shape, lambda *_: (0,) * nd, pipeline_mode=pl.Buffered(1))


def _tree_sum(xs):
    xs = list(xs)
    while len(xs) > 1:
        nxt = [xs[a] + xs[a + 1] for a in range(0, len(xs) - 1, 2)]
        if len(xs) % 2:
            nxt.append(xs[-1])
        xs = nxt
    return xs[0]


def _layer_norm_rows(y, g, b):
    mu = jnp.mean(y, axis=-1, keepdims=True)
    d = y - mu
    var = jnp.mean(d * d, axis=-1, keepdims=True)
    return d * lax.rsqrt(var + LN_EPS) * g + b


def _ffn_ln_kernel(x_ref, wi_ref, wo_ref, g_ref, b_ref, o_ref, *, alpha, dff):
    x = x_ref[...]
    xb = x.astype(BF16)
    acc = jnp.zeros(x.shape, F32)
    for c in range(dff // FF_CHUNK):
        lo = c * FF_CHUNK
        a = jnp.dot(xb, wi_ref[:, lo:lo + FF_CHUNK], preferred_element_type=F32)
        u = jnp.dot(xb, wi_ref[:, dff + lo:dff + lo + FF_CHUNK], preferred_element_type=F32)
        hm = (a * jax.nn.sigmoid(a)) * u
        acc = acc + jnp.dot(hm.astype(BF16), wo_ref[lo:lo + FF_CHUNK, :],
                            preferred_element_type=F32)
    y = alpha * x + 0.5 * acc
    o_ref[...] = _layer_norm_rows(y, g_ref[...], b_ref[...])


def _ffn_ln(x, wi, wo, g, b, alpha):
    n, d = x.shape
    dff = wo.shape[0]
    tm = TOKEN_TILE
    return pl.pallas_call(
        functools.partial(_ffn_ln_kernel, alpha=alpha, dff=dff),
        out_shape=jax.ShapeDtypeStruct((n, d), F32),
        grid=(n // tm,),
        in_specs=[pl.BlockSpec((tm, d), lambda i: (i, 0)),
                  _const_spec((d, 2 * dff)), _const_spec((dff, d)),
                  _const_spec((1, d)), _const_spec((1, d))],
        out_specs=pl.BlockSpec((tm, d), lambda i: (i, 0)),
        compiler_params=_cparams(("parallel",)),
        name="ffn_ln",
    )(x, wi, wo, g.reshape(1, d), b.reshape(1, d))


_C_QA, _C_KA, _C_VA, _C_QI, _C_KI, _C_WI, _C_QB, _C_KB, _C_VB, _C_END = (
    0, 512, 1024, 1536, 1792, 1920, 2048, 2560, 3072, 3584)


def _inproj_kernel(h_ref, w_ref, qaT_ref, ka_ref, kab_ref, va_ref, vaT_ref, qiT_ref, ki_ref,
                   kk_ref, wiT_ref, qbT_ref, kbb_ref, vbT_ref, kbl_ref, vbl_ref, *, per):
    hb = h_ref[...].astype(BF16)

    def proj(c0, c1):
        return jnp.dot(hb, w_ref[:, c0:c1], preferred_element_type=F32)

    def put_tiles(ref, zt):
        tm = zt.shape[1]
        ones_slab = jnp.where(lax.broadcasted_iota(I32, (V_ROWS - HEAD_DIM, tm), 0) == 0, 1.0, 0.0)
        aug = jnp.concatenate(
            [piece for h in range(N_HEADS)
             for piece in (zt[h * HEAD_DIM:(h + 1) * HEAD_DIM, :], ones_slab)], axis=0)
        for c in range(ref.shape[0]):
            ref[c] = aug[:, c * ATT_TILE:(c + 1) * ATT_TILE].astype(ref.dtype)

    scale = HEAD_DIM ** -0.5 * LOG2E
    qaT_ref[...] = (proj(_C_QA, _C_KA) * scale).T.astype(BF16)
    ka = proj(_C_KA, _C_VA)
    ka_ref[...] = ka
    kab_ref[...] = ka.astype(BF16)
    va = proj(_C_VA, _C_QI)
    va_ref[...] = va
    put_tiles(vaT_ref, va.T)
    qiT_ref[...] = proj(_C_QI, _C_KI).T.astype(BF16)
    kk = proj(_C_KI, _C_WI)
    ki_ref[...] = kk[:, :IDX_DIM]
    kk_ref[...] = kk.astype(BF16)
    wiT_ref[...] = proj(_C_WI, _C_QB).T[:8, :]
    qbT_ref[...] = (proj(_C_QB, _C_KB) * scale).T.astype(BF16)
    kb = proj(_C_KB, _C_VB)
    kbb_ref[...] = kb.astype(BF16)
    vb = proj(_C_VB, _C_END)
    put_tiles(vbT_ref, vb.T)

    @pl.when(pl.program_id(0) % per == per - 1)
    def _():
        kbl_ref[...] = kb
        vbl_ref[...] = vb


def _in_proj(h, w, per):
    n, d = h.shape
    tm = TOKEN_TILE
    nt = n // tm
    sub = tm // ATT_TILE
    row = lambda width: pl.BlockSpec((tm, width), lambda i: (i, 0))
    col = lambda rows: pl.BlockSpec((rows, tm), lambda i: (0, i))
    til = pl.BlockSpec((sub, N_HEADS * V_ROWS, ATT_TILE), lambda i: (i, 0, 0))
    last = pl.BlockSpec((tm, WIDTH), lambda i: (i // per, 0))
    sds = jax.ShapeDtypeStruct
    out_shape = (
        sds((WIDTH, n), BF16),
        sds((n, WIDTH), F32), sds((n, WIDTH), BF16),
        sds((n, WIDTH), F32), sds((n // ATT_TILE, N_HEADS * V_ROWS, ATT_TILE), BF16),
        sds((N_HEADS_IDX * IDX_DIM, n), BF16),
        sds((n, IDX_DIM), F32), sds((n, 128), BF16),
        sds((8, n), F32),
        sds((WIDTH, n), BF16),
        sds((n, WIDTH), BF16),
        sds((n // ATT_TILE, N_HEADS * V_ROWS, ATT_TILE), BF16),
        sds((nt // per * tm, WIDTH), F32), sds((nt // per * tm, WIDTH), F32),
    )
    out_specs = (col(WIDTH), row(WIDTH), row(WIDTH), row(WIDTH), til,
                 col(N_HEADS_IDX * IDX_DIM), row(IDX_DIM), row(128), col(8),
                 col(WIDTH), row(WIDTH), til, last, last)
    return pl.pallas_call(
        functools.partial(_inproj_kernel, per=per),
        out_shape=out_shape,
        grid=(nt,),
        in_specs=[pl.BlockSpec((tm, d), lambda i: (i, 0)), _const_spec((d, _C_END))],
        out_specs=out_specs,
        compiler_params=_cparams(("arbitrary",)),
        name="in_proj",
    )(h, w)


def _flip(bits):
    return bits ^ (lax.shift_right_arithmetic(bits, 31) & 0x7FFFFFFF)


def _idx_mask_kernel(qiT_ref, wiT_ref, kk_ref, mask_ref, st_ref, hi_ref, lo_ref, d_ref, incl_ref,
                     *, tq, qpos0, n_valid, n_sel):
    tk = ATT_TILE
    nkt_total = st_ref.shape[0] // tk
    q0 = qpos0 + pl.program_id(1) * tq
    n_keys = jnp.minimum(q0 + tq, n_valid)
    nt = jnp.minimum(lax.shift_right_logical(n_keys + tk - 1, int(math.log2(tk))), nkt_total)

    lane = lax.broadcasted_iota(I32, (8, tq), 1)
    q_chunk = lax.shift_right_logical(q0 + lane, CHUNK_SHIFT)
    n_adm = jnp.minimum((q_chunk + 1) * CHUNK, n_valid)
    k_row = jnp.minimum(n_adm, n_sel)

    zeros = jnp.zeros((128 - IDX_DIM, tq), BF16)
    qh = [jnp.concatenate([qiT_ref[h * IDX_DIM:(h + 1) * IDX_DIM, :], zeros], axis=0)
          for h in range(N_HEADS_IDX)]
    scale_exact = math.frexp(IDX_SCALE)[0] == 0.5
    wh = [wiT_ref[h:h + 1, :] * IDX_SCALE if scale_exact else wiT_ref[h:h + 1, :]
          for h in range(N_HEADS_IDX)]

    def dots(j, h):
        r0 = pl.multiple_of(j * tk, tk)
        d_ref[h] = jnp.dot(kk_ref[pl.ds(r0, tk), :], qh[h], preferred_element_type=F32)

    def score_tile(j, last):
        r0 = pl.multiple_of(j * tk, tk)
        sc = None
        for h in range(N_HEADS_IDX):
            d = d_ref[h]
            if not last:
                dots(j + 1, h)
            term = wh[h] * jnp.maximum(d, 0.0)
            sc = term if sc is None else sc + term
        if not scale_exact:
            sc = sc * IDX_SCALE
        bits = lax.bitcast_convert_type(sc, I32)
        sign = lax.shift_right_arithmetic(bits, 31)
        key = (bits ^ (sign & 0x7FFFFFFF)) - sign
        if last:
            kpos = r0 + lax.broadcasted_iota(I32, (tk, tq), 0)
            key = jnp.where(kpos < n_adm[0:1, :], key, INT_MIN)
        st_ref[pl.ds(r0, tk), :] = key
        hi_ref[pl.ds(r0, tk), :] = lax.shift_right_arithmetic(key, 16).astype(I16)
        lo_ref[pl.ds(r0, tk), :] = (key ^ 0x8000).astype(I16)

    for h in range(N_HEADS_IDX):
        dots(0, h)

    def score_body(j, carry):
        score_tile(j, False)
        return carry

    lax.fori_loop(0, nt - 1, score_body, 0)
    score_tile(nt - 1, True)

    def count16(ref, pred, thr8):
        thr = jnp.concatenate([thr8, thr8], axis=0).astype(I16)

        def body(j, acc):
            r0 = pl.multiple_of(j * tk, tk)
            blk = ref[pl.ds(r0, tk), :]
            parts = [jnp.where(pred(blk[r * 16:(r + 1) * 16, :], thr), jnp.int16(1), jnp.int16(0))
                     for r in range(tk // 16)]
            return acc + _tree_sum(parts)

        acc = lax.fori_loop(0, nt, body, jnp.zeros((16, tq), I16))
        return jnp.broadcast_to(jnp.sum(acc.astype(I32), axis=0, keepdims=True), (8, tq))

    ge = lambda a, b: a >= b
    gt = lambda a, b: a > b

    def kth_largest16(ref, want):
        def step(k, t8):
            cand = t8 + lax.shift_left(jnp.int32(1), 15 - k)
            return jnp.where(count16(ref, ge, cand) >= want, cand, t8)
        return lax.fori_loop(0, 16, step, jnp.full((8, tq), -2 ** 15, I32))

    t_hi = kth_largest16(hi_ref, k_row)
    n_above = count16(hi_ref, gt, t_hi)
    t_hi16 = jnp.concatenate([t_hi, t_hi], axis=0).astype(I16)

    def park(j, carry):
        r0 = pl.multiple_of(j * tk, tk)
        hi, lo = hi_ref[pl.ds(r0, tk), :], lo_ref[pl.ds(r0, tk), :]
        lo_ref[pl.ds(r0, tk), :] = jnp.concatenate(
            [jnp.where(hi[r * 16:(r + 1) * 16, :] == t_hi16, lo[r * 16:(r + 1) * 16, :],
                       jnp.int16(-2 ** 15)) for r in range(tk // 16)], axis=0)
        return carry

    lax.fori_loop(0, nt, park, 0)
    t_lo = kth_largest16(lo_ref, k_row - n_above)
    n_gt = n_above + count16(lo_ref, gt, t_lo)
    t_row = (t_hi * 65536 + (t_lo + 2 ** 15))[0:1, :]
    ties_wanted = (k_row - n_gt)[0:1, :].astype(F32)

    tri = jnp.where(lax.broadcasted_iota(I32, (tk, tk), 0) >= lax.broadcasted_iota(I32, (tk, tk), 1),
                    1.0, 0.0).astype(BF16)

    def tie_prefix(j):
        r0 = pl.multiple_of(j * tk, tk)
        eq = jnp.where(st_ref[pl.ds(r0, tk), :] == t_row, 1.0, 0.0)
        incl_ref[...] = jnp.dot(tri, eq.astype(BF16), preferred_element_type=F32)

    def final_tile(j, seen, last):
        r0 = pl.multiple_of(j * tk, tk)
        incl = incl_ref[...]
        if not last:
            tie_prefix(j + 1)
        rank = seen + incl
        bar = t_row + jnp.where(rank <= ties_wanted, 0, 1)
        sel = jnp.where(st_ref[pl.ds(r0, tk), :] >= bar, 1, 0)
        mask_ref[pl.ds(r0, tk), :] = sel.astype(jnp.int8)
        return seen + incl[tk - 1:tk, :]

    tie_prefix(0)
    seen = lax.fori_loop(0, nt - 1, lambda j, seen: final_tile(j, seen, False),
                         jnp.zeros((1, tq), F32))
    final_tile(nt - 1, seen, True)

    def zero_tile(j, carry):
        r0 = pl.multiple_of(j * tk, tk)
        mask_ref[pl.ds(r0, tk), :] = jnp.zeros((tk, tq), jnp.int8)
        return carry

    lax.fori_loop(nt, nkt_total, zero_tile, 0)


def _idx_mask(qiT, wiT, kk, *, nb, tq, qpos0, n_valid, n_sel):
    tkp = kk.shape[1]
    nq = qiT.shape[1] // nb // tq
    return pl.pallas_call(
        functools.partial(_idx_mask_kernel, tq=tq, qpos0=qpos0, n_valid=n_valid, n_sel=n_sel),
        out_shape=jax.ShapeDtypeStruct((nb, tkp, nq * tq), jnp.int8),
        grid=(nb, nq),
        in_specs=[pl.BlockSpec((N_HEADS_IDX * IDX_DIM, tq), lambda b, i: (0, b * nq + i)),
                  pl.BlockSpec((8, tq), lambda b, i: (0, b * nq + i)),
                  pl.BlockSpec((None, tkp, 128), lambda b, i: (b, 0, 0))],
        out_specs=pl.BlockSpec((None, tkp, tq), lambda b, i: (b, 0, i)),
        scratch_shapes=[pltpu.VMEM((tkp, tq), I32), pltpu.VMEM((tkp, tq), I16),
                        pltpu.VMEM((tkp, tq), I16), pltpu.VMEM((N_HEADS_IDX, ATT_TILE, tq), F32),
                        pltpu.VMEM((ATT_TILE, tq), F32)],
        compiler_params=_cparams(("parallel", "parallel")),
        name="idx_mask",
    )(qiT, wiT, kk)


def _load_qsel(qT_ref, qsel_ref, tq):
    zeros = jnp.zeros((HEAD_DIM, tq), BF16)
    for h in range(N_HEADS):
        blk = qT_ref[h * HEAD_DIM:(h + 1) * HEAD_DIM, :]
        qsel_ref[h] = jnp.concatenate([blk, zeros] if h % 2 == 0 else [zeros, blk], axis=0)


def _init_state(m_ref, acc_ref):
    m_ref[...] = jnp.full(m_ref.shape, NEG_INF, F32)
    acc_ref[...] = jnp.zeros(acc_ref.shape, F32)


def _scores(k_slab, qsel_ref, s_ref, h):
    s_ref[h] = jnp.dot(k_slab(h // 2), qsel_ref[h], preferred_element_type=F32)


def _heads_tile(v_rows, next_k_slab, qsel_ref, s_ref, m_ref, acc_ref, addend):
    for h in range(N_HEADS):
        s = s_ref[h]
        for a in addend(h):
            s = s + a
        if next_k_slab is not None:
            _scores(next_k_slab, qsel_ref, s_ref, h)
        m_old = m_ref[h:h + 1, :]
        m_new = jnp.maximum(m_old, jnp.max(s, axis=0, keepdims=True))
        alpha = jnp.exp2(m_old - m_new)
        p = jnp.exp2(s - m_new).astype(BF16)
        m_ref[h:h + 1, :] = m_new
        pv = jnp.dot(v_rows(h), p, preferred_element_type=F32)
        rows = slice(h * V_ROWS, (h + 1) * V_ROWS)
        acc_ref[rows, :] = alpha * acc_ref[rows, :] + pv


def _attn_scratch(tq):
    return [pltpu.VMEM((N_HEADS, 128, tq), BF16), pltpu.VMEM((N_HEADS, ATT_TILE, tq), F32),
            pltpu.VMEM((N_HEADS, tq), F32), pltpu.VMEM((N_HEADS * V_ROWS, tq), F32)]


def _store_out(o_ref, acc_ref):
    outs = [acc_ref[h * V_ROWS:h * V_ROWS + HEAD_DIM, :]
            / acc_ref[h * V_ROWS + HEAD_DIM:h * V_ROWS + HEAD_DIM + 1, :] for h in range(N_HEADS)]
    o_ref[...] = jnp.concatenate(outs, axis=0).T.astype(o_ref.dtype)


def _dsa_kernel(qT_ref, k_ref, vT_ref, mask_ref, bias_ref, o_ref, qsel_ref, s_ref, m_ref, acc_ref,
                *, tq, qpos0):
    tk = ATT_TILE
    jd = lax.shift_right_logical(qpos0 + pl.program_id(1) * tq, int(math.log2(tk)))
    _load_qsel(qT_ref, qsel_ref, tq)
    _init_state(m_ref, acc_ref)

    def k_slab(j):
        r0 = pl.multiple_of(j * tk, tk)
        return lambda g: k_ref[pl.ds(r0, tk), g * 128:(g + 1) * 128]

    def tile(j, near, last=False):
        r0 = pl.multiple_of(j * tk, tk)
        unselected = jnp.where(mask_ref[pl.ds(r0, tk), :].astype(I32) != 0, 0.0, NEG_INF)
        addend = ((lambda h: (unselected,)) if near is None
                  else (lambda h: (unselected, bias_ref[near, h])))
        _heads_tile(lambda h: vT_ref[j, h * V_ROWS:(h + 1) * V_ROWS, :],
                    None if last else k_slab(j + 1),
                    qsel_ref, s_ref, m_ref, acc_ref, addend)

    for h in range(N_HEADS):
        _scores(k_slab(0), qsel_ref, s_ref, h)

    def far(j, carry):
        tile(j, None)
        return carry

    lax.fori_loop(0, jnp.maximum(jd - 1, 0), far, 0)

    @pl.when(jd >= 1)
    def _():
        tile(jd - 1, 0)

    tile(jd, 1, last=True)
    _store_out(o_ref, acc_ref)


def _dsa_attn(qT, k, vT, maskT, bias, *, nb, tq, qpos0):
    tkp = k.shape[1]
    nkt = tkp // ATT_TILE
    nq = qT.shape[1] // nb // tq
    return pl.pallas_call(
        functools.partial(_dsa_kernel, tq=tq, qpos0=qpos0),
        out_shape=jax.ShapeDtypeStruct((nb * nq * tq, WIDTH), BF16),
        grid=(nb, nq),
        in_specs=[pl.BlockSpec((WIDTH, tq), lambda b, i: (0, b * nq + i)),
                  pl.BlockSpec((None, tkp, WIDTH), lambda b, i: (b, 0, 0),
                               pipeline_mode=pl.Buffered(1)),
                  pl.BlockSpec((nkt, N_HEADS * V_ROWS, ATT_TILE), lambda b, i: (b, 0, 0),
                               pipeline_mode=pl.Buffered(1)),
                  pl.BlockSpec((None, tkp, tq), lambda b, i: (b, 0, i)),
                  _const_spec(bias.shape)],
        out_specs=pl.BlockSpec((tq, WIDTH), lambda b, i: (b * nq + i, 0)),
        scratch_shapes=_attn_scratch(tq),
        compiler_params=_cparams(("parallel", "parallel")),
        name="dsa_attn",
    )(qT, k, vT, maskT, bias)


_BAND_TILES = BAND_CHUNKS * CHUNK // ATT_TILE + 1


def _band_kernel(qT_ref, *refs, tq, off):
    k_refs = refs[:_BAND_TILES]
    v_refs = refs[_BAND_TILES:2 * _BAND_TILES]
    bias_ref, o_ref, qsel_ref, s_ref, m_ref, acc_ref = refs[2 * _BAND_TILES:]
    _load_qsel(qT_ref, qsel_ref, tq)
    _init_state(m_ref, acc_ref)
    k_slab = lambda w: (lambda g: k_refs[w][:, g * 128:(g + 1) * 128])
    for h in range(N_HEADS):
        _scores(k_slab(0), qsel_ref, s_ref, h)
    for w in range(_BAND_TILES):
        entry = jnp.where(pl.program_id(1) + off + w >= 0, w, _BAND_TILES)
        _heads_tile(lambda h: v_refs[w][0, h * V_ROWS:(h + 1) * V_ROWS, :],
                    None if w == _BAND_TILES - 1 else k_slab(w + 1),
                    qsel_ref, s_ref, m_ref, acc_ref, lambda h: (bias_ref[entry, h],))
    _store_out(o_ref, acc_ref)


def _band_attn(qT, k, vT, bias, *, nb, tq, off):
    nkt = k.shape[1] // ATT_TILE
    nq = qT.shape[1] // nb // tq
    tile_of = lambda i, w: jnp.maximum(i + off + w, 0)
    k_specs = [pl.BlockSpec((None, ATT_TILE, WIDTH), lambda b, i, w=w: (b, tile_of(i, w), 0))
               for w in range(_BAND_TILES)]
    v_specs = [pl.BlockSpec((1, N_HEADS * V_ROWS, ATT_TILE),
                            lambda b, i, w=w: (b * nkt + tile_of(i, w), 0, 0))
               for w in range(_BAND_TILES)]
    return pl.pallas_call(
        functools.partial(_band_kernel, tq=tq, off=off),
        out_shape=jax.ShapeDtypeStruct((nb * nq * tq, WIDTH), BF16),
        grid=(nb, nq),
        in_specs=[pl.BlockSpec((WIDTH, tq), lambda b, i: (0, b * nq + i))] + k_specs + v_specs
                 + [_const_spec(bias.shape)],
        out_specs=pl.BlockSpec((tq, WIDTH), lambda b, i: (b * nq + i, 0)),
        scratch_shapes=_attn_scratch(tq),
        compiler_params=_cparams(("parallel", "parallel")),
        name="band_attn",
    )(qT, *([k] * _BAND_TILES), *([vT] * _BAND_TILES), bias)


def _mix_ln_kernel(h_ref, oa_ref, ob_ref, wg_ref, wba_ref, wbb_ref, wo_ref, g_ref, b_ref, o_ref,
                   *, alpha):
    h = h_ref[...]
    hb = h.astype(BF16)
    d = h.shape[1]
    ga = jnp.dot(hb, wg_ref[:, :d], preferred_element_type=F32)
    gb = jnp.dot(hb, wg_ref[:, d:], preferred_element_type=F32)
    ya = jnp.dot(oa_ref[...], wba_ref[...], preferred_element_type=F32)
    yb = jnp.dot(ob_ref[...], wbb_ref[...], preferred_element_type=F32)
    gated = jax.nn.sigmoid(ga) * ya + jax.nn.sigmoid(gb) * yb
    mix = jnp.dot(gated.astype(BF16), wo_ref[...], preferred_element_type=F32)
    o_ref[...] = _layer_norm_rows(alpha * h + mix, g_ref[...], b_ref[...])


def _mix_ln(h, oa, ob, wg, wba, wbb, wo, g, b, alpha):
    n, d = h.shape
    tm = TOKEN_TILE
    row = lambda width: pl.BlockSpec((tm, width), lambda i: (i, 0))
    return pl.pallas_call(
        functools.partial(_mix_ln_kernel, alpha=alpha),
        out_shape=jax.ShapeDtypeStruct((n, d), F32),
        grid=(n // tm,),
        in_specs=[row(d), row(WIDTH), row(WIDTH), _const_spec(wg.shape), _const_spec(wba.shape),
                  _const_spec(wbb.shape), _const_spec(wo.shape), _const_spec((1, d)),
                  _const_spec((1, d))],
        out_specs=row(d),
        compiler_params=_cparams(("parallel",)),
        name="mix_ln",
    )(h, oa, ob, wg, wba, wbb, wo, g.reshape(1, d), b.reshape(1, d))


def _t5_bucket(rel):
    half = T5_BUCKETS // 2
    exact = half // 2
    n = jnp.abs(rel)
    log_ratio = jnp.log(jnp.maximum(n, 1).astype(F32) / exact) / math.log(T5_MAX_DIST / exact)
    large = jnp.minimum(exact + (log_ratio * (half - exact)).astype(I32), half - 1)
    return (rel > 0).astype(I32) * half + jnp.where(n < exact, n, large)


def _rel_line(n_tiles, first, tq):
    period = ATT_TILE + tq
    y = jnp.arange(period, dtype=I32)[None, :]
    c = (jnp.arange(n_tiles, dtype=I32)[:, None] + first) * ATT_TILE
    return jnp.where(y < tq, c - y, c + period - y)


def _toeplitz(v, tq):
    period = v.shape[-1]
    flat = jnp.tile(v, (1,) * (v.ndim - 1) + (ATT_TILE,))[..., :ATT_TILE * (period - 1)]
    return flat.reshape(v.shape[:-1] + (ATT_TILE, period - 1))[..., :tq]


def _dsa_bias_table(t5_bias, tq):
    assert T5_MAX_DIST <= ATT_TILE + 1
    far = t5_bias[:, _t5_bucket(jnp.int32(-(ATT_TILE + 1)))]
    line = t5_bias[:, _t5_bucket(_rel_line(2, -1, tq))] - far[:, None, None]
    return _toeplitz(jnp.moveaxis(line, 0, 1), tq) * LOG2E


def _band_bias_table(rel_bias, tq):
    first = 1 - _BAND_TILES
    ridx = jnp.clip(_rel_line(_BAND_TILES, first, tq), -REL_BACK, CHUNK - 1) + REL_BACK
    bias = _toeplitz(jnp.moveaxis(rel_bias[:, ridx], 0, 1), tq)
    w = jnp.arange(_BAND_TILES, dtype=I32)[:, None, None]
    j = jnp.arange(ATT_TILE, dtype=I32)[None, :, None]
    i = jnp.arange(tq, dtype=I32)[None, None, :]
    dchunk = ((w + first) * ATT_TILE + j) // CHUNK - i // CHUNK
    ok = (dchunk <= 0) & (dchunk >= -BAND_CHUNKS)
    table = jnp.where(ok[:, None], bias * LOG2E, NEG_INF)
    return jnp.concatenate([table, jnp.full_like(table[:1], NEG_INF)], axis=0)


def _regroup_w_in(w_in):
    sizes = (WIDTH, WIDTH, WIDTH, N_HEADS_IDX * IDX_DIM, IDX_DIM, N_HEADS_IDX,
             WIDTH, WIDTH, WIDTH, w_in.shape[0], w_in.shape[0])
    offs = [0]
    for s in sizes:
        offs.append(offs[-1] + s)
    qa, ka, va, qi, ki, wi, qb, kb, vb, ga, gb = (w_in[:, offs[t]:offs[t + 1]] for t in range(11))
    wi_pad = jnp.pad(wi, ((0, 0), (0, 128 - N_HEADS_IDX)))
    w_proj = jnp.concatenate([qa, ka, va, qi, ki, ki, wi_pad, qb, kb, vb], axis=1).astype(BF16)
    w_gate = jnp.concatenate([ga, gb], axis=1).astype(BF16)
    return w_proj, w_gate


def _key_tiles_T(v):
    nb, tkp, wd = v.shape
    vt = v.reshape(nb, tkp // ATT_TILE, ATT_TILE, N_HEADS, HEAD_DIM).transpose(0, 1, 3, 4, 2)
    ones_slab = jnp.zeros(vt.shape[:3] + (V_ROWS - HEAD_DIM, ATT_TILE), v.dtype).at[..., 0, :].set(1)
    return jnp.concatenate([vt, ones_slab], axis=3).reshape(-1, N_HEADS * V_ROWS, ATT_TILE)


def _pad_keys(x, tkp):
    return jnp.pad(x, ((0, 0), (0, tkp - x.shape[1]), (0, 0)))


def kernel(x_prompt, x_sample, cache_k_a, cache_v_a, cache_kidx_a, cache_k_b, cache_v_b, t5_bias,
           ln1_g, ln1_b, ffn1_wi, ffn1_wo, ln2_g, ln2_b, w_in, rel_bias_b, w_branch_a, w_branch_b,
           w_out, ln3_g, ln3_b, ffn2_wi, ffn2_wo):
    depth = ln1_g.shape[0]
    alpha = (2.0 * depth) ** 0.25
    nbp, seq, d = x_prompt.shape
    nbs, dec, _ = x_sample.shape
    past = cache_k_a.shape[2]
    band = cache_k_b.shape[2]
    keep = min(BAND_CHUNKS * CHUNK, seq)
    tq_p, tq_s = ATT_TILE, 128
    assert seq % TOKEN_TILE == 0 and keep == TOKEN_TILE and dec <= tq_s and past % ATT_TILE == 0
    assert band == BAND_CHUNKS * CHUNK and (nbs * tq_s) % TOKEN_TILE == 0

    yp = x_prompt.reshape(nbp * seq, d)
    ys = jnp.pad(x_sample, ((0, 0), (0, tq_s - dec), (0, 0))).reshape(nbs * tq_s, d)
    dsa_bias_p = _dsa_bias_table(t5_bias, tq_p)
    dsa_bias_s = dsa_bias_p[..., :tq_s]
    n_sel_p = min(TOPK_MAX, seq // 4)
    n_sel_s = min(TOPK_MAX, (past + dec) // 4)
    tk_s = -(-(past + tq_s) // ATT_TILE) * ATT_TILE
    tkb_s = _BAND_TILES * ATT_TILE

    st_p, st_s = [], []
    for l in range(depth):
        w_proj, w_gate = _regroup_w_in(w_in[l])
        wi1, wo1 = ffn1_wi[l].astype(BF16), ffn1_wo[l].astype(BF16)
        wi2, wo2 = ffn2_wi[l].astype(BF16), ffn2_wo[l].astype(BF16)
        wba, wbb, wo = (w_branch_a[l].astype(BF16), w_branch_b[l].astype(BF16),
                        w_out[l].astype(BF16))
        band_bias_p = _band_bias_table(rel_bias_b[l], tq_p)
        band_bias_s = band_bias_p[..., :tq_s]

        h = _ffn_ln(yp, wi1, wo1, ln1_g[l], ln1_b[l], alpha)
        (qaT, ka, kab, va, vaT, qiT, ki, kk, wiT, qbT, kbb, vbT, kbl, vbl) = _in_proj(
            h, w_proj, seq // TOKEN_TILE)
        maskT = _idx_mask(qiT, wiT, kk.reshape(nbp, seq, 128), nb=nbp, tq=tq_p, qpos0=0,
                          n_valid=seq, n_sel=n_sel_p)
        oa = _dsa_attn(qaT, kab.reshape(nbp, seq, WIDTH), vaT, maskT, dsa_bias_p,
                       nb=nbp, tq=tq_p, qpos0=0)
        ob = _band_attn(qbT, kbb.reshape(nbp, seq, WIDTH), vbT, band_bias_p,
                        nb=nbp, tq=tq_p, off=1 - _BAND_TILES)
        h2 = _mix_ln(h, oa, ob, w_gate, wba, wbb, wo, ln2_g[l], ln2_b[l], alpha)
        yp = _ffn_ln(h2, wi2, wo2, ln3_g[l], ln3_b[l], alpha)
        st_p.append((ka.reshape(nbp, seq, N_HEADS, HEAD_DIM), va.reshape(nbp, seq, N_HEADS, HEAD_DIM),
                     ki.reshape(nbp, seq, IDX_DIM), kbl.reshape(nbp, keep, N_HEADS, HEAD_DIM),
                     vbl.reshape(nbp, keep, N_HEADS, HEAD_DIM)))

        h = _ffn_ln(ys, wi1, wo1, ln1_g[l], ln1_b[l], alpha)
        (qaT, ka, _, va, _, qiT, ki, _, wiT, qbT, _, _, kbl, vbl) = _in_proj(h, w_proj, 1)
        new = lambda a: a.reshape(nbs, tq_s, -1)[:, :dec]
        ka_n, va_n, ki_n, kb_n, vb_n = new(ka), new(va), new(ki), new(kbl), new(vbl)
        ka_all = jnp.concatenate([cache_k_a[l].reshape(nbs, past, WIDTH), ka_n], axis=1)
        va_all = jnp.concatenate([cache_v_a[l].reshape(nbs, past, WIDTH), va_n], axis=1)
        ki_all = jnp.concatenate([cache_kidx_a[l], ki_n], axis=1)
        kb_all = jnp.concatenate([cache_k_b[l].reshape(nbs, band, WIDTH), kb_n], axis=1)
        vb_all = jnp.concatenate([cache_v_b[l].reshape(nbs, band, WIDTH), vb_n], axis=1)
        kk_s = _pad_keys(jnp.concatenate([ki_all, ki_all], axis=2).astype(BF16), tk_s)
        maskT = _idx_mask(qiT, wiT, kk_s, nb=nbs, tq=tq_s, qpos0=past, n_valid=past + dec,
                          n_sel=n_sel_s)
        oa = _dsa_attn(qaT, _pad_keys(ka_all.astype(BF16), tk_s),
                       _key_tiles_T(_pad_keys(va_all.astype(BF16), tk_s)), maskT, dsa_bias_s,
                       nb=nbs, tq=tq_s, qpos0=past)
        ob = _band_attn(qbT, _pad_keys(kb_all.astype(BF16), tkb_s),
                        _key_tiles_T(_pad_keys(vb_all.astype(BF16), tkb_s)), band_bias_s,
                        nb=nbs, tq=tq_s, off=0)
        h2 = _mix_ln(h, oa, ob, w_gate, wba, wbb, wo, ln2_g[l], ln2_b[l], alpha)
        ys = _ffn_ln(h2, wi2, wo2, ln3_g[l], ln3_b[l], alpha)
        heads = lambda a: a.reshape(nbs, -1, N_HEADS, HEAD_DIM)
        st_s.append((heads(ka_n), heads(va_n), ki_n, heads(kb_all[:, -band:]),
                     heads(vb_all[:, -band:])))

    y_prompt = yp.reshape(nbp, seq, d)
    y_sample = ys.reshape(nbs, tq_s, d)[:, :dec]
    stack = lambda sts, t: jnp.stack([s[t] for s in sts])
    return (y_prompt, y_sample,
            stack(st_p, 0), stack(st_p, 1), stack(st_p, 2), stack(st_p, 3), stack(st_p, 4),
            stack(st_s, 0), stack(st_s, 1), stack(st_s, 2), stack(st_s, 3), stack(st_s, 4))
```

```python
import functools
import math

import jax
import jax.numpy as jnp
from jax import lax
from jax.experimental import pallas as pl
from jax.experimental.pallas import tpu as pltpu

F32 = jnp.float32
BF16 = jnp.bfloat16
I32 = jnp.int32
I16 = jnp.int16

CHUNK = 64
CHUNK_SHIFT = 6
HEAD_DIM = 64
N_HEADS = 8
WIDTH = N_HEADS * HEAD_DIM
N_HEADS_IDX = 4
IDX_DIM = 64
TOPK_MAX = 256
IDX_SCALE = (IDX_DIM * N_HEADS_IDX) ** -0.5
BAND_CHUNKS = 8
REL_BACK = 128
T5_BUCKETS = 32
T5_MAX_DIST = 128
LN_EPS = 1e-5
NEG_INF = -1e30
LOG2E = math.log2(math.e)
V_ROWS = HEAD_DIM + 16
INT_MIN = -2 ** 31

ATT_TILE = 256
TOKEN_TILE = 512
FF_CHUNK = 256
VMEM_LIMIT_MB = 56


def _cparams(sem):
    return pltpu.CompilerParams(dimension_semantics=sem,
                                vmem_limit_bytes=VMEM_LIMIT_MB * 1024 * 1024)


def _const_spec(shape):
    nd = len(shape)
    return pl.BlockSpec(shape, lambda *_: (0,) * nd, pipeline_mode=pl.Buffered(1))


def _tree_sum(xs):
    xs = list(xs)
    while len(xs) > 1:
        nxt = [xs[a] + xs[a + 1] for a in range(0, len(xs) - 1, 2)]
        if len(xs) % 2:
            nxt.append(xs[-1])
        xs = nxt
    return xs[0]


def _layer_norm_rows(y, g, b):
    mu = jnp.mean(y, axis=-1, keepdims=True)
    d = y - mu
    var = jnp.mean(d * d, axis=-1, keepdims=True)
    return d * lax.rsqrt(var + LN_EPS) * g + b


def _ffn_ln_kernel(x_ref, wi_ref, wo_ref, g_ref, b_ref, o_ref, *, alpha, dff):
    x = x_ref[...]
    xb = x.astype(BF16)
    acc = jnp.zeros(x.shape, F32)
    for c in range(dff // FF_CHUNK):
        lo = c * FF_CHUNK
        a = jnp.dot(xb, wi_ref[:, lo:lo + FF_CHUNK], preferred_element_type=F32)
        u = jnp.dot(xb, wi_ref[:, dff + lo:dff + lo + FF_CHUNK], preferred_element_type=F32)
        hm = (a * jax.nn.sigmoid(a)) * u
        acc = acc + jnp.dot(hm.astype(BF16), wo_ref[lo:lo + FF_CHUNK, :],
                            preferred_element_type=F32)
    y = alpha * x + 0.5 * acc
    o_ref[...] = _layer_norm_rows(y, g_ref[...], b_ref[...])


def _ffn_ln(x, wi, wo, g, b, alpha):
    n, d = x.shape
    dff = wo.shape[0]
    tm = TOKEN_TILE
    return pl.pallas_call(
        functools.partial(_ffn_ln_kernel, alpha=alpha, dff=dff),
        out_shape=jax.ShapeDtypeStruct((n, d), F32),
        grid=(n // tm,),
        in_specs=[pl.BlockSpec((tm, d), lambda i: (i, 0)),
                  _const_spec((d, 2 * dff)), _const_spec((dff, d)),
                  _const_spec((1, d)), _const_spec((1, d))],
        out_specs=pl.BlockSpec((tm, d), lambda i: (i, 0)),
        compiler_params=_cparams(("parallel",)),
        name="ffn_ln",
    )(x, wi, wo, g.reshape(1, d), b.reshape(1, d))


_C_QA, _C_KA, _C_VA, _C_QI, _C_KI, _C_WI, _C_QB, _C_KB, _C_VB, _C_END = (
    0, 512, 1024, 1536, 1792, 1920, 2048, 2560, 3072, 3584)


def _inproj_kernel(h_ref, w_ref, qaT_ref, ka_ref, kab_ref, va_ref, vaT_ref, qiT_ref, ki_ref,
                   kk_ref, wiT_ref, qbT_ref, kbb_ref, vbT_ref, kbl_ref, vbl_ref, *, per):
    hb = h_ref[...].astype(BF16)

    def proj(c0, c1):
        return jnp.dot(hb, w_ref[:, c0:c1], preferred_element_type=F32)

    def put_tiles(ref, zt):
        tm = zt.shape[1]
        ones_slab = jnp.where(lax.broadcasted_iota(I32, (V_ROWS - HEAD_DIM, tm), 0) == 0, 1.0, 0.0)
        aug = jnp.concatenate(
            [piece for h in range(N_HEADS)
             for piece in (zt[h * HEAD_DIM:(h + 1) * HEAD_DIM, :], ones_slab)], axis=0)
        for c in range(ref.shape[0]):
            ref[c] = aug[:, c * ATT_TILE:(c + 1) * ATT_TILE].astype(ref.dtype)

    scale = HEAD_DIM ** -0.5 * LOG2E
    qaT_ref[...] = (proj(_C_QA, _C_KA) * scale).T.astype(BF16)
    ka = proj(_C_KA, _C_VA)
    ka_ref[...] = ka
    kab_ref[...] = ka.astype(BF16)
    va = proj(_C_VA, _C_QI)
    va_ref[...] = va
    put_tiles(vaT_ref, va.T)
    qiT_ref[...] = proj(_C_QI, _C_KI).T.astype(BF16)
    kk = proj(_C_KI, _C_WI)
    ki_ref[...] = kk[:, :IDX_DIM]
    kk_ref[...] = kk.astype(BF16)
    wiT_ref[...] = proj(_C_WI, _C_QB).T[:8, :]
    qbT_ref[...] = (proj(_C_QB, _C_KB) * scale).T.astype(BF16)
    kb = proj(_C_KB, _C_VB)
    kbb_ref[...] = kb.astype(BF16)
    vb = proj(_C_VB, _C_END)
    put_tiles(vbT_ref, vb.T)

    @pl.when(pl.program_id(0) % per == per - 1)
    def _():
        kbl_ref[...] = kb
        vbl_ref[...] = vb


def _in_proj(h, w, per):
    n, d = h.shape
    tm = TOKEN_TILE
    nt = n // tm
    sub = tm // ATT_TILE
    row = lambda width: pl.BlockSpec((tm, width), lambda i: (i, 0))
    col = lambda rows: pl.BlockSpec((rows, tm), lambda i: (0, i))
    til = pl.BlockSpec((sub, N_HEADS * V_ROWS, ATT_TILE), lambda i: (i, 0, 0))
    last = pl.BlockSpec((tm, WIDTH), lambda i: (i // per, 0))
    sds = jax.ShapeDtypeStruct
    out_shape = (
        sds((WIDTH, n), BF16),
        sds((n, WIDTH), F32), sds((n, WIDTH), BF16),
        sds((n, WIDTH), F32), sds((n // ATT_TILE, N_HEADS * V_ROWS, ATT_TILE), BF16),
        sds((N_HEADS_IDX * IDX_DIM, n), BF16),
        sds((n, IDX_DIM), F32), sds((n, 128), BF16),
        sds((8, n), F32),
        sds((WIDTH, n), BF16),
        sds((n, WIDTH), BF16),
        sds((n // ATT_TILE, N_HEADS * V_ROWS, ATT_TILE), BF16),
        sds((nt // per * tm, WIDTH), F32), sds((nt // per * tm, WIDTH), F32),
    )
    out_specs = (col(WIDTH), row(WIDTH), row(WIDTH), row(WIDTH), til,
                 col(N_HEADS_IDX * IDX_DIM), row(IDX_DIM), row(128), col(8),
                 col(WIDTH), row(WIDTH), til, last, last)
    return pl.pallas_call(
        functools.partial(_inproj_kernel, per=per),
        out_shape=out_shape,
        grid=(nt,),
        in_specs=[pl.BlockSpec((tm, d), lambda i: (i, 0)), _const_spec((d, _C_END))],
        out_specs=out_specs,
        compiler_params=_cparams(("arbitrary",)),
        name="in_proj",
    )(h, w)


def _flip(bits):
    return bits ^ (lax.shift_right_arithmetic(bits, 31) & 0x7FFFFFFF)


def _idx_mask_kernel(qiT_ref, wiT_ref, kk_ref, mask_ref, st_ref, hi_ref, lo_ref, d_ref, incl_ref,
                     *, tq, qpos0, n_valid, n_sel):
    tk = ATT_TILE
    nkt_total = st_ref.shape[0] // tk
    q0 = qpos0 + pl.program_id(1) * tq
    n_keys = jnp.minimum(q0 + tq, n_valid)
    nt = jnp.minimum(lax.shift_right_logical(n_keys + tk - 1, int(math.log2(tk))), nkt_total)

    lane = lax.broadcasted_iota(I32, (8, tq), 1)
    q_chunk = lax.shift_right_logical(q0 + lane, CHUNK_SHIFT)
    n_adm = jnp.minimum((q_chunk + 1) * CHUNK, n_valid)
    k_row = jnp.minimum(n_adm, n_sel)

    zeros = jnp.zeros((128 - IDX_DIM, tq), BF16)
    qh = [jnp.concatenate([qiT_ref[h * IDX_DIM:(h + 1) * IDX_DIM, :], zeros], axis=0)
          for h in range(N_HEADS_IDX)]
    scale_exact = math.frexp(IDX_SCALE)[0] == 0.5
    wh = [wiT_ref[h:h + 1, :] * IDX_SCALE if scale_exact else wiT_ref[h:h + 1, :]
          for h in range(N_HEADS_IDX)]

    def dots(j, h):
        r0 = pl.multiple_of(j * tk, tk)
        d_ref[h] = jnp.dot(kk_ref[pl.ds(r0, tk), :], qh[h], preferred_element_type=F32)

    def score_tile(j, last):
        r0 = pl.multiple_of(j * tk, tk)
        sc = None
        for h in range(N_HEADS_IDX):
            d = d_ref[h]
            if not last:
                dots(j + 1, h)
            term = wh[h] * jnp.maximum(d, 0.0)
            sc = term if sc is None else sc + term
        if not scale_exact:
            sc = sc * IDX_SCALE
        bits = lax.bitcast_convert_type(sc, I32)
        sign = lax.shift_right_arithmetic(bits, 31)
        key = (bits ^ (sign & 0x7FFFFFFF)) - sign
        if last:
            kpos = r0 + lax.broadcasted_iota(I32, (tk, tq), 0)
            key = jnp.where(kpos < n_adm[0:1, :], key, INT_MIN)
        st_ref[pl.ds(r0, tk), :] = key
        hi_ref[pl.ds(r0, tk), :] = lax.shift_right_arithmetic(key, 16).astype(I16)
        lo_ref[pl.ds(r0, tk), :] = (key ^ 0x8000).astype(I16)

    n_pairs = lax.shift_right_logical(nt + 1, 1)

    @pl.when((nt & 1) == 1)
    def _():
        r0 = pl.multiple_of(nt * tk, tk)
        hi_ref[pl.ds(r0, tk), :] = jnp.full((tk, tq), -2 ** 15, I16)
        lo_ref[pl.ds(r0, tk), :] = jnp.full((tk, tq), -2 ** 15, I16)

    for h in range(N_HEADS_IDX):
        dots(0, h)

    def score_body(j, carry):
        score_tile(j, False)
        return carry

    lax.fori_loop(0, nt - 1, score_body, 0)
    score_tile(nt - 1, True)

    def count16(ref, pred, thr8):
        thr = jnp.concatenate([thr8, thr8], axis=0).astype(I16)

        def body(u, acc):
            r0 = pl.multiple_of(u * (2 * tk), 2 * tk)
            blk = ref[pl.ds(r0, 2 * tk), :]
            parts = [jnp.where(pred(blk[r * 16:(r + 1) * 16, :], thr), jnp.int16(1), jnp.int16(0))
                     for r in range(2 * tk // 16)]
            return acc + _tree_sum(parts)

        acc = lax.fori_loop(0, n_pairs, body, jnp.zeros((16, tq), I16))
        return jnp.broadcast_to(jnp.sum(acc.astype(I32), axis=0, keepdims=True), (8, tq))

    ge = lambda a, b: a >= b
    gt = lambda a, b: a > b

    def kth_largest16(ref, want):
        def step(k, t8):
            cand = t8 + lax.shift_left(jnp.int32(1), 15 - k)
            return jnp.where(count16(ref, ge, cand) >= want, cand, t8)
        return lax.fori_loop(0, 16, step, jnp.full((8, tq), -2 ** 15, I32))

    t_hi = kth_largest16(hi_ref, k_row)
    n_above = count16(hi_ref, gt, t_hi)
    t_hi16 = jnp.concatenate([t_hi, t_hi], axis=0).astype(I16)

    def park(j, carry):
        r0 = pl.multiple_of(j * tk, tk)
        hi, lo = hi_ref[pl.ds(r0, tk), :], lo_ref[pl.ds(r0, tk), :]
        lo_ref[pl.ds(r0, tk), :] = jnp.concatenate(
            [jnp.where(hi[r * 16:(r + 1) * 16, :] == t_hi16, lo[r * 16:(r + 1) * 16, :],
                       jnp.int16(-2 ** 15)) for r in range(tk // 16)], axis=0)
        return carry

    lax.fori_loop(0, nt, park, 0)
    t_lo = kth_largest16(lo_ref, k_row - n_above)
    n_gt = n_above + count16(lo_ref, gt, t_lo)
    t_row = (t_hi * 65536 + (t_lo + 2 ** 15))[0:1, :]
    ties_wanted = (k_row - n_gt)[0:1, :].astype(F32)

    tri = jnp.where(lax.broadcasted_iota(I32, (tk, tk), 0) >= lax.broadcasted_iota(I32, (tk, tk), 1),
                    1.0, 0.0).astype(BF16)

    def tie_prefix(j):
        r0 = pl.multiple_of(j * tk, tk)
        eq = jnp.where(st_ref[pl.ds(r0, tk), :] == t_row, 1.0, 0.0)
        incl_ref[...] = jnp.dot(tri, eq.astype(BF16), preferred_element_type=F32)

    def final_tile(j, seen, last):
        r0 = pl.multiple_of(j * tk, tk)
        incl = incl_ref[...]
        if not last:
            tie_prefix(j + 1)
        rank = seen + incl
        bar = t_row + jnp.where(rank <= ties_wanted, 0, 1)
        sel = jnp.where(st_ref[pl.ds(r0, tk), :] >= bar, 1, 0)
        mask_ref[pl.ds(r0, tk), :] = sel.astype(jnp.int8)
        return seen + incl[tk - 1:tk, :]

    tie_prefix(0)
    seen = lax.fori_loop(0, nt - 1, lambda j, seen: final_tile(j, seen, False),
                         jnp.zeros((1, tq), F32))
    final_tile(nt - 1, seen, True)

    def zero_tile(j, carry):
        r0 = pl.multiple_of(j * tk, tk)
        mask_ref[pl.ds(r0, tk), :] = jnp.zeros((tk, tq), jnp.int8)
        return carry

    lax.fori_loop(nt, nkt_total, zero_tile, 0)


def _idx_mask(qiT, wiT, kk, *, nb, tq, qpos0, n_valid, n_sel):
    tkp = kk.shape[1]
    nq = qiT.shape[1] // nb // tq
    return pl.pallas_call(
        functools.partial(_idx_mask_kernel, tq=tq, qpos0=qpos0, n_valid=n_valid, n_sel=n_sel),
        out_shape=jax.ShapeDtypeStruct((nb, tkp, nq * tq), jnp.int8),
        grid=(nb, nq),
        in_specs=[pl.BlockSpec((N_HEADS_IDX * IDX_DIM, tq), lambda b, i: (0, b * nq + i)),
                  pl.BlockSpec((8, tq), lambda b, i: (0, b * nq + i)),
                  pl.BlockSpec((None, tkp, 128), lambda b, i: (b, 0, 0))],
        out_specs=pl.BlockSpec((None, tkp, tq), lambda b, i: (b, 0, i)),
        scratch_shapes=[pltpu.VMEM((tkp, tq), I32), pltpu.VMEM((tkp + ATT_TILE, tq), I16),
                        pltpu.VMEM((tkp + ATT_TILE, tq), I16),
                        pltpu.VMEM((N_HEADS_IDX, ATT_TILE, tq), F32),
                        pltpu.VMEM((ATT_TILE, tq), F32)],
        compiler_params=_cparams(("parallel", "parallel")),
        name="idx_mask",
    )(qiT, wiT, kk)


def _load_qsel(qT_ref, qsel_ref, tq):
    zeros = jnp.zeros((HEAD_DIM, tq), BF16)
    for h in range(N_HEADS):
        blk = qT_ref[h * HEAD_DIM:(h + 1) * HEAD_DIM, :]
        qsel_ref[h] = jnp.concatenate([blk, zeros] if h % 2 == 0 else [zeros, blk], axis=0)


def _init_state(m_ref, acc_ref):
    m_ref[...] = jnp.full(m_ref.shape, NEG_INF, F32)
    acc_ref[...] = jnp.zeros(acc_ref.shape, F32)


def _scores(k_slab, qsel_ref, s_ref, h):
    s_ref[h] = jnp.dot(k_slab(h // 2), qsel_ref[h], preferred_element_type=F32)


def _heads_tile(v_rows, next_k_slab, qsel_ref, s_ref, m_ref, acc_ref, addend):
    for h in range(N_HEADS):
        s = s_ref[h]
        for a in addend(h):
            s = s + a
        if next_k_slab is not None:
            _scores(next_k_slab, qsel_ref, s_ref, h)
        m_old = m_ref[h:h + 1, :]
        m_new = jnp.maximum(m_old, jnp.max(s, axis=0, keepdims=True))
        alpha = jnp.exp2(m_old - m_new)
        p = jnp.exp2(s - m_new).astype(BF16)
        m_ref[h:h + 1, :] = m_new
        pv = jnp.dot(v_rows(h), p, preferred_element_type=F32)
        rows = slice(h * V_ROWS, (h + 1) * V_ROWS)
        acc_ref[rows, :] = alpha * acc_ref[rows, :] + pv


def _attn_scratch(tq):
    return [pltpu.VMEM((N_HEADS, 128, tq), BF16), pltpu.VMEM((N_HEADS, ATT_TILE, tq), F32),
            pltpu.VMEM((N_HEADS, tq), F32), pltpu.VMEM((N_HEADS * V_ROWS, tq), F32)]


def _store_out(o_ref, acc_ref):
    outs = [acc_ref[h * V_ROWS:h * V_ROWS + HEAD_DIM, :]
            / acc_ref[h * V_ROWS + HEAD_DIM:h * V_ROWS + HEAD_DIM + 1, :] for h in range(N_HEADS)]
    o_ref[...] = jnp.concatenate(outs, axis=0).T.astype(o_ref.dtype)


def _dsa_kernel(qT_ref, k_ref, vT_ref, mask_ref, bias_ref, o_ref, qsel_ref, s_ref, m_ref, acc_ref,
                *, tq, qpos0):
    tk = ATT_TILE
    jd = lax.shift_right_logical(qpos0 + pl.program_id(1) * tq, int(math.log2(tk)))
    _load_qsel(qT_ref, qsel_ref, tq)
    _init_state(m_ref, acc_ref)

    def k_slab(j):
        r0 = pl.multiple_of(j * tk, tk)
        return lambda g: k_ref[pl.ds(r0, tk), g * 128:(g + 1) * 128]

    def tile(j, near, last=False):
        r0 = pl.multiple_of(j * tk, tk)
        unselected = jnp.where(mask_ref[pl.ds(r0, tk), :].astype(I32) != 0, 0.0, NEG_INF)
        addend = ((lambda h: (unselected,)) if near is None
                  else (lambda h: (unselected, bias_ref[near, h])))
        _heads_tile(lambda h: vT_ref[j, h * V_ROWS:(h + 1) * V_ROWS, :],
                    None if last else k_slab(j + 1),
                    qsel_ref, s_ref, m_ref, acc_ref, addend)

    for h in range(N_HEADS):
        _scores(k_slab(0), qsel_ref, s_ref, h)

    n_far = jnp.maximum(jd - 1, 0)

    def far_pair(u, carry):
        tile(2 * u, None)
        tile(2 * u + 1, None)
        return carry

    lax.fori_loop(0, lax.shift_right_logical(n_far, 1), far_pair, 0)

    @pl.when((n_far & 1) == 1)
    def _():
        tile(n_far - 1, None)

    @pl.when(jd >= 1)
    def _():
        tile(jd - 1, 0)

    tile(jd, 1, last=True)
    _store_out(o_ref, acc_ref)


def _dsa_attn(qT, k, vT, maskT, bias, *, nb, tq, qpos0):
    tkp = k.shape[1]
    nkt = tkp // ATT_TILE
    nq = qT.shape[1] // nb // tq
    return pl.pallas_call(
        functools.partial(_dsa_kernel, tq=tq, qpos0=qpos0),
        out_shape=jax.ShapeDtypeStruct((nb * nq * tq, WIDTH), BF16),
        grid=(nb, nq),
        in_specs=[pl.BlockSpec((WIDTH, tq), lambda b, i: (0, b * nq + i)),
                  pl.BlockSpec((None, tkp, WIDTH), lambda b, i: (b, 0, 0),
                               pipeline_mode=pl.Buffered(1)),
                  pl.BlockSpec((nkt, N_HEADS * V_ROWS, ATT_TILE), lambda b, i: (b, 0, 0),
                               pipeline_mode=pl.Buffered(1)),
                  pl.BlockSpec((None, tkp, tq), lambda b, i: (b, 0, i)),
                  _const_spec(bias.shape)],
        out_specs=pl.BlockSpec((tq, WIDTH), lambda b, i: (b * nq + i, 0)),
        scratch_shapes=_attn_scratch(tq),
        compiler_params=_cparams(("parallel", "parallel")),
        name="dsa_attn",
    )(qT, k, vT, maskT, bias)


_BAND_TILES = BAND_CHUNKS * CHUNK // ATT_TILE + 1


def _band_kernel(qT_ref, *refs, tq, off):
    k_refs = refs[:_BAND_TILES]
    v_refs = refs[_BAND_TILES:2 * _BAND_TILES]
    bias_ref, o_ref, qsel_ref, s_ref, m_ref, acc_ref = refs[2 * _BAND_TILES:]
    _load_qsel(qT_ref, qsel_ref, tq)
    _init_state(m_ref, acc_ref)
    k_slab = lambda w: (lambda g: k_refs[w][:, g * 128:(g + 1) * 128])
    for h in range(N_HEADS):
        _scores(k_slab(0), qsel_ref, s_ref, h)
    for w in range(_BAND_TILES):
        entry = jnp.where(pl.program_id(1) + off + w >= 0, w, _BAND_TILES)
        _heads_tile(lambda h: v_refs[w][0, h * V_ROWS:(h + 1) * V_ROWS, :],
                    None if w == _BAND_TILES - 1 else k_slab(w + 1),
                    qsel_ref, s_ref, m_ref, acc_ref, lambda h: (bias_ref[entry, h],))
    _store_out(o_ref, acc_ref)


def _band_attn(qT, k, vT, bias, *, nb, tq, off):
    nkt = k.shape[1] // ATT_TILE
    nq = qT.shape[1] // nb // tq
    tile_of = lambda i, w: jnp.maximum(i + off + w, 0)
    k_specs = [pl.BlockSpec((None, ATT_TILE, WIDTH), lambda b, i, w=w: (b, tile_of(i, w), 0))
               for w in range(_BAND_TILES)]
    v_specs = [pl.BlockSpec((1, N_HEADS * V_ROWS, ATT_TILE),
                            lambda b, i, w=w: (b * nkt + tile_of(i, w), 0, 0))
               for w in range(_BAND_TILES)]
    return pl.pallas_call(
        functools.partial(_band_kernel, tq=tq, off=off),
        out_shape=jax.ShapeDtypeStruct((nb * nq * tq, WIDTH), BF16),
        grid=(nb, nq),
        in_specs=[pl.BlockSpec((WIDTH, tq), lambda b, i: (0, b * nq + i))] + k_specs + v_specs
                 + [_const_spec(bias.shape)],
        out_specs=pl.BlockSpec((tq, WIDTH), lambda b, i: (b * nq + i, 0)),
        scratch_shapes=_attn_scratch(tq),
        compiler_params=_cparams(("parallel", "parallel")),
        name="band_attn",
    )(qT, *([k] * _BAND_TILES), *([vT] * _BAND_TILES), bias)


def _mix_ln_kernel(h_ref, oa_ref, ob_ref, wg_ref, wba_ref, wbb_ref, wo_ref, g_ref, b_ref, o_ref,
                   *, alpha):
    h = h_ref[...]
    hb = h.astype(BF16)
    d = h.shape[1]
    ga = jnp.dot(hb, wg_ref[:, :d], preferred_element_type=F32)
    gb = jnp.dot(hb, wg_ref[:, d:], preferred_element_type=F32)
    ya = jnp.dot(oa_ref[...], wba_ref[...], preferred_element_type=F32)
    yb = jnp.dot(ob_ref[...], wbb_ref[...], preferred_element_type=F32)
    gated = jax.nn.sigmoid(ga) * ya + jax.nn.sigmoid(gb) * yb
    mix = jnp.dot(gated.astype(BF16), wo_ref[...], preferred_element_type=F32)
    o_ref[...] = _layer_norm_rows(alpha * h + mix, g_ref[...], b_ref[...])


def _mix_ln(h, oa, ob, wg, wba, wbb, wo, g, b, alpha):
    n, d = h.shape
    tm = TOKEN_TILE
    row = lambda width: pl.BlockSpec((tm, width), lambda i: (i, 0))
    return pl.pallas_call(
        functools.partial(_mix_ln_kernel, alpha=alpha),
        out_shape=jax.ShapeDtypeStruct((n, d), F32),
        grid=(n // tm,),
        in_specs=[row(d), row(WIDTH), row(WIDTH), _const_spec(wg.shape), _const_spec(wba.shape),
                  _const_spec(wbb.shape), _const_spec(wo.shape), _const_spec((1, d)),
                  _const_spec((1, d))],
        out_specs=row(d),
        compiler_params=_cparams(("parallel",)),
        name="mix_ln",
    )(h, oa, ob, wg, wba, wbb, wo, g.reshape(1, d), b.reshape(1, d))


def _t5_bucket(rel):
    half = T5_BUCKETS // 2
    exact = half // 2
    n = jnp.abs(rel)
    log_ratio = jnp.log(jnp.maximum(n, 1).astype(F32) / exact) / math.log(T5_MAX_DIST / exact)
    large = jnp.minimum(exact + (log_ratio * (half - exact)).astype(I32), half - 1)
    return (rel > 0).astype(I32) * half + jnp.where(n < exact, n, large)


def _rel_line(n_tiles, first, tq):
    period = ATT_TILE + tq
    y = jnp.arange(period, dtype=I32)[None, :]
    c = (jnp.arange(n_tiles, dtype=I32)[:, None] + first) * ATT_TILE
    return jnp.where(y < tq, c - y, c + period - y)


def _toeplitz(v, tq):
    period = v.shape[-1]
    flat = jnp.tile(v, (1,) * (v.ndim - 1) + (ATT_TILE,))[..., :ATT_TILE * (period - 1)]
    return flat.reshape(v.shape[:-1] + (ATT_TILE, period - 1))[..., :tq]


def _dsa_bias_table(t5_bias, tq):
    assert T5_MAX_DIST <= ATT_TILE + 1
    far = t5_bias[:, _t5_bucket(jnp.int32(-(ATT_TILE + 1)))]
    line = t5_bias[:, _t5_bucket(_rel_line(2, -1, tq))] - far[:, None, None]
    return _toeplitz(jnp.moveaxis(line, 0, 1), tq) * LOG2E


def _band_bias_table(rel_bias, tq):
    first = 1 - _BAND_TILES
    ridx = jnp.clip(_rel_line(_BAND_TILES, first, tq), -REL_BACK, CHUNK - 1) + REL_BACK
    bias = _toeplitz(jnp.moveaxis(rel_bias[:, ridx], 0, 1), tq)
    w = jnp.arange(_BAND_TILES, dtype=I32)[:, None, None]
    j = jnp.arange(ATT_TILE, dtype=I32)[None, :, None]
    i = jnp.arange(tq, dtype=I32)[None, None, :]
    dchunk = ((w + first) * ATT_TILE + j) // CHUNK - i // CHUNK
    ok = (dchunk <= 0) & (dchunk >= -BAND_CHUNKS)
    table = jnp.where(ok[:, None], bias * LOG2E, NEG_INF)
    return jnp.concatenate([table, jnp.full_like(table[:1], NEG_INF)], axis=0)


def _regroup_w_in(w_in):
    sizes = (WIDTH, WIDTH, WIDTH, N_HEADS_IDX * IDX_DIM, IDX_DIM, N_HEADS_IDX,
             WIDTH, WIDTH, WIDTH, w_in.shape[0], w_in.shape[0])
    offs = [0]
    for s in sizes:
        offs.append(offs[-1] + s)
    qa, ka, va, qi, ki, wi, qb, kb, vb, ga, gb = (w_in[:, offs[t]:offs[t + 1]] for t in range(11))
    wi_pad = jnp.pad(wi, ((0, 0), (0, 128 - N_HEADS_IDX)))
    w_proj = jnp.concatenate([qa, ka, va, qi, ki, ki, wi_pad, qb, kb, vb], axis=1).astype(BF16)
    w_gate = jnp.concatenate([ga, gb], axis=1).astype(BF16)
    return w_proj, w_gate


def _key_tiles_T(v):
    nb, tkp, wd = v.shape
    vt = v.reshape(nb, tkp // ATT_TILE, ATT_TILE, N_HEADS, HEAD_DIM).transpose(0, 1, 3, 4, 2)
    ones_slab = jnp.zeros(vt.shape[:3] + (V_ROWS - HEAD_DIM, ATT_TILE), v.dtype).at[..., 0, :].set(1)
    return jnp.concatenate([vt, ones_slab], axis=3).reshape(-1, N_HEADS * V_ROWS, ATT_TILE)


def _pad_keys(x, tkp):
    return jnp.pad(x, ((0, 0), (0, tkp - x.shape[1]), (0, 0)))


def kernel(x_prompt, x_sample, cache_k_a, cache_v_a, cache_kidx_a, cache_k_b, cache_v_b, t5_bias,
           ln1_g, ln1_b, ffn1_wi, ffn1_wo, ln2_g, ln2_b, w_in, rel_bias_b, w_branch_a, w_branch_b,
           w_out, ln3_g, ln3_b, ffn2_wi, ffn2_wo):
    depth = ln1_g.shape[0]
    alpha = (2.0 * depth) ** 0.25
    nbp, seq, d = x_prompt.shape
    nbs, dec, _ = x_sample.shape
    past = cache_k_a.shape[2]
    band = cache_k_b.shape[2]
    keep = min(BAND_CHUNKS * CHUNK, seq)
    tq_p, tq_s = ATT_TILE, 128
    assert seq % TOKEN_TILE == 0 and keep == TOKEN_TILE and dec <= tq_s and past % ATT_TILE == 0
    assert band == BAND_CHUNKS * CHUNK and (nbs * tq_s) % TOKEN_TILE == 0

    yp = x_prompt.reshape(nbp * seq, d)
    ys = jnp.pad(x_sample, ((0, 0), (0, tq_s - dec), (0, 0))).reshape(nbs * tq_s, d)
    dsa_bias_p = _dsa_bias_table(t5_bias, tq_p)
    dsa_bias_s = dsa_bias_p[..., :tq_s]
    n_sel_p = min(TOPK_MAX, seq // 4)
    n_sel_s = min(TOPK_MAX, (past + dec) // 4)
    tk_s = -(-(past + tq_s) // ATT_TILE) * ATT_TILE
    tkb_s = _BAND_TILES * ATT_TILE

    st_p, st_s = [], []
    for l in range(depth):
        w_proj, w_gate = _regroup_w_in(w_in[l])
        wi1, wo1 = ffn1_wi[l].astype(BF16), ffn1_wo[l].astype(BF16)
        wi2, wo2 = ffn2_wi[l].astype(BF16), ffn2_wo[l].astype(BF16)
        wba, wbb, wo = (w_branch_a[l].astype(BF16), w_branch_b[l].astype(BF16),
                        w_out[l].astype(BF16))
        band_bias_p = _band_bias_table(rel_bias_b[l], tq_p)
        band_bias_s = band_bias_p[..., :tq_s]

        h = _ffn_ln(yp, wi1, wo1, ln1_g[l], ln1_b[l], alpha)
        (qaT, ka, kab, va, vaT, qiT, ki, kk, wiT, qbT, kbb, vbT, kbl, vbl) = _in_proj(
            h, w_proj, seq // TOKEN_TILE)
        maskT = _idx_mask(qiT, wiT, kk.reshape(nbp, seq, 128), nb=nbp, tq=tq_p, qpos0=0,
                          n_valid=seq, n_sel=n_sel_p)
        oa = _dsa_attn(qaT, kab.reshape(nbp, seq, WIDTH), vaT, maskT, dsa_bias_p,
                       nb=nbp, tq=tq_p, qpos0=0)
        ob = _band_attn(qbT, kbb.reshape(nbp, seq, WIDTH), vbT, band_bias_p,
                        nb=nbp, tq=tq_p, off=1 - _BAND_TILES)
        h2 = _mix_ln(h, oa, ob, w_gate, wba, wbb, wo, ln2_g[l], ln2_b[l], alpha)
        yp = _ffn_ln(h2, wi2, wo2, ln3_g[l], ln3_b[l], alpha)
        st_p.append((ka.reshape(nbp, seq, N_HEADS, HEAD_DIM), va.reshape(nbp, seq, N_HEADS, HEAD_DIM),
                     ki.reshape(nbp, seq, IDX_DIM), kbl.reshape(nbp, keep, N_HEADS, HEAD_DIM),
                     vbl.reshape(nbp, keep, N_HEADS, HEAD_DIM)))

        h = _ffn_ln(ys, wi1, wo1, ln1_g[l], ln1_b[l], alpha)
        (qaT, ka, _, va, _, qiT, ki, _, wiT, qbT, _, _, kbl, vbl) = _in_proj(h, w_proj, 1)
        new = lambda a: a.reshape(nbs, tq_s, -1)[:, :dec]
        ka_n, va_n, ki_n, kb_n, vb_n = new(ka), new(va), new(ki), new(kbl), new(vbl)
        ka_all = jnp.concatenate([cache_k_a[l].reshape(nbs, past, WIDTH), ka_n], axis=1)
        va_all = jnp.concatenate([cache_v_a[l].reshape(nbs, past, WIDTH), va_n], axis=1)
        ki_all = jnp.concatenate([cache_kidx_a[l], ki_n], axis=1)
        kb_all = jnp.concatenate([cache_k_b[l].reshape(nbs, band, WIDTH), kb_n], axis=1)
        vb_all = jnp.concatenate([cache_v_b[l].reshape(nbs, band, WIDTH), vb_n], axis=1)
        kk_s = _pad_keys(jnp.concatenate([ki_all, ki_all], axis=2).astype(BF16), tk_s)
        maskT = _idx_mask(qiT, wiT, kk_s, nb=nbs, tq=tq_s, qpos0=past, n_valid=past + dec,
                          n_sel=n_sel_s)
        oa = _dsa_attn(qaT, _pad_keys(ka_all.astype(BF16), tk_s),
                       _key_tiles_T(_pad_keys(va_all.astype(BF16), tk_s)), maskT, dsa_bias_s,
                       nb=nbs, tq=tq_s, qpos0=past)
        ob = _band_attn(qbT, _pad_keys(kb_all.astype(BF16), tkb_s),
                        _key_tiles_T(_pad_keys(vb_all.astype(BF16), tkb_s)), band_bias_s,
                        nb=nbs, tq=tq_s, off=0)
        h2 = _mix_ln(h, oa, ob, w_gate, wba, wbb, wo, ln2_g[l], ln2_b[l], alpha)
        ys = _ffn_ln(h2, wi2, wo2, ln3_g[l], ln3_b[l], alpha)
        heads = lambda a: a.reshape(nbs, -1, N_HEADS, HEAD_DIM)
        st_s.append((heads(ka_n), heads(va_n), ki_n, heads(kb_all[:, -band:]),
                     heads(vb_all[:, -band:])))

    y_prompt = yp.reshape(nbp, seq, d)
    y_sample = ys.reshape(nbs, tq_s, d)[:, :dec]
    stack = lambda sts, t: jnp.stack([s[t] for s in sts])
    return (y_prompt, y_sample,
            stack(st_p, 0), stack(st_p, 1), stack(st_p, 2), stack(st_p, 3), stack(st_p, 4),
            stack(st_s, 0), stack(st_s, 1), stack(st_s, 2), stack(st_s, 3), stack(st_s, 4))
```

```python
import functools
import math

import jax
import jax.numpy as jnp
from jax import lax
from jax.experimental import pallas as pl
from jax.experimental.pallas import tpu as pltpu

F32 = jnp.float32
BF16 = jnp.bfloat16
I32 = jnp.int32
I16 = jnp.int16

CHUNK = 64
CHUNK_SHIFT = 6
HEAD_DIM = 64
N_HEADS = 8
WIDTH = N_HEADS * HEAD_DIM
N_HEADS_IDX = 4
IDX_DIM = 64
TOPK_MAX = 256
IDX_SCALE = (IDX_DIM * N_HEADS_IDX) ** -0.5
BAND_CHUNKS = 8
REL_BACK = 128
T5_BUCKETS = 32
T5_MAX_DIST = 128
LN_EPS = 1e-5
NEG_INF = -1e30
LOG2E = math.log2(math.e)
V_ROWS = HEAD_DIM + 16
INT_MIN = -2 ** 31

ATT_TILE = 256
TOKEN_TILE = 512
FF_CHUNK = 256
VMEM_LIMIT_MB = 56


def _cparams(sem):
    return pltpu.CompilerParams(dimension_semantics=sem,
                                vmem_limit_bytes=VMEM_LIMIT_MB * 1024 * 1024)


def _const_spec(shape):
    nd = len(shape)
    return pl.BlockSpec(shape, lambda *_: (0,) * nd, pipeline_mode=pl.Buffered(1))


def _tree_sum(xs):
    xs = list(xs)
    while len(xs) > 1:
        nxt = [xs[a] + xs[a + 1] for a in range(0, len(xs) - 1, 2)]
        if len(xs) % 2:
            nxt.append(xs[-1])
        xs = nxt
    return xs[0]


def _layer_norm_rows(y, g, b):
    mu = jnp.mean(y, axis=-1, keepdims=True)
    d = y - mu
    var = jnp.mean(d * d, axis=-1, keepdims=True)
    return d * lax.rsqrt(var + LN_EPS) * g + b


def _ffn_ln_kernel(x_ref, wi_ref, wo_ref, g_ref, b_ref, o_ref, *, alpha, dff):
    x = x_ref[...]
    xb = x.astype(BF16)
    acc = jnp.zeros(x.shape, F32)
    for c in range(dff // FF_CHUNK):
        lo = c * FF_CHUNK
        a = jnp.dot(xb, wi_ref[:, lo:lo + FF_CHUNK], preferred_element_type=F32)
        u = jnp.dot(xb, wi_ref[:, dff + lo:dff + lo + FF_CHUNK], preferred_element_type=F32)
        hm = (a * jax.nn.sigmoid(a)) * u
        acc = acc + jnp.dot(hm.astype(BF16), wo_ref[lo:lo + FF_CHUNK, :],
                            preferred_element_type=F32)
    y = alpha * x + 0.5 * acc
    o_ref[...] = _layer_norm_rows(y, g_ref[...], b_ref[...])


def _ffn_ln(x, wi, wo, g, b, alpha):
    n, d = x.shape
    dff = wo.shape[0]
    tm = TOKEN_TILE
    return pl.pallas_call(
        functools.partial(_ffn_ln_kernel, alpha=alpha, dff=dff),
        out_shape=jax.ShapeDtypeStruct((n, d), F32),
        grid=(n // tm,),
        in_specs=[pl.BlockSpec((tm, d), lambda i: (i, 0)),
                  _const_spec((d, 2 * dff)), _const_spec((dff, d)),
                  _const_spec((1, d)), _const_spec((1, d))],
        out_specs=pl.BlockSpec((tm, d), lambda i: (i, 0)),
        compiler_params=_cparams(("parallel",)),
        name="ffn_ln",
    )(x, wi, wo, g.reshape(1, d), b.reshape(1, d))


_C_QA, _C_KA, _C_VA, _C_QI, _C_KI, _C_WI, _C_QB, _C_KB, _C_VB, _C_END = (
    0, 512, 1024, 1536, 1792, 1920, 2048, 2560, 3072, 3584)


def _inproj_kernel(h_ref, w_ref, qaT_ref, ka_ref, kab_ref, va_ref, vaT_ref, qiT_ref, ki_ref,
                   kk_ref, wiT_ref, qbT_ref, kbb_ref, vbT_ref, kbl_ref, vbl_ref, *, per):
    hb = h_ref[...].astype(BF16)

    def proj(c0, c1):
        return jnp.dot(hb, w_ref[:, c0:c1], preferred_element_type=F32)

    def put_tiles(ref, zt):
        tm = zt.shape[1]
        ones_slab = jnp.where(lax.broadcasted_iota(I32, (V_ROWS - HEAD_DIM, tm), 0) == 0, 1.0, 0.0)
        aug = jnp.concatenate(
            [piece for h in range(N_HEADS)
             for piece in (zt[h * HEAD_DIM:(h + 1) * HEAD_DIM, :], ones_slab)], axis=0)
        for c in range(ref.shape[0]):
            ref[c] = aug[:, c * ATT_TILE:(c + 1) * ATT_TILE].astype(ref.dtype)

    scale = HEAD_DIM ** -0.5 * LOG2E
    qaT_ref[...] = (proj(_C_QA, _C_KA) * scale).T.astype(BF16)
    ka = proj(_C_KA, _C_VA)
    ka_ref[...] = ka
    kab_ref[...] = ka.astype(BF16)
    va = proj(_C_VA, _C_QI)
    va_ref[...] = va
    put_tiles(vaT_ref, va.T)
    qiT_ref[...] = proj(_C_QI, _C_KI).T.astype(BF16)
    kk = proj(_C_KI, _C_WI)
    ki_ref[...] = kk[:, :IDX_DIM]
    kk_ref[...] = kk.astype(BF16)
    wiT_ref[...] = proj(_C_WI, _C_QB).T[:8, :]
    qbT_ref[...] = (proj(_C_QB, _C_KB) * scale).T.astype(BF16)
    kb = proj(_C_KB, _C_VB)
    kbb_ref[...] = kb.astype(BF16)
    vb = proj(_C_VB, _C_END)
    put_tiles(vbT_ref, vb.T)

    @pl.when(pl.program_id(0) % per == per - 1)
    def _():
        kbl_ref[...] = kb
        vbl_ref[...] = vb


def _in_proj(h, w, per):
    n, d = h.shape
    tm = TOKEN_TILE
    nt = n // tm
    sub = tm // ATT_TILE
    row = lambda width: pl.BlockSpec((tm, width), lambda i: (i, 0))
    col = lambda rows: pl.BlockSpec((rows, tm), lambda i: (0, i))
    til = pl.BlockSpec((sub, N_HEADS * V_ROWS, ATT_TILE), lambda i: (i, 0, 0))
    last = pl.BlockSpec((tm, WIDTH), lambda i: (i // per, 0))
    sds = jax.ShapeDtypeStruct
    out_shape = (
        sds((WIDTH, n), BF16),
        sds((n, WIDTH), F32), sds((n, WIDTH), BF16),
        sds((n, WIDTH), F32), sds((n // ATT_TILE, N_HEADS * V_ROWS, ATT_TILE), BF16),
        sds((N_HEADS_IDX * IDX_DIM, n), BF16),
        sds((n, IDX_DIM), F32), sds((n, 128), BF16),
        sds((8, n), F32),
        sds((WIDTH, n), BF16),
        sds((n, WIDTH), BF16),
        sds((n // ATT_TILE, N_HEADS * V_ROWS, ATT_TILE), BF16),
        sds((nt // per * tm, WIDTH), F32), sds((nt // per * tm, WIDTH), F32),
    )
    out_specs = (col(WIDTH), row(WIDTH), row(WIDTH), row(WIDTH), til,
                 col(N_HEADS_IDX * IDX_DIM), row(IDX_DIM), row(128), col(8),
                 col(WIDTH), row(WIDTH), til, last, last)
    return pl.pallas_call(
        functools.partial(_inproj_kernel, per=per),
        out_shape=out_shape,
        grid=(nt,),
        in_specs=[pl.BlockSpec((tm, d), lambda i: (i, 0)), _const_spec((d, _C_END))],
        out_specs=out_specs,
        compiler_params=_cparams(("arbitrary",)),
        name="in_proj",
    )(h, w)


def _flip(bits):
    return bits ^ (lax.shift_right_arithmetic(bits, 31) & 0x7FFFFFFF)


def _idx_mask_kernel(qiT_ref, wiT_ref, kk_ref, mask_ref, st_ref, hi_ref, lo_ref, d_ref, incl_ref,
                     *, tq, qpos0, n_valid, n_sel):
    tk = ATT_TILE
    nkt_total = st_ref.shape[0] // tk
    q0 = qpos0 + pl.program_id(1) * tq
    n_keys = jnp.minimum(q0 + tq, n_valid)
    nt = jnp.minimum(lax.shift_right_logical(n_keys + tk - 1, int(math.log2(tk))), nkt_total)

    lane = lax.broadcasted_iota(I32, (8, tq), 1)
    q_chunk = lax.shift_right_logical(q0 + lane, CHUNK_SHIFT)
    n_adm = jnp.minimum((q_chunk + 1) * CHUNK, n_valid)
    k_row = jnp.minimum(n_adm, n_sel)

    zeros = jnp.zeros((128 - IDX_DIM, tq), BF16)
    qh = [jnp.concatenate([qiT_ref[h * IDX_DIM:(h + 1) * IDX_DIM, :], zeros], axis=0)
          for h in range(N_HEADS_IDX)]
    scale_exact = math.frexp(IDX_SCALE)[0] == 0.5
    wh = [wiT_ref[h:h + 1, :] * IDX_SCALE if scale_exact else wiT_ref[h:h + 1, :]
          for h in range(N_HEADS_IDX)]

    def dots(j, h):
        r0 = pl.multiple_of(j * tk, tk)
        d_ref[h] = jnp.dot(kk_ref[pl.ds(r0, tk), :], qh[h], preferred_element_type=F32)

    def score_tile(j, last):
        r0 = pl.multiple_of(j * tk, tk)
        sc = None
        for h in range(N_HEADS_IDX):
            d = d_ref[h]
            if not last:
                dots(j + 1, h)
            term = wh[h] * jnp.maximum(d, 0.0)
            sc = term if sc is None else sc + term
        if not scale_exact:
            sc = sc * IDX_SCALE
        bits = lax.bitcast_convert_type(sc, I32)
        sign = lax.shift_right_arithmetic(bits, 31)
        key = (bits ^ (sign & 0x7FFFFFFF)) - sign
        if last:
            kpos = r0 + lax.broadcasted_iota(I32, (tk, tq), 0)
            key = jnp.where(kpos < n_adm[0:1, :], key, INT_MIN)
        st_ref[pl.ds(r0, tk), :] = key
        hi_ref[pl.ds(r0, tk), :] = lax.shift_right_arithmetic(key, 16).astype(I16)
        lo_ref[pl.ds(r0, tk), :] = (key ^ 0x8000).astype(I16)

    n_pairs = lax.shift_right_logical(nt + 1, 1)

    @pl.when((nt & 1) == 1)
    def _():
        r0 = pl.multiple_of(nt * tk, tk)
        hi_ref[pl.ds(r0, tk), :] = jnp.full((tk, tq), -2 ** 15, I16)
        lo_ref[pl.ds(r0, tk), :] = jnp.full((tk, tq), -2 ** 15, I16)

    for h in range(N_HEADS_IDX):
        dots(0, h)

    def score_pair(u, carry):
        score_tile(2 * u, False)
        score_tile(2 * u + 1, False)
        return carry

    lax.fori_loop(0, lax.shift_right_logical(nt - 1, 1), score_pair, 0)

    @pl.when(((nt - 1) & 1) == 1)
    def _():
        score_tile(nt - 2, False)

    score_tile(nt - 1, True)

    group = incl_ref.shape[0]
    n_groups = nt if group == 1 else lax.shift_right_logical(nt + 3, 2)
    for g in range(group - 1):
        @pl.when(nt + g < n_groups * group)
        def _(g=g):
            r0 = pl.multiple_of((nt + g) * tk, tk)
            st_ref[pl.ds(r0, tk), :] = jnp.full((tk, tq), INT_MIN, I32)

    def count16(ref, pred, thr8):
        thr = jnp.concatenate([thr8, thr8], axis=0).astype(I16)

        def body(u, acc):
            r0 = pl.multiple_of(u * (2 * tk), 2 * tk)
            blk = ref[pl.ds(r0, 2 * tk), :]
            parts = [jnp.where(pred(blk[r * 16:(r + 1) * 16, :], thr), jnp.int16(1), jnp.int16(0))
                     for r in range(2 * tk // 16)]
            return acc + _tree_sum(parts)

        acc = lax.fori_loop(0, n_pairs, body, jnp.zeros((16, tq), I16))
        return jnp.broadcast_to(jnp.sum(acc.astype(I32), axis=0, keepdims=True), (8, tq))

    ge = lambda a, b: a >= b
    gt = lambda a, b: a > b

    def kth_largest16(ref, want):
        def step(k, t8):
            cand = t8 + lax.shift_left(jnp.int32(1), 15 - k)
            return jnp.where(count16(ref, ge, cand) >= want, cand, t8)
        return lax.fori_loop(0, 16, step, jnp.full((8, tq), -2 ** 15, I32))

    t_hi = kth_largest16(hi_ref, k_row)
    n_above = count16(hi_ref, gt, t_hi)
    t_hi16 = jnp.concatenate([t_hi, t_hi], axis=0).astype(I16)

    def park(j, carry):
        r0 = pl.multiple_of(j * tk, tk)
        hi, lo = hi_ref[pl.ds(r0, tk), :], lo_ref[pl.ds(r0, tk), :]
        lo_ref[pl.ds(r0, tk), :] = jnp.concatenate(
            [jnp.where(hi[r * 16:(r + 1) * 16, :] == t_hi16, lo[r * 16:(r + 1) * 16, :],
                       jnp.int16(-2 ** 15)) for r in range(tk // 16)], axis=0)
        return carry

    lax.fori_loop(0, nt, park, 0)
    t_lo = kth_largest16(lo_ref, k_row - n_above)
    n_gt = n_above + count16(lo_ref, gt, t_lo)
    t_row = (t_hi * 65536 + (t_lo + 2 ** 15))[0:1, :]
    ties_wanted = (k_row - n_gt)[0:1, :].astype(F32)

    tri = jnp.where(lax.broadcasted_iota(I32, (tk, tk), 0) >= lax.broadcasted_iota(I32, (tk, tk), 1),
                    1.0, 0.0).astype(BF16)

    def tie_prefix(j, g):
        r0 = pl.multiple_of(j * tk, tk)
        eq = jnp.where(st_ref[pl.ds(r0, tk), :] == t_row, 1.0, 0.0)
        incl_ref[g] = jnp.dot(tri, eq.astype(BF16), preferred_element_type=F32)

    def final_group(u, seen, last):
        for g in range(group):
            j = u * group + g
            r0 = pl.multiple_of(j * tk, tk)
            incl = incl_ref[g]
            if not last:
                tie_prefix(j + group, g)
            rank = seen + incl
            bar = t_row + jnp.where(rank <= ties_wanted, 0, 1)
            sel = jnp.where(st_ref[pl.ds(r0, tk), :] >= bar, 1, 0)
            mask_ref[pl.ds(r0, tk), :] = sel.astype(jnp.int8)
            seen = seen + incl[tk - 1:tk, :]
        return seen

    for g in range(group):
        tie_prefix(g, g)
    seen = lax.fori_loop(0, n_groups - 1, lambda u, seen: final_group(u, seen, False),
                         jnp.zeros((1, tq), F32))
    final_group(n_groups - 1, seen, True)

    def zero_tile(j, carry):
        r0 = pl.multiple_of(j * tk, tk)
        mask_ref[pl.ds(r0, tk), :] = jnp.zeros((tk, tq), jnp.int8)
        return carry

    lax.fori_loop(n_groups * group, nkt_total, zero_tile, 0)


def _idx_mask(qiT, wiT, kk, *, nb, tq, qpos0, n_valid, n_sel):
    tkp = kk.shape[1]
    nq = qiT.shape[1] // nb // tq
    return pl.pallas_call(
        functools.partial(_idx_mask_kernel, tq=tq, qpos0=qpos0, n_valid=n_valid, n_sel=n_sel),
        out_shape=jax.ShapeDtypeStruct((nb, tkp, nq * tq), jnp.int8),
        grid=(nb, nq),
        in_specs=[pl.BlockSpec((N_HEADS_IDX * IDX_DIM, tq), lambda b, i: (0, b * nq + i)),
                  pl.BlockSpec((8, tq), lambda b, i: (0, b * nq + i)),
                  pl.BlockSpec((None, tkp, 128), lambda b, i: (b, 0, 0))],
        out_specs=pl.BlockSpec((None, tkp, tq), lambda b, i: (b, 0, i)),
        scratch_shapes=[pltpu.VMEM((tkp, tq), I32), pltpu.VMEM((tkp + ATT_TILE, tq), I16),
                        pltpu.VMEM((tkp + ATT_TILE, tq), I16),
                        pltpu.VMEM((N_HEADS_IDX, ATT_TILE, tq), F32),
                        pltpu.VMEM((4 if (tkp // ATT_TILE) % 4 == 0 else 1, ATT_TILE, tq), F32)],
        compiler_params=_cparams(("parallel", "parallel")),
        name="idx_mask",
    )(qiT, wiT, kk)


def _load_qsel(qT_ref, qsel_ref, tq):
    zeros = jnp.zeros((HEAD_DIM, tq), BF16)
    for h in range(N_HEADS):
        blk = qT_ref[h * HEAD_DIM:(h + 1) * HEAD_DIM, :]
        qsel_ref[h] = jnp.concatenate([blk, zeros] if h % 2 == 0 else [zeros, blk], axis=0)


def _init_state(m_ref, acc_ref):
    m_ref[...] = jnp.full(m_ref.shape, NEG_INF, F32)
    acc_ref[...] = jnp.zeros(acc_ref.shape, F32)


def _scores(k_slab, qsel_ref, s_ref, h):
    s_ref[h] = jnp.dot(k_slab(h // 2), qsel_ref[h], preferred_element_type=F32)


def _heads_tile(v_rows, next_k_slab, qsel_ref, s_ref, m_ref, acc_ref, addend):
    for h in range(N_HEADS):
        s = s_ref[h]
        for a in addend(h):
            s = s + a
        if next_k_slab is not None:
            _scores(next_k_slab, qsel_ref, s_ref, h)
        m_old = m_ref[h:h + 1, :]
        m_new = jnp.maximum(m_old, jnp.max(s, axis=0, keepdims=True))
        alpha = jnp.exp2(m_old - m_new)
        p = jnp.exp2(s - m_new).astype(BF16)
        m_ref[h:h + 1, :] = m_new
        pv = jnp.dot(v_rows(h), p, preferred_element_type=F32)
        rows = slice(h * V_ROWS, (h + 1) * V_ROWS)
        acc_ref[rows, :] = alpha * acc_ref[rows, :] + pv


def _attn_scratch(tq):
    return [pltpu.VMEM((N_HEADS, 128, tq), BF16), pltpu.VMEM((N_HEADS, ATT_TILE, tq), F32),
            pltpu.VMEM((N_HEADS, tq), F32), pltpu.VMEM((N_HEADS * V_ROWS, tq), F32)]


def _store_out(o_ref, acc_ref):
    outs = [acc_ref[h * V_ROWS:h * V_ROWS + HEAD_DIM, :]
            / acc_ref[h * V_ROWS + HEAD_DIM:h * V_ROWS + HEAD_DIM + 1, :] for h in range(N_HEADS)]
    o_ref[...] = jnp.concatenate(outs, axis=0).T.astype(o_ref.dtype)


def _dsa_kernel(qT_ref, k_ref, vT_ref, mask_ref, bias_ref, o_ref, qsel_ref, s_ref, m_ref, acc_ref,
                *, tq, qpos0):
    tk = ATT_TILE
    jd = lax.shift_right_logical(qpos0 + pl.program_id(1) * tq, int(math.log2(tk)))
    _load_qsel(qT_ref, qsel_ref, tq)
    _init_state(m_ref, acc_ref)

    def k_slab(j):
        r0 = pl.multiple_of(j * tk, tk)
        return lambda g: k_ref[pl.ds(r0, tk), g * 128:(g + 1) * 128]

    def tile(j, near, last=False):
        r0 = pl.multiple_of(j * tk, tk)
        unselected = jnp.where(mask_ref[pl.ds(r0, tk), :].astype(I32) != 0, 0.0, NEG_INF)
        addend = ((lambda h: (unselected,)) if near is None
                  else (lambda h: (unselected, bias_ref[near, h])))
        _heads_tile(lambda h: vT_ref[j, h * V_ROWS:(h + 1) * V_ROWS, :],
                    None if last else k_slab(j + 1),
                    qsel_ref, s_ref, m_ref, acc_ref, addend)

    for h in range(N_HEADS):
        _scores(k_slab(0), qsel_ref, s_ref, h)

    n_far = jnp.maximum(jd - 1, 0)

    def far_pair(u, carry):
        tile(2 * u, None)
        tile(2 * u + 1, None)
        return carry

    lax.fori_loop(0, lax.shift_right_logical(n_far, 1), far_pair, 0)

    @pl.when((n_far & 1) == 1)
    def _():
        tile(n_far - 1, None)

    @pl.when(jd >= 1)
    def _():
        tile(jd - 1, 0)

    tile(jd, 1, last=True)
    _store_out(o_ref, acc_ref)


def _dsa_attn(qT, k, vT, maskT, bias, *, nb, tq, qpos0):
    tkp = k.shape[1]
    nkt = tkp // ATT_TILE
    nq = qT.shape[1] // nb // tq
    return pl.pallas_call(
        functools.partial(_dsa_kernel, tq=tq, qpos0=qpos0),
        out_shape=jax.ShapeDtypeStruct((nb * nq * tq, WIDTH), BF16),
        grid=(nb, nq),
        in_specs=[pl.BlockSpec((WIDTH, tq), lambda b, i: (0, b * nq + i)),
                  pl.BlockSpec((None, tkp, WIDTH), lambda b, i: (b, 0, 0),
                               pipeline_mode=pl.Buffered(1)),
                  pl.BlockSpec((nkt, N_HEADS * V_ROWS, ATT_TILE), lambda b, i: (b, 0, 0),
                               pipeline_mode=pl.Buffered(1)),
                  pl.BlockSpec((None, tkp, tq), lambda b, i: (b, 0, i)),
                  _const_spec(bias.shape)],
        out_specs=pl.BlockSpec((tq, WIDTH), lambda b, i: (b * nq + i, 0)),
        scratch_shapes=_attn_scratch(tq),
        compiler_params=_cparams(("parallel", "parallel")),
        name="dsa_attn",
    )(qT, k, vT, maskT, bias)


_BAND_TILES = BAND_CHUNKS * CHUNK // ATT_TILE + 1


def _band_kernel(qT_ref, *refs, tq, off):
    k_refs = refs[:_BAND_TILES]
    v_refs = refs[_BAND_TILES:2 * _BAND_TILES]
    bias_ref, o_ref, qsel_ref, s_ref, m_ref, acc_ref = refs[2 * _BAND_TILES:]
    _load_qsel(qT_ref, qsel_ref, tq)
    _init_state(m_ref, acc_ref)
    k_slab = lambda w: (lambda g: k_refs[w][:, g * 128:(g + 1) * 128])
    for h in range(N_HEADS):
        _scores(k_slab(0), qsel_ref, s_ref, h)
    for w in range(_BAND_TILES):
        entry = jnp.where(pl.program_id(1) + off + w >= 0, w, _BAND_TILES)
        _heads_tile(lambda h: v_refs[w][0, h * V_ROWS:(h + 1) * V_ROWS, :],
                    None if w == _BAND_TILES - 1 else k_slab(w + 1),
                    qsel_ref, s_ref, m_ref, acc_ref, lambda h: (bias_ref[entry, h],))
    _store_out(o_ref, acc_ref)


def _band_attn(qT, k, vT, bias, *, nb, tq, off):
    nkt = k.shape[1] // ATT_TILE
    nq = qT.shape[1] // nb // tq
    tile_of = lambda i, w: jnp.maximum(i + off + w, 0)
    k_specs = [pl.BlockSpec((None, ATT_TILE, WIDTH), lambda b, i, w=w: (b, tile_of(i, w), 0))
               for w in range(_BAND_TILES)]
    v_specs = [pl.BlockSpec((1, N_HEADS * V_ROWS, ATT_TILE),
                            lambda b, i, w=w: (b * nkt + tile_of(i, w), 0, 0))
               for w in range(_BAND_TILES)]
    return pl.pallas_call(
        functools.partial(_band_kernel, tq=tq, off=off),
        out_shape=jax.ShapeDtypeStruct((nb * nq * tq, WIDTH), BF16),
        grid=(nb, nq),
        in_specs=[pl.BlockSpec((WIDTH, tq), lambda b, i: (0, b * nq + i))] + k_specs + v_specs
                 + [_const_spec(bias.shape)],
        out_specs=pl.BlockSpec((tq, WIDTH), lambda b, i: (b * nq + i, 0)),
        scratch_shapes=_attn_scratch(tq),
        compiler_params=_cparams(("parallel", "parallel")),
        name="band_attn",
    )(qT, *([k] * _BAND_TILES), *([vT] * _BAND_TILES), bias)


def _mix_ln_kernel(h_ref, oa_ref, ob_ref, wg_ref, wba_ref, wbb_ref, wo_ref, g_ref, b_ref, o_ref,
                   *, alpha):
    h = h_ref[...]
    hb = h.astype(BF16)
    d = h.shape[1]
    ga = jnp.dot(hb, wg_ref[:, :d], preferred_element_type=F32)
    gb = jnp.dot(hb, wg_ref[:, d:], preferred_element_type=F32)
    ya = jnp.dot(oa_ref[...], wba_ref[...], preferred_element_type=F32)
    yb = jnp.dot(ob_ref[...], wbb_ref[...], preferred_element_type=F32)
    gated = jax.nn.sigmoid(ga) * ya + jax.nn.sigmoid(gb) * yb
    mix = jnp.dot(gated.astype(BF16), wo_ref[...], preferred_element_type=F32)
    o_ref[...] = _layer_norm_rows(alpha * h + mix, g_ref[...], b_ref[...])


def _mix_ln(h, oa, ob, wg, wba, wbb, wo, g, b, alpha):
    n, d = h.shape
    tm = TOKEN_TILE
    row = lambda width: pl.BlockSpec((tm, width), lambda i: (i, 0))
    return pl.pallas_call(
        functools.partial(_mix_ln_kernel, alpha=alpha),
        out_shape=jax.ShapeDtypeStruct((n, d), F32),
        grid=(n // tm,),
        in_specs=[row(d), row(WIDTH), row(WIDTH), _const_spec(wg.shape), _const_spec(wba.shape),
                  _const_spec(wbb.shape), _const_spec(wo.shape), _const_spec((1, d)),
                  _const_spec((1, d))],
        out_specs=row(d),
        compiler_params=_cparams(("parallel",)),
        name="mix_ln",
    )(h, oa, ob, wg, wba, wbb, wo, g.reshape(1, d), b.reshape(1, d))


def _t5_bucket(rel):
    half = T5_BUCKETS // 2
    exact = half // 2
    n = jnp.abs(rel)
    log_ratio = jnp.log(jnp.maximum(n, 1).astype(F32) / exact) / math.log(T5_MAX_DIST / exact)
    large = jnp.minimum(exact + (log_ratio * (half - exact)).astype(I32), half - 1)
    return (rel > 0).astype(I32) * half + jnp.where(n < exact, n, large)


def _rel_line(n_tiles, first, tq):
    period = ATT_TILE + tq
    y = jnp.arange(period, dtype=I32)[None, :]
    c = (jnp.arange(n_tiles, dtype=I32)[:, None] + first) * ATT_TILE
    return jnp.where(y < tq, c - y, c + period - y)


def _toeplitz(v, tq):
    period = v.shape[-1]
    flat = jnp.tile(v, (1,) * (v.ndim - 1) + (ATT_TILE,))[..., :ATT_TILE * (period - 1)]
    return flat.reshape(v.shape[:-1] + (ATT_TILE, period - 1))[..., :tq]


def _dsa_bias_table(t5_bias, tq):
    assert T5_MAX_DIST <= ATT_TILE + 1
    far = t5_bias[:, _t5_bucket(jnp.int32(-(ATT_TILE + 1)))]
    line = t5_bias[:, _t5_bucket(_rel_line(2, -1, tq))] - far[:, None, None]
    return _toeplitz(jnp.moveaxis(line, 0, 1), tq) * LOG2E


def _band_bias_table(rel_bias, tq):
    first = 1 - _BAND_TILES
    ridx = jnp.clip(_rel_line(_BAND_TILES, first, tq), -REL_BACK, CHUNK - 1) + REL_BACK
    bias = _toeplitz(jnp.moveaxis(rel_bias[:, ridx], 0, 1), tq)
    w = jnp.arange(_BAND_TILES, dtype=I32)[:, None, None]
    j = jnp.arange(ATT_TILE, dtype=I32)[None, :, None]
    i = jnp.arange(tq, dtype=I32)[None, None, :]
    dchunk = ((w + first) * ATT_TILE + j) // CHUNK - i // CHUNK
    ok = (dchunk <= 0) & (dchunk >= -BAND_CHUNKS)
    table = jnp.where(ok[:, None], bias * LOG2E, NEG_INF)
    return jnp.concatenate([table, jnp.full_like(table[:1], NEG_INF)], axis=0)


def _regroup_w_in(w_in):
    sizes = (WIDTH, WIDTH, WIDTH, N_HEADS_IDX * IDX_DIM, IDX_DIM, N_HEADS_IDX,
             WIDTH, WIDTH, WIDTH, w_in.shape[0], w_in.shape[0])
    offs = [0]
    for s in sizes:
        offs.append(offs[-1] + s)
    qa, ka, va, qi, ki, wi, qb, kb, vb, ga, gb = (w_in[:, offs[t]:offs[t + 1]] for t in range(11))
    wi_pad = jnp.pad(wi, ((0, 0), (0, 128 - N_HEADS_IDX)))
    w_proj = jnp.concatenate([qa, ka, va, qi, ki, ki, wi_pad, qb, kb, vb], axis=1).astype(BF16)
    w_gate = jnp.concatenate([ga, gb], axis=1).astype(BF16)
    return w_proj, w_gate


def _key_tiles_T(v):
    nb, tkp, wd = v.shape
    vt = v.reshape(nb, tkp // ATT_TILE, ATT_TILE, N_HEADS, HEAD_DIM).transpose(0, 1, 3, 4, 2)
    ones_slab = jnp.zeros(vt.shape[:3] + (V_ROWS - HEAD_DIM, ATT_TILE), v.dtype).at[..., 0, :].set(1)
    return jnp.concatenate([vt, ones_slab], axis=3).reshape(-1, N_HEADS * V_ROWS, ATT_TILE)


def _pad_keys(x, tkp):
    return jnp.pad(x, ((0, 0), (0, tkp - x.shape[1]), (0, 0)))


def kernel(x_prompt, x_sample, cache_k_a, cache_v_a, cache_kidx_a, cache_k_b, cache_v_b, t5_bias,
           ln1_g, ln1_b, ffn1_wi, ffn1_wo, ln2_g, ln2_b, w_in, rel_bias_b, w_branch_a, w_branch_b,
           w_out, ln3_g, ln3_b, ffn2_wi, ffn2_wo):
    depth = ln1_g.shape[0]
    alpha = (2.0 * depth) ** 0.25
    nbp, seq, d = x_prompt.shape
    nbs, dec, _ = x_sample.shape
    past = cache_k_a.shape[2]
    band = cache_k_b.shape[2]
    keep = min(BAND_CHUNKS * CHUNK, seq)
    tq_p, tq_s = ATT_TILE, 128
    assert seq % TOKEN_TILE == 0 and keep == TOKEN_TILE and dec <= tq_s and past % ATT_TILE == 0
    assert band == BAND_CHUNKS * CHUNK and (nbs * tq_s) % TOKEN_TILE == 0

    yp = x_prompt.reshape(nbp * seq, d)
    ys = jnp.pad(x_sample, ((0, 0), (0, tq_s - dec), (0, 0))).reshape(nbs * tq_s, d)
    dsa_bias_p = _dsa_bias_table(t5_bias, tq_p)
    dsa_bias_s = dsa_bias_p[..., :tq_s]
    n_sel_p = min(TOPK_MAX, seq // 4)
    n_sel_s = min(TOPK_MAX, (past + dec) // 4)
    tk_s = -(-(past + tq_s) // ATT_TILE) * ATT_TILE
    tkb_s = _BAND_TILES * ATT_TILE

    st_p, st_s = [], []
    for l in range(depth):
        w_proj, w_gate = _regroup_w_in(w_in[l])
        wi1, wo1 = ffn1_wi[l].astype(BF16), ffn1_wo[l].astype(BF16)
        wi2, wo2 = ffn2_wi[l].astype(BF16), ffn2_wo[l].astype(BF16)
        wba, wbb, wo = (w_branch_a[l].astype(BF16), w_branch_b[l].astype(BF16),
                        w_out[l].astype(BF16))
        band_bias_p = _band_bias_table(rel_bias_b[l], tq_p)
        band_bias_s = band_bias_p[..., :tq_s]

        h = _ffn_ln(yp, wi1, wo1, ln1_g[l], ln1_b[l], alpha)
        (qaT, ka, kab, va, vaT, qiT, ki, kk, wiT, qbT, kbb, vbT, kbl, vbl) = _in_proj(
            h, w_proj, seq // TOKEN_TILE)
        maskT = _idx_mask(qiT, wiT, kk.reshape(nbp, seq, 128), nb=nbp, tq=tq_p, qpos0=0,
                          n_valid=seq, n_sel=n_sel_p)
        oa = _dsa_attn(qaT, kab.reshape(nbp, seq, WIDTH), vaT, maskT, dsa_bias_p,
                       nb=nbp, tq=tq_p, qpos0=0)
        ob = _band_attn(qbT, kbb.reshape(nbp, seq, WIDTH), vbT, band_bias_p,
                        nb=nbp, tq=tq_p, off=1 - _BAND_TILES)
        h2 = _mix_ln(h, oa, ob, w_gate, wba, wbb, wo, ln2_g[l], ln2_b[l], alpha)
        yp = _ffn_ln(h2, wi2, wo2, ln3_g[l], ln3_b[l], alpha)
        st_p.append((ka.reshape(nbp, seq, N_HEADS, HEAD_DIM), va.reshape(nbp, seq, N_HEADS, HEAD_DIM),
                     ki.reshape(nbp, seq, IDX_DIM), kbl.reshape(nbp, keep, N_HEADS, HEAD_DIM),
                     vbl.reshape(nbp, keep, N_HEADS, HEAD_DIM)))

        h = _ffn_ln(ys, wi1, wo1, ln1_g[l], ln1_b[l], alpha)
        (qaT, ka, _, va, _, qiT, ki, _, wiT, qbT, _, _, kbl, vbl) = _in_proj(h, w_proj, 1)
        new = lambda a: a.reshape(nbs, tq_s, -1)[:, :dec]
        ka_n, va_n, ki_n, kb_n, vb_n = new(ka), new(va), new(ki), new(kbl), new(vbl)
        ka_all = jnp.concatenate([cache_k_a[l].reshape(nbs, past, WIDTH), ka_n], axis=1)
        va_all = jnp.concatenate([cache_v_a[l].reshape(nbs, past, WIDTH), va_n], axis=1)
        ki_all = jnp.concatenate([cache_kidx_a[l], ki_n], axis=1)
        kb_all = jnp.concatenate([cache_k_b[l].reshape(nbs, band, WIDTH), kb_n], axis=1)
        vb_all = jnp.concatenate([cache_v_b[l].reshape(nbs, band, WIDTH), vb_n], axis=1)
        kk_s = _pad_keys(jnp.concatenate([ki_all, ki_all], axis=2).astype(BF16), tk_s)
        maskT = _idx_mask(qiT, wiT, kk_s, nb=nbs, tq=tq_s, qpos0=past, n_valid=past + dec,
                          n_sel=n_sel_s)
        oa = _dsa_attn(qaT, _pad_keys(ka_all.astype(BF16), tk_s),
                       _key_tiles_T(_pad_keys(va_all.astype(BF16), tk_s)), maskT, dsa_bias_s,
                       nb=nbs, tq=tq_s, qpos0=past)
        ob = _band_attn(qbT, _pad_keys(kb_all.astype(BF16), tkb_s),
                        _key_tiles_T(_pad_keys(vb_all.astype(BF16), tkb_s)), band_bias_s,
                        nb=nbs, tq=tq_s, off=0)
        h2 = _mix_ln(h, oa, ob, w_gate, wba, wbb, wo, ln2_g[l], ln2_b[l], alpha)
        ys = _ffn_ln(h2, wi2, wo2, ln3_g[l], ln3_b[l], alpha)
        heads = lambda a: a.reshape(nbs, -1, N_HEADS, HEAD_DIM)
        st_s.append((heads(ka_n), heads(va_n), ki_n, heads(kb_all[:, -band:]),
                     heads(vb_all[:, -band:])))

    y_prompt = yp.reshape(nbp, seq, d)
    y_sample = ys.reshape(nbs, tq_s, d)[:, :dec]
    stack = lambda sts, t: jnp.stack([s[t] for s in sts])
    return (y_prompt, y_sample,
            stack(st_p, 0), stack(st_p, 1), stack(st_p, 2), stack(st_p, 3), stack(st_p, 4),
            stack(st_s, 0), stack(st_s, 1), stack(st_s, 2), stack(st_s, 3), stack(st_s, 4))
```

```python
import functools
import math

import jax
import jax.numpy as jnp
from jax import lax
from jax.experimental import pallas as pl
from jax.experimental.pallas import tpu as pltpu

F32 = jnp.float32
BF16 = jnp.bfloat16
I32 = jnp.int32
I16 = jnp.int16

CHUNK = 64
CHUNK_SHIFT = 6
HEAD_DIM = 64
N_HEADS = 8
WIDTH = N_HEADS * HEAD_DIM
N_HEADS_IDX = 4
IDX_DIM = 64
TOPK_MAX = 256
IDX_SCALE = (IDX_DIM * N_HEADS_IDX) ** -0.5
BAND_CHUNKS = 8
REL_BACK = 128
T5_BUCKETS = 32
T5_MAX_DIST = 128
LN_EPS = 1e-5
MASKED = -2.0 ** 100
LOG2E = math.log2(math.e)
V_ROWS = HEAD_DIM + 16
INT_MIN = -2 ** 31

ATT_TILE = 256
TOKEN_TILE = 512
FF_CHUNK = 256
VMEM_LIMIT_MB = 56


def _cparams(sem):
    return pltpu.CompilerParams(dimension_semantics=sem,
                                vmem_limit_bytes=VMEM_LIMIT_MB * 1024 * 1024)


def _const_spec(shape):
    nd = len(shape)
    return pl.BlockSpec(shape, lambda *_: (0,) * nd, pipeline_mode=pl.Buffered(1))


def _tree_sum(xs):
    return _tree_reduce(lambda a, b: a + b, xs)


def _tree_reduce(fn, xs):
    xs = list(xs)
    while len(xs) > 1:
        nxt = [fn(xs[a], xs[a + 1]) for a in range(0, len(xs) - 1, 2)]
        if len(xs) % 2:
            nxt.append(xs[-1])
        xs = nxt
    return xs[0]


def _layer_norm_rows(y, g, b):
    mu = jnp.mean(y, axis=-1, keepdims=True)
    d = y - mu
    var = jnp.mean(d * d, axis=-1, keepdims=True)
    return d * lax.rsqrt(var + LN_EPS) * g + b


def _ffn_ln_kernel(x_ref, wi_ref, wo_ref, g_ref, b_ref, o_ref, *, alpha, dff):
    x = x_ref[...]
    xb = x.astype(BF16)
    acc = jnp.zeros(x.shape, F32)
    for c in range(dff // FF_CHUNK):
        lo = c * FF_CHUNK
        a = jnp.dot(xb, wi_ref[:, lo:lo + FF_CHUNK], preferred_element_type=F32)
        u = jnp.dot(xb, wi_ref[:, dff + lo:dff + lo + FF_CHUNK], preferred_element_type=F32)
        hm = (a * jax.nn.sigmoid(a)) * u
        acc = acc + jnp.dot(hm.astype(BF16), wo_ref[lo:lo + FF_CHUNK, :],
                            preferred_element_type=F32)
    y = alpha * x + 0.5 * acc
    o_ref[...] = _layer_norm_rows(y, g_ref[...], b_ref[...])


def _ffn_ln(x, wi, wo, g, b, alpha):
    n, d = x.shape
    dff = wo.shape[0]
    tm = TOKEN_TILE
    return pl.pallas_call(
        functools.partial(_ffn_ln_kernel, alpha=alpha, dff=dff),
        out_shape=jax.ShapeDtypeStruct((n, d), F32),
        grid=(n // tm,),
        in_specs=[pl.BlockSpec((tm, d), lambda i: (i, 0)),
                  _const_spec((d, 2 * dff)), _const_spec((dff, d)),
                  _const_spec((1, d)), _const_spec((1, d))],
        out_specs=pl.BlockSpec((tm, d), lambda i: (i, 0)),
        compiler_params=_cparams(("parallel",)),
        name="ffn_ln",
    )(x, wi, wo, g.reshape(1, d), b.reshape(1, d))


_C_QA, _C_KA, _C_VA, _C_QI, _C_KI, _C_WI, _C_QB, _C_KB, _C_VB, _C_END = (
    0, 512, 1024, 1536, 1792, 1920, 2048, 2560, 3072, 3584)


def _inproj_kernel(h_ref, w_ref, qaT_ref, ka_ref, kab_ref, va_ref, vaT_ref, qiT_ref, ki_ref,
                   kk_ref, wiT_ref, qbT_ref, kbb_ref, vbT_ref, kbl_ref, vbl_ref, *, per):
    hb = h_ref[...].astype(BF16)

    def proj(c0, c1):
        return jnp.dot(hb, w_ref[:, c0:c1], preferred_element_type=F32)

    def put_tiles(ref, zt):
        tm = zt.shape[1]
        ones_slab = jnp.where(lax.broadcasted_iota(I32, (V_ROWS - HEAD_DIM, tm), 0) == 0, 1.0, 0.0)
        aug = jnp.concatenate(
            [piece for h in range(N_HEADS)
             for piece in (zt[h * HEAD_DIM:(h + 1) * HEAD_DIM, :], ones_slab)], axis=0)
        for c in range(ref.shape[0]):
            ref[c] = aug[:, c * ATT_TILE:(c + 1) * ATT_TILE].astype(ref.dtype)

    scale = HEAD_DIM ** -0.5 * LOG2E
    qaT_ref[...] = (proj(_C_QA, _C_KA) * scale).T.astype(BF16)
    ka = proj(_C_KA, _C_VA)
    ka_ref[...] = ka
    kab_ref[...] = ka.astype(BF16)
    va = proj(_C_VA, _C_QI)
    va_ref[...] = va
    put_tiles(vaT_ref, va.T)
    qiT_ref[...] = proj(_C_QI, _C_KI).T.astype(BF16)
    kk = proj(_C_KI, _C_WI)
    ki_ref[...] = kk[:, :IDX_DIM]
    kk_ref[...] = kk.astype(BF16)
    wiT_ref[...] = proj(_C_WI, _C_QB).T[:8, :]
    qbT_ref[...] = (proj(_C_QB, _C_KB) * scale).T.astype(BF16)
    kb = proj(_C_KB, _C_VB)
    kbb_ref[...] = kb.astype(BF16)
    vb = proj(_C_VB, _C_END)
    put_tiles(vbT_ref, vb.T)

    @pl.when(pl.program_id(0) % per == per - 1)
    def _():
        kbl_ref[...] = kb
        vbl_ref[...] = vb


def _in_proj(h, w, per):
    n, d = h.shape
    tm = TOKEN_TILE
    nt = n // tm
    sub = tm // ATT_TILE
    row = lambda width: pl.BlockSpec((tm, width), lambda i: (i, 0))
    col = lambda rows: pl.BlockSpec((rows, tm), lambda i: (0, i))
    til = pl.BlockSpec((sub, N_HEADS * V_ROWS, ATT_TILE), lambda i: (i, 0, 0))
    last = pl.BlockSpec((tm, WIDTH), lambda i: (i // per, 0))
    sds = jax.ShapeDtypeStruct
    out_shape = (
        sds((WIDTH, n), BF16),
        sds((n, WIDTH), F32), sds((n, WIDTH), BF16),
        sds((n, WIDTH), F32), sds((n // ATT_TILE, N_HEADS * V_ROWS, ATT_TILE), BF16),
        sds((N_HEADS_IDX * IDX_DIM, n), BF16),
        sds((n, IDX_DIM), F32), sds((n, 128), BF16),
        sds((8, n), F32),
        sds((WIDTH, n), BF16),
        sds((n, WIDTH), BF16),
        sds((n // ATT_TILE, N_HEADS * V_ROWS, ATT_TILE), BF16),
        sds((nt // per * tm, WIDTH), F32), sds((nt // per * tm, WIDTH), F32),
    )
    out_specs = (col(WIDTH), row(WIDTH), row(WIDTH), row(WIDTH), til,
                 col(N_HEADS_IDX * IDX_DIM), row(IDX_DIM), row(128), col(8),
                 col(WIDTH), row(WIDTH), til, last, last)
    return pl.pallas_call(
        functools.partial(_inproj_kernel, per=per),
        out_shape=out_shape,
        grid=(nt,),
        in_specs=[pl.BlockSpec((tm, d), lambda i: (i, 0)), _const_spec((d, _C_END))],
        out_specs=out_specs,
        compiler_params=_cparams(("arbitrary",)),
        name="in_proj",
    )(h, w)


def _flip(bits):
    return bits ^ (lax.shift_right_arithmetic(bits, 31) & 0x7FFFFFFF)


def _idx_mask_kernel(qiT_ref, wiT_ref, kk_ref, mask_ref, st_ref, hi_ref, lo_ref, d_ref, incl_ref,
                     *, tq, qpos0, n_valid, n_sel):
    tk = ATT_TILE
    nkt_total = st_ref.shape[0] // tk
    q0 = qpos0 + pl.program_id(1) * tq
    n_keys = jnp.minimum(q0 + tq, n_valid)
    nt = jnp.minimum(lax.shift_right_logical(n_keys + tk - 1, int(math.log2(tk))), nkt_total)

    lane = lax.broadcasted_iota(I32, (8, tq), 1)
    q_chunk = lax.shift_right_logical(q0 + lane, CHUNK_SHIFT)
    n_adm = jnp.minimum((q_chunk + 1) * CHUNK, n_valid)
    k_row = jnp.minimum(n_adm, n_sel)

    zeros = jnp.zeros((128 - IDX_DIM, tq), BF16)
    qh = [jnp.concatenate([qiT_ref[h * IDX_DIM:(h + 1) * IDX_DIM, :], zeros], axis=0)
          for h in range(N_HEADS_IDX)]
    scale_exact = math.frexp(IDX_SCALE)[0] == 0.5
    wh = [wiT_ref[h:h + 1, :] * IDX_SCALE if scale_exact else wiT_ref[h:h + 1, :]
          for h in range(N_HEADS_IDX)]

    def dots(j, h):
        r0 = pl.multiple_of(j * tk, tk)
        d_ref[h] = jnp.dot(kk_ref[pl.ds(r0, tk), :], qh[h], preferred_element_type=F32)

    def score_tile(j, last):
        r0 = pl.multiple_of(j * tk, tk)
        sc = None
        for h in range(N_HEADS_IDX):
            d = d_ref[h]
            if not last:
                dots(j + 1, h)
            term = wh[h] * jnp.maximum(d, 0.0)
            sc = term if sc is None else sc + term
        if not scale_exact:
            sc = sc * IDX_SCALE
        bits = lax.bitcast_convert_type(sc, I32)
        sign = lax.shift_right_arithmetic(bits, 31)
        key = (bits ^ (sign & 0x7FFFFFFF)) - sign
        if last:
            kpos = r0 + lax.broadcasted_iota(I32, (tk, tq), 0)
            key = jnp.where(kpos < n_adm[0:1, :], key, INT_MIN)
        st_ref[pl.ds(r0, tk), :] = key
        hi_ref[pl.ds(r0, tk), :] = lax.shift_right_arithmetic(key, 16).astype(I16)
        lo_ref[pl.ds(r0, tk), :] = (key ^ 0x8000).astype(I16)

    n_pairs = lax.shift_right_logical(nt + 1, 1)

    @pl.when((nt & 1) == 1)
    def _():
        r0 = pl.multiple_of(nt * tk, tk)
        hi_ref[pl.ds(r0, tk), :] = jnp.full((tk, tq), -2 ** 15, I16)
        lo_ref[pl.ds(r0, tk), :] = jnp.full((tk, tq), -2 ** 15, I16)

    for h in range(N_HEADS_IDX):
        dots(0, h)

    def score_pair(u, carry):
        score_tile(2 * u, False)
        score_tile(2 * u + 1, False)
        return carry

    lax.fori_loop(0, lax.shift_right_logical(nt - 1, 1), score_pair, 0)

    @pl.when(((nt - 1) & 1) == 1)
    def _():
        score_tile(nt - 2, False)

    score_tile(nt - 1, True)

    group = incl_ref.shape[0]
    n_groups = nt if group == 1 else lax.shift_right_logical(nt + 3, 2)
    for g in range(group - 1):
        @pl.when(nt + g < n_groups * group)
        def _(g=g):
            r0 = pl.multiple_of((nt + g) * tk, tk)
            st_ref[pl.ds(r0, tk), :] = jnp.full((tk, tq), INT_MIN, I32)

    def count16(ref, pred, thr8):
        thr = jnp.concatenate([thr8, thr8], axis=0).astype(I16)

        def body(u, acc):
            r0 = pl.multiple_of(u * (2 * tk), 2 * tk)
            blk = ref[pl.ds(r0, 2 * tk), :]
            parts = [jnp.where(pred(blk[r * 16:(r + 1) * 16, :], thr), jnp.int16(1), jnp.int16(0))
                     for r in range(2 * tk // 16)]
            return acc + _tree_sum(parts)

        acc = lax.fori_loop(0, n_pairs, body, jnp.zeros((16, tq), I16))
        return jnp.broadcast_to(jnp.sum(acc.astype(I32), axis=0, keepdims=True), (8, tq))

    ge = lambda a, b: a >= b
    gt = lambda a, b: a > b

    def kth_largest16(ref, want):
        def step(k, t8):
            cand = t8 + lax.shift_left(jnp.int32(1), 15 - k)
            return jnp.where(count16(ref, ge, cand) >= want, cand, t8)
        return lax.fori_loop(0, 16, step, jnp.full((8, tq), -2 ** 15, I32))

    t_hi = kth_largest16(hi_ref, k_row)
    n_above = count16(hi_ref, gt, t_hi)
    t_hi16 = jnp.concatenate([t_hi, t_hi], axis=0).astype(I16)

    def park(j, carry):
        r0 = pl.multiple_of(j * tk, tk)
        hi, lo = hi_ref[pl.ds(r0, tk), :], lo_ref[pl.ds(r0, tk), :]
        lo_ref[pl.ds(r0, tk), :] = jnp.concatenate(
            [jnp.where(hi[r * 16:(r + 1) * 16, :] == t_hi16, lo[r * 16:(r + 1) * 16, :],
                       jnp.int16(-2 ** 15)) for r in range(tk // 16)], axis=0)
        return carry

    lax.fori_loop(0, nt, park, 0)
    t_lo = kth_largest16(lo_ref, k_row - n_above)
    n_gt = n_above + count16(lo_ref, gt, t_lo)
    t_row = (t_hi * 65536 + (t_lo + 2 ** 15))[0:1, :]
    ties_wanted = (k_row - n_gt)[0:1, :].astype(F32)

    tri = jnp.where(lax.broadcasted_iota(I32, (tk, tk), 0) >= lax.broadcasted_iota(I32, (tk, tk), 1),
                    1.0, 0.0).astype(BF16)

    def tie_prefix(j, g):
        r0 = pl.multiple_of(j * tk, tk)
        eq = jnp.where(st_ref[pl.ds(r0, tk), :] == t_row, 1.0, 0.0)
        incl_ref[g] = jnp.dot(tri, eq.astype(BF16), preferred_element_type=F32)

    def final_group(u, seen, last):
        for g in range(group):
            j = u * group + g
            r0 = pl.multiple_of(j * tk, tk)
            incl = incl_ref[g]
            if not last:
                tie_prefix(j + group, g)
            rank = seen + incl
            bar = t_row + jnp.where(rank <= ties_wanted, 0, 1)
            sel = jnp.where(st_ref[pl.ds(r0, tk), :] >= bar, 1, 0)
            mask_ref[pl.ds(r0, tk), :] = sel.astype(jnp.int8)
            seen = seen + incl[tk - 1:tk, :]
        return seen

    for g in range(group):
        tie_prefix(g, g)
    seen = lax.fori_loop(0, n_groups - 1, lambda u, seen: final_group(u, seen, False),
                         jnp.zeros((1, tq), F32))
    final_group(n_groups - 1, seen, True)

    def zero_tile(j, carry):
        r0 = pl.multiple_of(j * tk, tk)
        mask_ref[pl.ds(r0, tk), :] = jnp.zeros((tk, tq), jnp.int8)
        return carry

    lax.fori_loop(n_groups * group, nkt_total, zero_tile, 0)


def _idx_mask(qiT, wiT, kk, *, nb, tq, qpos0, n_valid, n_sel):
    tkp = kk.shape[1]
    nq = qiT.shape[1] // nb // tq
    return pl.pallas_call(
        functools.partial(_idx_mask_kernel, tq=tq, qpos0=qpos0, n_valid=n_valid, n_sel=n_sel),
        out_shape=jax.ShapeDtypeStruct((nb, tkp, nq * tq), jnp.int8),
        grid=(nb, nq),
        in_specs=[pl.BlockSpec((N_HEADS_IDX * IDX_DIM, tq), lambda b, i: (0, b * nq + i)),
                  pl.BlockSpec((8, tq), lambda b, i: (0, b * nq + i)),
                  pl.BlockSpec((None, tkp, 128), lambda b, i: (b, 0, 0))],
        out_specs=pl.BlockSpec((None, tkp, tq), lambda b, i: (b, 0, i)),
        scratch_shapes=[pltpu.VMEM((tkp, tq), I32), pltpu.VMEM((tkp + ATT_TILE, tq), I16),
                        pltpu.VMEM((tkp + ATT_TILE, tq), I16),
                        pltpu.VMEM((N_HEADS_IDX, ATT_TILE, tq), F32),
                        pltpu.VMEM((4 if (tkp // ATT_TILE) % 4 == 0 else 1, ATT_TILE, tq), F32)],
        compiler_params=_cparams(("parallel", "parallel")),
        name="idx_mask",
    )(qiT, wiT, kk)


def _load_qsel(qT_ref, qsel_ref, tq):
    zeros = jnp.zeros((HEAD_DIM, tq), BF16)
    for h in range(N_HEADS):
        blk = qT_ref[h * HEAD_DIM:(h + 1) * HEAD_DIM, :]
        qsel_ref[h] = jnp.concatenate([blk, zeros] if h % 2 == 0 else [zeros, blk], axis=0)


def _init_state(m_ref, acc_ref):
    m_ref[...] = jnp.full(m_ref.shape, MASKED, F32)
    acc_ref[...] = jnp.zeros(acc_ref.shape, F32)


def _scores(k_slab, qsel_ref, s_ref, h):
    s_ref[h] = jnp.dot(k_slab(h // 2), qsel_ref[h], preferred_element_type=F32)


def _heads_tile(v_rows, next_k_slab, qsel_ref, s_ref, m_ref, acc_ref, addend):
    tq = s_ref.shape[2]
    slabs = s_ref.shape[1] // 16
    for h in range(N_HEADS):
        s = s_ref[h].astype(BF16)
        for a in addend(h):
            s = s + a
        if next_k_slab is not None:
            _scores(next_k_slab, qsel_ref, s_ref, h)
        m_old = m_ref[h:h + 1, :]
        top = _tree_reduce(jnp.maximum, [s[r * 16:(r + 1) * 16, :] for r in range(slabs)])
        m_new = jnp.maximum(m_old, jnp.max(top.astype(F32), axis=0, keepdims=True))
        alpha = jnp.exp2(m_old - m_new)
        m16 = jnp.broadcast_to(m_new, (16, tq)).astype(BF16)
        p = jnp.concatenate([jnp.exp2(s[r * 16:(r + 1) * 16, :] - m16) for r in range(slabs)], axis=0)
        m_ref[h:h + 1, :] = m_new
        pv = jnp.dot(v_rows(h), p, preferred_element_type=F32)
        rows = slice(h * V_ROWS, (h + 1) * V_ROWS)
        acc_ref[rows, :] = alpha * acc_ref[rows, :] + pv


def _attn_scratch(tq):
    return [pltpu.VMEM((N_HEADS, 128, tq), BF16), pltpu.VMEM((N_HEADS, ATT_TILE, tq), F32),
            pltpu.VMEM((N_HEADS, tq), F32), pltpu.VMEM((N_HEADS * V_ROWS, tq), F32)]


def _store_out(o_ref, acc_ref):
    outs = [acc_ref[h * V_ROWS:h * V_ROWS + HEAD_DIM, :]
            / acc_ref[h * V_ROWS + HEAD_DIM:h * V_ROWS + HEAD_DIM + 1, :] for h in range(N_HEADS)]
    o_ref[...] = jnp.concatenate(outs, axis=0).T.astype(o_ref.dtype)


def _dsa_kernel(qT_ref, k_ref, vT_ref, mask_ref, bias_ref, o_ref, qsel_ref, s_ref, m_ref, acc_ref,
                *, tq, qpos0):
    tk = ATT_TILE
    jd = lax.shift_right_logical(qpos0 + pl.program_id(1) * tq, int(math.log2(tk)))
    _load_qsel(qT_ref, qsel_ref, tq)
    _init_state(m_ref, acc_ref)

    def k_slab(j):
        r0 = pl.multiple_of(j * tk, tk)
        return lambda g: k_ref[pl.ds(r0, tk), g * 128:(g + 1) * 128]

    def tile(j, near, last=False):
        r0 = pl.multiple_of(j * tk, tk)
        unselected = jnp.where(mask_ref[pl.ds(r0, tk), :].astype(I32) != 0, 0.0,
                               MASKED).astype(BF16)
        addend = ((lambda h: (unselected,)) if near is None
                  else (lambda h: (unselected, bias_ref[near, h])))
        _heads_tile(lambda h: vT_ref[j, h * V_ROWS:(h + 1) * V_ROWS, :],
                    None if last else k_slab(j + 1),
                    qsel_ref, s_ref, m_ref, acc_ref, addend)

    for h in range(N_HEADS):
        _scores(k_slab(0), qsel_ref, s_ref, h)

    n_far = jnp.maximum(jd - 1, 0)

    def far_quad(u, carry):
        for c in range(4):
            tile(4 * u + c, None)
        return carry

    n_quads = lax.shift_right_logical(n_far, 2)
    lax.fori_loop(0, n_quads, far_quad, 0)

    @pl.when((n_far & 2) == 2)
    def _():
        tile(4 * n_quads, None)
        tile(4 * n_quads + 1, None)

    @pl.when((n_far & 1) == 1)
    def _():
        tile(n_far - 1, None)

    @pl.when(jd >= 1)
    def _():
        tile(jd - 1, 0)

    tile(jd, 1, last=True)
    _store_out(o_ref, acc_ref)


def _dsa_attn(qT, k, vT, maskT, bias, *, nb, tq, qpos0):
    tkp = k.shape[1]
    nkt = tkp // ATT_TILE
    nq = qT.shape[1] // nb // tq
    return pl.pallas_call(
        functools.partial(_dsa_kernel, tq=tq, qpos0=qpos0),
        out_shape=jax.ShapeDtypeStruct((nb * nq * tq, WIDTH), BF16),
        grid=(nb, nq),
        in_specs=[pl.BlockSpec((WIDTH, tq), lambda b, i: (0, b * nq + i)),
                  pl.BlockSpec((None, tkp, WIDTH), lambda b, i: (b, 0, 0),
                               pipeline_mode=pl.Buffered(1)),
                  pl.BlockSpec((nkt, N_HEADS * V_ROWS, ATT_TILE), lambda b, i: (b, 0, 0),
                               pipeline_mode=pl.Buffered(1)),
                  pl.BlockSpec((None, tkp, tq), lambda b, i: (b, 0, i)),
                  _const_spec(bias.shape)],
        out_specs=pl.BlockSpec((tq, WIDTH), lambda b, i: (b * nq + i, 0)),
        scratch_shapes=_attn_scratch(tq),
        compiler_params=_cparams(("parallel", "parallel")),
        name="dsa_attn",
    )(qT, k, vT, maskT, bias)


_BAND_TILES = BAND_CHUNKS * CHUNK // ATT_TILE + 1


def _band_kernel(qT_ref, *refs, tq, off):
    k_refs = refs[:_BAND_TILES]
    v_refs = refs[_BAND_TILES:2 * _BAND_TILES]
    bias_ref, o_ref, qsel_ref, s_ref, m_ref, acc_ref = refs[2 * _BAND_TILES:]
    _load_qsel(qT_ref, qsel_ref, tq)
    _init_state(m_ref, acc_ref)
    k_slab = lambda w: (lambda g: k_refs[w][:, g * 128:(g + 1) * 128])
    for h in range(N_HEADS):
        _scores(k_slab(0), qsel_ref, s_ref, h)
    for w in range(_BAND_TILES):
        entry = jnp.where(pl.program_id(1) + off + w >= 0, w, _BAND_TILES)
        _heads_tile(lambda h: v_refs[w][0, h * V_ROWS:(h + 1) * V_ROWS, :],
                    None if w == _BAND_TILES - 1 else k_slab(w + 1),
                    qsel_ref, s_ref, m_ref, acc_ref, lambda h: (bias_ref[entry, h],))
    _store_out(o_ref, acc_ref)


def _band_attn(qT, k, vT, bias, *, nb, tq, off):
    nkt = k.shape[1] // ATT_TILE
    nq = qT.shape[1] // nb // tq
    tile_of = lambda i, w: jnp.maximum(i + off + w, 0)
    k_specs = [pl.BlockSpec((None, ATT_TILE, WIDTH), lambda b, i, w=w: (b, tile_of(i, w), 0))
               for w in range(_BAND_TILES)]
    v_specs = [pl.BlockSpec((1, N_HEADS * V_ROWS, ATT_TILE),
                            lambda b, i, w=w: (b * nkt + tile_of(i, w), 0, 0))
               for w in range(_BAND_TILES)]
    return pl.pallas_call(
        functools.partial(_band_kernel, tq=tq, off=off),
        out_shape=jax.ShapeDtypeStruct((nb * nq * tq, WIDTH), BF16),
        grid=(nb, nq),
        in_specs=[pl.BlockSpec((WIDTH, tq), lambda b, i: (0, b * nq + i))] + k_specs + v_specs
                 + [_const_spec(bias.shape)],
        out_specs=pl.BlockSpec((tq, WIDTH), lambda b, i: (b * nq + i, 0)),
        scratch_shapes=_attn_scratch(tq),
        compiler_params=_cparams(("parallel", "parallel")),
        name="band_attn",
    )(qT, *([k] * _BAND_TILES), *([vT] * _BAND_TILES), bias)


def _mix_ln_kernel(h_ref, oa_ref, ob_ref, wg_ref, wba_ref, wbb_ref, wo_ref, g_ref, b_ref, o_ref,
                   *, alpha):
    h = h_ref[...]
    hb = h.astype(BF16)
    d = h.shape[1]
    ga = jnp.dot(hb, wg_ref[:, :d], preferred_element_type=F32)
    gb = jnp.dot(hb, wg_ref[:, d:], preferred_element_type=F32)
    ya = jnp.dot(oa_ref[...], wba_ref[...], preferred_element_type=F32)
    yb = jnp.dot(ob_ref[...], wbb_ref[...], preferred_element_type=F32)
    gated = jax.nn.sigmoid(ga) * ya + jax.nn.sigmoid(gb) * yb
    mix = jnp.dot(gated.astype(BF16), wo_ref[...], preferred_element_type=F32)
    o_ref[...] = _layer_norm_rows(alpha * h + mix, g_ref[...], b_ref[...])


def _mix_ln(h, oa, ob, wg, wba, wbb, wo, g, b, alpha):
    n, d = h.shape
    tm = TOKEN_TILE
    row = lambda width: pl.BlockSpec((tm, width), lambda i: (i, 0))
    return pl.pallas_call(
        functools.partial(_mix_ln_kernel, alpha=alpha),
        out_shape=jax.ShapeDtypeStruct((n, d), F32),
        grid=(n // tm,),
        in_specs=[row(d), row(WIDTH), row(WIDTH), _const_spec(wg.shape), _const_spec(wba.shape),
                  _const_spec(wbb.shape), _const_spec(wo.shape), _const_spec((1, d)),
                  _const_spec((1, d))],
        out_specs=row(d),
        compiler_params=_cparams(("parallel",)),
        name="mix_ln",
    )(h, oa, ob, wg, wba, wbb, wo, g.reshape(1, d), b.reshape(1, d))


def _t5_bucket(rel):
    half = T5_BUCKETS // 2
    exact = half // 2
    n = jnp.abs(rel)
    log_ratio = jnp.log(jnp.maximum(n, 1).astype(F32) / exact) / math.log(T5_MAX_DIST / exact)
    large = jnp.minimum(exact + (log_ratio * (half - exact)).astype(I32), half - 1)
    return (rel > 0).astype(I32) * half + jnp.where(n < exact, n, large)


def _rel_line(n_tiles, first, tq):
    period = ATT_TILE + tq
    y = jnp.arange(period, dtype=I32)[None, :]
    c = (jnp.arange(n_tiles, dtype=I32)[:, None] + first) * ATT_TILE
    return jnp.where(y < tq, c - y, c + period - y)


def _toeplitz(v, tq):
    period = v.shape[-1]
    flat = jnp.tile(v, (1,) * (v.ndim - 1) + (ATT_TILE,))[..., :ATT_TILE * (period - 1)]
    return flat.reshape(v.shape[:-1] + (ATT_TILE, period - 1))[..., :tq]


def _dsa_bias_table(t5_bias, tq):
    assert T5_MAX_DIST <= ATT_TILE + 1
    far = t5_bias[:, _t5_bucket(jnp.int32(-(ATT_TILE + 1)))]
    line = t5_bias[:, _t5_bucket(_rel_line(2, -1, tq))] - far[:, None, None]
    return (_toeplitz(jnp.moveaxis(line, 0, 1), tq) * LOG2E).astype(BF16)


def _band_bias_table(rel_bias, tq):
    first = 1 - _BAND_TILES
    ridx = jnp.clip(_rel_line(_BAND_TILES, first, tq), -REL_BACK, CHUNK - 1) + REL_BACK
    bias = _toeplitz(jnp.moveaxis(rel_bias[:, ridx], 0, 1), tq)
    w = jnp.arange(_BAND_TILES, dtype=I32)[:, None, None]
    j = jnp.arange(ATT_TILE, dtype=I32)[None, :, None]
    i = jnp.arange(tq, dtype=I32)[None, None, :]
    dchunk = ((w + first) * ATT_TILE + j) // CHUNK - i // CHUNK
    ok = (dchunk <= 0) & (dchunk >= -BAND_CHUNKS)
    table = jnp.where(ok[:, None], bias * LOG2E, MASKED)
    return jnp.concatenate([table, jnp.full_like(table[:1], MASKED)], axis=0).astype(BF16)


def _regroup_w_in(w_in):
    sizes = (WIDTH, WIDTH, WIDTH, N_HEADS_IDX * IDX_DIM, IDX_DIM, N_HEADS_IDX,
             WIDTH, WIDTH, WIDTH, w_in.shape[0], w_in.shape[0])
    offs = [0]
    for s in sizes:
        offs.append(offs[-1] + s)
    qa, ka, va, qi, ki, wi, qb, kb, vb, ga, gb = (w_in[:, offs[t]:offs[t + 1]] for t in range(11))
    wi_pad = jnp.pad(wi, ((0, 0), (0, 128 - N_HEADS_IDX)))
    w_proj = jnp.concatenate([qa, ka, va, qi, ki, ki, wi_pad, qb, kb, vb], axis=1).astype(BF16)
    w_gate = jnp.concatenate([ga, gb], axis=1).astype(BF16)
    return w_proj, w_gate


def _key_tiles_T(v):
    nb, tkp, wd = v.shape
    vt = v.reshape(nb, tkp // ATT_TILE, ATT_TILE, N_HEADS, HEAD_DIM).transpose(0, 1, 3, 4, 2)
    ones_slab = jnp.zeros(vt.shape[:3] + (V_ROWS - HEAD_DIM, ATT_TILE), v.dtype).at[..., 0, :].set(1)
    return jnp.concatenate([vt, ones_slab], axis=3).reshape(-1, N_HEADS * V_ROWS, ATT_TILE)


def _pad_keys(x, tkp):
    return jnp.pad(x, ((0, 0), (0, tkp - x.shape[1]), (0, 0)))


def kernel(x_prompt, x_sample, cache_k_a, cache_v_a, cache_kidx_a, cache_k_b, cache_v_b, t5_bias,
           ln1_g, ln1_b, ffn1_wi, ffn1_wo, ln2_g, ln2_b, w_in, rel_bias_b, w_branch_a, w_branch_b,
           w_out, ln3_g, ln3_b, ffn2_wi, ffn2_wo):
    depth = ln1_g.shape[0]
    alpha = (2.0 * depth) ** 0.25
    nbp, seq, d = x_prompt.shape
    nbs, dec, _ = x_sample.shape
    past = cache_k_a.shape[2]
    band = cache_k_b.shape[2]
    keep = min(BAND_CHUNKS * CHUNK, seq)
    tq_p, tq_s = ATT_TILE, 128
    assert seq % TOKEN_TILE == 0 and keep == TOKEN_TILE and dec <= tq_s and past % ATT_TILE == 0
    assert band == BAND_CHUNKS * CHUNK and (nbs * tq_s) % TOKEN_TILE == 0

    yp = x_prompt.reshape(nbp * seq, d)
    ys = jnp.pad(x_sample, ((0, 0), (0, tq_s - dec), (0, 0))).reshape(nbs * tq_s, d)
    dsa_bias_p = _dsa_bias_table(t5_bias, tq_p)
    dsa_bias_s = dsa_bias_p[..., :tq_s]
    n_sel_p = min(TOPK_MAX, seq // 4)
    n_sel_s = min(TOPK_MAX, (past + dec) // 4)
    tk_s = -(-(past + tq_s) // ATT_TILE) * ATT_TILE
    tkb_s = _BAND_TILES * ATT_TILE

    st_p, st_s = [], []
    for l in range(depth):
        w_proj, w_gate = _regroup_w_in(w_in[l])
        wi1, wo1 = ffn1_wi[l].astype(BF16), ffn1_wo[l].astype(BF16)
        wi2, wo2 = ffn2_wi[l].astype(BF16), ffn2_wo[l].astype(BF16)
        wba, wbb, wo = (w_branch_a[l].astype(BF16), w_branch_b[l].astype(BF16),
                        w_out[l].astype(BF16))
        band_bias_p = _band_bias_table(rel_bias_b[l], tq_p)
        band_bias_s = band_bias_p[..., :tq_s]

        h = _ffn_ln(yp, wi1, wo1, ln1_g[l], ln1_b[l], alpha)
        (qaT, ka, kab, va, vaT, qiT, ki, kk, wiT, qbT, kbb, vbT, kbl, vbl) = _in_proj(
            h, w_proj, seq // TOKEN_TILE)
        maskT = _idx_mask(qiT, wiT, kk.reshape(nbp, seq, 128), nb=nbp, tq=tq_p, qpos0=0,
                          n_valid=seq, n_sel=n_sel_p)
        oa = _dsa_attn(qaT, kab.reshape(nbp, seq, WIDTH), vaT, maskT, dsa_bias_p,
                       nb=nbp, tq=tq_p, qpos0=0)
        ob = _band_attn(qbT, kbb.reshape(nbp, seq, WIDTH), vbT, band_bias_p,
                        nb=nbp, tq=tq_p, off=1 - _BAND_TILES)
        h2 = _mix_ln(h, oa, ob, w_gate, wba, wbb, wo, ln2_g[l], ln2_b[l], alpha)
        yp = _ffn_ln(h2, wi2, wo2, ln3_g[l], ln3_b[l], alpha)
        st_p.append((ka.reshape(nbp, seq, N_HEADS, HEAD_DIM), va.reshape(nbp, seq, N_HEADS, HEAD_DIM),
                     ki.reshape(nbp, seq, IDX_DIM), kbl.reshape(nbp, keep, N_HEADS, HEAD_DIM),
                     vbl.reshape(nbp, keep, N_HEADS, HEAD_DIM)))

        h = _ffn_ln(ys, wi1, wo1, ln1_g[l], ln1_b[l], alpha)
        (qaT, ka, _, va, _, qiT, ki, _, wiT, qbT, _, _, kbl, vbl) = _in_proj(h, w_proj, 1)
        new = lambda a: a.reshape(nbs, tq_s, -1)[:, :dec]
        ka_n, va_n, ki_n, kb_n, vb_n = new(ka), new(va), new(ki), new(kbl), new(vbl)
        ka_all = jnp.concatenate([cache_k_a[l].reshape(nbs, past, WIDTH), ka_n], axis=1)
        va_all = jnp.concatenate([cache_v_a[l].reshape(nbs, past, WIDTH), va_n], axis=1)
        ki_all = jnp.concatenate([cache_kidx_a[l], ki_n], axis=1)
        kb_all = jnp.concatenate([cache_k_b[l].reshape(nbs, band, WIDTH), kb_n], axis=1)
        vb_all = jnp.concatenate([cache_v_b[l].reshape(nbs, band, WIDTH), vb_n], axis=1)
        kk_s = _pad_keys(jnp.concatenate([ki_all, ki_all], axis=2).astype(BF16), tk_s)
        maskT = _idx_mask(qiT, wiT, kk_s, nb=nbs, tq=tq_s, qpos0=past, n_valid=past + dec,
                          n_sel=n_sel_s)
        oa = _dsa_attn(qaT, _pad_keys(ka_all.astype(BF16), tk_s),
                       _key_tiles_T(_pad_keys(va_all.astype(BF16), tk_s)), maskT, dsa_bias_s,
                       nb=nbs, tq=tq_s, qpos0=past)
        ob = _band_attn(qbT, _pad_keys(kb_all.astype(BF16), tkb_s),
                        _key_tiles_T(_pad_keys(vb_all.astype(BF16), tkb_s)), band_bias_s,
                        nb=nbs, tq=tq_s, off=0)
        h2 = _mix_ln(h, oa, ob, w_gate, wba, wbb, wo, ln2_g[l], ln2_b[l], alpha)
        ys = _ffn_ln(h2, wi2, wo2, ln3_g[l], ln3_b[l], alpha)
        heads = lambda a: a.reshape(nbs, -1, N_HEADS, HEAD_DIM)
        st_s.append((heads(ka_n), heads(va_n), ki_n, heads(kb_all[:, -band:]),
                     heads(vb_all[:, -band:])))

    y_prompt = yp.reshape(nbp, seq, d)
    y_sample = ys.reshape(nbs, tq_s, d)[:, :dec]
    stack = lambda sts, t: jnp.stack([s[t] for s in sts])
    return (y_prompt, y_sample,
            stack(st_p, 0), stack(st_p, 1), stack(st_p, 2), stack(st_p, 3), stack(st_p, 4),
            stack(st_s, 0), stack(st_s, 1), stack(st_s, 2), stack(st_s, 3), stack(st_s, 4))
```

```python
import functools
import math

import jax
import jax.numpy as jnp
from jax import lax
from jax.experimental import pallas as pl
from jax.experimental.pallas import tpu as pltpu

F32 = jnp.float32
BF16 = jnp.bfloat16
I32 = jnp.int32
I16 = jnp.int16

CHUNK = 64
CHUNK_SHIFT = 6
HEAD_DIM = 64
N_HEADS = 8
WIDTH = N_HEADS * HEAD_DIM
N_HEADS_IDX = 4
IDX_DIM = 64
TOPK_MAX = 256
IDX_SCALE = (IDX_DIM * N_HEADS_IDX) ** -0.5
BAND_CHUNKS = 8
REL_BACK = 128
T5_BUCKETS = 32
T5_MAX_DIST = 128
LN_EPS = 1e-5
MASKED = -2.0 ** 100
LOG2E = math.log2(math.e)
V_ROWS = HEAD_DIM + 16
INT_MIN = -2 ** 31

ATT_TILE = 256
TOKEN_TILE = 512
FF_CHUNK = 256
VMEM_LIMIT_MB = 56


def _cparams(sem):
    return pltpu.CompilerParams(dimension_semantics=sem,
                                vmem_limit_bytes=VMEM_LIMIT_MB * 1024 * 1024)


def _const_spec(shape):
    nd = len(shape)
    return pl.BlockSpec(shape, lambda *_: (0,) * nd, pipeline_mode=pl.Buffered(1))


def _tree_sum(xs):
    return _tree_reduce(lambda a, b: a + b, xs)


def _tree_reduce(fn, xs):
    xs = list(xs)
    while len(xs) > 1:
        nxt = [fn(xs[a], xs[a + 1]) for a in range(0, len(xs) - 1, 2)]
        if len(xs) % 2:
            nxt.append(xs[-1])
        xs = nxt
    return xs[0]


def _layer_norm_rows(y, g, b):
    mu = jnp.mean(y, axis=-1, keepdims=True)
    d = y - mu
    var = jnp.mean(d * d, axis=-1, keepdims=True)
    return d * lax.rsqrt(var + LN_EPS) * g + b


def _ffn_ln_kernel(x_ref, wi_ref, wo_ref, g_ref, b_ref, o_ref, *, alpha, dff):
    x = x_ref[...]
    xb = x.astype(BF16)
    acc = jnp.zeros(x.shape, F32)
    for c in range(dff // FF_CHUNK):
        lo = c * FF_CHUNK
        a = jnp.dot(xb, wi_ref[:, lo:lo + FF_CHUNK], preferred_element_type=F32)
        u = jnp.dot(xb, wi_ref[:, dff + lo:dff + lo + FF_CHUNK], preferred_element_type=F32)
        hm = (a * jax.nn.sigmoid(a)) * u
        acc = acc + jnp.dot(hm.astype(BF16), wo_ref[lo:lo + FF_CHUNK, :],
                            preferred_element_type=F32)
    y = alpha * x + 0.5 * acc
    o_ref[...] = _layer_norm_rows(y, g_ref[...], b_ref[...])


def _ffn_ln(x, wi, wo, g, b, alpha):
    n, d = x.shape
    dff = wo.shape[0]
    tm = TOKEN_TILE
    return pl.pallas_call(
        functools.partial(_ffn_ln_kernel, alpha=alpha, dff=dff),
        out_shape=jax.ShapeDtypeStruct((n, d), F32),
        grid=(n // tm,),
        in_specs=[pl.BlockSpec((tm, d), lambda i: (i, 0)),
                  _const_spec((d, 2 * dff)), _const_spec((dff, d)),
                  _const_spec((1, d)), _const_spec((1, d))],
        out_specs=pl.BlockSpec((tm, d), lambda i: (i, 0)),
        compiler_params=_cparams(("parallel",)),
        name="ffn_ln",
    )(x, wi, wo, g.reshape(1, d), b.reshape(1, d))


_C_QA, _C_KA, _C_VA, _C_QI, _C_KI, _C_WI, _C_QB, _C_KB, _C_VB, _C_END = (
    0, 512, 1024, 1536, 1792, 1920, 2048, 2560, 3072, 3584)


def _inproj_kernel(h_ref, w_ref, qaT_ref, ka_ref, kab_ref, va_ref, vaT_ref, qiT_ref, ki_ref,
                   kk_ref, wiT_ref, qbT_ref, kbb_ref, vbT_ref, kbl_ref, vbl_ref, *, per):
    hb = h_ref[...].astype(BF16)

    def proj(c0, c1):
        return jnp.dot(hb, w_ref[:, c0:c1], preferred_element_type=F32)

    def put_tiles(ref, zt):
        tm = zt.shape[1]
        ones_slab = jnp.where(lax.broadcasted_iota(I32, (V_ROWS - HEAD_DIM, tm), 0) == 0, 1.0, 0.0)
        aug = jnp.concatenate(
            [piece for h in range(N_HEADS)
             for piece in (zt[h * HEAD_DIM:(h + 1) * HEAD_DIM, :], ones_slab)], axis=0)
        for c in range(ref.shape[0]):
            ref[c] = aug[:, c * ATT_TILE:(c + 1) * ATT_TILE].astype(ref.dtype)

    scale = HEAD_DIM ** -0.5 * LOG2E
    qaT_ref[...] = (proj(_C_QA, _C_KA) * scale).T.astype(BF16)
    ka = proj(_C_KA, _C_VA)
    ka_ref[...] = ka
    kab_ref[...] = ka.astype(BF16)
    va = proj(_C_VA, _C_QI)
    va_ref[...] = va
    put_tiles(vaT_ref, va.T)
    qiT_ref[...] = proj(_C_QI, _C_KI).T.astype(BF16)
    kk = proj(_C_KI, _C_WI)
    ki_ref[...] = kk[:, :IDX_DIM]
    kk_ref[...] = kk.astype(BF16)
    wiT_ref[...] = proj(_C_WI, _C_QB).T[:8, :]
    qbT_ref[...] = (proj(_C_QB, _C_KB) * scale).T.astype(BF16)
    kb = proj(_C_KB, _C_VB)
    kbb_ref[...] = kb.astype(BF16)
    vb = proj(_C_VB, _C_END)
    put_tiles(vbT_ref, vb.T)

    @pl.when(pl.program_id(0) % per == per - 1)
    def _():
        kbl_ref[...] = kb
        vbl_ref[...] = vb


def _in_proj(h, w, per):
    n, d = h.shape
    tm = TOKEN_TILE
    nt = n // tm
    sub = tm // ATT_TILE
    row = lambda width: pl.BlockSpec((tm, width), lambda i: (i, 0))
    col = lambda rows: pl.BlockSpec((rows, tm), lambda i: (0, i))
    til = pl.BlockSpec((sub, N_HEADS * V_ROWS, ATT_TILE), lambda i: (i, 0, 0))
    last = pl.BlockSpec((tm, WIDTH), lambda i: (i // per, 0))
    sds = jax.ShapeDtypeStruct
    out_shape = (
        sds((WIDTH, n), BF16),
        sds((n, WIDTH), F32), sds((n, WIDTH), BF16),
        sds((n, WIDTH), F32), sds((n // ATT_TILE, N_HEADS * V_ROWS, ATT_TILE), BF16),
        sds((N_HEADS_IDX * IDX_DIM, n), BF16),
        sds((n, IDX_DIM), F32), sds((n, 128), BF16),
        sds((8, n), F32),
        sds((WIDTH, n), BF16),
        sds((n, WIDTH), BF16),
        sds((n // ATT_TILE, N_HEADS * V_ROWS, ATT_TILE), BF16),
        sds((nt // per * tm, WIDTH), F32), sds((nt // per * tm, WIDTH), F32),
    )
    out_specs = (col(WIDTH), row(WIDTH), row(WIDTH), row(WIDTH), til,
                 col(N_HEADS_IDX * IDX_DIM), row(IDX_DIM), row(128), col(8),
                 col(WIDTH), row(WIDTH), til, last, last)
    return pl.pallas_call(
        functools.partial(_inproj_kernel, per=per),
        out_shape=out_shape,
        grid=(nt,),
        in_specs=[pl.BlockSpec((tm, d), lambda i: (i, 0)), _const_spec((d, _C_END))],
        out_specs=out_specs,
        compiler_params=_cparams(("arbitrary",)),
        name="in_proj",
    )(h, w)


def _flip(bits):
    return bits ^ (lax.shift_right_arithmetic(bits, 31) & 0x7FFFFFFF)


def _bit_planes(words):
    a = list(reversed(words))
    j, m = 16, 0x0000FFFF
    while j:
        k = 0
        while k < 32:
            t = (a[k] ^ lax.shift_right_logical(a[k + j], j)) & m
            a[k] = a[k] ^ t
            a[k + j] = a[k + j] ^ lax.shift_left(t, j)
            k = (k + j + 1) & ~j
        j >>= 1
        m = (m ^ (m << j)) & 0xFFFFFFFF
    return a


def _idx_mask_kernel(qiT_ref, wiT_ref, kk_ref, mask_ref, st_ref, p_ref, c_ref, d_ref, incl_ref,
                     *, tq, qpos0, n_valid, n_sel):
    tk = ATT_TILE
    assert tk == 32 * 8
    nkt_total = st_ref.shape[0] // tk

    @pl.when((pl.program_id(0) == 0) & (pl.program_id(1) == 0))
    def _():
        p_ref[...] = jnp.zeros(p_ref.shape, I32)

    q0 = qpos0 + pl.program_id(1) * tq
    n_keys = jnp.minimum(q0 + tq, n_valid)
    nt = jnp.minimum(lax.shift_right_logical(n_keys + tk - 1, int(math.log2(tk))), nkt_total)

    lane = lax.broadcasted_iota(I32, (8, tq), 1)
    q_chunk = lax.shift_right_logical(q0 + lane, CHUNK_SHIFT)
    n_adm = jnp.minimum((q_chunk + 1) * CHUNK, n_valid)
    k_row = jnp.minimum(n_adm, n_sel)

    zeros = jnp.zeros((128 - IDX_DIM, tq), BF16)
    qh = [jnp.concatenate([qiT_ref[h * IDX_DIM:(h + 1) * IDX_DIM, :], zeros], axis=0)
          for h in range(N_HEADS_IDX)]
    scale_exact = math.frexp(IDX_SCALE)[0] == 0.5
    wh = [wiT_ref[h:h + 1, :] * IDX_SCALE if scale_exact else wiT_ref[h:h + 1, :]
          for h in range(N_HEADS_IDX)]

    def dots(j, h):
        r0 = pl.multiple_of(j * tk, tk)
        d_ref[h] = jnp.dot(kk_ref[pl.ds(r0, tk), :], qh[h], preferred_element_type=F32)

    def score_tile(j, last):
        r0 = pl.multiple_of(j * tk, tk)
        sc = None
        for h in range(N_HEADS_IDX):
            d = d_ref[h]
            if not last:
                dots(j + 1, h)
            term = wh[h] * jnp.maximum(d, 0.0)
            sc = term if sc is None else sc + term
        if not scale_exact:
            sc = sc * IDX_SCALE
        bits = lax.bitcast_convert_type(sc, I32)
        sign = lax.shift_right_arithmetic(bits, 31)
        key = (bits ^ (sign & 0x7FFFFFFF)) - sign
        if last:
            kpos = r0 + lax.broadcasted_iota(I32, (tk, tq), 0)
            key = jnp.where(kpos < n_adm[0:1, :], key, INT_MIN)
        st_ref[pl.ds(r0, tk), :] = key
        planes = _bit_planes([(key ^ INT_MIN)[v * 8:(v + 1) * 8, :] for v in range(32)])
        for i in range(32):
            p_ref[i, j] = planes[i]

    for h in range(N_HEADS_IDX):
        dots(0, h)

    def score_pair(u, carry):
        score_tile(2 * u, False)
        score_tile(2 * u + 1, False)
        return carry

    lax.fori_loop(0, lax.shift_right_logical(nt - 1, 1), score_pair, 0)

    @pl.when(((nt - 1) & 1) == 1)
    def _():
        score_tile(nt - 2, False)

    score_tile(nt - 1, True)

    group = incl_ref.shape[0]
    n_groups = nt if group == 1 else lax.shift_right_logical(nt + 3, 2)
    for g in range(group - 1):
        @pl.when(nt + g < n_groups * group)
        def _(g=g):
            r0 = pl.multiple_of((nt + g) * tk, tk)
            st_ref[pl.ds(r0, tk), :] = jnp.full((tk, tq), INT_MIN, I32)

    limit = n_adm - (nt - 1) * tk
    n_low = jnp.clip(lax.shift_right_arithmetic(limit - lax.broadcasted_iota(I32, (8, tq), 0) + 7, 3),
                     0, 32)
    last_word = jnp.where(n_low >= 32, -1, lax.shift_left(1, jnp.minimum(n_low, 31)) - 1)
    for jj in range(nkt_total):
        c_ref[jj] = jnp.where(jj < nt - 1, -1, jnp.where(jj == nt - 1, last_word, 0))

    def radix_pass(i, carry):
        want, t = carry
        n_set = _tree_sum([lax.population_count(c_ref[jj] & p_ref[i, jj]) for jj in range(nkt_total)])
        n_set = jnp.broadcast_to(jnp.sum(n_set, axis=0, keepdims=True), (8, tq))
        keep_set = n_set >= want
        for jj in range(nkt_total):
            cand = c_ref[jj]
            with_bit = cand & p_ref[i, jj]
            c_ref[jj] = jnp.where(keep_set, with_bit, cand ^ with_bit)
        return (jnp.where(keep_set, want, want - n_set),
                t | jnp.where(keep_set, lax.shift_left(jnp.int32(1), 31 - i), 0))

    want, t_bits = lax.fori_loop(0, 32, radix_pass, (k_row, jnp.zeros((8, tq), I32)))
    t_row = (t_bits ^ INT_MIN)[0:1, :]
    ties_wanted = want[0:1, :].astype(F32)

    tri = jnp.where(lax.broadcasted_iota(I32, (tk, tk), 0) >= lax.broadcasted_iota(I32, (tk, tk), 1),
                    1.0, 0.0).astype(BF16)

    def tie_prefix(j, g):
        r0 = pl.multiple_of(j * tk, tk)
        eq = jnp.where(st_ref[pl.ds(r0, tk), :] == t_row, 1.0, 0.0)
        incl_ref[g] = jnp.dot(tri, eq.astype(BF16), preferred_element_type=F32)

    def final_group(u, seen, last):
        for g in range(group):
            j = u * group + g
            r0 = pl.multiple_of(j * tk, tk)
            incl = incl_ref[g]
            if not last:
                tie_prefix(j + group, g)
            rank = seen + incl
            bar = t_row + jnp.where(rank <= ties_wanted, 0, 1)
            sel = jnp.where(st_ref[pl.ds(r0, tk), :] >= bar, 1, 0)
            mask_ref[pl.ds(r0, tk), :] = sel.astype(jnp.int8)
            seen = seen + incl[tk - 1:tk, :]
        return seen

    for g in range(group):
        tie_prefix(g, g)
    seen = lax.fori_loop(0, n_groups - 1, lambda u, seen: final_group(u, seen, False),
                         jnp.zeros((1, tq), F32))
    final_group(n_groups - 1, seen, True)

    def zero_tile(j, carry):
        r0 = pl.multiple_of(j * tk, tk)
        mask_ref[pl.ds(r0, tk), :] = jnp.zeros((tk, tq), jnp.int8)
        return carry

    lax.fori_loop(n_groups * group, nkt_total, zero_tile, 0)


def _idx_mask(qiT, wiT, kk, *, nb, tq, qpos0, n_valid, n_sel):
    tkp = kk.shape[1]
    nq = qiT.shape[1] // nb // tq
    return pl.pallas_call(
        functools.partial(_idx_mask_kernel, tq=tq, qpos0=qpos0, n_valid=n_valid, n_sel=n_sel),
        out_shape=jax.ShapeDtypeStruct((nb, tkp, nq * tq), jnp.int8),
        grid=(nb, nq),
        in_specs=[pl.BlockSpec((N_HEADS_IDX * IDX_DIM, tq), lambda b, i: (0, b * nq + i)),
                  pl.BlockSpec((8, tq), lambda b, i: (0, b * nq + i)),
                  pl.BlockSpec((None, tkp, 128), lambda b, i: (b, 0, 0))],
        out_specs=pl.BlockSpec((None, tkp, tq), lambda b, i: (b, 0, i)),
        scratch_shapes=[pltpu.VMEM((tkp, tq), I32),
                        pltpu.VMEM((32, tkp // ATT_TILE, 8, tq), I32),
                        pltpu.VMEM((tkp // ATT_TILE, 8, tq), I32),
                        pltpu.VMEM((N_HEADS_IDX, ATT_TILE, tq), F32),
                        pltpu.VMEM((4 if (tkp // ATT_TILE) % 4 == 0 else 1, ATT_TILE, tq), F32)],
        compiler_params=_cparams(("arbitrary", "arbitrary")),
        name="idx_mask",
    )(qiT, wiT, kk)


def _load_qsel(qT_ref, qsel_ref, tq):
    zeros = jnp.zeros((HEAD_DIM, tq), BF16)
    for h in range(N_HEADS):
        blk = qT_ref[h * HEAD_DIM:(h + 1) * HEAD_DIM, :]
        qsel_ref[h] = jnp.concatenate([blk, zeros] if h % 2 == 0 else [zeros, blk], axis=0)


def _init_state(m_ref, acc_ref):
    m_ref[...] = jnp.full(m_ref.shape, MASKED, F32)
    acc_ref[...] = jnp.zeros(acc_ref.shape, F32)


def _scores(k_slab, qsel_ref, s_ref, h):
    s_ref[h] = jnp.dot(k_slab(h // 2), qsel_ref[h], preferred_element_type=F32)


def _heads_tile(v_rows, next_k_slab, qsel_ref, s_ref, m_ref, acc_ref, addend):
    tq = s_ref.shape[2]
    slabs = s_ref.shape[1] // 16
    for h in range(N_HEADS):
        s = s_ref[h].astype(BF16)
        for a in addend(h):
            s = s + a
        if next_k_slab is not None:
            _scores(next_k_slab, qsel_ref, s_ref, h)
        m_old = m_ref[h:h + 1, :]
        top = _tree_reduce(jnp.maximum, [s[r * 16:(r + 1) * 16, :] for r in range(slabs)])
        m_new = jnp.maximum(m_old, jnp.max(top.astype(F32), axis=0, keepdims=True))
        alpha = jnp.exp2(m_old - m_new)
        m16 = jnp.broadcast_to(m_new, (16, tq)).astype(BF16)
        p = jnp.concatenate([jnp.exp2(s[r * 16:(r + 1) * 16, :] - m16) for r in range(slabs)], axis=0)
        m_ref[h:h + 1, :] = m_new
        pv = jnp.dot(v_rows(h), p, preferred_element_type=F32)
        rows = slice(h * V_ROWS, (h + 1) * V_ROWS)
        acc_ref[rows, :] = alpha * acc_ref[rows, :] + pv


def _attn_scratch(tq):
    return [pltpu.VMEM((N_HEADS, 128, tq), BF16), pltpu.VMEM((N_HEADS, ATT_TILE, tq), F32),
            pltpu.VMEM((N_HEADS, tq), F32), pltpu.VMEM((N_HEADS * V_ROWS, tq), F32)]


def _store_out(o_ref, acc_ref):
    outs = [acc_ref[h * V_ROWS:h * V_ROWS + HEAD_DIM, :]
            / acc_ref[h * V_ROWS + HEAD_DIM:h * V_ROWS + HEAD_DIM + 1, :] for h in range(N_HEADS)]
    o_ref[...] = jnp.concatenate(outs, axis=0).T.astype(o_ref.dtype)


def _dsa_kernel(qT_ref, k_ref, vT_ref, mask_ref, bias_ref, o_ref, qsel_ref, s_ref, m_ref, acc_ref,
                *, tq, qpos0):
    tk = ATT_TILE
    jd = lax.shift_right_logical(qpos0 + pl.program_id(1) * tq, int(math.log2(tk)))
    _load_qsel(qT_ref, qsel_ref, tq)
    _init_state(m_ref, acc_ref)

    def k_slab(j):
        r0 = pl.multiple_of(j * tk, tk)
        return lambda g: k_ref[pl.ds(r0, tk), g * 128:(g + 1) * 128]

    def tile(j, near, last=False):
        r0 = pl.multiple_of(j * tk, tk)
        unselected = jnp.where(mask_ref[pl.ds(r0, tk), :].astype(I32) != 0, 0.0,
                               MASKED).astype(BF16)
        addend = ((lambda h: (unselected,)) if near is None
                  else (lambda h: (unselected, bias_ref[near, h])))
        _heads_tile(lambda h: vT_ref[j, h * V_ROWS:(h + 1) * V_ROWS, :],
                    None if last else k_slab(j + 1),
                    qsel_ref, s_ref, m_ref, acc_ref, addend)

    for h in range(N_HEADS):
        _scores(k_slab(0), qsel_ref, s_ref, h)

    n_far = jnp.maximum(jd - 1, 0)

    def far_quad(u, carry):
        for c in range(4):
            tile(4 * u + c, None)
        return carry

    n_quads = lax.shift_right_logical(n_far, 2)
    lax.fori_loop(0, n_quads, far_quad, 0)

    @pl.when((n_far & 2) == 2)
    def _():
        tile(4 * n_quads, None)
        tile(4 * n_quads + 1, None)

    @pl.when((n_far & 1) == 1)
    def _():
        tile(n_far - 1, None)

    @pl.when(jd >= 1)
    def _():
        tile(jd - 1, 0)

    tile(jd, 1, last=True)
    _store_out(o_ref, acc_ref)


def _dsa_attn(qT, k, vT, maskT, bias, *, nb, tq, qpos0):
    tkp = k.shape[1]
    nkt = tkp // ATT_TILE
    nq = qT.shape[1] // nb // tq
    return pl.pallas_call(
        functools.partial(_dsa_kernel, tq=tq, qpos0=qpos0),
        out_shape=jax.ShapeDtypeStruct((nb * nq * tq, WIDTH), BF16),
        grid=(nb, nq),
        in_specs=[pl.BlockSpec((WIDTH, tq), lambda b, i: (0, b * nq + i)),
                  pl.BlockSpec((None, tkp, WIDTH), lambda b, i: (b, 0, 0),
                               pipeline_mode=pl.Buffered(1)),
                  pl.BlockSpec((nkt, N_HEADS * V_ROWS, ATT_TILE), lambda b, i: (b, 0, 0),
                               pipeline_mode=pl.Buffered(1)),
                  pl.BlockSpec((None, tkp, tq), lambda b, i: (b, 0, i)),
                  _const_spec(bias.shape)],
        out_specs=pl.BlockSpec((tq, WIDTH), lambda b, i: (b * nq + i, 0)),
        scratch_shapes=_attn_scratch(tq),
        compiler_params=_cparams(("parallel", "parallel")),
        name="dsa_attn",
    )(qT, k, vT, maskT, bias)


_BAND_TILES = BAND_CHUNKS * CHUNK // ATT_TILE + 1


def _band_kernel(qT_ref, *refs, tq, off):
    k_refs = refs[:_BAND_TILES]
    v_refs = refs[_BAND_TILES:2 * _BAND_TILES]
    bias_ref, o_ref, qsel_ref, s_ref, m_ref, acc_ref = refs[2 * _BAND_TILES:]
    _load_qsel(qT_ref, qsel_ref, tq)
    _init_state(m_ref, acc_ref)
    k_slab = lambda w: (lambda g: k_refs[w][:, g * 128:(g + 1) * 128])
    for h in range(N_HEADS):
        _scores(k_slab(0), qsel_ref, s_ref, h)
    for w in range(_BAND_TILES):
        entry = jnp.where(pl.program_id(1) + off + w >= 0, w, _BAND_TILES)
        _heads_tile(lambda h: v_refs[w][0, h * V_ROWS:(h + 1) * V_ROWS, :],
                    None if w == _BAND_TILES - 1 else k_slab(w + 1),
                    qsel_ref, s_ref, m_ref, acc_ref, lambda h: (bias_ref[entry, h],))
    _store_out(o_ref, acc_ref)


def _band_attn(qT, k, vT, bias, *, nb, tq, off):
    nkt = k.shape[1] // ATT_TILE
    nq = qT.shape[1] // nb // tq
    tile_of = lambda i, w: jnp.maximum(i + off + w, 0)
    k_specs = [pl.BlockSpec((None, ATT_TILE, WIDTH), lambda b, i, w=w: (b, tile_of(i, w), 0))
               for w in range(_BAND_TILES)]
    v_specs = [pl.BlockSpec((1, N_HEADS * V_ROWS, ATT_TILE),
                            lambda b, i, w=w: (b * nkt + tile_of(i, w), 0, 0))
               for w in range(_BAND_TILES)]
    return pl.pallas_call(
        functools.partial(_band_kernel, tq=tq, off=off),
        out_shape=jax.ShapeDtypeStruct((nb * nq * tq, WIDTH), BF16),
        grid=(nb, nq),
        in_specs=[pl.BlockSpec((WIDTH, tq), lambda b, i: (0, b * nq + i))] + k_specs + v_specs
                 + [_const_spec(bias.shape)],
        out_specs=pl.BlockSpec((tq, WIDTH), lambda b, i: (b * nq + i, 0)),
        scratch_shapes=_attn_scratch(tq),
        compiler_params=_cparams(("parallel", "parallel")),
        name="band_attn",
    )(qT, *([k] * _BAND_TILES), *([vT] * _BAND_TILES), bias)


def _mix_ln_kernel(h_ref, oa_ref, ob_ref, wg_ref, wba_ref, wbb_ref, wo_ref, g_ref, b_ref, o_ref,
                   *, alpha):
    h = h_ref[...]
    hb = h.astype(BF16)
    d = h.shape[1]
    ga = jnp.dot(hb, wg_ref[:, :d], preferred_element_type=F32)
    gb = jnp.dot(hb, wg_ref[:, d:], preferred_element_type=F32)
    ya = jnp.dot(oa_ref[...], wba_ref[...], preferred_element_type=F32)
    yb = jnp.dot(ob_ref[...], wbb_ref[...], preferred_element_type=F32)
    gated = jax.nn.sigmoid(ga) * ya + jax.nn.sigmoid(gb) * yb
    mix = jnp.dot(gated.astype(BF16), wo_ref[...], preferred_element_type=F32)
    o_ref[...] = _layer_norm_rows(alpha * h + mix, g_ref[...], b_ref[...])


def _mix_ln(h, oa, ob, wg, wba, wbb, wo, g, b, alpha):
    n, d = h.shape
    tm = TOKEN_TILE
    row = lambda width: pl.BlockSpec((tm, width), lambda i: (i, 0))
    return pl.pallas_call(
        functools.partial(_mix_ln_kernel, alpha=alpha),
        out_shape=jax.ShapeDtypeStruct((n, d), F32),
        grid=(n // tm,),
        in_specs=[row(d), row(WIDTH), row(WIDTH), _const_spec(wg.shape), _const_spec(wba.shape),
                  _const_spec(wbb.shape), _const_spec(wo.shape), _const_spec((1, d)),
                  _const_spec((1, d))],
        out_specs=row(d),
        compiler_params=_cparams(("parallel",)),
        name="mix_ln",
    )(h, oa, ob, wg, wba, wbb, wo, g.reshape(1, d), b.reshape(1, d))


def _t5_bucket(rel):
    half = T5_BUCKETS // 2
    exact = half // 2
    n = jnp.abs(rel)
    log_ratio = jnp.log(jnp.maximum(n, 1).astype(F32) / exact) / math.log(T5_MAX_DIST / exact)
    large = jnp.minimum(exact + (log_ratio * (half - exact)).astype(I32), half - 1)
    return (rel > 0).astype(I32) * half + jnp.where(n < exact, n, large)


def _rel_line(n_tiles, first, tq):
    period = ATT_TILE + tq
    y = jnp.arange(period, dtype=I32)[None, :]
    c = (jnp.arange(n_tiles, dtype=I32)[:, None] + first) * ATT_TILE
    return jnp.where(y < tq, c - y, c + period - y)


def _toeplitz(v, tq):
    period = v.shape[-1]
    flat = jnp.tile(v, (1,) * (v.ndim - 1) + (ATT_TILE,))[..., :ATT_TILE * (period - 1)]
    return flat.reshape(v.shape[:-1] + (ATT_TILE, period - 1))[..., :tq]


def _dsa_bias_table(t5_bias, tq):
    assert T5_MAX_DIST <= ATT_TILE + 1
    far = t5_bias[:, _t5_bucket(jnp.int32(-(ATT_TILE + 1)))]
    line = t5_bias[:, _t5_bucket(_rel_line(2, -1, tq))] - far[:, None, None]
    return (_toeplitz(jnp.moveaxis(line, 0, 1), tq) * LOG2E).astype(BF16)


def _band_bias_table(rel_bias, tq):
    first = 1 - _BAND_TILES
    ridx = jnp.clip(_rel_line(_BAND_TILES, first, tq), -REL_BACK, CHUNK - 1) + REL_BACK
    bias = _toeplitz(jnp.moveaxis(rel_bias[:, ridx], 0, 1), tq)
    w = jnp.arange(_BAND_TILES, dtype=I32)[:, None, None]
    j = jnp.arange(ATT_TILE, dtype=I32)[None, :, None]
    i = jnp.arange(tq, dtype=I32)[None, None, :]
    dchunk = ((w + first) * ATT_TILE + j) // CHUNK - i // CHUNK
    ok = (dchunk <= 0) & (dchunk >= -BAND_CHUNKS)
    table = jnp.where(ok[:, None], bias * LOG2E, MASKED)
    return jnp.concatenate([table, jnp.full_like(table[:1], MASKED)], axis=0).astype(BF16)


def _regroup_w_in(w_in):
    sizes = (WIDTH, WIDTH, WIDTH, N_HEADS_IDX * IDX_DIM, IDX_DIM, N_HEADS_IDX,
             WIDTH, WIDTH, WIDTH, w_in.shape[0], w_in.shape[0])
    offs = [0]
    for s in sizes:
        offs.append(offs[-1] + s)
    qa, ka, va, qi, ki, wi, qb, kb, vb, ga, gb = (w_in[:, offs[t]:offs[t + 1]] for t in range(11))
    wi_pad = jnp.pad(wi, ((0, 0), (0, 128 - N_HEADS_IDX)))
    w_proj = jnp.concatenate([qa, ka, va, qi, ki, ki, wi_pad, qb, kb, vb], axis=1).astype(BF16)
    w_gate = jnp.concatenate([ga, gb], axis=1).astype(BF16)
    return w_proj, w_gate


def _key_tiles_T(v):
    nb, tkp, wd = v.shape
    vt = v.reshape(nb, tkp // ATT_TILE, ATT_TILE, N_HEADS, HEAD_DIM).transpose(0, 1, 3, 4, 2)
    ones_slab = jnp.zeros(vt.shape[:3] + (V_ROWS - HEAD_DIM, ATT_TILE), v.dtype).at[..., 0, :].set(1)
    return jnp.concatenate([vt, ones_slab], axis=3).reshape(-1, N_HEADS * V_ROWS, ATT_TILE)


def _pad_keys(x, tkp):
    return jnp.pad(x, ((0, 0), (0, tkp - x.shape[1]), (0, 0)))


def kernel(x_prompt, x_sample, cache_k_a, cache_v_a, cache_kidx_a, cache_k_b, cache_v_b, t5_bias,
           ln1_g, ln1_b, ffn1_wi, ffn1_wo, ln2_g, ln2_b, w_in, rel_bias_b, w_branch_a, w_branch_b,
           w_out, ln3_g, ln3_b, ffn2_wi, ffn2_wo):
    depth = ln1_g.shape[0]
    alpha = (2.0 * depth) ** 0.25
    nbp, seq, d = x_prompt.shape
    nbs, dec, _ = x_sample.shape
    past = cache_k_a.shape[2]
    band = cache_k_b.shape[2]
    keep = min(BAND_CHUNKS * CHUNK, seq)
    tq_p, tq_s = ATT_TILE, 128
    assert seq % TOKEN_TILE == 0 and keep == TOKEN_TILE and dec <= tq_s and past % ATT_TILE == 0
    assert band == BAND_CHUNKS * CHUNK and (nbs * tq_s) % TOKEN_TILE == 0

    yp = x_prompt.reshape(nbp * seq, d)
    ys = jnp.pad(x_sample, ((0, 0), (0, tq_s - dec), (0, 0))).reshape(nbs * tq_s, d)
    dsa_bias_p = _dsa_bias_table(t5_bias, tq_p)
    dsa_bias_s = dsa_bias_p[..., :tq_s]
    n_sel_p = min(TOPK_MAX, seq // 4)
    n_sel_s = min(TOPK_MAX, (past + dec) // 4)
    tk_s = -(-(past + tq_s) // ATT_TILE) * ATT_TILE
    tkb_s = _BAND_TILES * ATT_TILE

    st_p, st_s = [], []
    for l in range(depth):
        w_proj, w_gate = _regroup_w_in(w_in[l])
        wi1, wo1 = ffn1_wi[l].astype(BF16), ffn1_wo[l].astype(BF16)
        wi2, wo2 = ffn2_wi[l].astype(BF16), ffn2_wo[l].astype(BF16)
        wba, wbb, wo = (w_branch_a[l].astype(BF16), w_branch_b[l].astype(BF16),
                        w_out[l].astype(BF16))
        band_bias_p = _band_bias_table(rel_bias_b[l], tq_p)
        band_bias_s = band_bias_p[..., :tq_s]

        h = _ffn_ln(yp, wi1, wo1, ln1_g[l], ln1_b[l], alpha)
        (qaT, ka, kab, va, vaT, qiT, ki, kk, wiT, qbT, kbb, vbT, kbl, vbl) = _in_proj(
            h, w_proj, seq // TOKEN_TILE)
        maskT = _idx_mask(qiT, wiT, kk.reshape(nbp, seq, 128), nb=nbp, tq=tq_p, qpos0=0,
                          n_valid=seq, n_sel=n_sel_p)
        oa = _dsa_attn(qaT, kab.reshape(nbp, seq, WIDTH), vaT, maskT, dsa_bias_p,
                       nb=nbp, tq=tq_p, qpos0=0)
        ob = _band_attn(qbT, kbb.reshape(nbp, seq, WIDTH), vbT, band_bias_p,
                        nb=nbp, tq=tq_p, off=1 - _BAND_TILES)
        h2 = _mix_ln(h, oa, ob, w_gate, wba, wbb, wo, ln2_g[l], ln2_b[l], alpha)
        yp = _ffn_ln(h2, wi2, wo2, ln3_g[l], ln3_b[l], alpha)
        st_p.append((ka.reshape(nbp, seq, N_HEADS, HEAD_DIM), va.reshape(nbp, seq, N_HEADS, HEAD_DIM),
                     ki.reshape(nbp, seq, IDX_DIM), kbl.reshape(nbp, keep, N_HEADS, HEAD_DIM),
                     vbl.reshape(nbp, keep, N_HEADS, HEAD_DIM)))

        h = _ffn_ln(ys, wi1, wo1, ln1_g[l], ln1_b[l], alpha)
        (qaT, ka, _, va, _, qiT, ki, _, wiT, qbT, _, _, kbl, vbl) = _in_proj(h, w_proj, 1)
        new = lambda a: a.reshape(nbs, tq_s, -1)[:, :dec]
        ka_n, va_n, ki_n, kb_n, vb_n = new(ka), new(va), new(ki), new(kbl), new(vbl)
        ka_all = jnp.concatenate([cache_k_a[l].reshape(nbs, past, WIDTH), ka_n], axis=1)
        va_all = jnp.concatenate([cache_v_a[l].reshape(nbs, past, WIDTH), va_n], axis=1)
        ki_all = jnp.concatenate([cache_kidx_a[l], ki_n], axis=1)
        kb_all = jnp.concatenate([cache_k_b[l].reshape(nbs, band, WIDTH), kb_n], axis=1)
        vb_all = jnp.concatenate([cache_v_b[l].reshape(nbs, band, WIDTH), vb_n], axis=1)
        kk_s = _pad_keys(jnp.concatenate([ki_all, ki_all], axis=2).astype(BF16), tk_s)
        maskT = _idx_mask(qiT, wiT, kk_s, nb=nbs, tq=tq_s, qpos0=past, n_valid=past + dec,
                          n_sel=n_sel_s)
        oa = _dsa_attn(qaT, _pad_keys(ka_all.astype(BF16), tk_s),
                       _key_tiles_T(_pad_keys(va_all.astype(BF16), tk_s)), maskT, dsa_bias_s,
                       nb=nbs, tq=tq_s, qpos0=past)
        ob = _band_attn(qbT, _pad_keys(kb_all.astype(BF16), tkb_s),
                        _key_tiles_T(_pad_keys(vb_all.astype(BF16), tkb_s)), band_bias_s,
                        nb=nbs, tq=tq_s, off=0)
        h2 = _mix_ln(h, oa, ob, w_gate, wba, wbb, wo, ln2_g[l], ln2_b[l], alpha)
        ys = _ffn_ln(h2, wi2, wo2, ln3_g[l], ln3_b[l], alpha)
        heads = lambda a: a.reshape(nbs, -1, N_HEADS, HEAD_DIM)
        st_s.append((heads(ka_n), heads(va_n), ki_n, heads(kb_all[:, -band:]),
                     heads(vb_all[:, -band:])))

    y_prompt = yp.reshape(nbp, seq, d)
    y_sample = ys.reshape(nbs, tq_s, d)[:, :dec]
    stack = lambda sts, t: jnp.stack([s[t] for s in sts])
    return (y_prompt, y_sample,
            stack(st_p, 0), stack(st_p, 1), stack(st_p, 2), stack(st_p, 3), stack(st_p, 4),
            stack(st_s, 0), stack(st_s, 1), stack(st_s, 2), stack(st_s, 3), stack(st_s, 4))
```

```python
import functools
import math

import jax
import jax.numpy as jnp
from jax import lax
from jax.experimental import pallas as pl
from jax.experimental.pallas import tpu as pltpu

F32 = jnp.float32
BF16 = jnp.bfloat16
I32 = jnp.int32
I16 = jnp.int16

CHUNK = 64
CHUNK_SHIFT = 6
HEAD_DIM = 64
N_HEADS = 8
WIDTH = N_HEADS * HEAD_DIM
N_HEADS_IDX = 4
IDX_DIM = 64
TOPK_MAX = 256
IDX_SCALE = (IDX_DIM * N_HEADS_IDX) ** -0.5
BAND_CHUNKS = 8
REL_BACK = 128
T5_BUCKETS = 32
T5_MAX_DIST = 128
LN_EPS = 1e-5
MASKED = -2.0 ** 100
LOG2E = math.log2(math.e)
V_ROWS = HEAD_DIM + 16
INT_MIN = -2 ** 31

ATT_TILE = 256
TOKEN_TILE = 512
FF_CHUNK = 256
VMEM_LIMIT_MB = 56


def _cparams(sem):
    return pltpu.CompilerParams(dimension_semantics=sem,
                                vmem_limit_bytes=VMEM_LIMIT_MB * 1024 * 1024)


def _const_spec(shape):
    nd = len(shape)
    return pl.BlockSpec(shape, lambda *_: (0,) * nd, pipeline_mode=pl.Buffered(1))


def _tree_sum(xs):
    return _tree_reduce(lambda a, b: a + b, xs)


def _tree_reduce(fn, xs):
    xs = list(xs)
    while len(xs) > 1:
        nxt = [fn(xs[a], xs[a + 1]) for a in range(0, len(xs) - 1, 2)]
        if len(xs) % 2:
            nxt.append(xs[-1])
        xs = nxt
    return xs[0]


def _layer_norm_rows(y, g, b):
    mu = jnp.mean(y, axis=-1, keepdims=True)
    d = y - mu
    var = jnp.mean(d * d, axis=-1, keepdims=True)
    return d * lax.rsqrt(var + LN_EPS) * g + b


def _ffn_ln_kernel(x_ref, wi_ref, wo_ref, g_ref, b_ref, o_ref, *, alpha, dff):
    x = x_ref[...]
    xb = x.astype(BF16)
    acc = jnp.zeros(x.shape, F32)
    for c in range(dff // FF_CHUNK):
        lo = c * FF_CHUNK
        a = jnp.dot(xb, wi_ref[:, lo:lo + FF_CHUNK], preferred_element_type=F32)
        u = jnp.dot(xb, wi_ref[:, dff + lo:dff + lo + FF_CHUNK], preferred_element_type=F32)
        hm = (a * jax.nn.sigmoid(a)) * u
        acc = acc + jnp.dot(hm.astype(BF16), wo_ref[lo:lo + FF_CHUNK, :],
                            preferred_element_type=F32)
    y = alpha * x + 0.5 * acc
    o_ref[...] = _layer_norm_rows(y, g_ref[...], b_ref[...])


def _ffn_ln(x, wi, wo, g, b, alpha):
    n, d = x.shape
    dff = wo.shape[0]
    tm = TOKEN_TILE
    return pl.pallas_call(
        functools.partial(_ffn_ln_kernel, alpha=alpha, dff=dff),
        out_shape=jax.ShapeDtypeStruct((n, d), F32),
        grid=(n // tm,),
        in_specs=[pl.BlockSpec((tm, d), lambda i: (i, 0)),
                  _const_spec((d, 2 * dff)), _const_spec((dff, d)),
                  _const_spec((1, d)), _const_spec((1, d))],
        out_specs=pl.BlockSpec((tm, d), lambda i: (i, 0)),
        compiler_params=_cparams(("parallel",)),
        name="ffn_ln",
    )(x, wi, wo, g.reshape(1, d), b.reshape(1, d))


_C_QA, _C_KA, _C_VA, _C_QI, _C_KI, _C_WI, _C_QB, _C_KB, _C_VB, _C_END = (
    0, 512, 1024, 1536, 1792, 1920, 2048, 2560, 3072, 3584)


def _inproj_kernel(h_ref, w_ref, qaT_ref, ka_ref, kab_ref, va_ref, vaT_ref, qiT_ref, ki_ref,
                   kk_ref, wiT_ref, qbT_ref, kbb_ref, vbT_ref, kbl_ref, vbl_ref, *, per):
    hb = h_ref[...].astype(BF16)

    def proj(c0, c1):
        return jnp.dot(hb, w_ref[:, c0:c1], preferred_element_type=F32)

    def put_tiles(ref, zt):
        tm = zt.shape[1]
        ones_slab = jnp.where(lax.broadcasted_iota(I32, (V_ROWS - HEAD_DIM, tm), 0) == 0, 1.0, 0.0)
        aug = jnp.concatenate(
            [piece for h in range(N_HEADS)
             for piece in (zt[h * HEAD_DIM:(h + 1) * HEAD_DIM, :], ones_slab)], axis=0)
        for c in range(ref.shape[0]):
            ref[c] = aug[:, c * ATT_TILE:(c + 1) * ATT_TILE].astype(ref.dtype)

    scale = HEAD_DIM ** -0.5 * LOG2E
    qaT_ref[...] = (proj(_C_QA, _C_KA) * scale).T.astype(BF16)
    ka = proj(_C_KA, _C_VA)
    ka_ref[...] = ka
    kab_ref[...] = ka.astype(BF16)
    va = proj(_C_VA, _C_QI)
    va_ref[...] = va
    put_tiles(vaT_ref, va.T)
    qiT_ref[...] = proj(_C_QI, _C_KI).T.astype(BF16)
    kk = proj(_C_KI, _C_WI)
    ki_ref[...] = kk[:, :IDX_DIM]
    kk_ref[...] = kk.astype(BF16)
    wiT_ref[...] = proj(_C_WI, _C_QB).T[:8, :]
    qbT_ref[...] = (proj(_C_QB, _C_KB) * scale).T.astype(BF16)
    kb = proj(_C_KB, _C_VB)
    kbb_ref[...] = kb.astype(BF16)
    vb = proj(_C_VB, _C_END)
    put_tiles(vbT_ref, vb.T)

    @pl.when(pl.program_id(0) % per == per - 1)
    def _():
        kbl_ref[...] = kb
        vbl_ref[...] = vb


def _in_proj(h, w, per):
    n, d = h.shape
    tm = TOKEN_TILE
    nt = n // tm
    sub = tm // ATT_TILE
    row = lambda width: pl.BlockSpec((tm, width), lambda i: (i, 0))
    col = lambda rows: pl.BlockSpec((rows, tm), lambda i: (0, i))
    til = pl.BlockSpec((sub, N_HEADS * V_ROWS, ATT_TILE), lambda i: (i, 0, 0))
    last = pl.BlockSpec((tm, WIDTH), lambda i: (i // per, 0))
    sds = jax.ShapeDtypeStruct
    out_shape = (
        sds((WIDTH, n), BF16),
        sds((n, WIDTH), F32), sds((n, WIDTH), BF16),
        sds((n, WIDTH), F32), sds((n // ATT_TILE, N_HEADS * V_ROWS, ATT_TILE), BF16),
        sds((N_HEADS_IDX * IDX_DIM, n), BF16),
        sds((n, IDX_DIM), F32), sds((n, 128), BF16),
        sds((8, n), F32),
        sds((WIDTH, n), BF16),
        sds((n, WIDTH), BF16),
        sds((n // ATT_TILE, N_HEADS * V_ROWS, ATT_TILE), BF16),
        sds((nt // per * tm, WIDTH), F32), sds((nt // per * tm, WIDTH), F32),
    )
    out_specs = (col(WIDTH), row(WIDTH), row(WIDTH), row(WIDTH), til,
                 col(N_HEADS_IDX * IDX_DIM), row(IDX_DIM), row(128), col(8),
                 col(WIDTH), row(WIDTH), til, last, last)
    return pl.pallas_call(
        functools.partial(_inproj_kernel, per=per),
        out_shape=out_shape,
        grid=(nt,),
        in_specs=[pl.BlockSpec((tm, d), lambda i: (i, 0)), _const_spec((d, _C_END))],
        out_specs=out_specs,
        compiler_params=_cparams(("arbitrary",)),
        name="in_proj",
    )(h, w)


def _flip(bits):
    return bits ^ (lax.shift_right_arithmetic(bits, 31) & 0x7FFFFFFF)


def _bit_planes(words):
    a = list(reversed(words))
    j, m = 16, 0x0000FFFF
    while j:
        k = 0
        while k < 32:
            t = (a[k] ^ lax.shift_right_logical(a[k + j], j)) & m
            a[k] = a[k] ^ t
            a[k + j] = a[k + j] ^ lax.shift_left(t, j)
            k = (k + j + 1) & ~j
        j >>= 1
        m = (m ^ (m << j)) & 0xFFFFFFFF
    return a


def _idx_mask_kernel(qiT_ref, wiT_ref, kk_ref, mask_ref, st_ref, p_ref, c_ref, d_ref, incl_ref,
                     *, tq, qpos0, n_valid, n_sel):
    tk = ATT_TILE
    assert tk == 32 * 8
    nkt_total = st_ref.shape[0] // tk

    @pl.when((pl.program_id(0) == 0) & (pl.program_id(1) == 0))
    def _():
        p_ref[...] = jnp.zeros(p_ref.shape, I32)

    q0 = qpos0 + pl.program_id(1) * tq
    n_keys = jnp.minimum(q0 + tq, n_valid)
    nt = jnp.minimum(lax.shift_right_logical(n_keys + tk - 1, int(math.log2(tk))), nkt_total)

    lane = lax.broadcasted_iota(I32, (8, tq), 1)
    q_chunk = lax.shift_right_logical(q0 + lane, CHUNK_SHIFT)
    n_adm = jnp.minimum((q_chunk + 1) * CHUNK, n_valid)
    k_row = jnp.minimum(n_adm, n_sel)

    zeros = jnp.zeros((128 - IDX_DIM, tq), BF16)
    qh = [jnp.concatenate([qiT_ref[h * IDX_DIM:(h + 1) * IDX_DIM, :], zeros], axis=0)
          for h in range(N_HEADS_IDX)]
    scale_exact = math.frexp(IDX_SCALE)[0] == 0.5
    wh = [wiT_ref[h:h + 1, :] * IDX_SCALE if scale_exact else wiT_ref[h:h + 1, :]
          for h in range(N_HEADS_IDX)]

    def dots(j, h):
        r0 = pl.multiple_of(j * tk, tk)
        d_ref[h] = jnp.dot(kk_ref[pl.ds(r0, tk), :], qh[h], preferred_element_type=F32)

    def score_tile(j, last):
        r0 = pl.multiple_of(j * tk, tk)
        sc = None
        for h in range(N_HEADS_IDX):
            d = d_ref[h]
            if not last:
                dots(j + 1, h)
            term = wh[h] * jnp.maximum(d, 0.0)
            sc = term if sc is None else sc + term
        if not scale_exact:
            sc = sc * IDX_SCALE
        bits = lax.bitcast_convert_type(sc, I32)
        sign = lax.shift_right_arithmetic(bits, 31)
        key = (bits ^ (sign & 0x7FFFFFFF)) - sign
        if last:
            kpos = r0 + lax.broadcasted_iota(I32, (tk, tq), 0)
            key = jnp.where(kpos < n_adm[0:1, :], key, INT_MIN)
        st_ref[pl.ds(r0, tk), :] = key
        planes = _bit_planes([(key ^ INT_MIN)[v * 8:(v + 1) * 8, :] for v in range(32)])
        for i in range(32):
            p_ref[i, j] = planes[i]

    for h in range(N_HEADS_IDX):
        dots(0, h)

    def score_pair(u, carry):
        score_tile(2 * u, False)
        score_tile(2 * u + 1, False)
        return carry

    lax.fori_loop(0, lax.shift_right_logical(nt - 1, 1), score_pair, 0)

    @pl.when(((nt - 1) & 1) == 1)
    def _():
        score_tile(nt - 2, False)

    score_tile(nt - 1, True)

    group = incl_ref.shape[0]
    n_groups = nt if group == 1 else lax.shift_right_logical(nt + 3, 2)
    for g in range(group - 1):
        @pl.when(nt + g < n_groups * group)
        def _(g=g):
            r0 = pl.multiple_of((nt + g) * tk, tk)
            st_ref[pl.ds(r0, tk), :] = jnp.full((tk, tq), INT_MIN, I32)

    limit = n_adm - (nt - 1) * tk
    n_low = jnp.clip(lax.shift_right_arithmetic(limit - lax.broadcasted_iota(I32, (8, tq), 0) + 7, 3),
                     0, 32)
    last_word = jnp.where(n_low >= 32, -1, lax.shift_left(1, jnp.minimum(n_low, 31)) - 1)
    for jj in range(nkt_total):
        c_ref[jj] = jnp.where(jj < nt - 1, -1, jnp.where(jj == nt - 1, last_word, 0))

    def radix_pass(i, carry):
        want, t = carry
        n_set = _tree_sum([lax.population_count(c_ref[jj] & p_ref[i, jj]) for jj in range(nkt_total)])
        n_set = jnp.broadcast_to(jnp.sum(n_set, axis=0, keepdims=True), (8, tq))
        keep_set = n_set >= want
        for jj in range(nkt_total):
            cand = c_ref[jj]
            with_bit = cand & p_ref[i, jj]
            c_ref[jj] = jnp.where(keep_set, with_bit, cand ^ with_bit)
        return (jnp.where(keep_set, want, want - n_set),
                t | jnp.where(keep_set, lax.shift_left(jnp.int32(1), 31 - i), 0))

    want, t_bits = lax.fori_loop(0, 32, radix_pass, (k_row, jnp.zeros((8, tq), I32)))
    t_row = (t_bits ^ INT_MIN)[0:1, :]
    ties_wanted = want[0:1, :].astype(F32)

    tri = jnp.where(lax.broadcasted_iota(I32, (tk, tk), 0) >= lax.broadcasted_iota(I32, (tk, tk), 1),
                    1.0, 0.0).astype(BF16)

    def tie_prefix(j, g):
        r0 = pl.multiple_of(j * tk, tk)
        eq = jnp.where(st_ref[pl.ds(r0, tk), :] == t_row, 1.0, 0.0)
        incl_ref[g] = jnp.dot(tri, eq.astype(BF16), preferred_element_type=F32)

    def final_group(u, seen, last):
        for g in range(group):
            j = u * group + g
            r0 = pl.multiple_of(j * tk, tk)
            incl = incl_ref[g]
            if not last:
                tie_prefix(j + group, g)
            rank = seen + incl
            bar = t_row + jnp.where(rank <= ties_wanted, 0, 1)
            sel = jnp.where(st_ref[pl.ds(r0, tk), :] >= bar, 1, 0)
            mask_ref[pl.ds(r0, tk), :] = sel.astype(jnp.int8)
            seen = seen + incl[tk - 1:tk, :]
        return seen

    for g in range(group):
        tie_prefix(g, g)
    seen = lax.fori_loop(0, n_groups - 1, lambda u, seen: final_group(u, seen, False),
                         jnp.zeros((1, tq), F32))
    final_group(n_groups - 1, seen, True)

    def zero_tile(j, carry):
        r0 = pl.multiple_of(j * tk, tk)
        mask_ref[pl.ds(r0, tk), :] = jnp.zeros((tk, tq), jnp.int8)
        return carry

    lax.fori_loop(n_groups * group, nkt_total, zero_tile, 0)


def _idx_mask(qiT, wiT, kk, *, nb, tq, qpos0, n_valid, n_sel):
    tkp = kk.shape[1]
    nq = qiT.shape[1] // nb // tq
    return pl.pallas_call(
        functools.partial(_idx_mask_kernel, tq=tq, qpos0=qpos0, n_valid=n_valid, n_sel=n_sel),
        out_shape=jax.ShapeDtypeStruct((nb, tkp, nq * tq), jnp.int8),
        grid=(nb, nq),
        in_specs=[pl.BlockSpec((N_HEADS_IDX * IDX_DIM, tq), lambda b, i: (0, b * nq + i)),
                  pl.BlockSpec((8, tq), lambda b, i: (0, b * nq + i)),
                  pl.BlockSpec((None, tkp, 128), lambda b, i: (b, 0, 0))],
        out_specs=pl.BlockSpec((None, tkp, tq), lambda b, i: (b, 0, i)),
        scratch_shapes=[pltpu.VMEM((tkp, tq), I32),
                        pltpu.VMEM((32, tkp // ATT_TILE, 8, tq), I32),
                        pltpu.VMEM((tkp // ATT_TILE, 8, tq), I32),
                        pltpu.VMEM((N_HEADS_IDX, ATT_TILE, tq), F32),
                        pltpu.VMEM((4 if (tkp // ATT_TILE) % 4 == 0 else 1, ATT_TILE, tq), F32)],
        compiler_params=_cparams(("arbitrary", "arbitrary")),
        name="idx_mask",
    )(qiT, wiT, kk)


def _load_qsel(qT_ref, qsel_ref, tq):
    zeros = jnp.zeros((HEAD_DIM, tq), BF16)
    for h in range(N_HEADS):
        blk = qT_ref[h * HEAD_DIM:(h + 1) * HEAD_DIM, :]
        qsel_ref[h] = jnp.concatenate([blk, zeros] if h % 2 == 0 else [zeros, blk], axis=0)


def _init_state(m_ref, acc_ref):
    m_ref[...] = jnp.full(m_ref.shape, MASKED, F32)
    acc_ref[...] = jnp.zeros(acc_ref.shape, F32)


def _scores(k_slab, qsel_ref, s_ref, h):
    s_ref[h] = jnp.dot(k_slab(h // 2), qsel_ref[h], preferred_element_type=F32)


def _heads_tile(v_rows, next_k_slab, qsel_ref, s_ref, m_ref, acc_ref, addend):
    tq = s_ref.shape[2]
    slabs = s_ref.shape[1] // 16
    for h in range(N_HEADS):
        s = s_ref[h].astype(BF16)
        for a in addend(h):
            s = s + a
        if next_k_slab is not None:
            _scores(next_k_slab, qsel_ref, s_ref, h)
        m_old = m_ref[h:h + 1, :]
        top = _tree_reduce(jnp.maximum, [s[r * 16:(r + 1) * 16, :] for r in range(slabs)])
        m_new = jnp.maximum(m_old, jnp.max(top.astype(F32), axis=0, keepdims=True))
        alpha = jnp.exp2(m_old - m_new)
        m16 = jnp.broadcast_to(m_new, (16, tq)).astype(BF16)
        p = jnp.concatenate([jnp.exp2(s[r * 16:(r + 1) * 16, :] - m16) for r in range(slabs)], axis=0)
        m_ref[h:h + 1, :] = m_new
        pv = jnp.dot(v_rows(h), p, preferred_element_type=F32)
        rows = slice(h * V_ROWS, (h + 1) * V_ROWS)
        acc_ref[rows, :] = alpha * acc_ref[rows, :] + pv


def _attn_scratch(tq):
    return [pltpu.VMEM((N_HEADS, 128, tq), BF16), pltpu.VMEM((N_HEADS, ATT_TILE, tq), F32),
            pltpu.VMEM((N_HEADS, tq), F32), pltpu.VMEM((N_HEADS * V_ROWS, tq), F32)]


def _store_out(o_ref, acc_ref):
    outs = [acc_ref[h * V_ROWS:h * V_ROWS + HEAD_DIM, :]
            / acc_ref[h * V_ROWS + HEAD_DIM:h * V_ROWS + HEAD_DIM + 1, :] for h in range(N_HEADS)]
    o_ref[...] = jnp.concatenate(outs, axis=0).T.astype(o_ref.dtype)


def _dsa_kernel(qT_ref, k_ref, vT_ref, mask_ref, bias_ref, o_ref, qsel_ref, s_ref, m_ref, acc_ref,
                *, tq, qpos0):
    tk = ATT_TILE
    jd = lax.shift_right_logical(qpos0 + pl.program_id(1) * tq, int(math.log2(tk)))
    _load_qsel(qT_ref, qsel_ref, tq)
    _init_state(m_ref, acc_ref)

    def k_slab(j):
        r0 = pl.multiple_of(j * tk, tk)
        return lambda g: k_ref[pl.ds(r0, tk), g * 128:(g + 1) * 128]

    def tile(j, near, last=False):
        r0 = pl.multiple_of(j * tk, tk)
        unselected = jnp.where(mask_ref[pl.ds(r0, tk), :].astype(I32) != 0, 0.0,
                               MASKED).astype(BF16)
        addend = ((lambda h: (unselected,)) if near is None
                  else (lambda h: (unselected, bias_ref[near, h])))
        _heads_tile(lambda h: vT_ref[j, h * V_ROWS:(h + 1) * V_ROWS, :],
                    None if last else k_slab(j + 1),
                    qsel_ref, s_ref, m_ref, acc_ref, addend)

    for h in range(N_HEADS):
        _scores(k_slab(0), qsel_ref, s_ref, h)

    n_far = jnp.maximum(jd - 1, 0)

    def far_quad(u, carry):
        for c in range(4):
            tile(4 * u + c, None)
        return carry

    n_quads = lax.shift_right_logical(n_far, 2)
    lax.fori_loop(0, n_quads, far_quad, 0)

    @pl.when((n_far & 2) == 2)
    def _():
        tile(4 * n_quads, None)
        tile(4 * n_quads + 1, None)

    @pl.when((n_far & 1) == 1)
    def _():
        tile(n_far - 1, None)

    @pl.when(jd >= 1)
    def _():
        tile(jd - 1, 0)

    tile(jd, 1, last=True)
    _store_out(o_ref, acc_ref)


def _dsa_attn(qT, k, vT, maskT, bias, *, nb, tq, qpos0):
    tkp = k.shape[1]
    nkt = tkp // ATT_TILE
    nq = qT.shape[1] // nb // tq
    return pl.pallas_call(
        functools.partial(_dsa_kernel, tq=tq, qpos0=qpos0),
        out_shape=jax.ShapeDtypeStruct((nb * nq * tq, WIDTH), BF16),
        grid=(nb, nq),
        in_specs=[pl.BlockSpec((WIDTH, tq), lambda b, i: (0, b * nq + i)),
                  pl.BlockSpec((None, tkp, WIDTH), lambda b, i: (b, 0, 0),
                               pipeline_mode=pl.Buffered(1)),
                  pl.BlockSpec((nkt, N_HEADS * V_ROWS, ATT_TILE), lambda b, i: (b, 0, 0),
                               pipeline_mode=pl.Buffered(1)),
                  pl.BlockSpec((None, tkp, tq), lambda b, i: (b, 0, i)),
                  _const_spec(bias.shape)],
        out_specs=pl.BlockSpec((tq, WIDTH), lambda b, i: (b * nq + i, 0)),
        scratch_shapes=_attn_scratch(tq),
        compiler_params=_cparams(("parallel", "parallel")),
        name="dsa_attn",
    )(qT, k, vT, maskT, bias)


_BAND_TILES = BAND_CHUNKS * CHUNK // ATT_TILE + 1


def _band_kernel(qT_ref, *refs, tq, off):
    k_refs = refs[:_BAND_TILES]
    v_refs = refs[_BAND_TILES:2 * _BAND_TILES]
    bias_ref, o_ref, qsel_ref, s_ref, m_ref, acc_ref = refs[2 * _BAND_TILES:]
    _load_qsel(qT_ref, qsel_ref, tq)
    _init_state(m_ref, acc_ref)
    k_slab = lambda w: (lambda g: k_refs[w][:, g * 128:(g + 1) * 128])
    for h in range(N_HEADS):
        _scores(k_slab(0), qsel_ref, s_ref, h)
    for w in range(_BAND_TILES):
        entry = jnp.where(pl.program_id(1) + off + w >= 0, w, _BAND_TILES)
        _heads_tile(lambda h: v_refs[w][0, h * V_ROWS:(h + 1) * V_ROWS, :],
                    None if w == _BAND_TILES - 1 else k_slab(w + 1),
                    qsel_ref, s_ref, m_ref, acc_ref, lambda h: (bias_ref[entry, h],))
    _store_out(o_ref, acc_ref)


def _band_attn(qT, k, vT, bias, *, nb, tq, off):
    nkt = k.shape[1] // ATT_TILE
    nq = qT.shape[1] // nb // tq
    tile_of = lambda i, w: jnp.maximum(i + off + w, 0)
    k_specs = [pl.BlockSpec((None, ATT_TILE, WIDTH), lambda b, i, w=w: (b, tile_of(i, w), 0))
               for w in range(_BAND_TILES)]
    v_specs = [pl.BlockSpec((1, N_HEADS * V_ROWS, ATT_TILE),
                            lambda b, i, w=w: (b * nkt + tile_of(i, w), 0, 0))
               for w in range(_BAND_TILES)]
    return pl.pallas_call(
        functools.partial(_band_kernel, tq=tq, off=off),
        out_shape=jax.ShapeDtypeStruct((nb * nq * tq, WIDTH), BF16),
        grid=(nb, nq),
        in_specs=[pl.BlockSpec((WIDTH, tq), lambda b, i: (0, b * nq + i))] + k_specs + v_specs
                 + [_const_spec(bias.shape)],
        out_specs=pl.BlockSpec((tq, WIDTH), lambda b, i: (b * nq + i, 0)),
        scratch_shapes=_attn_scratch(tq),
        compiler_params=_cparams(("parallel", "parallel")),
        name="band_attn",
    )(qT, *([k] * _BAND_TILES), *([vT] * _BAND_TILES), bias)


def _mix_ln_kernel(h_ref, oa_ref, ob_ref, wg_ref, wba_ref, wbb_ref, wo_ref, g_ref, b_ref, o_ref,
                   *, alpha):
    h = h_ref[...]
    hb = h.astype(BF16)
    d = h.shape[1]
    ga = jnp.dot(hb, wg_ref[:, :d], preferred_element_type=F32)
    gb = jnp.dot(hb, wg_ref[:, d:], preferred_element_type=F32)
    ya = jnp.dot(oa_ref[...], wba_ref[...], preferred_element_type=F32)
    yb = jnp.dot(ob_ref[...], wbb_ref[...], preferred_element_type=F32)
    gated = jax.nn.sigmoid(ga) * ya + jax.nn.sigmoid(gb) * yb
    mix = jnp.dot(gated.astype(BF16), wo_ref[...], preferred_element_type=F32)
    o_ref[...] = _layer_norm_rows(alpha * h + mix, g_ref[...], b_ref[...])


def _mix_ln(h, oa, ob, wg, wba, wbb, wo, g, b, alpha):
    n, d = h.shape
    tm = TOKEN_TILE
    row = lambda width: pl.BlockSpec((tm, width), lambda i: (i, 0))
    return pl.pallas_call(
        functools.partial(_mix_ln_kernel, alpha=alpha),
        out_shape=jax.ShapeDtypeStruct((n, d), F32),
        grid=(n // tm,),
        in_specs=[row(d), row(WIDTH), row(WIDTH), _const_spec(wg.shape), _const_spec(wba.shape),
                  _const_spec(wbb.shape), _const_spec(wo.shape), _const_spec((1, d)),
                  _const_spec((1, d))],
        out_specs=row(d),
        compiler_params=_cparams(("parallel",)),
        name="mix_ln",
    )(h, oa, ob, wg, wba, wbb, wo, g.reshape(1, d), b.reshape(1, d))


def _t5_bucket(rel):
    half = T5_BUCKETS // 2
    exact = half // 2
    n = jnp.abs(rel)
    log_ratio = jnp.log(jnp.maximum(n, 1).astype(F32) / exact) / math.log(T5_MAX_DIST / exact)
    large = jnp.minimum(exact + (log_ratio * (half - exact)).astype(I32), half - 1)
    return (rel > 0).astype(I32) * half + jnp.where(n < exact, n, large)


def _rel_line(n_tiles, first, tq):
    period = ATT_TILE + tq
    y = jnp.arange(period, dtype=I32)[None, :]
    c = (jnp.arange(n_tiles, dtype=I32)[:, None] + first) * ATT_TILE
    return jnp.where(y < tq, c - y, c + period - y)


def _toeplitz(v, tq):
    period = v.shape[-1]
    flat = jnp.tile(v, (1,) * (v.ndim - 1) + (ATT_TILE,))[..., :ATT_TILE * (period - 1)]
    return flat.reshape(v.shape[:-1] + (ATT_TILE, period - 1))[..., :tq]


def _dsa_bias_table(t5_bias, tq):
    assert T5_MAX_DIST <= ATT_TILE + 1
    far = t5_bias[:, _t5_bucket(jnp.int32(-(ATT_TILE + 1)))]
    line = t5_bias[:, _t5_bucket(_rel_line(2, -1, tq))] - far[:, None, None]
    return (_toeplitz(jnp.moveaxis(line, 0, 1), tq) * LOG2E).astype(BF16)


def _band_bias_table(rel_bias, tq):
    first = 1 - _BAND_TILES
    ridx = jnp.clip(_rel_line(_BAND_TILES, first, tq), -REL_BACK, CHUNK - 1) + REL_BACK
    bias = _toeplitz(jnp.moveaxis(rel_bias[:, ridx], 0, 1), tq)
    w = jnp.arange(_BAND_TILES, dtype=I32)[:, None, None]
    j = jnp.arange(ATT_TILE, dtype=I32)[None, :, None]
    i = jnp.arange(tq, dtype=I32)[None, None, :]
    dchunk = ((w + first) * ATT_TILE + j) // CHUNK - i // CHUNK
    ok = (dchunk <= 0) & (dchunk >= -BAND_CHUNKS)
    table = jnp.where(ok[:, None], bias * LOG2E, MASKED)
    return jnp.concatenate([table, jnp.full_like(table[:1], MASKED)], axis=0).astype(BF16)


def _regroup_w_in(w_in):
    c_ki = 3 * WIDTH + N_HEADS_IDX * IDX_DIM
    c_wi = c_ki + IDX_DIM
    c_qb = c_wi + N_HEADS_IDX
    c_gate = c_qb + 3 * WIDTH
    ki = w_in[:, c_ki:c_wi]
    wi_pad = jnp.pad(w_in[:, c_wi:c_qb], ((0, 0), (0, 128 - N_HEADS_IDX)))
    w_proj = jnp.concatenate([w_in[:, :c_ki], ki, ki, wi_pad, w_in[:, c_qb:c_gate]], axis=1)
    return w_proj.astype(BF16), w_in[:, c_gate:].astype(BF16)


def _key_tiles_T(v):
    nb, tkp, wd = v.shape
    vt = v.reshape(nb, tkp // ATT_TILE, ATT_TILE, N_HEADS, HEAD_DIM).transpose(0, 1, 3, 4, 2)
    ones_slab = jnp.zeros(vt.shape[:3] + (V_ROWS - HEAD_DIM, ATT_TILE), v.dtype).at[..., 0, :].set(1)
    return jnp.concatenate([vt, ones_slab], axis=3).reshape(-1, N_HEADS * V_ROWS, ATT_TILE)


def _pad_keys(x, tkp):
    return jnp.pad(x, ((0, 0), (0, tkp - x.shape[1]), (0, 0)))


def kernel(x_prompt, x_sample, cache_k_a, cache_v_a, cache_kidx_a, cache_k_b, cache_v_b, t5_bias,
           ln1_g, ln1_b, ffn1_wi, ffn1_wo, ln2_g, ln2_b, w_in, rel_bias_b, w_branch_a, w_branch_b,
           w_out, ln3_g, ln3_b, ffn2_wi, ffn2_wo):
    depth = ln1_g.shape[0]
    alpha = (2.0 * depth) ** 0.25
    nbp, seq, d = x_prompt.shape
    nbs, dec, _ = x_sample.shape
    past = cache_k_a.shape[2]
    band = cache_k_b.shape[2]
    keep = min(BAND_CHUNKS * CHUNK, seq)
    tq_p, tq_s = ATT_TILE, 128
    assert seq % TOKEN_TILE == 0 and keep == TOKEN_TILE and dec <= tq_s and past % ATT_TILE == 0
    assert band == BAND_CHUNKS * CHUNK and (nbs * tq_s) % TOKEN_TILE == 0

    yp = x_prompt.reshape(nbp * seq, d)
    ys = jnp.pad(x_sample, ((0, 0), (0, tq_s - dec), (0, 0))).reshape(nbs * tq_s, d)
    dsa_bias_p = _dsa_bias_table(t5_bias, tq_p)
    dsa_bias_s = dsa_bias_p[..., :tq_s]
    n_sel_p = min(TOPK_MAX, seq // 4)
    n_sel_s = min(TOPK_MAX, (past + dec) // 4)
    tk_s = -(-(past + tq_s) // ATT_TILE) * ATT_TILE
    tkb_s = _BAND_TILES * ATT_TILE

    st_p, st_s = [], []
    for l in range(depth):
        w_proj, w_gate = _regroup_w_in(w_in[l])
        wi1, wo1 = ffn1_wi[l].astype(BF16), ffn1_wo[l].astype(BF16)
        wi2, wo2 = ffn2_wi[l].astype(BF16), ffn2_wo[l].astype(BF16)
        wba, wbb, wo = (w_branch_a[l].astype(BF16), w_branch_b[l].astype(BF16),
                        w_out[l].astype(BF16))
        band_bias_p = _band_bias_table(rel_bias_b[l], tq_p)
        band_bias_s = band_bias_p[..., :tq_s]

        h = _ffn_ln(yp, wi1, wo1, ln1_g[l], ln1_b[l], alpha)
        (qaT, ka, kab, va, vaT, qiT, ki, kk, wiT, qbT, kbb, vbT, kbl, vbl) = _in_proj(
            h, w_proj, seq // TOKEN_TILE)
        maskT = _idx_mask(qiT, wiT, kk.reshape(nbp, seq, 128), nb=nbp, tq=tq_p, qpos0=0,
                          n_valid=seq, n_sel=n_sel_p)
        oa = _dsa_attn(qaT, kab.reshape(nbp, seq, WIDTH), vaT, maskT, dsa_bias_p,
                       nb=nbp, tq=tq_p, qpos0=0)
        ob = _band_attn(qbT, kbb.reshape(nbp, seq, WIDTH), vbT, band_bias_p,
                        nb=nbp, tq=tq_p, off=1 - _BAND_TILES)
        h2 = _mix_ln(h, oa, ob, w_gate, wba, wbb, wo, ln2_g[l], ln2_b[l], alpha)
        yp = _ffn_ln(h2, wi2, wo2, ln3_g[l], ln3_b[l], alpha)
        st_p.append((ka.reshape(nbp, seq, N_HEADS, HEAD_DIM), va.reshape(nbp, seq, N_HEADS, HEAD_DIM),
                     ki.reshape(nbp, seq, IDX_DIM), kbl.reshape(nbp, keep, N_HEADS, HEAD_DIM),
                     vbl.reshape(nbp, keep, N_HEADS, HEAD_DIM)))

        h = _ffn_ln(ys, wi1, wo1, ln1_g[l], ln1_b[l], alpha)
        (qaT, ka, _, va, _, qiT, ki, _, wiT, qbT, _, _, kbl, vbl) = _in_proj(h, w_proj, 1)
        new = lambda a: a.reshape(nbs, tq_s, -1)[:, :dec]
        ka_n, va_n, ki_n, kb_n, vb_n = new(ka), new(va), new(ki), new(kbl), new(vbl)
        ka_all = jnp.concatenate([cache_k_a[l].reshape(nbs, past, WIDTH), ka_n], axis=1)
        va_all = jnp.concatenate([cache_v_a[l].reshape(nbs, past, WIDTH), va_n], axis=1)
        ki_all = jnp.concatenate([cache_kidx_a[l], ki_n], axis=1)
        kb_all = jnp.concatenate([cache_k_b[l].reshape(nbs, band, WIDTH), kb_n], axis=1)
        vb_all = jnp.concatenate([cache_v_b[l].reshape(nbs, band, WIDTH), vb_n], axis=1)
        kk_s = _pad_keys(jnp.concatenate([ki_all, ki_all], axis=2).astype(BF16), tk_s)
        maskT = _idx_mask(qiT, wiT, kk_s, nb=nbs, tq=tq_s, qpos0=past, n_valid=past + dec,
                          n_sel=n_sel_s)
        oa = _dsa_attn(qaT, _pad_keys(ka_all.astype(BF16), tk_s),
                       _key_tiles_T(_pad_keys(va_all.astype(BF16), tk_s)), maskT, dsa_bias_s,
                       nb=nbs, tq=tq_s, qpos0=past)
        ob = _band_attn(qbT, _pad_keys(kb_all.astype(BF16), tkb_s),
                        _key_tiles_T(_pad_keys(vb_all.astype(BF16), tkb_s)), band_bias_s,
                        nb=nbs, tq=tq_s, off=0)
        h2 = _mix_ln(h, oa, ob, w_gate, wba, wbb, wo, ln2_g[l], ln2_b[l], alpha)
        ys = _ffn_ln(h2, wi2, wo2, ln3_g[l], ln3_b[l], alpha)
        heads = lambda a: a.reshape(nbs, -1, N_HEADS, HEAD_DIM)
        st_s.append((heads(ka_n), heads(va_n), ki_n, heads(kb_all[:, -band:]),
                     heads(vb_all[:, -band:])))

    y_prompt = yp.reshape(nbp, seq, d)
    y_sample = ys.reshape(nbs, tq_s, d)[:, :dec]
    stack = lambda sts, t: jnp.stack([s[t] for s in sts])
    return (y_prompt, y_sample,
            stack(st_p, 0), stack(st_p, 1), stack(st_p, 2), stack(st_p, 3), stack(st_p, 4),
            stack(st_s, 0), stack(st_s, 1), stack(st_s, 2), stack(st_s, 3), stack(st_s, 4))
```

```python
import functools
import math

import jax
import jax.numpy as jnp
from jax import lax
from jax.experimental import pallas as pl
from jax.experimental.pallas import tpu as pltpu

F32 = jnp.float32
BF16 = jnp.bfloat16
I32 = jnp.int32
I16 = jnp.int16

CHUNK = 64
CHUNK_SHIFT = 6
HEAD_DIM = 64
N_HEADS = 8
WIDTH = N_HEADS * HEAD_DIM
N_HEADS_IDX = 4
IDX_DIM = 64
TOPK_MAX = 256
IDX_SCALE = (IDX_DIM * N_HEADS_IDX) ** -0.5
BAND_CHUNKS = 8
REL_BACK = 128
T5_BUCKETS = 32
T5_MAX_DIST = 128
LN_EPS = 1e-5
MASKED = -2.0 ** 100
LOG2E = math.log2(math.e)
V_ROWS = HEAD_DIM + 16
INT_MIN = -2 ** 31

ATT_TILE = 256
TOKEN_TILE = 512
FF_CHUNK = 256
VMEM_LIMIT_MB = 56


def _cparams(sem):
    return pltpu.CompilerParams(dimension_semantics=sem,
                                vmem_limit_bytes=VMEM_LIMIT_MB * 1024 * 1024)


def _const_spec(shape):
    nd = len(shape)
    return pl.BlockSpec(shape, lambda *_: (0,) * nd, pipeline_mode=pl.Buffered(1))


def _tree_sum(xs):
    return _tree_reduce(lambda a, b: a + b, xs)


def _tree_reduce(fn, xs):
    xs = list(xs)
    while len(xs) > 1:
        nxt = [fn(xs[a], xs[a + 1]) for a in range(0, len(xs) - 1, 2)]
        if len(xs) % 2:
            nxt.append(xs[-1])
        xs = nxt
    return xs[0]


def _layer_norm_rows(y, g, b):
    mu = jnp.mean(y, axis=-1, keepdims=True)
    d = y - mu
    var = jnp.mean(d * d, axis=-1, keepdims=True)
    return d * lax.rsqrt(var + LN_EPS) * g + b


def _ffn_ln_kernel(x_ref, wi_ref, wo_ref, g_ref, b_ref, o_ref, *, alpha, dff):
    x = x_ref[...]
    xb = x.astype(BF16)
    acc = jnp.zeros(x.shape, F32)
    for c in range(dff // FF_CHUNK):
        lo = c * FF_CHUNK
        a = jnp.dot(xb, wi_ref[:, lo:lo + FF_CHUNK], preferred_element_type=F32)
        u = jnp.dot(xb, wi_ref[:, dff + lo:dff + lo + FF_CHUNK], preferred_element_type=F32)
        hm = (a * jax.nn.sigmoid(a)) * u
        acc = acc + jnp.dot(hm.astype(BF16), wo_ref[lo:lo + FF_CHUNK, :],
                            preferred_element_type=F32)
    y = alpha * x + 0.5 * acc
    o_ref[...] = _layer_norm_rows(y, g_ref[...], b_ref[...])


def _ffn_ln(x, wi, wo, g, b, alpha):
    n, d = x.shape
    dff = wo.shape[0]
    tm = TOKEN_TILE
    return pl.pallas_call(
        functools.partial(_ffn_ln_kernel, alpha=alpha, dff=dff),
        out_shape=jax.ShapeDtypeStruct((n, d), F32),
        grid=(n // tm,),
        in_specs=[pl.BlockSpec((tm, d), lambda i: (i, 0)),
                  _const_spec((d, 2 * dff)), _const_spec((dff, d)),
                  _const_spec((1, d)), _const_spec((1, d))],
        out_specs=pl.BlockSpec((tm, d), lambda i: (i, 0)),
        compiler_params=_cparams(("parallel",)),
        name="ffn_ln",
    )(x, wi, wo, g.reshape(1, d), b.reshape(1, d))


_C_QA, _C_KA, _C_VA, _C_QI, _C_KI, _C_WI, _C_QB, _C_KB, _C_VB, _C_END = (
    0, 512, 1024, 1536, 1792, 1920, 2048, 2560, 3072, 3584)


def _value_slabs(zt):
    n = zt.shape[1]
    ones_slab = jnp.where(lax.broadcasted_iota(I32, (V_ROWS - HEAD_DIM, n), 0) == 0, 1.0, 0.0)
    return jnp.concatenate([piece for h in range(N_HEADS)
                            for piece in (zt[h * HEAD_DIM:(h + 1) * HEAD_DIM, :], ones_slab)], axis=0)


def _inproj_kernel(h_ref, w_ref, qaT_ref, ka_ref, kab_ref, va_ref, vaT_ref, qiT_ref, ki_ref,
                   kk_ref, wiT_ref, qbT_ref, kbb_ref, vbT_ref, kbl_ref, vbl_ref, *, per):
    hb = h_ref[...].astype(BF16)

    def proj(c0, c1):
        return jnp.dot(hb, w_ref[:, c0:c1], preferred_element_type=F32)

    def put_tiles(ref, zt):
        aug = _value_slabs(zt)
        for c in range(ref.shape[0]):
            ref[c] = aug[:, c * ATT_TILE:(c + 1) * ATT_TILE].astype(ref.dtype)

    scale = HEAD_DIM ** -0.5 * LOG2E
    qaT_ref[...] = (proj(_C_QA, _C_KA) * scale).T.astype(BF16)
    ka = proj(_C_KA, _C_VA)
    ka_ref[...] = ka
    kab_ref[...] = ka.astype(BF16)
    va = proj(_C_VA, _C_QI)
    va_ref[...] = va
    put_tiles(vaT_ref, va.T)
    qiT_ref[...] = proj(_C_QI, _C_KI).T.astype(BF16)
    kk = proj(_C_KI, _C_WI)
    ki_ref[...] = kk[:, :IDX_DIM]
    kk_ref[...] = kk.astype(BF16)
    wiT_ref[...] = proj(_C_WI, _C_QB).T[:8, :]
    qbT_ref[...] = (proj(_C_QB, _C_KB) * scale).T.astype(BF16)
    kb = proj(_C_KB, _C_VB)
    kbb_ref[...] = kb.astype(BF16)
    vb = proj(_C_VB, _C_END)
    put_tiles(vbT_ref, vb.T)

    @pl.when(pl.program_id(0) % per == per - 1)
    def _():
        kbl_ref[...] = kb
        vbl_ref[...] = vb


def _in_proj(h, w, per):
    n, d = h.shape
    tm = TOKEN_TILE
    nt = n // tm
    sub = tm // ATT_TILE
    row = lambda width: pl.BlockSpec((tm, width), lambda i: (i, 0))
    col = lambda rows: pl.BlockSpec((rows, tm), lambda i: (0, i))
    til = pl.BlockSpec((sub, N_HEADS * V_ROWS, ATT_TILE), lambda i: (i, 0, 0))
    last = pl.BlockSpec((tm, WIDTH), lambda i: (i // per, 0))
    sds = jax.ShapeDtypeStruct
    out_shape = (
        sds((WIDTH, n), BF16),
        sds((n, WIDTH), F32), sds((n, WIDTH), BF16),
        sds((n, WIDTH), F32), sds((n // ATT_TILE, N_HEADS * V_ROWS, ATT_TILE), BF16),
        sds((N_HEADS_IDX * IDX_DIM, n), BF16),
        sds((n, IDX_DIM), F32), sds((n, 128), BF16),
        sds((8, n), F32),
        sds((WIDTH, n), BF16),
        sds((n, WIDTH), BF16),
        sds((n // ATT_TILE, N_HEADS * V_ROWS, ATT_TILE), BF16),
        sds((nt // per * tm, WIDTH), F32), sds((nt // per * tm, WIDTH), F32),
    )
    out_specs = (col(WIDTH), row(WIDTH), row(WIDTH), row(WIDTH), til,
                 col(N_HEADS_IDX * IDX_DIM), row(IDX_DIM), row(128), col(8),
                 col(WIDTH), row(WIDTH), til, last, last)
    return pl.pallas_call(
        functools.partial(_inproj_kernel, per=per),
        out_shape=out_shape,
        grid=(nt,),
        in_specs=[pl.BlockSpec((tm, d), lambda i: (i, 0)), _const_spec((d, _C_END))],
        out_specs=out_specs,
        compiler_params=_cparams(("arbitrary",)),
        name="in_proj",
    )(h, w)


def _flip(bits):
    return bits ^ (lax.shift_right_arithmetic(bits, 31) & 0x7FFFFFFF)


def _bit_planes(words):
    a = list(reversed(words))
    j, m = 16, 0x0000FFFF
    while j:
        k = 0
        while k < 32:
            t = (a[k] ^ lax.shift_right_logical(a[k + j], j)) & m
            a[k] = a[k] ^ t
            a[k + j] = a[k + j] ^ lax.shift_left(t, j)
            k = (k + j + 1) & ~j
        j >>= 1
        m = (m ^ (m << j)) & 0xFFFFFFFF
    return a


def _idx_mask_kernel(qiT_ref, wiT_ref, kk_ref, mask_ref, st_ref, p_ref, c_ref, d_ref, incl_ref,
                     *, tq, qpos0, n_valid, n_sel):
    tk = ATT_TILE
    assert tk == 32 * 8
    nkt_total = st_ref.shape[0] // tk

    @pl.when((pl.program_id(0) == 0) & (pl.program_id(1) == 0))
    def _():
        p_ref[...] = jnp.zeros(p_ref.shape, I32)

    q0 = qpos0 + pl.program_id(1) * tq
    n_keys = jnp.minimum(q0 + tq, n_valid)
    nt = jnp.minimum(lax.shift_right_logical(n_keys + tk - 1, int(math.log2(tk))), nkt_total)

    lane = lax.broadcasted_iota(I32, (8, tq), 1)
    q_chunk = lax.shift_right_logical(q0 + lane, CHUNK_SHIFT)
    n_adm = jnp.minimum((q_chunk + 1) * CHUNK, n_valid)
    k_row = jnp.minimum(n_adm, n_sel)

    zeros = jnp.zeros((128 - IDX_DIM, tq), BF16)
    qh = [jnp.concatenate([qiT_ref[h * IDX_DIM:(h + 1) * IDX_DIM, :], zeros], axis=0)
          for h in range(N_HEADS_IDX)]
    scale_exact = math.frexp(IDX_SCALE)[0] == 0.5
    wh = [wiT_ref[h:h + 1, :] * IDX_SCALE if scale_exact else wiT_ref[h:h + 1, :]
          for h in range(N_HEADS_IDX)]

    def dots(j, h):
        r0 = pl.multiple_of(j * tk, tk)
        d_ref[h] = jnp.dot(kk_ref[pl.ds(r0, tk), :], qh[h], preferred_element_type=F32)

    def score_tile(j, last):
        r0 = pl.multiple_of(j * tk, tk)
        sc = None
        for h in range(N_HEADS_IDX):
            d = d_ref[h]
            if not last:
                dots(j + 1, h)
            term = wh[h] * jnp.maximum(d, 0.0)
            sc = term if sc is None else sc + term
        if not scale_exact:
            sc = sc * IDX_SCALE
        bits = lax.bitcast_convert_type(sc, I32)
        sign = lax.shift_right_arithmetic(bits, 31)
        key = (bits ^ (sign & 0x7FFFFFFF)) - sign
        if last:
            kpos = r0 + lax.broadcasted_iota(I32, (tk, tq), 0)
            key = jnp.where(kpos < n_adm[0:1, :], key, INT_MIN)
        st_ref[pl.ds(r0, tk), :] = key
        planes = _bit_planes([(key ^ INT_MIN)[v * 8:(v + 1) * 8, :] for v in range(32)])
        for i in range(32):
            p_ref[i, j] = planes[i]

    for h in range(N_HEADS_IDX):
        dots(0, h)

    def score_pair(u, carry):
        score_tile(2 * u, False)
        score_tile(2 * u + 1, False)
        return carry

    lax.fori_loop(0, lax.shift_right_logical(nt - 1, 1), score_pair, 0)

    @pl.when(((nt - 1) & 1) == 1)
    def _():
        score_tile(nt - 2, False)

    score_tile(nt - 1, True)

    group = incl_ref.shape[0]
    n_groups = nt if group == 1 else lax.shift_right_logical(nt + 3, 2)
    for g in range(group - 1):
        @pl.when(nt + g < n_groups * group)
        def _(g=g):
            r0 = pl.multiple_of((nt + g) * tk, tk)
            st_ref[pl.ds(r0, tk), :] = jnp.full((tk, tq), INT_MIN, I32)

    limit = n_adm - (nt - 1) * tk
    n_low = jnp.clip(lax.shift_right_arithmetic(limit - lax.broadcasted_iota(I32, (8, tq), 0) + 7, 3),
                     0, 32)
    last_word = jnp.where(n_low >= 32, -1, lax.shift_left(1, jnp.minimum(n_low, 31)) - 1)
    for jj in range(nkt_total):
        c_ref[jj] = jnp.where(jj < nt - 1, -1, jnp.where(jj == nt - 1, last_word, 0))

    def radix_pass(i, carry):
        want, t = carry
        n_set = _tree_sum([lax.population_count(c_ref[jj] & p_ref[i, jj]) for jj in range(nkt_total)])
        n_set = jnp.broadcast_to(jnp.sum(n_set, axis=0, keepdims=True), (8, tq))
        keep_set = n_set >= want
        for jj in range(nkt_total):
            cand = c_ref[jj]
            with_bit = cand & p_ref[i, jj]
            c_ref[jj] = jnp.where(keep_set, with_bit, cand ^ with_bit)
        return (jnp.where(keep_set, want, want - n_set),
                t | jnp.where(keep_set, lax.shift_left(jnp.int32(1), 31 - i), 0))

    want, t_bits = lax.fori_loop(0, 32, radix_pass, (k_row, jnp.zeros((8, tq), I32)))
    t_row = (t_bits ^ INT_MIN)[0:1, :]
    ties_wanted = want[0:1, :].astype(F32)

    tri = jnp.where(lax.broadcasted_iota(I32, (tk, tk), 0) >= lax.broadcasted_iota(I32, (tk, tk), 1),
                    1.0, 0.0).astype(BF16)

    def tie_prefix(j, g):
        r0 = pl.multiple_of(j * tk, tk)
        eq = jnp.where(st_ref[pl.ds(r0, tk), :] == t_row, 1.0, 0.0)
        incl_ref[g] = jnp.dot(tri, eq.astype(BF16), preferred_element_type=F32)

    def final_group(u, seen, last):
        for g in range(group):
            j = u * group + g
            r0 = pl.multiple_of(j * tk, tk)
            incl = incl_ref[g]
            if not last:
                tie_prefix(j + group, g)
            rank = seen + incl
            bar = t_row + jnp.where(rank <= ties_wanted, 0, 1)
            sel = jnp.where(st_ref[pl.ds(r0, tk), :] >= bar, 1, 0)
            mask_ref[pl.ds(r0, tk), :] = sel.astype(jnp.int8)
            seen = seen + incl[tk - 1:tk, :]
        return seen

    for g in range(group):
        tie_prefix(g, g)
    seen = lax.fori_loop(0, n_groups - 1, lambda u, seen: final_group(u, seen, False),
                         jnp.zeros((1, tq), F32))
    final_group(n_groups - 1, seen, True)

    def zero_tile(j, carry):
        r0 = pl.multiple_of(j * tk, tk)
        mask_ref[pl.ds(r0, tk), :] = jnp.zeros((tk, tq), jnp.int8)
        return carry

    lax.fori_loop(n_groups * group, nkt_total, zero_tile, 0)


def _idx_mask(qiT, wiT, kk, *, nb, tq, qpos0, n_valid, n_sel):
    tkp = kk.shape[1]
    nq = qiT.shape[1] // nb // tq
    return pl.pallas_call(
        functools.partial(_idx_mask_kernel, tq=tq, qpos0=qpos0, n_valid=n_valid, n_sel=n_sel),
        out_shape=jax.ShapeDtypeStruct((nb, tkp, nq * tq), jnp.int8),
        grid=(nb, nq),
        in_specs=[pl.BlockSpec((N_HEADS_IDX * IDX_DIM, tq), lambda b, i: (0, b * nq + i)),
                  pl.BlockSpec((8, tq), lambda b, i: (0, b * nq + i)),
                  pl.BlockSpec((None, tkp, 128), lambda b, i: (b, 0, 0))],
        out_specs=pl.BlockSpec((None, tkp, tq), lambda b, i: (b, 0, i)),
        scratch_shapes=[pltpu.VMEM((tkp, tq), I32),
                        pltpu.VMEM((32, tkp // ATT_TILE, 8, tq), I32),
                        pltpu.VMEM((tkp // ATT_TILE, 8, tq), I32),
                        pltpu.VMEM((N_HEADS_IDX, ATT_TILE, tq), F32),
                        pltpu.VMEM((4 if (tkp // ATT_TILE) % 4 == 0 else 1, ATT_TILE, tq), F32)],
        compiler_params=_cparams(("arbitrary", "arbitrary")),
        name="idx_mask",
    )(qiT, wiT, kk)


def _load_qsel(qT_ref, qsel_ref, tq):
    zeros = jnp.zeros((HEAD_DIM, tq), BF16)
    for h in range(N_HEADS):
        blk = qT_ref[h * HEAD_DIM:(h + 1) * HEAD_DIM, :]
        qsel_ref[h] = jnp.concatenate([blk, zeros] if h % 2 == 0 else [zeros, blk], axis=0)


def _init_state(m_ref, acc_ref):
    m_ref[...] = jnp.full(m_ref.shape, MASKED, F32)
    acc_ref[...] = jnp.zeros(acc_ref.shape, F32)


def _scores(k_slab, qsel_ref, s_ref, h):
    s_ref[h] = jnp.dot(k_slab(h // 2), qsel_ref[h], preferred_element_type=F32)


def _heads_tile(v_rows, next_k_slab, qsel_ref, s_ref, m_ref, acc_ref, addend):
    tq = s_ref.shape[2]
    slabs = s_ref.shape[1] // 16
    for h in range(N_HEADS):
        s = s_ref[h].astype(BF16)
        for a in addend(h):
            s = s + a
        if next_k_slab is not None:
            _scores(next_k_slab, qsel_ref, s_ref, h)
        m_old = m_ref[h:h + 1, :]
        top = _tree_reduce(jnp.maximum, [s[r * 16:(r + 1) * 16, :] for r in range(slabs)])
        m_new = jnp.maximum(m_old, jnp.max(top.astype(F32), axis=0, keepdims=True))
        alpha = jnp.exp2(m_old - m_new)
        m16 = jnp.broadcast_to(m_new, (16, tq)).astype(BF16)
        p = jnp.concatenate([jnp.exp2(s[r * 16:(r + 1) * 16, :] - m16) for r in range(slabs)], axis=0)
        m_ref[h:h + 1, :] = m_new
        pv = jnp.dot(v_rows(h), p, preferred_element_type=F32)
        rows = slice(h * V_ROWS, (h + 1) * V_ROWS)
        acc_ref[rows, :] = alpha * acc_ref[rows, :] + pv


def _attn_scratch(tq):
    return [pltpu.VMEM((N_HEADS, 128, tq), BF16), pltpu.VMEM((N_HEADS, ATT_TILE, tq), F32),
            pltpu.VMEM((N_HEADS, tq), F32), pltpu.VMEM((N_HEADS * V_ROWS, tq), F32)]


def _store_out(o_ref, acc_ref):
    outs = [acc_ref[h * V_ROWS:h * V_ROWS + HEAD_DIM, :]
            / acc_ref[h * V_ROWS + HEAD_DIM:h * V_ROWS + HEAD_DIM + 1, :] for h in range(N_HEADS)]
    o_ref[...] = jnp.concatenate(outs, axis=0).T.astype(o_ref.dtype)


def _dsa_kernel(qT_ref, k_ref, vT_ref, mask_ref, bias_ref, o_ref, qsel_ref, s_ref, m_ref, acc_ref,
                *, tq, qpos0):
    tk = ATT_TILE
    jd = lax.shift_right_logical(qpos0 + pl.program_id(1) * tq, int(math.log2(tk)))
    _load_qsel(qT_ref, qsel_ref, tq)
    _init_state(m_ref, acc_ref)

    def k_slab(j):
        r0 = pl.multiple_of(j * tk, tk)
        return lambda g: k_ref[pl.ds(r0, tk), g * 128:(g + 1) * 128]

    def tile(j, near, last=False):
        r0 = pl.multiple_of(j * tk, tk)
        unselected = jnp.where(mask_ref[pl.ds(r0, tk), :].astype(I32) != 0, 0.0,
                               MASKED).astype(BF16)
        addend = ((lambda h: (unselected,)) if near is None
                  else (lambda h: (unselected, bias_ref[near, h])))
        _heads_tile(lambda h: vT_ref[j, h * V_ROWS:(h + 1) * V_ROWS, :],
                    None if last else k_slab(j + 1),
                    qsel_ref, s_ref, m_ref, acc_ref, addend)

    for h in range(N_HEADS):
        _scores(k_slab(0), qsel_ref, s_ref, h)

    n_far = jnp.maximum(jd - 1, 0)

    def far_quad(u, carry):
        for c in range(4):
            tile(4 * u + c, None)
        return carry

    n_quads = lax.shift_right_logical(n_far, 2)
    lax.fori_loop(0, n_quads, far_quad, 0)

    @pl.when((n_far & 2) == 2)
    def _():
        tile(4 * n_quads, None)
        tile(4 * n_quads + 1, None)

    @pl.when((n_far & 1) == 1)
    def _():
        tile(n_far - 1, None)

    @pl.when(jd >= 1)
    def _():
        tile(jd - 1, 0)

    tile(jd, 1, last=True)
    _store_out(o_ref, acc_ref)


def _dsa_attn(qT, k, vT, maskT, bias, *, nb, tq, qpos0):
    tkp = k.shape[1]
    nkt = tkp // ATT_TILE
    nq = qT.shape[1] // nb // tq
    return pl.pallas_call(
        functools.partial(_dsa_kernel, tq=tq, qpos0=qpos0),
        out_shape=jax.ShapeDtypeStruct((nb * nq * tq, WIDTH), BF16),
        grid=(nb, nq),
        in_specs=[pl.BlockSpec((WIDTH, tq), lambda b, i: (0, b * nq + i)),
                  pl.BlockSpec((None, tkp, WIDTH), lambda b, i: (b, 0, 0),
                               pipeline_mode=pl.Buffered(1)),
                  pl.BlockSpec((nkt, N_HEADS * V_ROWS, ATT_TILE), lambda b, i: (b, 0, 0),
                               pipeline_mode=pl.Buffered(1)),
                  pl.BlockSpec((None, tkp, tq), lambda b, i: (b, 0, i)),
                  _const_spec(bias.shape)],
        out_specs=pl.BlockSpec((tq, WIDTH), lambda b, i: (b * nq + i, 0)),
        scratch_shapes=_attn_scratch(tq),
        compiler_params=_cparams(("parallel", "parallel")),
        name="dsa_attn",
    )(qT, k, vT, maskT, bias)


_BAND_TILES = BAND_CHUNKS * CHUNK // ATT_TILE + 1


def _band_kernel(qT_ref, *refs, tq, off):
    k_refs = refs[:_BAND_TILES]
    v_refs = refs[_BAND_TILES:2 * _BAND_TILES]
    bias_ref, o_ref, qsel_ref, s_ref, m_ref, acc_ref = refs[2 * _BAND_TILES:]
    _load_qsel(qT_ref, qsel_ref, tq)
    _init_state(m_ref, acc_ref)
    k_slab = lambda w: (lambda g: k_refs[w][:, g * 128:(g + 1) * 128])
    for h in range(N_HEADS):
        _scores(k_slab(0), qsel_ref, s_ref, h)
    for w in range(_BAND_TILES):
        entry = jnp.where(pl.program_id(1) + off + w >= 0, w, _BAND_TILES)
        _heads_tile(lambda h: v_refs[w][0, h * V_ROWS:(h + 1) * V_ROWS, :],
                    None if w == _BAND_TILES - 1 else k_slab(w + 1),
                    qsel_ref, s_ref, m_ref, acc_ref, lambda h: (bias_ref[entry, h],))
    _store_out(o_ref, acc_ref)


def _band_attn(qT, k, vT, bias, *, nb, tq, off):
    nkt = k.shape[1] // ATT_TILE
    nq = qT.shape[1] // nb // tq
    tile_of = lambda i, w: jnp.maximum(i + off + w, 0)
    k_specs = [pl.BlockSpec((None, ATT_TILE, WIDTH), lambda b, i, w=w: (b, tile_of(i, w), 0))
               for w in range(_BAND_TILES)]
    v_specs = [pl.BlockSpec((1, N_HEADS * V_ROWS, ATT_TILE),
                            lambda b, i, w=w: (b * nkt + tile_of(i, w), 0, 0))
               for w in range(_BAND_TILES)]
    return pl.pallas_call(
        functools.partial(_band_kernel, tq=tq, off=off),
        out_shape=jax.ShapeDtypeStruct((nb * nq * tq, WIDTH), BF16),
        grid=(nb, nq),
        in_specs=[pl.BlockSpec((WIDTH, tq), lambda b, i: (0, b * nq + i))] + k_specs + v_specs
                 + [_const_spec(bias.shape)],
        out_specs=pl.BlockSpec((tq, WIDTH), lambda b, i: (b * nq + i, 0)),
        scratch_shapes=_attn_scratch(tq),
        compiler_params=_cparams(("parallel", "parallel")),
        name="band_attn",
    )(qT, *([k] * _BAND_TILES), *([vT] * _BAND_TILES), bias)


def _mix_ln_kernel(h_ref, oa_ref, ob_ref, wg_ref, wba_ref, wbb_ref, wo_ref, g_ref, b_ref, o_ref,
                   *, alpha):
    h = h_ref[...]
    hb = h.astype(BF16)
    d = h.shape[1]
    ga = jnp.dot(hb, wg_ref[:, :d], preferred_element_type=F32)
    gb = jnp.dot(hb, wg_ref[:, d:], preferred_element_type=F32)
    ya = jnp.dot(oa_ref[...], wba_ref[...], preferred_element_type=F32)
    yb = jnp.dot(ob_ref[...], wbb_ref[...], preferred_element_type=F32)
    gated = jax.nn.sigmoid(ga) * ya + jax.nn.sigmoid(gb) * yb
    mix = jnp.dot(gated.astype(BF16), wo_ref[...], preferred_element_type=F32)
    o_ref[...] = _layer_norm_rows(alpha * h + mix, g_ref[...], b_ref[...])


def _mix_ln(h, oa, ob, wg, wba, wbb, wo, g, b, alpha):
    n, d = h.shape
    tm = TOKEN_TILE
    row = lambda width: pl.BlockSpec((tm, width), lambda i: (i, 0))
    return pl.pallas_call(
        functools.partial(_mix_ln_kernel, alpha=alpha),
        out_shape=jax.ShapeDtypeStruct((n, d), F32),
        grid=(n // tm,),
        in_specs=[row(d), row(WIDTH), row(WIDTH), _const_spec(wg.shape), _const_spec(wba.shape),
                  _const_spec(wbb.shape), _const_spec(wo.shape), _const_spec((1, d)),
                  _const_spec((1, d))],
        out_specs=row(d),
        compiler_params=_cparams(("parallel",)),
        name="mix_ln",
    )(h, oa, ob, wg, wba, wbb, wo, g.reshape(1, d), b.reshape(1, d))


def _t5_bucket(rel):
    half = T5_BUCKETS // 2
    exact = half // 2
    n = jnp.abs(rel)
    log_ratio = jnp.log(jnp.maximum(n, 1).astype(F32) / exact) / math.log(T5_MAX_DIST / exact)
    large = jnp.minimum(exact + (log_ratio * (half - exact)).astype(I32), half - 1)
    return (rel > 0).astype(I32) * half + jnp.where(n < exact, n, large)


def _rel_line(n_tiles, first, tq):
    period = ATT_TILE + tq
    y = jnp.arange(period, dtype=I32)[None, :]
    c = (jnp.arange(n_tiles, dtype=I32)[:, None] + first) * ATT_TILE
    return jnp.where(y < tq, c - y, c + period - y)


def _toeplitz(v, tq):
    period = v.shape[-1]
    flat = jnp.tile(v, (1,) * (v.ndim - 1) + (ATT_TILE,))[..., :ATT_TILE * (period - 1)]
    return flat.reshape(v.shape[:-1] + (ATT_TILE, period - 1))[..., :tq]


def _dsa_bias_table(t5_bias, tq):
    assert T5_MAX_DIST <= ATT_TILE + 1
    far = t5_bias[:, _t5_bucket(jnp.int32(-(ATT_TILE + 1)))]
    line = t5_bias[:, _t5_bucket(_rel_line(2, -1, tq))] - far[:, None, None]
    return (_toeplitz(jnp.moveaxis(line, 0, 1), tq) * LOG2E).astype(BF16)


def _band_bias_table(rel_bias, tq):
    first = 1 - _BAND_TILES
    ridx = jnp.clip(_rel_line(_BAND_TILES, first, tq), -REL_BACK, CHUNK - 1) + REL_BACK
    bias = _toeplitz(jnp.moveaxis(rel_bias[:, ridx], 0, 1), tq)
    w = jnp.arange(_BAND_TILES, dtype=I32)[:, None, None]
    j = jnp.arange(ATT_TILE, dtype=I32)[None, :, None]
    i = jnp.arange(tq, dtype=I32)[None, None, :]
    dchunk = ((w + first) * ATT_TILE + j) // CHUNK - i // CHUNK
    ok = (dchunk <= 0) & (dchunk >= -BAND_CHUNKS)
    table = jnp.where(ok[:, None], bias * LOG2E, MASKED)
    return jnp.concatenate([table, jnp.full_like(table[:1], MASKED)], axis=0).astype(BF16)


def _regroup_w_in(w_in):
    d, n_in = w_in.shape
    c_ki = 3 * WIDTH + N_HEADS_IDX * IDX_DIM
    c_wi = c_ki + IDX_DIM
    c_qb = c_wi + N_HEADS_IDX
    c_gate = c_qb + 3 * WIDTH
    rows = 128

    def regroup(w_ref, proj_ref, gate_ref):
        w = w_ref[...]
        ki = w[:, c_ki:c_wi]
        wi_pad = jnp.concatenate([w[:, c_wi:c_qb], jnp.zeros((rows, 128 - N_HEADS_IDX), F32)], axis=1)
        proj_ref[...] = jnp.concatenate([w[:, :c_ki], ki, ki, wi_pad, w[:, c_qb:c_gate]],
                                        axis=1).astype(BF16)
        gate_ref[...] = w[:, c_gate:].astype(BF16)

    return pl.pallas_call(
        regroup,
        out_shape=(jax.ShapeDtypeStruct((d, _C_END), BF16),
                   jax.ShapeDtypeStruct((d, n_in - c_gate), BF16)),
        grid=(d // rows,),
        in_specs=[pl.BlockSpec((rows, n_in), lambda i: (i, 0))],
        out_specs=(pl.BlockSpec((rows, _C_END), lambda i: (i, 0)),
                   pl.BlockSpec((rows, n_in - c_gate), lambda i: (i, 0))),
        compiler_params=_cparams(("parallel",)),
        name="regroup_w_in",
    )(w_in)


def _kv_prep_kernel(ck_ref, nk_ref, cv_ref, nv_ref, k_ref, vT_ref, *, n_cache_tiles, dec):
    cached = pl.program_id(1) < n_cache_tiles

    def tile(c_ref, n_ref):
        rows = lax.broadcasted_iota(I32, n_ref.shape, 0)
        fresh = jnp.where(rows < dec, n_ref[...], 0.0)
        fresh = jnp.concatenate([fresh, jnp.zeros((ATT_TILE - fresh.shape[0], WIDTH), F32)], axis=0)
        return jnp.where(cached, c_ref[...], fresh)

    k_ref[...] = tile(ck_ref, nk_ref).astype(BF16)
    vT_ref[0] = _value_slabs(tile(cv_ref, nv_ref).T).astype(BF16)


def _kv_prep(cache_k, new_k, cache_v, new_v, dec):
    nb, lc, _ = cache_k.shape
    tq = new_k.shape[1]
    nct = lc // ATT_TILE
    cache_spec = pl.BlockSpec((None, ATT_TILE, WIDTH), lambda b, t: (b, jnp.minimum(t, nct - 1), 0))
    new_spec = pl.BlockSpec((None, tq, WIDTH), lambda b, t: (b, 0, 0))
    return pl.pallas_call(
        functools.partial(_kv_prep_kernel, n_cache_tiles=nct, dec=dec),
        out_shape=(jax.ShapeDtypeStruct((nb, lc + ATT_TILE, WIDTH), BF16),
                   jax.ShapeDtypeStruct((nb * (nct + 1), N_HEADS * V_ROWS, ATT_TILE), BF16)),
        grid=(nb, nct + 1),
        in_specs=[cache_spec, new_spec, cache_spec, new_spec],
        out_specs=(pl.BlockSpec((None, ATT_TILE, WIDTH), lambda b, t: (b, t, 0)),
                   pl.BlockSpec((1, N_HEADS * V_ROWS, ATT_TILE), lambda b, t: (b * (nct + 1) + t, 0, 0))),
        compiler_params=_cparams(("parallel", "parallel")),
        name="kv_prep",
    )(cache_k, new_k, cache_v, new_v)


def _pad_keys(x, tkp):
    return jnp.pad(x, ((0, 0), (0, tkp - x.shape[1]), (0, 0)))


def kernel(x_prompt, x_sample, cache_k_a, cache_v_a, cache_kidx_a, cache_k_b, cache_v_b, t5_bias,
           ln1_g, ln1_b, ffn1_wi, ffn1_wo, ln2_g, ln2_b, w_in, rel_bias_b, w_branch_a, w_branch_b,
           w_out, ln3_g, ln3_b, ffn2_wi, ffn2_wo):
    depth = ln1_g.shape[0]
    alpha = (2.0 * depth) ** 0.25
    nbp, seq, d = x_prompt.shape
    nbs, dec, _ = x_sample.shape
    past = cache_k_a.shape[2]
    band = cache_k_b.shape[2]
    keep = min(BAND_CHUNKS * CHUNK, seq)
    tq_p, tq_s = ATT_TILE, 128
    assert seq % TOKEN_TILE == 0 and keep == TOKEN_TILE and dec <= tq_s and past % ATT_TILE == 0
    assert band == BAND_CHUNKS * CHUNK and (nbs * tq_s) % TOKEN_TILE == 0

    yp = x_prompt.reshape(nbp * seq, d)
    ys = jnp.pad(x_sample, ((0, 0), (0, tq_s - dec), (0, 0))).reshape(nbs * tq_s, d)
    dsa_bias_p = _dsa_bias_table(t5_bias, tq_p)
    dsa_bias_s = dsa_bias_p[..., :tq_s]
    n_sel_p = min(TOPK_MAX, seq // 4)
    n_sel_s = min(TOPK_MAX, (past + dec) // 4)
    tk_s = past + ATT_TILE
    assert band + ATT_TILE == _BAND_TILES * ATT_TILE

    st_p, st_s = [], []
    for l in range(depth):
        w_proj, w_gate = _regroup_w_in(w_in[l])
        wi1, wo1 = ffn1_wi[l].astype(BF16), ffn1_wo[l].astype(BF16)
        wi2, wo2 = ffn2_wi[l].astype(BF16), ffn2_wo[l].astype(BF16)
        wba, wbb, wo = (w_branch_a[l].astype(BF16), w_branch_b[l].astype(BF16),
                        w_out[l].astype(BF16))
        band_bias_p = _band_bias_table(rel_bias_b[l], tq_p)
        band_bias_s = band_bias_p[..., :tq_s]

        h = _ffn_ln(yp, wi1, wo1, ln1_g[l], ln1_b[l], alpha)
        (qaT, ka, kab, va, vaT, qiT, ki, kk, wiT, qbT, kbb, vbT, kbl, vbl) = _in_proj(
            h, w_proj, seq // TOKEN_TILE)
        maskT = _idx_mask(qiT, wiT, kk.reshape(nbp, seq, 128), nb=nbp, tq=tq_p, qpos0=0,
                          n_valid=seq, n_sel=n_sel_p)
        oa = _dsa_attn(qaT, kab.reshape(nbp, seq, WIDTH), vaT, maskT, dsa_bias_p,
                       nb=nbp, tq=tq_p, qpos0=0)
        ob = _band_attn(qbT, kbb.reshape(nbp, seq, WIDTH), vbT, band_bias_p,
                        nb=nbp, tq=tq_p, off=1 - _BAND_TILES)
        h2 = _mix_ln(h, oa, ob, w_gate, wba, wbb, wo, ln2_g[l], ln2_b[l], alpha)
        yp = _ffn_ln(h2, wi2, wo2, ln3_g[l], ln3_b[l], alpha)
        st_p.append((ka.reshape(nbp, seq, N_HEADS, HEAD_DIM), va.reshape(nbp, seq, N_HEADS, HEAD_DIM),
                     ki.reshape(nbp, seq, IDX_DIM), kbl.reshape(nbp, keep, N_HEADS, HEAD_DIM),
                     vbl.reshape(nbp, keep, N_HEADS, HEAD_DIM)))

        h = _ffn_ln(ys, wi1, wo1, ln1_g[l], ln1_b[l], alpha)
        (qaT, ka, _, va, _, qiT, ki, _, wiT, qbT, _, _, kbl, vbl) = _in_proj(h, w_proj, 1)
        per_stream = lambda a: a.reshape(nbs, tq_s, -1)
        new = lambda a: per_stream(a)[:, :dec]
        ka_n, va_n, ki_n, kb_n, vb_n = new(ka), new(va), new(ki), new(kbl), new(vbl)
        cache_kb = cache_k_b[l].reshape(nbs, band, WIDTH)
        cache_vb = cache_v_b[l].reshape(nbs, band, WIDTH)
        ki_all = jnp.concatenate([cache_kidx_a[l], ki_n], axis=1)
        kb_all = jnp.concatenate([cache_kb, kb_n], axis=1)
        vb_all = jnp.concatenate([cache_vb, vb_n], axis=1)
        kk_s = _pad_keys(jnp.concatenate([ki_all, ki_all], axis=2).astype(BF16), tk_s)
        maskT = _idx_mask(qiT, wiT, kk_s, nb=nbs, tq=tq_s, qpos0=past, n_valid=past + dec,
                          n_sel=n_sel_s)
        k_s, vT_s = _kv_prep(cache_k_a[l].reshape(nbs, past, WIDTH), per_stream(ka),
                             cache_v_a[l].reshape(nbs, past, WIDTH), per_stream(va), dec)
        oa = _dsa_attn(qaT, k_s, vT_s, maskT, dsa_bias_s, nb=nbs, tq=tq_s, qpos0=past)
        kb_s, vbT_s = _kv_prep(cache_kb, per_stream(kbl), cache_vb, per_stream(vbl), dec)
        ob = _band_attn(qbT, kb_s, vbT_s, band_bias_s, nb=nbs, tq=tq_s, off=0)
        h2 = _mix_ln(h, oa, ob, w_gate, wba, wbb, wo, ln2_g[l], ln2_b[l], alpha)
        ys = _ffn_ln(h2, wi2, wo2, ln3_g[l], ln3_b[l], alpha)
        heads = lambda a: a.reshape(nbs, -1, N_HEADS, HEAD_DIM)
        st_s.append((heads(ka_n), heads(va_n), ki_n, heads(kb_all[:, -band:]),
                     heads(vb_all[:, -band:])))

    y_prompt = yp.reshape(nbp, seq, d)
    y_sample = ys.reshape(nbs, tq_s, d)[:, :dec]
    stack = lambda sts, t: jnp.stack([s[t] for s in sts])
    return (y_prompt, y_sample,
            stack(st_p, 0), stack(st_p, 1), stack(st_p, 2), stack(st_p, 3), stack(st_p, 4),
            stack(st_s, 0), stack(st_s, 1), stack(st_s, 2), stack(st_s, 3), stack(st_s, 4))
```

```python
import functools
import math

import jax
import jax.numpy as jnp
from jax import lax
from jax.experimental import pallas as pl
from jax.experimental.pallas import tpu as pltpu

F32 = jnp.float32
BF16 = jnp.bfloat16
I32 = jnp.int32
I16 = jnp.int16

CHUNK = 64
CHUNK_SHIFT = 6
HEAD_DIM = 64
N_HEADS = 8
WIDTH = N_HEADS * HEAD_DIM
N_HEADS_IDX = 4
IDX_DIM = 64
TOPK_MAX = 256
IDX_SCALE = (IDX_DIM * N_HEADS_IDX) ** -0.5
BAND_CHUNKS = 8
REL_BACK = 128
T5_BUCKETS = 32
T5_MAX_DIST = 128
LN_EPS = 1e-5
MASKED = -2.0 ** 100
LOG2E = math.log2(math.e)
V_ROWS = HEAD_DIM + 16
INT_MIN = -2 ** 31

ATT_TILE = 256
TOKEN_TILE = 512
FF_CHUNK = 256
VMEM_LIMIT_MB = 56


def _cparams(sem):
    return pltpu.CompilerParams(dimension_semantics=sem,
                                vmem_limit_bytes=VMEM_LIMIT_MB * 1024 * 1024)


def _const_spec(shape):
    nd = len(shape)
    return pl.BlockSpec(shape, lambda *_: (0,) * nd, pipeline_mode=pl.Buffered(1))


def _tree_sum(xs):
    return _tree_reduce(lambda a, b: a + b, xs)


def _tree_reduce(fn, xs):
    xs = list(xs)
    while len(xs) > 1:
        nxt = [fn(xs[a], xs[a + 1]) for a in range(0, len(xs) - 1, 2)]
        if len(xs) % 2:
            nxt.append(xs[-1])
        xs = nxt
    return xs[0]


def _layer_norm_rows(y, g, b):
    mu = jnp.mean(y, axis=-1, keepdims=True)
    d = y - mu
    var = jnp.mean(d * d, axis=-1, keepdims=True)
    return d * lax.rsqrt(var + LN_EPS) * g + b


def _ffn_ln_kernel(x_ref, wi_ref, wo_ref, g_ref, b_ref, o_ref, *, alpha, dff):
    x = x_ref[...]
    xb = x.astype(BF16)
    acc = jnp.zeros(x.shape, F32)
    for c in range(dff // FF_CHUNK):
        lo = c * FF_CHUNK
        a = jnp.dot(xb, wi_ref[:, lo:lo + FF_CHUNK], preferred_element_type=F32)
        u = jnp.dot(xb, wi_ref[:, dff + lo:dff + lo + FF_CHUNK], preferred_element_type=F32)
        hm = (a * jax.nn.sigmoid(a)) * u
        acc = acc + jnp.dot(hm.astype(BF16), wo_ref[lo:lo + FF_CHUNK, :],
                            preferred_element_type=F32)
    y = alpha * x + 0.5 * acc
    o_ref[...] = _layer_norm_rows(y, g_ref[...], b_ref[...])


def _ffn_ln(x, wi, wo, g, b, alpha):
    n, d = x.shape
    dff = wo.shape[0]
    tm = TOKEN_TILE
    return pl.pallas_call(
        functools.partial(_ffn_ln_kernel, alpha=alpha, dff=dff),
        out_shape=jax.ShapeDtypeStruct((n, d), F32),
        grid=(n // tm,),
        in_specs=[pl.BlockSpec((tm, d), lambda i: (i, 0)),
                  _const_spec((d, 2 * dff)), _const_spec((dff, d)),
                  _const_spec((1, d)), _const_spec((1, d))],
        out_specs=pl.BlockSpec((tm, d), lambda i: (i, 0)),
        compiler_params=_cparams(("parallel",)),
        name="ffn_ln",
    )(x, wi, wo, g.reshape(1, d), b.reshape(1, d))


_C_QA, _C_KA, _C_VA, _C_QI, _C_KI, _C_WI, _C_QB, _C_KB, _C_VB, _C_END = (
    0, 512, 1024, 1536, 1792, 1920, 2048, 2560, 3072, 3584)


def _value_slabs(zt):
    n = zt.shape[1]
    ones_slab = jnp.where(lax.broadcasted_iota(I32, (V_ROWS - HEAD_DIM, n), 0) == 0, 1.0, 0.0)
    return jnp.concatenate([piece for h in range(N_HEADS)
                            for piece in (zt[h * HEAD_DIM:(h + 1) * HEAD_DIM, :], ones_slab)], axis=0)


def _inproj_kernel(h_ref, w_ref, qaT_ref, ka_ref, kab_ref, va_ref, vaT_ref, qiT_ref, ki_ref,
                   kk_ref, wiT_ref, qbT_ref, kbb_ref, vbT_ref, kbl_ref, vbl_ref, *, per):
    hb = h_ref[...].astype(BF16)

    def proj(c0, c1):
        return jnp.dot(hb, w_ref[:, c0:c1], preferred_element_type=F32)

    def put_tiles(ref, zt):
        aug = _value_slabs(zt)
        for c in range(ref.shape[0]):
            ref[c] = aug[:, c * ATT_TILE:(c + 1) * ATT_TILE].astype(ref.dtype)

    scale = HEAD_DIM ** -0.5 * LOG2E
    qaT_ref[...] = (proj(_C_QA, _C_KA) * scale).T.astype(BF16)
    ka = proj(_C_KA, _C_VA)
    ka_ref[...] = ka
    kab_ref[...] = ka.astype(BF16)
    va = proj(_C_VA, _C_QI)
    va_ref[...] = va
    put_tiles(vaT_ref, va.T)
    qiT_ref[...] = proj(_C_QI, _C_KI).T.astype(BF16)
    kk = proj(_C_KI, _C_WI)
    ki_ref[...] = kk[:, :IDX_DIM]
    kk_ref[...] = kk.astype(BF16)
    wiT_ref[...] = proj(_C_WI, _C_QB).T[:8, :]
    qbT_ref[...] = (proj(_C_QB, _C_KB) * scale).T.astype(BF16)
    kb = proj(_C_KB, _C_VB)
    kbb_ref[...] = kb.astype(BF16)
    vb = proj(_C_VB, _C_END)
    put_tiles(vbT_ref, vb.T)

    @pl.when(pl.program_id(0) % per == per - 1)
    def _():
        kbl_ref[...] = kb
        vbl_ref[...] = vb


def _in_proj(h, w, per):
    n, d = h.shape
    tm = TOKEN_TILE
    nt = n // tm
    sub = tm // ATT_TILE
    row = lambda width: pl.BlockSpec((tm, width), lambda i: (i, 0))
    col = lambda rows: pl.BlockSpec((rows, tm), lambda i: (0, i))
    til = pl.BlockSpec((sub, N_HEADS * V_ROWS, ATT_TILE), lambda i: (i, 0, 0))
    last = pl.BlockSpec((tm, WIDTH), lambda i: (i // per, 0))
    sds = jax.ShapeDtypeStruct
    out_shape = (
        sds((WIDTH, n), BF16),
        sds((n, WIDTH), F32), sds((n, WIDTH), BF16),
        sds((n, WIDTH), F32), sds((n // ATT_TILE, N_HEADS * V_ROWS, ATT_TILE), BF16),
        sds((N_HEADS_IDX * IDX_DIM, n), BF16),
        sds((n, IDX_DIM), F32), sds((n, 128), BF16),
        sds((8, n), F32),
        sds((WIDTH, n), BF16),
        sds((n, WIDTH), BF16),
        sds((n // ATT_TILE, N_HEADS * V_ROWS, ATT_TILE), BF16),
        sds((nt // per * tm, WIDTH), F32), sds((nt // per * tm, WIDTH), F32),
    )
    out_specs = (col(WIDTH), row(WIDTH), row(WIDTH), row(WIDTH), til,
                 col(N_HEADS_IDX * IDX_DIM), row(IDX_DIM), row(128), col(8),
                 col(WIDTH), row(WIDTH), til, last, last)
    return pl.pallas_call(
        functools.partial(_inproj_kernel, per=per),
        out_shape=out_shape,
        grid=(nt,),
        in_specs=[pl.BlockSpec((tm, d), lambda i: (i, 0)), _const_spec((d, _C_END))],
        out_specs=out_specs,
        compiler_params=_cparams(("arbitrary",)),
        name="in_proj",
    )(h, w)


def _flip(bits):
    return bits ^ (lax.shift_right_arithmetic(bits, 31) & 0x7FFFFFFF)


def _bit_planes(words):
    a = list(reversed(words))
    j, m = 16, 0x0000FFFF
    while j:
        k = 0
        while k < 32:
            t = (a[k] ^ lax.shift_right_logical(a[k + j], j)) & m
            a[k] = a[k] ^ t
            a[k + j] = a[k + j] ^ lax.shift_left(t, j)
            k = (k + j + 1) & ~j
        j >>= 1
        m = (m ^ (m << j)) & 0xFFFFFFFF
    return a


def _idx_mask_kernel(qiT_ref, wiT_ref, kk_ref, mask_ref, st_ref, p_ref, c_ref, d_ref, incl_ref,
                     *, tq, qpos0, n_valid, n_sel):
    tk = ATT_TILE
    assert tk == 32 * 8
    nkt_total = st_ref.shape[0] // tk

    @pl.when((pl.program_id(0) == 0) & (pl.program_id(1) == 0))
    def _():
        p_ref[...] = jnp.zeros(p_ref.shape, I32)

    q0 = qpos0 + pl.program_id(1) * tq
    n_keys = jnp.minimum(q0 + tq, n_valid)
    nt = jnp.minimum(lax.shift_right_logical(n_keys + tk - 1, int(math.log2(tk))), nkt_total)

    lane = lax.broadcasted_iota(I32, (8, tq), 1)
    q_chunk = lax.shift_right_logical(q0 + lane, CHUNK_SHIFT)
    n_adm = jnp.minimum((q_chunk + 1) * CHUNK, n_valid)
    k_row = jnp.minimum(n_adm, n_sel)

    zeros = jnp.zeros((128 - IDX_DIM, tq), BF16)
    qh = [jnp.concatenate([qiT_ref[h * IDX_DIM:(h + 1) * IDX_DIM, :], zeros], axis=0)
          for h in range(N_HEADS_IDX)]
    scale_exact = math.frexp(IDX_SCALE)[0] == 0.5
    wh = [wiT_ref[h:h + 1, :] * IDX_SCALE if scale_exact else wiT_ref[h:h + 1, :]
          for h in range(N_HEADS_IDX)]

    def dots(j, h):
        r0 = pl.multiple_of(j * tk, tk)
        d_ref[h] = jnp.dot(kk_ref[pl.ds(r0, tk), :], qh[h], preferred_element_type=F32)

    def score_tile(j, last):
        r0 = pl.multiple_of(j * tk, tk)
        sc = None
        for h in range(N_HEADS_IDX):
            d = d_ref[h]
            if not last:
                dots(j + 1, h)
            term = wh[h] * jnp.maximum(d, 0.0)
            sc = term if sc is None else sc + term
        if not scale_exact:
            sc = sc * IDX_SCALE
        bits = lax.bitcast_convert_type(sc, I32)
        sign = lax.shift_right_arithmetic(bits, 31)
        key = (bits ^ (sign & 0x7FFFFFFF)) - sign
        if last:
            kpos = r0 + lax.broadcasted_iota(I32, (tk, tq), 0)
            key = jnp.where(kpos < n_adm[0:1, :], key, INT_MIN)
        st_ref[pl.ds(r0, tk), :] = key
        planes = _bit_planes([(key ^ INT_MIN)[v * 8:(v + 1) * 8, :] for v in range(32)])
        for i in range(32):
            p_ref[i, j] = planes[i]

    for h in range(N_HEADS_IDX):
        dots(0, h)

    def score_pair(u, carry):
        score_tile(2 * u, False)
        score_tile(2 * u + 1, False)
        return carry

    lax.fori_loop(0, lax.shift_right_logical(nt - 1, 1), score_pair, 0)

    @pl.when(((nt - 1) & 1) == 1)
    def _():
        score_tile(nt - 2, False)

    score_tile(nt - 1, True)

    group = incl_ref.shape[0]
    n_groups = nt if group == 1 else lax.shift_right_logical(nt + 3, 2)
    for g in range(group - 1):
        @pl.when(nt + g < n_groups * group)
        def _(g=g):
            r0 = pl.multiple_of((nt + g) * tk, tk)
            st_ref[pl.ds(r0, tk), :] = jnp.full((tk, tq), INT_MIN, I32)

    limit = n_adm - (nt - 1) * tk
    n_low = jnp.clip(lax.shift_right_arithmetic(limit - lax.broadcasted_iota(I32, (8, tq), 0) + 7, 3),
                     0, 32)
    last_word = jnp.where(n_low >= 32, -1, lax.shift_left(1, jnp.minimum(n_low, 31)) - 1)
    for jj in range(nkt_total):
        c_ref[jj] = jnp.where(jj < nt - 1, -1, jnp.where(jj == nt - 1, last_word, 0))

    def radix_pass(i, carry):
        want, t = carry
        n_set = _tree_sum([lax.population_count(c_ref[jj] & p_ref[i, jj]) for jj in range(nkt_total)])
        n_set = jnp.broadcast_to(jnp.sum(n_set, axis=0, keepdims=True), (8, tq))
        keep_set = n_set >= want
        flip = jnp.where(keep_set, 0, -1)
        for jj in range(nkt_total):
            c_ref[jj] = c_ref[jj] & (p_ref[i, jj] ^ flip)
        return (jnp.where(keep_set, want, want - n_set),
                t | jnp.where(keep_set, lax.shift_left(jnp.int32(1), 31 - i), 0))

    want, t_bits = lax.fori_loop(0, 32, radix_pass, (k_row, jnp.zeros((8, tq), I32)))
    t_row = (t_bits ^ INT_MIN)[0:1, :]
    ties_wanted = want[0:1, :].astype(F32)

    tri = jnp.where(lax.broadcasted_iota(I32, (tk, tk), 0) >= lax.broadcasted_iota(I32, (tk, tk), 1),
                    1.0, 0.0).astype(BF16)

    def tie_prefix(j, g):
        r0 = pl.multiple_of(j * tk, tk)
        eq = jnp.where(st_ref[pl.ds(r0, tk), :] == t_row, 1.0, 0.0)
        incl_ref[g] = jnp.dot(tri, eq.astype(BF16), preferred_element_type=F32)

    def final_group(u, seen, last):
        for g in range(group):
            j = u * group + g
            r0 = pl.multiple_of(j * tk, tk)
            incl = incl_ref[g]
            if not last:
                tie_prefix(j + group, g)
            rank = seen + incl
            bar = t_row + jnp.where(rank <= ties_wanted, 0, 1)
            sel = jnp.where(st_ref[pl.ds(r0, tk), :] >= bar, 1, 0)
            mask_ref[pl.ds(r0, tk), :] = sel.astype(jnp.int8)
            seen = seen + incl[tk - 1:tk, :]
        return seen

    for g in range(group):
        tie_prefix(g, g)
    seen = lax.fori_loop(0, n_groups - 1, lambda u, seen: final_group(u, seen, False),
                         jnp.zeros((1, tq), F32))
    final_group(n_groups - 1, seen, True)

    def zero_tile(j, carry):
        r0 = pl.multiple_of(j * tk, tk)
        mask_ref[pl.ds(r0, tk), :] = jnp.zeros((tk, tq), jnp.int8)
        return carry

    lax.fori_loop(n_groups * group, nkt_total, zero_tile, 0)


def _idx_mask(qiT, wiT, kk, *, nb, tq, qpos0, n_valid, n_sel):
    tkp = kk.shape[1]
    nq = qiT.shape[1] // nb // tq
    return pl.pallas_call(
        functools.partial(_idx_mask_kernel, tq=tq, qpos0=qpos0, n_valid=n_valid, n_sel=n_sel),
        out_shape=jax.ShapeDtypeStruct((nb, tkp, nq * tq), jnp.int8),
        grid=(nb, nq),
        in_specs=[pl.BlockSpec((N_HEADS_IDX * IDX_DIM, tq), lambda b, i: (0, b * nq + i)),
                  pl.BlockSpec((8, tq), lambda b, i: (0, b * nq + i)),
                  pl.BlockSpec((None, tkp, 128), lambda b, i: (b, 0, 0))],
        out_specs=pl.BlockSpec((None, tkp, tq), lambda b, i: (b, 0, i)),
        scratch_shapes=[pltpu.VMEM((tkp, tq), I32),
                        pltpu.VMEM((32, tkp // ATT_TILE, 8, tq), I32),
                        pltpu.VMEM((tkp // ATT_TILE, 8, tq), I32),
                        pltpu.VMEM((N_HEADS_IDX, ATT_TILE, tq), F32),
                        pltpu.VMEM((4 if (tkp // ATT_TILE) % 4 == 0 else 1, ATT_TILE, tq), F32)],
        compiler_params=_cparams(("arbitrary", "arbitrary")),
        name="idx_mask",
    )(qiT, wiT, kk)


def _load_qsel(qT_ref, qsel_ref, tq):
    zeros = jnp.zeros((HEAD_DIM, tq), BF16)
    for h in range(N_HEADS):
        blk = qT_ref[h * HEAD_DIM:(h + 1) * HEAD_DIM, :]
        qsel_ref[h] = jnp.concatenate([blk, zeros] if h % 2 == 0 else [zeros, blk], axis=0)


def _init_state(m_ref, acc_ref):
    m_ref[...] = jnp.full(m_ref.shape, MASKED, F32)
    acc_ref[...] = jnp.zeros(acc_ref.shape, F32)


def _scores(k_slab, qsel_ref, s_ref, h):
    s_ref[h] = jnp.dot(k_slab(h // 2), qsel_ref[h], preferred_element_type=F32)


def _heads_tile(v_rows, next_k_slab, qsel_ref, s_ref, m_ref, acc_ref, addend):
    tq = s_ref.shape[2]
    slabs = s_ref.shape[1] // 16
    for h in range(N_HEADS):
        s = s_ref[h].astype(BF16)
        for a in addend(h):
            s = s + a
        if next_k_slab is not None:
            _scores(next_k_slab, qsel_ref, s_ref, h)
        m_old = m_ref[h:h + 1, :]
        top = _tree_reduce(jnp.maximum, [s[r * 16:(r + 1) * 16, :] for r in range(slabs)])
        m_new = jnp.maximum(m_old, jnp.max(top.astype(F32), axis=0, keepdims=True))
        alpha = jnp.exp2(m_old - m_new)
        m16 = jnp.broadcast_to(m_new, (16, tq)).astype(BF16)
        p = jnp.concatenate([jnp.exp2(s[r * 16:(r + 1) * 16, :] - m16) for r in range(slabs)], axis=0)
        m_ref[h:h + 1, :] = m_new
        pv = jnp.dot(v_rows(h), p, preferred_element_type=F32)
        rows = slice(h * V_ROWS, (h + 1) * V_ROWS)
        acc_ref[rows, :] = alpha * acc_ref[rows, :] + pv


def _attn_scratch(tq):
    return [pltpu.VMEM((N_HEADS, 128, tq), BF16), pltpu.VMEM((N_HEADS, ATT_TILE, tq), F32),
            pltpu.VMEM((N_HEADS, tq), F32), pltpu.VMEM((N_HEADS * V_ROWS, tq), F32)]


def _store_out(o_ref, acc_ref):
    outs = [acc_ref[h * V_ROWS:h * V_ROWS + HEAD_DIM, :]
            / acc_ref[h * V_ROWS + HEAD_DIM:h * V_ROWS + HEAD_DIM + 1, :] for h in range(N_HEADS)]
    o_ref[...] = jnp.concatenate(outs, axis=0).T.astype(o_ref.dtype)


def _dsa_kernel(qT_ref, k_ref, vT_ref, mask_ref, bias_ref, o_ref, qsel_ref, s_ref, m_ref, acc_ref,
                *, tq, qpos0):
    tk = ATT_TILE
    jd = lax.shift_right_logical(qpos0 + pl.program_id(1) * tq, int(math.log2(tk)))
    _load_qsel(qT_ref, qsel_ref, tq)
    _init_state(m_ref, acc_ref)

    def k_slab(j):
        r0 = pl.multiple_of(j * tk, tk)
        return lambda g: k_ref[pl.ds(r0, tk), g * 128:(g + 1) * 128]

    def tile(j, near, last=False):
        r0 = pl.multiple_of(j * tk, tk)
        unselected = jnp.where(mask_ref[pl.ds(r0, tk), :].astype(I32) != 0, 0.0,
                               MASKED).astype(BF16)
        addend = ((lambda h: (unselected,)) if near is None
                  else (lambda h: (unselected, bias_ref[near, h])))
        _heads_tile(lambda h: vT_ref[j, h * V_ROWS:(h + 1) * V_ROWS, :],
                    None if last else k_slab(j + 1),
                    qsel_ref, s_ref, m_ref, acc_ref, addend)

    for h in range(N_HEADS):
        _scores(k_slab(0), qsel_ref, s_ref, h)

    n_far = jnp.maximum(jd - 1, 0)

    def far_quad(u, carry):
        for c in range(4):
            tile(4 * u + c, None)
        return carry

    n_quads = lax.shift_right_logical(n_far, 2)
    lax.fori_loop(0, n_quads, far_quad, 0)

    @pl.when((n_far & 2) == 2)
    def _():
        tile(4 * n_quads, None)
        tile(4 * n_quads + 1, None)

    @pl.when((n_far & 1) == 1)
    def _():
        tile(n_far - 1, None)

    @pl.when(jd >= 1)
    def _():
        tile(jd - 1, 0)

    tile(jd, 1, last=True)
    _store_out(o_ref, acc_ref)


def _dsa_attn(qT, k, vT, maskT, bias, *, nb, tq, qpos0):
    tkp = k.shape[1]
    nkt = tkp // ATT_TILE
    nq = qT.shape[1] // nb // tq
    return pl.pallas_call(
        functools.partial(_dsa_kernel, tq=tq, qpos0=qpos0),
        out_shape=jax.ShapeDtypeStruct((nb * nq * tq, WIDTH), BF16),
        grid=(nb, nq),
        in_specs=[pl.BlockSpec((WIDTH, tq), lambda b, i: (0, b * nq + i)),
                  pl.BlockSpec((None, tkp, WIDTH), lambda b, i: (b, 0, 0),
                               pipeline_mode=pl.Buffered(1)),
                  pl.BlockSpec((nkt, N_HEADS * V_ROWS, ATT_TILE), lambda b, i: (b, 0, 0),
                               pipeline_mode=pl.Buffered(1)),
                  pl.BlockSpec((None, tkp, tq), lambda b, i: (b, 0, i)),
                  _const_spec(bias.shape)],
        out_specs=pl.BlockSpec((tq, WIDTH), lambda b, i: (b * nq + i, 0)),
        scratch_shapes=_attn_scratch(tq),
        compiler_params=_cparams(("parallel", "parallel")),
        name="dsa_attn",
    )(qT, k, vT, maskT, bias)


_BAND_TILES = BAND_CHUNKS * CHUNK // ATT_TILE + 1


def _band_kernel(qT_ref, *refs, tq, off):
    k_refs = refs[:_BAND_TILES]
    v_refs = refs[_BAND_TILES:2 * _BAND_TILES]
    bias_ref, o_ref, qsel_ref, s_ref, m_ref, acc_ref = refs[2 * _BAND_TILES:]
    _load_qsel(qT_ref, qsel_ref, tq)
    _init_state(m_ref, acc_ref)
    k_slab = lambda w: (lambda g: k_refs[w][:, g * 128:(g + 1) * 128])
    for h in range(N_HEADS):
        _scores(k_slab(0), qsel_ref, s_ref, h)
    for w in range(_BAND_TILES):
        entry = jnp.where(pl.program_id(1) + off + w >= 0, w, _BAND_TILES)
        _heads_tile(lambda h: v_refs[w][0, h * V_ROWS:(h + 1) * V_ROWS, :],
                    None if w == _BAND_TILES - 1 else k_slab(w + 1),
                    qsel_ref, s_ref, m_ref, acc_ref, lambda h: (bias_ref[entry, h],))
    _store_out(o_ref, acc_ref)


def _band_attn(qT, k, vT, bias, *, nb, tq, off):
    nkt = k.shape[1] // ATT_TILE
    nq = qT.shape[1] // nb // tq
    tile_of = lambda i, w: jnp.maximum(i + off + w, 0)
    k_specs = [pl.BlockSpec((None, ATT_TILE, WIDTH), lambda b, i, w=w: (b, tile_of(i, w), 0))
               for w in range(_BAND_TILES)]
    v_specs = [pl.BlockSpec((1, N_HEADS * V_ROWS, ATT_TILE),
                            lambda b, i, w=w: (b * nkt + tile_of(i, w), 0, 0))
               for w in range(_BAND_TILES)]
    return pl.pallas_call(
        functools.partial(_band_kernel, tq=tq, off=off),
        out_shape=jax.ShapeDtypeStruct((nb * nq * tq, WIDTH), BF16),
        grid=(nb, nq),
        in_specs=[pl.BlockSpec((WIDTH, tq), lambda b, i: (0, b * nq + i))] + k_specs + v_specs
                 + [_const_spec(bias.shape)],
        out_specs=pl.BlockSpec((tq, WIDTH), lambda b, i: (b * nq + i, 0)),
        scratch_shapes=_attn_scratch(tq),
        compiler_params=_cparams(("parallel", "parallel")),
        name="band_attn",
    )(qT, *([k] * _BAND_TILES), *([vT] * _BAND_TILES), bias)


def _mix_ln_kernel(h_ref, oa_ref, ob_ref, wg_ref, wba_ref, wbb_ref, wo_ref, g_ref, b_ref, o_ref,
                   *, alpha):
    h = h_ref[...]
    hb = h.astype(BF16)
    d = h.shape[1]
    ga = jnp.dot(hb, wg_ref[:, :d], preferred_element_type=F32)
    gb = jnp.dot(hb, wg_ref[:, d:], preferred_element_type=F32)
    ya = jnp.dot(oa_ref[...], wba_ref[...], preferred_element_type=F32)
    yb = jnp.dot(ob_ref[...], wbb_ref[...], preferred_element_type=F32)
    gated = jax.nn.sigmoid(ga) * ya + jax.nn.sigmoid(gb) * yb
    mix = jnp.dot(gated.astype(BF16), wo_ref[...], preferred_element_type=F32)
    o_ref[...] = _layer_norm_rows(alpha * h + mix, g_ref[...], b_ref[...])


def _mix_ln(h, oa, ob, wg, wba, wbb, wo, g, b, alpha):
    n, d = h.shape
    tm = TOKEN_TILE
    row = lambda width: pl.BlockSpec((tm, width), lambda i: (i, 0))
    return pl.pallas_call(
        functools.partial(_mix_ln_kernel, alpha=alpha),
        out_shape=jax.ShapeDtypeStruct((n, d), F32),
        grid=(n // tm,),
        in_specs=[row(d), row(WIDTH), row(WIDTH), _const_spec(wg.shape), _const_spec(wba.shape),
                  _const_spec(wbb.shape), _const_spec(wo.shape), _const_spec((1, d)),
                  _const_spec((1, d))],
        out_specs=row(d),
        compiler_params=_cparams(("parallel",)),
        name="mix_ln",
    )(h, oa, ob, wg, wba, wbb, wo, g.reshape(1, d), b.reshape(1, d))


def _t5_bucket(rel):
    half = T5_BUCKETS // 2
    exact = half // 2
    n = jnp.abs(rel)
    log_ratio = jnp.log(jnp.maximum(n, 1).astype(F32) / exact) / math.log(T5_MAX_DIST / exact)
    large = jnp.minimum(exact + (log_ratio * (half - exact)).astype(I32), half - 1)
    return (rel > 0).astype(I32) * half + jnp.where(n < exact, n, large)


def _rel_line(n_tiles, first, tq):
    period = ATT_TILE + tq
    y = jnp.arange(period, dtype=I32)[None, :]
    c = (jnp.arange(n_tiles, dtype=I32)[:, None] + first) * ATT_TILE
    return jnp.where(y < tq, c - y, c + period - y)


def _toeplitz(v, tq):
    period = v.shape[-1]
    flat = jnp.tile(v, (1,) * (v.ndim - 1) + (ATT_TILE,))[..., :ATT_TILE * (period - 1)]
    return flat.reshape(v.shape[:-1] + (ATT_TILE, period - 1))[..., :tq]


def _dsa_bias_table(t5_bias, tq):
    assert T5_MAX_DIST <= ATT_TILE + 1
    far = t5_bias[:, _t5_bucket(jnp.int32(-(ATT_TILE + 1)))]
    line = t5_bias[:, _t5_bucket(_rel_line(2, -1, tq))] - far[:, None, None]
    return (_toeplitz(jnp.moveaxis(line, 0, 1), tq) * LOG2E).astype(BF16)


def _band_bias_table(rel_bias, tq):
    first = 1 - _BAND_TILES
    ridx = jnp.clip(_rel_line(_BAND_TILES, first, tq), -REL_BACK, CHUNK - 1) + REL_BACK
    bias = _toeplitz(jnp.moveaxis(rel_bias[:, ridx], 0, 1), tq)
    w = jnp.arange(_BAND_TILES, dtype=I32)[:, None, None]
    j = jnp.arange(ATT_TILE, dtype=I32)[None, :, None]
    i = jnp.arange(tq, dtype=I32)[None, None, :]
    dchunk = ((w + first) * ATT_TILE + j) // CHUNK - i // CHUNK
    ok = (dchunk <= 0) & (dchunk >= -BAND_CHUNKS)
    table = jnp.where(ok[:, None], bias * LOG2E, MASKED)
    return jnp.concatenate([table, jnp.full_like(table[:1], MASKED)], axis=0).astype(BF16)


def _regroup_w_in(w_in):
    d, n_in = w_in.shape
    c_ki = 3 * WIDTH + N_HEADS_IDX * IDX_DIM
    c_wi = c_ki + IDX_DIM
    c_qb = c_wi + N_HEADS_IDX
    c_gate = c_qb + 3 * WIDTH
    rows = 128

    def regroup(w_ref, proj_ref, gate_ref):
        w = w_ref[...]
        ki = w[:, c_ki:c_wi]
        wi_pad = jnp.concatenate([w[:, c_wi:c_qb], jnp.zeros((rows, 128 - N_HEADS_IDX), F32)], axis=1)
        proj_ref[...] = jnp.concatenate([w[:, :c_ki], ki, ki, wi_pad, w[:, c_qb:c_gate]],
                                        axis=1).astype(BF16)
        gate_ref[...] = w[:, c_gate:].astype(BF16)

    return pl.pallas_call(
        regroup,
        out_shape=(jax.ShapeDtypeStruct((d, _C_END), BF16),
                   jax.ShapeDtypeStruct((d, n_in - c_gate), BF16)),
        grid=(d // rows,),
        in_specs=[pl.BlockSpec((rows, n_in), lambda i: (i, 0))],
        out_specs=(pl.BlockSpec((rows, _C_END), lambda i: (i, 0)),
                   pl.BlockSpec((rows, n_in - c_gate), lambda i: (i, 0))),
        compiler_params=_cparams(("parallel",)),
        name="regroup_w_in",
    )(w_in)


def _kv_prep_kernel(ck_ref, nk_ref, cv_ref, nv_ref, k_ref, vT_ref, *, dec):
    n_cache_tiles = ck_ref.shape[0] // ATT_TILE

    def fresh(n_ref):
        rows = lax.broadcasted_iota(I32, n_ref.shape, 0)
        new = jnp.where(rows < dec, n_ref[...], 0.0)
        return jnp.concatenate([new, jnp.zeros((ATT_TILE - new.shape[0], WIDTH), F32)], axis=0)

    for t in range(n_cache_tiles + 1):
        rows = slice(t * ATT_TILE, (t + 1) * ATT_TILE)
        k_tile = ck_ref[rows, :] if t < n_cache_tiles else fresh(nk_ref)
        v_tile = cv_ref[rows, :] if t < n_cache_tiles else fresh(nv_ref)
        k_ref[rows, :] = k_tile.astype(BF16)
        vT_ref[t] = _value_slabs(v_tile.T).astype(BF16)


def _kv_prep(cache_k, new_k, cache_v, new_v, dec):
    nb, lc, _ = cache_k.shape
    tq = new_k.shape[1]
    nct = lc // ATT_TILE
    cache_spec = pl.BlockSpec((None, lc, WIDTH), lambda b: (b, 0, 0))
    new_spec = pl.BlockSpec((None, tq, WIDTH), lambda b: (b, 0, 0))
    return pl.pallas_call(
        functools.partial(_kv_prep_kernel, dec=dec),
        out_shape=(jax.ShapeDtypeStruct((nb, lc + ATT_TILE, WIDTH), BF16),
                   jax.ShapeDtypeStruct((nb * (nct + 1), N_HEADS * V_ROWS, ATT_TILE), BF16)),
        grid=(nb,),
        in_specs=[cache_spec, new_spec, cache_spec, new_spec],
        out_specs=(pl.BlockSpec((None, lc + ATT_TILE, WIDTH), lambda b: (b, 0, 0)),
                   pl.BlockSpec((nct + 1, N_HEADS * V_ROWS, ATT_TILE), lambda b: (b, 0, 0))),
        compiler_params=_cparams(("parallel",)),
        name="kv_prep",
    )(cache_k, new_k, cache_v, new_v)


def _pad_keys(x, tkp):
    return jnp.pad(x, ((0, 0), (0, tkp - x.shape[1]), (0, 0)))


def kernel(x_prompt, x_sample, cache_k_a, cache_v_a, cache_kidx_a, cache_k_b, cache_v_b, t5_bias,
           ln1_g, ln1_b, ffn1_wi, ffn1_wo, ln2_g, ln2_b, w_in, rel_bias_b, w_branch_a, w_branch_b,
           w_out, ln3_g, ln3_b, ffn2_wi, ffn2_wo):
    depth = ln1_g.shape[0]
    alpha = (2.0 * depth) ** 0.25
    nbp, seq, d = x_prompt.shape
    nbs, dec, _ = x_sample.shape
    past = cache_k_a.shape[2]
    band = cache_k_b.shape[2]
    keep = min(BAND_CHUNKS * CHUNK, seq)
    tq_p, tq_s = ATT_TILE, 128
    assert seq % TOKEN_TILE == 0 and keep == TOKEN_TILE and dec <= tq_s and past % ATT_TILE == 0
    assert band == BAND_CHUNKS * CHUNK and (nbs * tq_s) % TOKEN_TILE == 0

    yp = x_prompt.reshape(nbp * seq, d)
    ys = jnp.pad(x_sample, ((0, 0), (0, tq_s - dec), (0, 0))).reshape(nbs * tq_s, d)
    dsa_bias_p = _dsa_bias_table(t5_bias, tq_p)
    dsa_bias_s = dsa_bias_p[..., :tq_s]
    n_sel_p = min(TOPK_MAX, seq // 4)
    n_sel_s = min(TOPK_MAX, (past + dec) // 4)
    tk_s = past + ATT_TILE
    assert band + ATT_TILE == _BAND_TILES * ATT_TILE

    st_p, st_s = [], []
    for l in range(depth):
        w_proj, w_gate = _regroup_w_in(w_in[l])
        wi1, wo1 = ffn1_wi[l].astype(BF16), ffn1_wo[l].astype(BF16)
        wi2, wo2 = ffn2_wi[l].astype(BF16), ffn2_wo[l].astype(BF16)
        wba, wbb, wo = (w_branch_a[l].astype(BF16), w_branch_b[l].astype(BF16),
                        w_out[l].astype(BF16))
        band_bias_p = _band_bias_table(rel_bias_b[l], tq_p)
        band_bias_s = band_bias_p[..., :tq_s]

        h = _ffn_ln(yp, wi1, wo1, ln1_g[l], ln1_b[l], alpha)
        (qaT, ka, kab, va, vaT, qiT, ki, kk, wiT, qbT, kbb, vbT, kbl, vbl) = _in_proj(
            h, w_proj, seq // TOKEN_TILE)
        maskT = _idx_mask(qiT, wiT, kk.reshape(nbp, seq, 128), nb=nbp, tq=tq_p, qpos0=0,
                          n_valid=seq, n_sel=n_sel_p)
        oa = _dsa_attn(qaT, kab.reshape(nbp, seq, WIDTH), vaT, maskT, dsa_bias_p,
                       nb=nbp, tq=tq_p, qpos0=0)
        ob = _band_attn(qbT, kbb.reshape(nbp, seq, WIDTH), vbT, band_bias_p,
                        nb=nbp, tq=tq_p, off=1 - _BAND_TILES)
        h2 = _mix_ln(h, oa, ob, w_gate, wba, wbb, wo, ln2_g[l], ln2_b[l], alpha)
        yp = _ffn_ln(h2, wi2, wo2, ln3_g[l], ln3_b[l], alpha)
        st_p.append((ka.reshape(nbp, seq, N_HEADS, HEAD_DIM), va.reshape(nbp, seq, N_HEADS, HEAD_DIM),
                     ki.reshape(nbp, seq, IDX_DIM), kbl.reshape(nbp, keep, N_HEADS, HEAD_DIM),
                     vbl.reshape(nbp, keep, N_HEADS, HEAD_DIM)))

        h = _ffn_ln(ys, wi1, wo1, ln1_g[l], ln1_b[l], alpha)
        (qaT, ka, _, va, _, qiT, ki, _, wiT, qbT, _, _, kbl, vbl) = _in_proj(h, w_proj, 1)
        per_stream = lambda a: a.reshape(nbs, tq_s, -1)
        new = lambda a: per_stream(a)[:, :dec]
        ka_n, va_n, ki_n, kb_n, vb_n = new(ka), new(va), new(ki), new(kbl), new(vbl)
        cache_kb = cache_k_b[l].reshape(nbs, band, WIDTH)
        cache_vb = cache_v_b[l].reshape(nbs, band, WIDTH)
        ki_all = jnp.concatenate([cache_kidx_a[l], ki_n], axis=1)
        kb_all = jnp.concatenate([cache_kb, kb_n], axis=1)
        vb_all = jnp.concatenate([cache_vb, vb_n], axis=1)
        kk_s = _pad_keys(jnp.concatenate([ki_all, ki_all], axis=2).astype(BF16), tk_s)
        maskT = _idx_mask(qiT, wiT, kk_s, nb=nbs, tq=tq_s, qpos0=past, n_valid=past + dec,
                          n_sel=n_sel_s)
        k_s, vT_s = _kv_prep(cache_k_a[l].reshape(nbs, past, WIDTH), per_stream(ka),
                             cache_v_a[l].reshape(nbs, past, WIDTH), per_stream(va), dec)
        oa = _dsa_attn(qaT, k_s, vT_s, maskT, dsa_bias_s, nb=nbs, tq=tq_s, qpos0=past)
        kb_s, vbT_s = _kv_prep(cache_kb, per_stream(kbl), cache_vb, per_stream(vbl), dec)
        ob = _band_attn(qbT, kb_s, vbT_s, band_bias_s, nb=nbs, tq=tq_s, off=0)
        h2 = _mix_ln(h, oa, ob, w_gate, wba, wbb, wo, ln2_g[l], ln2_b[l], alpha)
        ys = _ffn_ln(h2, wi2, wo2, ln3_g[l], ln3_b[l], alpha)
        heads = lambda a: a.reshape(nbs, -1, N_HEADS, HEAD_DIM)
        st_s.append((heads(ka_n), heads(va_n), ki_n, heads(kb_all[:, -band:]),
                     heads(vb_all[:, -band:])))

    y_prompt = yp.reshape(nbp, seq, d)
    y_sample = ys.reshape(nbs, tq_s, d)[:, :dec]
    stack = lambda sts, t: jnp.stack([s[t] for s in sts])
    return (y_prompt, y_sample,
            stack(st_p, 0), stack(st_p, 1), stack(st_p, 2), stack(st_p, 3), stack(st_p, 4),
            stack(st_s, 0), stack(st_s, 1), stack(st_s, 2), stack(st_s, 3), stack(st_s, 4))
```

```python
import functools
import math

import jax
import jax.numpy as jnp
from jax import lax
from jax.experimental import pallas as pl
from jax.experimental.pallas import tpu as pltpu

F32 = jnp.float32
BF16 = jnp.bfloat16
I32 = jnp.int32

CHUNK = 64
CHUNK_SHIFT = 6
HEAD_DIM = 64
N_HEADS = 8
WIDTH = N_HEADS * HEAD_DIM
N_HEADS_IDX = 4
IDX_DIM = 64
TOPK_MAX = 256
IDX_SCALE = (IDX_DIM * N_HEADS_IDX) ** -0.5
BAND_CHUNKS = 8
REL_BACK = 128
T5_BUCKETS = 32
T5_MAX_DIST = 128
LN_EPS = 1e-5
MASKED = -2.0 ** 100
LOG2E = math.log2(math.e)
V_ROWS = HEAD_DIM + 16
INT_MIN = -2 ** 31

ATT_TILE = 256
TOKEN_TILE = 512
FF_CHUNK = 256
VMEM_LIMIT_MB = 56


def _cparams(sem):
    return pltpu.CompilerParams(dimension_semantics=sem,
                                vmem_limit_bytes=VMEM_LIMIT_MB * 1024 * 1024)


def _const_spec(shape):
    nd = len(shape)
    return pl.BlockSpec(shape, lambda *_: (0,) * nd, pipeline_mode=pl.Buffered(1))


def _tree_sum(xs):
    return _tree_reduce(lambda a, b: a + b, xs)


def _tree_reduce(fn, xs):
    xs = list(xs)
    while len(xs) > 1:
        nxt = [fn(xs[a], xs[a + 1]) for a in range(0, len(xs) - 1, 2)]
        if len(xs) % 2:
            nxt.append(xs[-1])
        xs = nxt
    return xs[0]


def _layer_norm_rows(y, g, b):
    mu = jnp.mean(y, axis=-1, keepdims=True)
    d = y - mu
    var = jnp.mean(d * d, axis=-1, keepdims=True)
    return d * lax.rsqrt(var + LN_EPS) * g + b


def _ffn_ln_kernel(x_ref, wi_ref, wo_ref, g_ref, b_ref, o_ref, *, alpha, dff):
    x = x_ref[...]
    xb = x.astype(BF16)
    acc = jnp.zeros(x.shape, F32)
    for c in range(dff // FF_CHUNK):
        lo = c * FF_CHUNK
        a = jnp.dot(xb, wi_ref[:, lo:lo + FF_CHUNK], preferred_element_type=F32)
        u = jnp.dot(xb, wi_ref[:, dff + lo:dff + lo + FF_CHUNK], preferred_element_type=F32)
        hm = (a * jax.nn.sigmoid(a)) * u
        acc = acc + jnp.dot(hm.astype(BF16), wo_ref[lo:lo + FF_CHUNK, :],
                            preferred_element_type=F32)
    y = alpha * x + 0.5 * acc
    o_ref[...] = _layer_norm_rows(y, g_ref[...], b_ref[...])


def _ffn_ln(x, wi, wo, g, b, alpha):
    n, d = x.shape
    dff = wo.shape[0]
    tm = TOKEN_TILE
    return pl.pallas_call(
        functools.partial(_ffn_ln_kernel, alpha=alpha, dff=dff),
        out_shape=jax.ShapeDtypeStruct((n, d), F32),
        grid=(n // tm,),
        in_specs=[pl.BlockSpec((tm, d), lambda i: (i, 0)),
                  _const_spec((d, 2 * dff)), _const_spec((dff, d)),
                  _const_spec((1, d)), _const_spec((1, d))],
        out_specs=pl.BlockSpec((tm, d), lambda i: (i, 0)),
        compiler_params=_cparams(("parallel",)),
        name="ffn_ln",
    )(x, wi, wo, g.reshape(1, d), b.reshape(1, d))


_C_QA, _C_KA, _C_VA, _C_QI, _C_KI, _C_WI, _C_QB, _C_KB, _C_VB, _C_END = (
    0, 512, 1024, 1536, 1792, 1920, 2048, 2560, 3072, 3584)


def _value_slabs(zt):
    n = zt.shape[1]
    ones_slab = jnp.where(lax.broadcasted_iota(I32, (V_ROWS - HEAD_DIM, n), 0) == 0, 1.0, 0.0)
    return jnp.concatenate([piece for h in range(N_HEADS)
                            for piece in (zt[h * HEAD_DIM:(h + 1) * HEAD_DIM, :], ones_slab)], axis=0)


def _inproj_kernel(h_ref, w_ref, qaT_ref, ka_ref, kab_ref, va_ref, vaT_ref, qiT_ref, ki_ref,
                   kk_ref, wiT_ref, qbT_ref, kbb_ref, vbT_ref, kbl_ref, vbl_ref, *, per):
    hb = h_ref[...].astype(BF16)

    def proj(c0, c1):
        return jnp.dot(hb, w_ref[:, c0:c1], preferred_element_type=F32)

    def put_tiles(ref, zt):
        aug = _value_slabs(zt)
        for c in range(ref.shape[0]):
            ref[c] = aug[:, c * ATT_TILE:(c + 1) * ATT_TILE].astype(ref.dtype)

    scale = HEAD_DIM ** -0.5 * LOG2E
    qaT_ref[...] = (proj(_C_QA, _C_KA) * scale).T.astype(BF16)
    ka = proj(_C_KA, _C_VA)
    ka_ref[...] = ka
    kab_ref[...] = ka.astype(BF16)
    va = proj(_C_VA, _C_QI)
    va_ref[...] = va
    put_tiles(vaT_ref, va.T)
    qiT_ref[...] = proj(_C_QI, _C_KI).T.astype(BF16)
    kk = proj(_C_KI, _C_WI)
    ki_ref[...] = kk[:, :IDX_DIM]
    kk_ref[...] = kk.astype(BF16)
    wiT_ref[...] = proj(_C_WI, _C_QB).T[:8, :]
    qbT_ref[...] = (proj(_C_QB, _C_KB) * scale).T.astype(BF16)
    kb = proj(_C_KB, _C_VB)
    kbb_ref[...] = kb.astype(BF16)
    vb = proj(_C_VB, _C_END)
    put_tiles(vbT_ref, vb.T)

    @pl.when(pl.program_id(0) % per == per - 1)
    def _():
        kbl_ref[...] = kb
        vbl_ref[...] = vb


def _in_proj(h, w, per):
    n, d = h.shape
    tm = TOKEN_TILE
    nt = n // tm
    sub = tm // ATT_TILE
    row = lambda width: pl.BlockSpec((tm, width), lambda i: (i, 0))
    col = lambda rows: pl.BlockSpec((rows, tm), lambda i: (0, i))
    til = pl.BlockSpec((sub, N_HEADS * V_ROWS, ATT_TILE), lambda i: (i, 0, 0))
    last = pl.BlockSpec((tm, WIDTH), lambda i: (i // per, 0))
    sds = jax.ShapeDtypeStruct
    out_shape = (
        sds((WIDTH, n), BF16),
        sds((n, WIDTH), F32), sds((n, WIDTH), BF16),
        sds((n, WIDTH), F32), sds((n // ATT_TILE, N_HEADS * V_ROWS, ATT_TILE), BF16),
        sds((N_HEADS_IDX * IDX_DIM, n), BF16),
        sds((n, IDX_DIM), F32), sds((n, 128), BF16),
        sds((8, n), F32),
        sds((WIDTH, n), BF16),
        sds((n, WIDTH), BF16),
        sds((n // ATT_TILE, N_HEADS * V_ROWS, ATT_TILE), BF16),
        sds((nt // per * tm, WIDTH), F32), sds((nt // per * tm, WIDTH), F32),
    )
    out_specs = (col(WIDTH), row(WIDTH), row(WIDTH), row(WIDTH), til,
                 col(N_HEADS_IDX * IDX_DIM), row(IDX_DIM), row(128), col(8),
                 col(WIDTH), row(WIDTH), til, last, last)
    return pl.pallas_call(
        functools.partial(_inproj_kernel, per=per),
        out_shape=out_shape,
        grid=(nt,),
        in_specs=[pl.BlockSpec((tm, d), lambda i: (i, 0)), _const_spec((d, _C_END))],
        out_specs=out_specs,
        compiler_params=_cparams(("arbitrary",)),
        name="in_proj",
    )(h, w)


def _bit_planes(words):
    a = list(reversed(words))
    j, m = 16, 0x0000FFFF
    while j:
        k = 0
        while k < 32:
            t = (a[k] ^ lax.shift_right_logical(a[k + j], j)) & m
            a[k] = a[k] ^ t
            a[k + j] = a[k + j] ^ lax.shift_left(t, j)
            k = (k + j + 1) & ~j
        j >>= 1
        m = (m ^ (m << j)) & 0xFFFFFFFF
    return a


def _idx_mask_kernel(qiT_ref, wiT_ref, kk_ref, mask_ref, st_ref, p_ref, c_ref, d_ref, incl_ref,
                     *, tq, qpos0, n_valid, n_sel):
    tk = ATT_TILE
    assert tk == 32 * 8
    nkt_total = st_ref.shape[0] // tk

    @pl.when((pl.program_id(0) == 0) & (pl.program_id(1) == 0))
    def _():
        p_ref[...] = jnp.zeros(p_ref.shape, I32)

    q0 = qpos0 + pl.program_id(1) * tq
    n_keys = jnp.minimum(q0 + tq, n_valid)
    nt = jnp.minimum(lax.shift_right_logical(n_keys + tk - 1, int(math.log2(tk))), nkt_total)

    lane = lax.broadcasted_iota(I32, (8, tq), 1)
    q_chunk = lax.shift_right_logical(q0 + lane, CHUNK_SHIFT)
    n_adm = jnp.minimum((q_chunk + 1) * CHUNK, n_valid)
    k_row = jnp.minimum(n_adm, n_sel)

    zeros = jnp.zeros((128 - IDX_DIM, tq), BF16)
    qh = [jnp.concatenate([qiT_ref[h * IDX_DIM:(h + 1) * IDX_DIM, :], zeros], axis=0)
          for h in range(N_HEADS_IDX)]
    scale_exact = math.frexp(IDX_SCALE)[0] == 0.5
    wh = [wiT_ref[h:h + 1, :] * IDX_SCALE if scale_exact else wiT_ref[h:h + 1, :]
          for h in range(N_HEADS_IDX)]

    def dots(j, h):
        r0 = pl.multiple_of(j * tk, tk)
        d_ref[h] = jnp.dot(kk_ref[pl.ds(r0, tk), :], qh[h], preferred_element_type=F32)

    def score_tile(j, last):
        r0 = pl.multiple_of(j * tk, tk)
        sc = None
        for h in range(N_HEADS_IDX):
            d = d_ref[h]
            if not last:
                dots(j + 1, h)
            term = wh[h] * jnp.maximum(d, 0.0)
            sc = term if sc is None else sc + term
        if not scale_exact:
            sc = sc * IDX_SCALE
        bits = lax.bitcast_convert_type(sc, I32)
        sign = lax.shift_right_arithmetic(bits, 31)
        key = (bits ^ (sign & 0x7FFFFFFF)) - sign
        if last:
            kpos = r0 + lax.broadcasted_iota(I32, (tk, tq), 0)
            key = jnp.where(kpos < n_adm[0:1, :], key, INT_MIN)
        st_ref[pl.ds(r0, tk), :] = key
        planes = _bit_planes([key[v * 8:(v + 1) * 8, :] for v in range(32)])
        p_ref[0, j] = planes[0] ^ -1
        for i in range(1, 32):
            p_ref[i, j] = planes[i]

    for h in range(N_HEADS_IDX):
        dots(0, h)

    def score_pair(u, carry):
        score_tile(2 * u, False)
        score_tile(2 * u + 1, False)
        return carry

    lax.fori_loop(0, lax.shift_right_logical(nt - 1, 1), score_pair, 0)

    @pl.when(((nt - 1) & 1) == 1)
    def _():
        score_tile(nt - 2, False)

    score_tile(nt - 1, True)

    group = incl_ref.shape[0]
    n_groups = nt if group == 1 else lax.shift_right_logical(nt + 3, 2)
    for g in range(group - 1):
        @pl.when(nt + g < n_groups * group)
        def _(g=g):
            r0 = pl.multiple_of((nt + g) * tk, tk)
            st_ref[pl.ds(r0, tk), :] = jnp.full((tk, tq), INT_MIN, I32)

    limit = n_adm - (nt - 1) * tk
    n_low = jnp.clip(lax.shift_right_arithmetic(limit - lax.broadcasted_iota(I32, (8, tq), 0) + 7, 3),
                     0, 32)
    last_word = jnp.where(n_low >= 32, -1, lax.shift_left(1, jnp.minimum(n_low, 31)) - 1)
    for jj in range(nkt_total):
        c_ref[jj] = jnp.where(jj < nt - 1, -1, jnp.where(jj == nt - 1, last_word, 0))

    def radix_pass(i, carry):
        want, t = carry
        n_set = _tree_sum([lax.population_count(c_ref[jj] & p_ref[i, jj]) for jj in range(nkt_total)])
        n_set = jnp.broadcast_to(jnp.sum(n_set, axis=0, keepdims=True), (8, tq))
        keep_set = n_set >= want
        flip = jnp.where(keep_set, 0, -1)
        for jj in range(nkt_total):
            c_ref[jj] = c_ref[jj] & (p_ref[i, jj] ^ flip)
        return (jnp.where(keep_set, want, want - n_set),
                t | jnp.where(keep_set, lax.shift_left(jnp.int32(1), 31 - i), 0))

    want, t_bits = lax.fori_loop(0, 32, radix_pass, (k_row, jnp.zeros((8, tq), I32)))
    t_row = (t_bits ^ INT_MIN)[0:1, :]
    ties_wanted = want[0:1, :].astype(F32)

    tri = jnp.where(lax.broadcasted_iota(I32, (tk, tk), 0) >= lax.broadcasted_iota(I32, (tk, tk), 1),
                    1.0, 0.0).astype(BF16)

    def tie_prefix(j, g):
        r0 = pl.multiple_of(j * tk, tk)
        eq = jnp.where(st_ref[pl.ds(r0, tk), :] == t_row, 1.0, 0.0)
        incl_ref[g] = jnp.dot(tri, eq.astype(BF16), preferred_element_type=F32)

    def final_group(u, seen, last):
        for g in range(group):
            j = u * group + g
            r0 = pl.multiple_of(j * tk, tk)
            incl = incl_ref[g]
            if not last:
                tie_prefix(j + group, g)
            rank = seen + incl
            bar = t_row + jnp.where(rank <= ties_wanted, 0, 1)
            sel = jnp.where(st_ref[pl.ds(r0, tk), :] >= bar, 1, 0)
            mask_ref[pl.ds(r0, tk), :] = sel.astype(jnp.int8)
            seen = seen + incl[tk - 1:tk, :]
        return seen

    for g in range(group):
        tie_prefix(g, g)
    seen = lax.fori_loop(0, n_groups - 1, lambda u, seen: final_group(u, seen, False),
                         jnp.zeros((1, tq), F32))
    final_group(n_groups - 1, seen, True)

    def zero_tile(j, carry):
        r0 = pl.multiple_of(j * tk, tk)
        mask_ref[pl.ds(r0, tk), :] = jnp.zeros((tk, tq), jnp.int8)
        return carry

    lax.fori_loop(n_groups * group, nkt_total, zero_tile, 0)


def _idx_mask(qiT, wiT, kk, *, nb, tq, qpos0, n_valid, n_sel):
    tkp = kk.shape[1]
    nq = qiT.shape[1] // nb // tq
    return pl.pallas_call(
        functools.partial(_idx_mask_kernel, tq=tq, qpos0=qpos0, n_valid=n_valid, n_sel=n_sel),
        out_shape=jax.ShapeDtypeStruct((nb, tkp, nq * tq), jnp.int8),
        grid=(nb, nq),
        in_specs=[pl.BlockSpec((N_HEADS_IDX * IDX_DIM, tq), lambda b, i: (0, b * nq + i)),
                  pl.BlockSpec((8, tq), lambda b, i: (0, b * nq + i)),
                  pl.BlockSpec((None, tkp, 128), lambda b, i: (b, 0, 0))],
        out_specs=pl.BlockSpec((None, tkp, tq), lambda b, i: (b, 0, i)),
        scratch_shapes=[pltpu.VMEM((tkp, tq), I32),
                        pltpu.VMEM((32, tkp // ATT_TILE, 8, tq), I32),
                        pltpu.VMEM((tkp // ATT_TILE, 8, tq), I32),
                        pltpu.VMEM((N_HEADS_IDX, ATT_TILE, tq), F32),
                        pltpu.VMEM((4 if (tkp // ATT_TILE) % 4 == 0 else 1, ATT_TILE, tq), F32)],
        compiler_params=_cparams(("arbitrary", "arbitrary")),
        name="idx_mask",
    )(qiT, wiT, kk)


def _load_qsel(qT_ref, qsel_ref, tq):
    zeros = jnp.zeros((HEAD_DIM, tq), BF16)
    for h in range(N_HEADS):
        blk = qT_ref[h * HEAD_DIM:(h + 1) * HEAD_DIM, :]
        qsel_ref[h] = jnp.concatenate([blk, zeros] if h % 2 == 0 else [zeros, blk], axis=0)


def _init_state(m_ref, acc_ref):
    m_ref[...] = jnp.full(m_ref.shape, MASKED, F32)
    acc_ref[...] = jnp.zeros(acc_ref.shape, F32)


def _scores(k_slab, qsel_ref, s_ref, h):
    s_ref[h] = jnp.dot(k_slab(h // 2), qsel_ref[h], preferred_element_type=F32)


def _heads_tile(v_rows, next_k_slab, qsel_ref, s_ref, m_ref, acc_ref, addend):
    tq = s_ref.shape[2]
    slabs = s_ref.shape[1] // 16
    for h in range(N_HEADS):
        s = s_ref[h].astype(BF16)
        for a in addend(h):
            s = s + a
        if next_k_slab is not None:
            _scores(next_k_slab, qsel_ref, s_ref, h)
        m_old = m_ref[h:h + 1, :]
        top = _tree_reduce(jnp.maximum, [s[r * 16:(r + 1) * 16, :] for r in range(slabs)])
        m_new = jnp.maximum(m_old, jnp.max(top.astype(F32), axis=0, keepdims=True))
        alpha = jnp.exp2(m_old - m_new)
        m16 = jnp.broadcast_to(m_new, (16, tq)).astype(BF16)
        p = jnp.concatenate([jnp.exp2(s[r * 16:(r + 1) * 16, :] - m16) for r in range(slabs)], axis=0)
        m_ref[h:h + 1, :] = m_new
        pv = jnp.dot(v_rows(h), p, preferred_element_type=F32)
        rows = slice(h * V_ROWS, (h + 1) * V_ROWS)
        acc_ref[rows, :] = alpha * acc_ref[rows, :] + pv


def _attn_scratch(tq):
    return [pltpu.VMEM((N_HEADS, 128, tq), BF16), pltpu.VMEM((N_HEADS, ATT_TILE, tq), F32),
            pltpu.VMEM((N_HEADS, tq), F32), pltpu.VMEM((N_HEADS * V_ROWS, tq), F32)]


def _store_out(o_ref, acc_ref):
    outs = [acc_ref[h * V_ROWS:h * V_ROWS + HEAD_DIM, :]
            / acc_ref[h * V_ROWS + HEAD_DIM:h * V_ROWS + HEAD_DIM + 1, :] for h in range(N_HEADS)]
    o_ref[...] = jnp.concatenate(outs, axis=0).T.astype(o_ref.dtype)


def _dsa_kernel(qT_ref, k_ref, vT_ref, mask_ref, bias_ref, o_ref, qsel_ref, s_ref, m_ref, acc_ref,
                *, tq, qpos0):
    tk = ATT_TILE
    jd = lax.shift_right_logical(qpos0 + pl.program_id(1) * tq, int(math.log2(tk)))
    _load_qsel(qT_ref, qsel_ref, tq)
    _init_state(m_ref, acc_ref)

    def k_slab(j):
        r0 = pl.multiple_of(j * tk, tk)
        return lambda g: k_ref[pl.ds(r0, tk), g * 128:(g + 1) * 128]

    def tile(j, near, last=False):
        r0 = pl.multiple_of(j * tk, tk)
        unselected = jnp.where(mask_ref[pl.ds(r0, tk), :].astype(I32) != 0, 0.0,
                               MASKED).astype(BF16)
        addend = ((lambda h: (unselected,)) if near is None
                  else (lambda h: (unselected, bias_ref[near, h])))
        _heads_tile(lambda h: vT_ref[j, h * V_ROWS:(h + 1) * V_ROWS, :],
                    None if last else k_slab(j + 1),
                    qsel_ref, s_ref, m_ref, acc_ref, addend)

    for h in range(N_HEADS):
        _scores(k_slab(0), qsel_ref, s_ref, h)

    n_far = jnp.maximum(jd - 1, 0)

    def far_run(first, count):
        for c in range(count):
            tile(first + c, None)

    def far_octet(u, carry):
        far_run(8 * u, 8)
        return carry

    n_octets = lax.shift_right_logical(n_far, 3)
    lax.fori_loop(0, n_octets, far_octet, 0)
    done = 8 * n_octets
    for count in (4, 2, 1):
        @pl.when((n_far & count) == count)
        def _(count=count, done=done):
            far_run(done, count)
        done = done + (n_far & count)

    @pl.when(jd >= 1)
    def _():
        tile(jd - 1, 0)

    tile(jd, 1, last=True)
    _store_out(o_ref, acc_ref)


def _dsa_attn(qT, k, vT, maskT, bias, *, nb, tq, qpos0):
    tkp = k.shape[1]
    nkt = tkp // ATT_TILE
    nq = qT.shape[1] // nb // tq
    return pl.pallas_call(
        functools.partial(_dsa_kernel, tq=tq, qpos0=qpos0),
        out_shape=jax.ShapeDtypeStruct((nb * nq * tq, WIDTH), BF16),
        grid=(nb, nq),
        in_specs=[pl.BlockSpec((WIDTH, tq), lambda b, i: (0, b * nq + i)),
                  pl.BlockSpec((None, tkp, WIDTH), lambda b, i: (b, 0, 0),
                               pipeline_mode=pl.Buffered(1)),
                  pl.BlockSpec((nkt, N_HEADS * V_ROWS, ATT_TILE), lambda b, i: (b, 0, 0),
                               pipeline_mode=pl.Buffered(1)),
                  pl.BlockSpec((None, tkp, tq), lambda b, i: (b, 0, i)),
                  _const_spec(bias.shape)],
        out_specs=pl.BlockSpec((tq, WIDTH), lambda b, i: (b * nq + i, 0)),
        scratch_shapes=_attn_scratch(tq),
        compiler_params=_cparams(("parallel", "parallel")),
        name="dsa_attn",
    )(qT, k, vT, maskT, bias)


_BAND_TILES = BAND_CHUNKS * CHUNK // ATT_TILE + 1


def _band_kernel(qT_ref, *refs, tq, off):
    k_refs = refs[:_BAND_TILES]
    v_refs = refs[_BAND_TILES:2 * _BAND_TILES]
    bias_ref, o_ref, qsel_ref, s_ref, m_ref, acc_ref = refs[2 * _BAND_TILES:]
    _load_qsel(qT_ref, qsel_ref, tq)
    _init_state(m_ref, acc_ref)
    k_slab = lambda w: (lambda g: k_refs[w][:, g * 128:(g + 1) * 128])
    for h in range(N_HEADS):
        _scores(k_slab(0), qsel_ref, s_ref, h)
    for w in range(_BAND_TILES):
        entry = jnp.where(pl.program_id(1) + off + w >= 0, w, _BAND_TILES)
        _heads_tile(lambda h: v_refs[w][0, h * V_ROWS:(h + 1) * V_ROWS, :],
                    None if w == _BAND_TILES - 1 else k_slab(w + 1),
                    qsel_ref, s_ref, m_ref, acc_ref, lambda h: (bias_ref[entry, h],))
    _store_out(o_ref, acc_ref)


def _band_attn(qT, k, vT, bias, *, nb, tq, off):
    nkt = k.shape[1] // ATT_TILE
    nq = qT.shape[1] // nb // tq
    tile_of = lambda i, w: jnp.maximum(i + off + w, 0)
    k_specs = [pl.BlockSpec((None, ATT_TILE, WIDTH), lambda b, i, w=w: (b, tile_of(i, w), 0))
               for w in range(_BAND_TILES)]
    v_specs = [pl.BlockSpec((1, N_HEADS * V_ROWS, ATT_TILE),
                            lambda b, i, w=w: (b * nkt + tile_of(i, w), 0, 0))
               for w in range(_BAND_TILES)]
    return pl.pallas_call(
        functools.partial(_band_kernel, tq=tq, off=off),
        out_shape=jax.ShapeDtypeStruct((nb * nq * tq, WIDTH), BF16),
        grid=(nb, nq),
        in_specs=[pl.BlockSpec((WIDTH, tq), lambda b, i: (0, b * nq + i))] + k_specs + v_specs
                 + [_const_spec(bias.shape)],
        out_specs=pl.BlockSpec((tq, WIDTH), lambda b, i: (b * nq + i, 0)),
        scratch_shapes=_attn_scratch(tq),
        compiler_params=_cparams(("parallel", "parallel")),
        name="band_attn",
    )(qT, *([k] * _BAND_TILES), *([vT] * _BAND_TILES), bias)


def _mix_ln_kernel(h_ref, oa_ref, ob_ref, wg_ref, wba_ref, wbb_ref, wo_ref, g_ref, b_ref, o_ref,
                   *, alpha):
    h = h_ref[...]
    hb = h.astype(BF16)
    d = h.shape[1]
    ga = jnp.dot(hb, wg_ref[:, :d], preferred_element_type=F32)
    gb = jnp.dot(hb, wg_ref[:, d:], preferred_element_type=F32)
    ya = jnp.dot(oa_ref[...], wba_ref[...], preferred_element_type=F32)
    yb = jnp.dot(ob_ref[...], wbb_ref[...], preferred_element_type=F32)
    gated = jax.nn.sigmoid(ga) * ya + jax.nn.sigmoid(gb) * yb
    mix = jnp.dot(gated.astype(BF16), wo_ref[...], preferred_element_type=F32)
    o_ref[...] = _layer_norm_rows(alpha * h + mix, g_ref[...], b_ref[...])


def _mix_ln(h, oa, ob, wg, wba, wbb, wo, g, b, alpha):
    n, d = h.shape
    tm = TOKEN_TILE
    row = lambda width: pl.BlockSpec((tm, width), lambda i: (i, 0))
    return pl.pallas_call(
        functools.partial(_mix_ln_kernel, alpha=alpha),
        out_shape=jax.ShapeDtypeStruct((n, d), F32),
        grid=(n // tm,),
        in_specs=[row(d), row(WIDTH), row(WIDTH), _const_spec(wg.shape), _const_spec(wba.shape),
                  _const_spec(wbb.shape), _const_spec(wo.shape), _const_spec((1, d)),
                  _const_spec((1, d))],
        out_specs=row(d),
        compiler_params=_cparams(("parallel",)),
        name="mix_ln",
    )(h, oa, ob, wg, wba, wbb, wo, g.reshape(1, d), b.reshape(1, d))


def _t5_bucket(rel):
    half = T5_BUCKETS // 2
    exact = half // 2
    n = jnp.abs(rel)
    log_ratio = jnp.log(jnp.maximum(n, 1).astype(F32) / exact) / math.log(T5_MAX_DIST / exact)
    large = jnp.minimum(exact + (log_ratio * (half - exact)).astype(I32), half - 1)
    return (rel > 0).astype(I32) * half + jnp.where(n < exact, n, large)


def _rel_line(n_tiles, first, tq):
    period = ATT_TILE + tq
    y = jnp.arange(period, dtype=I32)[None, :]
    c = (jnp.arange(n_tiles, dtype=I32)[:, None] + first) * ATT_TILE
    return jnp.where(y < tq, c - y, c + period - y)


def _toeplitz(v, tq):
    period = v.shape[-1]
    flat = jnp.tile(v, (1,) * (v.ndim - 1) + (ATT_TILE,))[..., :ATT_TILE * (period - 1)]
    return flat.reshape(v.shape[:-1] + (ATT_TILE, period - 1))[..., :tq]


def _dsa_bias_table(t5_bias, tq):
    assert T5_MAX_DIST <= ATT_TILE + 1
    far = t5_bias[:, _t5_bucket(jnp.int32(-(ATT_TILE + 1)))]
    line = t5_bias[:, _t5_bucket(_rel_line(2, -1, tq))] - far[:, None, None]
    return (_toeplitz(jnp.moveaxis(line, 0, 1), tq) * LOG2E).astype(BF16)


def _band_bias_table(rel_bias, tq):
    first = 1 - _BAND_TILES
    ridx = jnp.clip(_rel_line(_BAND_TILES, first, tq), -REL_BACK, CHUNK - 1) + REL_BACK
    bias = _toeplitz(jnp.moveaxis(rel_bias[:, ridx], 0, 1), tq)
    w = jnp.arange(_BAND_TILES, dtype=I32)[:, None, None]
    j = jnp.arange(ATT_TILE, dtype=I32)[None, :, None]
    i = jnp.arange(tq, dtype=I32)[None, None, :]
    dchunk = ((w + first) * ATT_TILE + j) // CHUNK - i // CHUNK
    ok = (dchunk <= 0) & (dchunk >= -BAND_CHUNKS)
    table = jnp.where(ok[:, None], bias * LOG2E, MASKED)
    return jnp.concatenate([table, jnp.full_like(table[:1], MASKED)], axis=0).astype(BF16)


def _regroup_w_in(w_in):
    d, n_in = w_in.shape
    c_ki = 3 * WIDTH + N_HEADS_IDX * IDX_DIM
    c_wi = c_ki + IDX_DIM
    c_qb = c_wi + N_HEADS_IDX
    c_gate = c_qb + 3 * WIDTH
    rows = 128

    def regroup(w_ref, proj_ref, gate_ref):
        w = w_ref[...]
        ki = w[:, c_ki:c_wi]
        wi_pad = jnp.concatenate([w[:, c_wi:c_qb], jnp.zeros((rows, 128 - N_HEADS_IDX), F32)], axis=1)
        proj_ref[...] = jnp.concatenate([w[:, :c_ki], ki, ki, wi_pad, w[:, c_qb:c_gate]],
                                        axis=1).astype(BF16)
        gate_ref[...] = w[:, c_gate:].astype(BF16)

    return pl.pallas_call(
        regroup,
        out_shape=(jax.ShapeDtypeStruct((d, _C_END), BF16),
                   jax.ShapeDtypeStruct((d, n_in - c_gate), BF16)),
        grid=(d // rows,),
        in_specs=[pl.BlockSpec((rows, n_in), lambda i: (i, 0))],
        out_specs=(pl.BlockSpec((rows, _C_END), lambda i: (i, 0)),
                   pl.BlockSpec((rows, n_in - c_gate), lambda i: (i, 0))),
        compiler_params=_cparams(("parallel",)),
        name="regroup_w_in",
    )(w_in)


def _kv_prep_kernel(ck_ref, nk_ref, cv_ref, nv_ref, k_ref, vT_ref, *, dec):
    n_cache_tiles = ck_ref.shape[0] // ATT_TILE

    def fresh(n_ref):
        rows = lax.broadcasted_iota(I32, n_ref.shape, 0)
        new = jnp.where(rows < dec, n_ref[...], 0.0)
        return jnp.concatenate([new, jnp.zeros((ATT_TILE - new.shape[0], WIDTH), F32)], axis=0)

    for t in range(n_cache_tiles + 1):
        rows = slice(t * ATT_TILE, (t + 1) * ATT_TILE)
        k_tile = ck_ref[rows, :] if t < n_cache_tiles else fresh(nk_ref)
        v_tile = cv_ref[rows, :] if t < n_cache_tiles else fresh(nv_ref)
        k_ref[rows, :] = k_tile.astype(BF16)
        vT_ref[t] = _value_slabs(v_tile.T).astype(BF16)


def _kv_prep(cache_k, new_k, cache_v, new_v, dec):
    nb, lc, _ = cache_k.shape
    tq = new_k.shape[1]
    nct = lc // ATT_TILE
    cache_spec = pl.BlockSpec((None, lc, WIDTH), lambda b: (b, 0, 0))
    new_spec = pl.BlockSpec((None, tq, WIDTH), lambda b: (b, 0, 0))
    return pl.pallas_call(
        functools.partial(_kv_prep_kernel, dec=dec),
        out_shape=(jax.ShapeDtypeStruct((nb, lc + ATT_TILE, WIDTH), BF16),
                   jax.ShapeDtypeStruct((nb * (nct + 1), N_HEADS * V_ROWS, ATT_TILE), BF16)),
        grid=(nb,),
        in_specs=[cache_spec, new_spec, cache_spec, new_spec],
        out_specs=(pl.BlockSpec((None, lc + ATT_TILE, WIDTH), lambda b: (b, 0, 0)),
                   pl.BlockSpec((nct + 1, N_HEADS * V_ROWS, ATT_TILE), lambda b: (b, 0, 0))),
        compiler_params=_cparams(("parallel",)),
        name="kv_prep",
    )(cache_k, new_k, cache_v, new_v)


def _pad_keys(x, tkp):
    return jnp.pad(x, ((0, 0), (0, tkp - x.shape[1]), (0, 0)))


def kernel(x_prompt, x_sample, cache_k_a, cache_v_a, cache_kidx_a, cache_k_b, cache_v_b, t5_bias,
           ln1_g, ln1_b, ffn1_wi, ffn1_wo, ln2_g, ln2_b, w_in, rel_bias_b, w_branch_a, w_branch_b,
           w_out, ln3_g, ln3_b, ffn2_wi, ffn2_wo):
    depth = ln1_g.shape[0]
    alpha = (2.0 * depth) ** 0.25
    nbp, seq, d = x_prompt.shape
    nbs, dec, _ = x_sample.shape
    past = cache_k_a.shape[2]
    band = cache_k_b.shape[2]
    keep = min(BAND_CHUNKS * CHUNK, seq)
    tq_p, tq_s = ATT_TILE, 128
    assert seq % TOKEN_TILE == 0 and keep == TOKEN_TILE and dec <= tq_s and past % ATT_TILE == 0
    assert band == BAND_CHUNKS * CHUNK and (nbs * tq_s) % TOKEN_TILE == 0

    yp = x_prompt.reshape(nbp * seq, d)
    ys = jnp.pad(x_sample, ((0, 0), (0, tq_s - dec), (0, 0))).reshape(nbs * tq_s, d)
    dsa_bias_p = _dsa_bias_table(t5_bias, tq_p)
    dsa_bias_s = dsa_bias_p[..., :tq_s]
    n_sel_p = min(TOPK_MAX, seq // 4)
    n_sel_s = min(TOPK_MAX, (past + dec) // 4)
    tk_s = past + ATT_TILE
    assert band + ATT_TILE == _BAND_TILES * ATT_TILE

    st_p, st_s = [], []
    for l in range(depth):
        w_proj, w_gate = _regroup_w_in(w_in[l])
        wi1, wo1 = ffn1_wi[l].astype(BF16), ffn1_wo[l].astype(BF16)
        wi2, wo2 = ffn2_wi[l].astype(BF16), ffn2_wo[l].astype(BF16)
        wba, wbb, wo = (w_branch_a[l].astype(BF16), w_branch_b[l].astype(BF16),
                        w_out[l].astype(BF16))
        band_bias_p = _band_bias_table(rel_bias_b[l], tq_p)
        band_bias_s = band_bias_p[..., :tq_s]

        h = _ffn_ln(yp, wi1, wo1, ln1_g[l], ln1_b[l], alpha)
        (qaT, ka, kab, va, vaT, qiT, ki, kk, wiT, qbT, kbb, vbT, kbl, vbl) = _in_proj(
            h, w_proj, seq // TOKEN_TILE)
        maskT = _idx_mask(qiT, wiT, kk.reshape(nbp, seq, 128), nb=nbp, tq=tq_p, qpos0=0,
                          n_valid=seq, n_sel=n_sel_p)
        oa = _dsa_attn(qaT, kab.reshape(nbp, seq, WIDTH), vaT, maskT, dsa_bias_p,
                       nb=nbp, tq=tq_p, qpos0=0)
        ob = _band_attn(qbT, kbb.reshape(nbp, seq, WIDTH), vbT, band_bias_p,
                        nb=nbp, tq=tq_p, off=1 - _BAND_TILES)
        h2 = _mix_ln(h, oa, ob, w_gate, wba, wbb, wo, ln2_g[l], ln2_b[l], alpha)
        yp = _ffn_ln(h2, wi2, wo2, ln3_g[l], ln3_b[l], alpha)
        st_p.append((ka.reshape(nbp, seq, N_HEADS, HEAD_DIM), va.reshape(nbp, seq, N_HEADS, HEAD_DIM),
                     ki.reshape(nbp, seq, IDX_DIM), kbl.reshape(nbp, keep, N_HEADS, HEAD_DIM),
                     vbl.reshape(nbp, keep, N_HEADS, HEAD_DIM)))

        h = _ffn_ln(ys, wi1, wo1, ln1_g[l], ln1_b[l], alpha)
        (qaT, ka, _, va, _, qiT, ki, _, wiT, qbT, _, _, kbl, vbl) = _in_proj(h, w_proj, 1)
        per_stream = lambda a: a.reshape(nbs, tq_s, -1)
        new = lambda a: per_stream(a)[:, :dec]
        ka_n, va_n, ki_n, kb_n, vb_n = new(ka), new(va), new(ki), new(kbl), new(vbl)
        cache_kb = cache_k_b[l].reshape(nbs, band, WIDTH)
        cache_vb = cache_v_b[l].reshape(nbs, band, WIDTH)
        ki_all = jnp.concatenate([cache_kidx_a[l], ki_n], axis=1)
        kb_all = jnp.concatenate([cache_kb, kb_n], axis=1)
        vb_all = jnp.concatenate([cache_vb, vb_n], axis=1)
        kk_s = _pad_keys(jnp.concatenate([ki_all, ki_all], axis=2).astype(BF16), tk_s)
        maskT = _idx_mask(qiT, wiT, kk_s, nb=nbs, tq=tq_s, qpos0=past, n_valid=past + dec,
                          n_sel=n_sel_s)
        k_s, vT_s = _kv_prep(cache_k_a[l].reshape(nbs, past, WIDTH), per_stream(ka),
                             cache_v_a[l].reshape(nbs, past, WIDTH), per_stream(va), dec)
        oa = _dsa_attn(qaT, k_s, vT_s, maskT, dsa_bias_s, nb=nbs, tq=tq_s, qpos0=past)
        kb_s, vbT_s = _kv_prep(cache_kb, per_stream(kbl), cache_vb, per_stream(vbl), dec)
        ob = _band_attn(qbT, kb_s, vbT_s, band_bias_s, nb=nbs, tq=tq_s, off=0)
        h2 = _mix_ln(h, oa, ob, w_gate, wba, wbb, wo, ln2_g[l], ln2_b[l], alpha)
        ys = _ffn_ln(h2, wi2, wo2, ln3_g[l], ln3_b[l], alpha)
        heads = lambda a: a.reshape(nbs, -1, N_HEADS, HEAD_DIM)
        st_s.append((heads(ka_n), heads(va_n), ki_n, heads(kb_all[:, -band:]),
                     heads(vb_all[:, -band:])))

    y_prompt = yp.reshape(nbp, seq, d)
    y_sample = ys.reshape(nbs, tq_s, d)[:, :dec]
    stack = lambda sts, t: jnp.stack([s[t] for s in sts])
    return (y_prompt, y_sample,
            stack(st_p, 0), stack(st_p, 1), stack(st_p, 2), stack(st_p, 3), stack(st_p, 4),
            stack(st_s, 0), stack(st_s, 1), stack(st_s, 2), stack(st_s, 3), stack(st_s, 4))
```

```python
import functools
import math

import jax
import jax.numpy as jnp
from jax import lax
from jax.experimental import pallas as pl
from jax.experimental.pallas import tpu as pltpu

F32 = jnp.float32
BF16 = jnp.bfloat16
I32 = jnp.int32

CHUNK = 64
CHUNK_SHIFT = 6
HEAD_DIM = 64
N_HEADS = 8
WIDTH = N_HEADS * HEAD_DIM
N_HEADS_IDX = 4
IDX_DIM = 64
TOPK_MAX = 256
IDX_SCALE = (IDX_DIM * N_HEADS_IDX) ** -0.5
BAND_CHUNKS = 8
REL_BACK = 128
T5_BUCKETS = 32
T5_MAX_DIST = 128
LN_EPS = 1e-5
MASKED = -2.0 ** 100
LOG2E = math.log2(math.e)
V_ROWS = HEAD_DIM + 16
INT_MIN = -2 ** 31

ATT_TILE = 256
TOKEN_TILE = 512
FF_CHUNK = 256
VMEM_LIMIT_MB = 56


def _cparams(sem):
    return pltpu.CompilerParams(dimension_semantics=sem,
                                vmem_limit_bytes=VMEM_LIMIT_MB * 1024 * 1024)


def _const_spec(shape):
    nd = len(shape)
    return pl.BlockSpec(shape, lambda *_: (0,) * nd, pipeline_mode=pl.Buffered(1))


def _tree_sum(xs):
    return _tree_reduce(lambda a, b: a + b, xs)


def _tree_reduce(fn, xs):
    xs = list(xs)
    while len(xs) > 1:
        nxt = [fn(xs[a], xs[a + 1]) for a in range(0, len(xs) - 1, 2)]
        if len(xs) % 2:
            nxt.append(xs[-1])
        xs = nxt
    return xs[0]


def _layer_norm_rows(y, g, b):
    mu = jnp.mean(y, axis=-1, keepdims=True)
    d = y - mu
    var = jnp.mean(d * d, axis=-1, keepdims=True)
    return d * lax.rsqrt(var + LN_EPS) * g + b


def _ffn_ln_kernel(x_ref, wi_ref, wo_ref, g_ref, b_ref, o_ref, *, alpha, dff):
    x = x_ref[...]
    xb = x.astype(BF16)
    acc = jnp.zeros(x.shape, F32)
    for c in range(dff // FF_CHUNK):
        lo = c * FF_CHUNK
        a = jnp.dot(xb, wi_ref[:, lo:lo + FF_CHUNK], preferred_element_type=F32)
        u = jnp.dot(xb, wi_ref[:, dff + lo:dff + lo + FF_CHUNK], preferred_element_type=F32)
        hm = (a * jax.nn.sigmoid(a)) * u
        acc = acc + jnp.dot(hm.astype(BF16), wo_ref[lo:lo + FF_CHUNK, :],
                            preferred_element_type=F32)
    y = alpha * x + 0.5 * acc
    o_ref[...] = _layer_norm_rows(y, g_ref[...], b_ref[...])


def _ffn_ln(x, wi, wo, g, b, alpha):
    n, d = x.shape
    dff = wo.shape[0]
    tm = TOKEN_TILE
    return pl.pallas_call(
        functools.partial(_ffn_ln_kernel, alpha=alpha, dff=dff),
        out_shape=jax.ShapeDtypeStruct((n, d), F32),
        grid=(n // tm,),
        in_specs=[pl.BlockSpec((tm, d), lambda i: (i, 0)),
                  _const_spec((d, 2 * dff)), _const_spec((dff, d)),
                  _const_spec((1, d)), _const_spec((1, d))],
        out_specs=pl.BlockSpec((tm, d), lambda i: (i, 0)),
        compiler_params=_cparams(("parallel",)),
        name="ffn_ln",
    )(x, wi, wo, g.reshape(1, d), b.reshape(1, d))


_C_QA, _C_KA, _C_VA, _C_QI, _C_KI, _C_WI, _C_QB, _C_KB, _C_VB, _C_END = (
    0, 512, 1024, 1536, 1792, 1920, 2048, 2560, 3072, 3584)


def _value_slabs(zt):
    n = zt.shape[1]
    ones_slab = jnp.where(lax.broadcasted_iota(I32, (V_ROWS - HEAD_DIM, n), 0) == 0, 1.0, 0.0)
    return jnp.concatenate([piece for h in range(N_HEADS)
                            for piece in (zt[h * HEAD_DIM:(h + 1) * HEAD_DIM, :], ones_slab)], axis=0)


def _inproj_kernel(h_ref, w_ref, qaT_ref, ka_ref, kab_ref, va_ref, vaT_ref, qiT_ref, ki_ref,
                   kk_ref, wiT_ref, qbT_ref, kbb_ref, vbT_ref, kbl_ref, vbl_ref, *, per):
    hb = h_ref[...].astype(BF16)

    def proj(c0, c1):
        return jnp.dot(hb, w_ref[:, c0:c1], preferred_element_type=F32)

    def put_tiles(ref, zt):
        aug = _value_slabs(zt)
        for c in range(ref.shape[0]):
            ref[c] = aug[:, c * ATT_TILE:(c + 1) * ATT_TILE].astype(ref.dtype)

    scale = HEAD_DIM ** -0.5 * LOG2E
    qaT_ref[...] = (proj(_C_QA, _C_KA) * scale).T.astype(BF16)
    ka = proj(_C_KA, _C_VA)
    ka_ref[...] = ka
    kab_ref[...] = ka.astype(BF16)
    va = proj(_C_VA, _C_QI)
    va_ref[...] = va
    put_tiles(vaT_ref, va.T)
    qiT_ref[...] = proj(_C_QI, _C_KI).T.astype(BF16)
    kk = proj(_C_KI, _C_WI)
    ki_ref[...] = kk[:, :IDX_DIM]
    kk_ref[...] = kk.astype(BF16)
    wiT_ref[...] = proj(_C_WI, _C_QB).T[:8, :]
    qbT_ref[...] = (proj(_C_QB, _C_KB) * scale).T.astype(BF16)
    kb = proj(_C_KB, _C_VB)
    kbb_ref[...] = kb.astype(BF16)
    vb = proj(_C_VB, _C_END)
    put_tiles(vbT_ref, vb.T)

    @pl.when(pl.program_id(0) % per == per - 1)
    def _():
        kbl_ref[...] = kb
        vbl_ref[...] = vb


def _in_proj(h, w, per):
    n, d = h.shape
    tm = TOKEN_TILE
    nt = n // tm
    sub = tm // ATT_TILE
    row = lambda width: pl.BlockSpec((tm, width), lambda i: (i, 0))
    col = lambda rows: pl.BlockSpec((rows, tm), lambda i: (0, i))
    til = pl.BlockSpec((sub, N_HEADS * V_ROWS, ATT_TILE), lambda i: (i, 0, 0))
    last = pl.BlockSpec((tm, WIDTH), lambda i: (i // per, 0))
    sds = jax.ShapeDtypeStruct
    out_shape = (
        sds((WIDTH, n), BF16),
        sds((n, WIDTH), F32), sds((n, WIDTH), BF16),
        sds((n, WIDTH), F32), sds((n // ATT_TILE, N_HEADS * V_ROWS, ATT_TILE), BF16),
        sds((N_HEADS_IDX * IDX_DIM, n), BF16),
        sds((n, IDX_DIM), F32), sds((n, 128), BF16),
        sds((8, n), F32),
        sds((WIDTH, n), BF16),
        sds((n, WIDTH), BF16),
        sds((n // ATT_TILE, N_HEADS * V_ROWS, ATT_TILE), BF16),
        sds((nt // per * tm, WIDTH), F32), sds((nt // per * tm, WIDTH), F32),
    )
    out_specs = (col(WIDTH), row(WIDTH), row(WIDTH), row(WIDTH), til,
                 col(N_HEADS_IDX * IDX_DIM), row(IDX_DIM), row(128), col(8),
                 col(WIDTH), row(WIDTH), til, last, last)
    return pl.pallas_call(
        functools.partial(_inproj_kernel, per=per),
        out_shape=out_shape,
        grid=(nt,),
        in_specs=[pl.BlockSpec((tm, d), lambda i: (i, 0)), _const_spec((d, _C_END))],
        out_specs=out_specs,
        compiler_params=_cparams(("arbitrary",)),
        name="in_proj",
    )(h, w)


def _bit_planes(words):
    a = list(reversed(words))
    j, m = 16, 0x0000FFFF
    while j:
        k = 0
        while k < 32:
            t = (a[k] ^ lax.shift_right_logical(a[k + j], j)) & m
            a[k] = a[k] ^ t
            a[k + j] = a[k + j] ^ lax.shift_left(t, j)
            k = (k + j + 1) & ~j
        j >>= 1
        m = (m ^ (m << j)) & 0xFFFFFFFF
    return a


def _idx_mask_kernel(qiT_ref, wiT_ref, kk_ref, mask_ref, st_ref, p_ref, c_ref, d_ref, incl_ref,
                     *, tq, qpos0, n_valid, n_sel):
    tk = ATT_TILE
    assert tk == 32 * 8
    nkt_total = st_ref.shape[0] // tk

    @pl.when((pl.program_id(0) == 0) & (pl.program_id(1) == 0))
    def _():
        p_ref[...] = jnp.zeros(p_ref.shape, I32)

    q0 = qpos0 + pl.program_id(1) * tq
    n_keys = jnp.minimum(q0 + tq, n_valid)
    nt = jnp.minimum(lax.shift_right_logical(n_keys + tk - 1, int(math.log2(tk))), nkt_total)

    lane = lax.broadcasted_iota(I32, (8, tq), 1)
    q_chunk = lax.shift_right_logical(q0 + lane, CHUNK_SHIFT)
    n_adm = jnp.minimum((q_chunk + 1) * CHUNK, n_valid)
    k_row = jnp.minimum(n_adm, n_sel)

    zeros = jnp.zeros((128 - IDX_DIM, tq), BF16)
    qh = [jnp.concatenate([qiT_ref[h * IDX_DIM:(h + 1) * IDX_DIM, :], zeros], axis=0)
          for h in range(N_HEADS_IDX)]
    scale_exact = math.frexp(IDX_SCALE)[0] == 0.5
    wh = [wiT_ref[h:h + 1, :] * IDX_SCALE if scale_exact else wiT_ref[h:h + 1, :]
          for h in range(N_HEADS_IDX)]

    def dots(j, h):
        r0 = pl.multiple_of(j * tk, tk)
        d_ref[h] = jnp.dot(kk_ref[pl.ds(r0, tk), :], qh[h], preferred_element_type=F32)

    def score_tile(j, last):
        r0 = pl.multiple_of(j * tk, tk)
        sc = None
        for h in range(N_HEADS_IDX):
            d = d_ref[h]
            if not last:
                dots(j + 1, h)
            term = wh[h] * jnp.maximum(d, 0.0)
            sc = term if sc is None else sc + term
        if not scale_exact:
            sc = sc * IDX_SCALE
        bits = lax.bitcast_convert_type(sc, I32)
        sign = lax.shift_right_arithmetic(bits, 31)
        key = (bits ^ (sign & 0x7FFFFFFF)) - sign
        if last:
            kpos = r0 + lax.broadcasted_iota(I32, (tk, tq), 0)
            key = jnp.where(kpos < n_adm[0:1, :], key, INT_MIN)
        st_ref[pl.ds(r0, tk), :] = key
        planes = _bit_planes([key[v * 8:(v + 1) * 8, :] for v in range(32)])
        p_ref[0, j] = planes[0] ^ -1
        for i in range(1, 32):
            p_ref[i, j] = planes[i]

    for h in range(N_HEADS_IDX):
        dots(0, h)

    def score_run(first, count):
        for c in range(count):
            score_tile(first + c, False)

    def score_quad(u, carry):
        score_run(4 * u, 4)
        return carry

    n_open = nt - 1
    n_quads = lax.shift_right_logical(n_open, 2)
    lax.fori_loop(0, n_quads, score_quad, 0)
    done = 4 * n_quads
    for count in (2, 1):
        @pl.when((n_open & count) == count)
        def _(count=count, done=done):
            score_run(done, count)
        done = done + (n_open & count)

    score_tile(nt - 1, True)

    group = incl_ref.shape[0]
    n_groups = nt if group == 1 else lax.shift_right_logical(nt + 3, 2)
    for g in range(group - 1):
        @pl.when(nt + g < n_groups * group)
        def _(g=g):
            r0 = pl.multiple_of((nt + g) * tk, tk)
            st_ref[pl.ds(r0, tk), :] = jnp.full((tk, tq), INT_MIN, I32)

    limit = n_adm - (nt - 1) * tk
    n_low = jnp.clip(lax.shift_right_arithmetic(limit - lax.broadcasted_iota(I32, (8, tq), 0) + 7, 3),
                     0, 32)
    last_word = jnp.where(n_low >= 32, -1, lax.shift_left(1, jnp.minimum(n_low, 31)) - 1)
    for jj in range(nkt_total):
        c_ref[jj] = jnp.where(jj < nt - 1, -1, jnp.where(jj == nt - 1, last_word, 0))

    def radix_select(n_scan):
        def radix_pass(i, carry):
            want, t = carry
            n_set = _tree_sum([lax.population_count(c_ref[jj] & p_ref[i, jj]) for jj in range(n_scan)])
            n_set = jnp.broadcast_to(jnp.sum(n_set, axis=0, keepdims=True), (8, tq))
            keep_set = n_set >= want
            flip = jnp.where(keep_set, 0, -1)
            for jj in range(n_scan):
                c_ref[jj] = c_ref[jj] & (p_ref[i, jj] ^ flip)
            return (jnp.where(keep_set, want, want - n_set),
                    t | jnp.where(keep_set, lax.shift_left(jnp.int32(1), 31 - i), 0))

        return lax.fori_loop(0, 32, radix_pass, (k_row, jnp.zeros((8, tq), I32)))

    if nkt_total >= 16 and nkt_total % 2 == 0:
        want, t_bits = lax.cond(nt <= nkt_total // 2, lambda: radix_select(nkt_total // 2),
                                lambda: radix_select(nkt_total))
    else:
        want, t_bits = radix_select(nkt_total)
    t_row = (t_bits ^ INT_MIN)[0:1, :]
    ties_wanted = want[0:1, :].astype(F32)

    tri = jnp.where(lax.broadcasted_iota(I32, (tk, tk), 0) >= lax.broadcasted_iota(I32, (tk, tk), 1),
                    1.0, 0.0).astype(BF16)

    def tie_prefix(j, g):
        r0 = pl.multiple_of(j * tk, tk)
        eq = jnp.where(st_ref[pl.ds(r0, tk), :] == t_row, 1.0, 0.0)
        incl_ref[g] = jnp.dot(tri, eq.astype(BF16), preferred_element_type=F32)

    def final_group(u, seen, last):
        for g in range(group):
            j = u * group + g
            r0 = pl.multiple_of(j * tk, tk)
            incl = incl_ref[g]
            if not last:
                tie_prefix(j + group, g)
            rank = seen + incl
            bar = t_row + jnp.where(rank <= ties_wanted, 0, 1)
            sel = jnp.where(st_ref[pl.ds(r0, tk), :] >= bar, 1, 0)
            mask_ref[pl.ds(r0, tk), :] = sel.astype(jnp.int8)
            seen = seen + incl[tk - 1:tk, :]
        return seen

    for g in range(group):
        tie_prefix(g, g)
    seen = lax.fori_loop(0, n_groups - 1, lambda u, seen: final_group(u, seen, False),
                         jnp.zeros((1, tq), F32))
    final_group(n_groups - 1, seen, True)

    def zero_tile(j, carry):
        r0 = pl.multiple_of(j * tk, tk)
        mask_ref[pl.ds(r0, tk), :] = jnp.zeros((tk, tq), jnp.int8)
        return carry

    lax.fori_loop(n_groups * group, nkt_total, zero_tile, 0)


def _idx_mask(qiT, wiT, kk, *, nb, tq, qpos0, n_valid, n_sel):
    tkp = kk.shape[1]
    nq = qiT.shape[1] // nb // tq
    return pl.pallas_call(
        functools.partial(_idx_mask_kernel, tq=tq, qpos0=qpos0, n_valid=n_valid, n_sel=n_sel),
        out_shape=jax.ShapeDtypeStruct((nb, tkp, nq * tq), jnp.int8),
        grid=(nb, nq),
        in_specs=[pl.BlockSpec((N_HEADS_IDX * IDX_DIM, tq), lambda b, i: (0, b * nq + i)),
                  pl.BlockSpec((8, tq), lambda b, i: (0, b * nq + i)),
                  pl.BlockSpec((None, tkp, 128), lambda b, i: (b, 0, 0))],
        out_specs=pl.BlockSpec((None, tkp, tq), lambda b, i: (b, 0, i)),
        scratch_shapes=[pltpu.VMEM((tkp, tq), I32),
                        pltpu.VMEM((32, tkp // ATT_TILE, 8, tq), I32),
                        pltpu.VMEM((tkp // ATT_TILE, 8, tq), I32),
                        pltpu.VMEM((N_HEADS_IDX, ATT_TILE, tq), F32),
                        pltpu.VMEM((4 if (tkp // ATT_TILE) % 4 == 0 else 1, ATT_TILE, tq), F32)],
        compiler_params=_cparams(("arbitrary", "arbitrary")),
        name="idx_mask",
    )(qiT, wiT, kk)


def _load_qsel(qT_ref, qsel_ref, tq):
    zeros = jnp.zeros((HEAD_DIM, tq), BF16)
    for h in range(N_HEADS):
        blk = qT_ref[h * HEAD_DIM:(h + 1) * HEAD_DIM, :]
        qsel_ref[h] = jnp.concatenate([blk, zeros] if h % 2 == 0 else [zeros, blk], axis=0)


def _init_state(m_ref, acc_ref):
    m_ref[...] = jnp.full(m_ref.shape, MASKED, F32)
    acc_ref[...] = jnp.zeros(acc_ref.shape, F32)


def _scores(k_slab, qsel_ref, s_ref, h):
    s_ref[h] = jnp.dot(k_slab(h // 2), qsel_ref[h], preferred_element_type=F32)


def _heads_tile(v_rows, next_k_slab, qsel_ref, s_ref, m_ref, acc_ref, addend):
    tq = s_ref.shape[2]
    slabs = s_ref.shape[1] // 16
    for h in range(N_HEADS):
        s = s_ref[h].astype(BF16)
        for a in addend(h):
            s = s + a
        if next_k_slab is not None:
            _scores(next_k_slab, qsel_ref, s_ref, h)
        m_old = m_ref[h:h + 1, :]
        top = _tree_reduce(jnp.maximum, [s[r * 16:(r + 1) * 16, :] for r in range(slabs)])
        m_new = jnp.maximum(m_old, jnp.max(top.astype(F32), axis=0, keepdims=True))
        alpha = jnp.exp2(m_old - m_new)
        m16 = jnp.broadcast_to(m_new, (16, tq)).astype(BF16)
        p = jnp.concatenate([jnp.exp2(s[r * 16:(r + 1) * 16, :] - m16) for r in range(slabs)], axis=0)
        m_ref[h:h + 1, :] = m_new
        pv = jnp.dot(v_rows(h), p, preferred_element_type=F32)
        rows = slice(h * V_ROWS, (h + 1) * V_ROWS)
        acc_ref[rows, :] = alpha * acc_ref[rows, :] + pv


def _attn_scratch(tq):
    return [pltpu.VMEM((N_HEADS, 128, tq), BF16), pltpu.VMEM((N_HEADS, ATT_TILE, tq), F32),
            pltpu.VMEM((N_HEADS, tq), F32), pltpu.VMEM((N_HEADS * V_ROWS, tq), F32)]


def _store_out(o_ref, acc_ref):
    outs = [acc_ref[h * V_ROWS:h * V_ROWS + HEAD_DIM, :]
            / acc_ref[h * V_ROWS + HEAD_DIM:h * V_ROWS + HEAD_DIM + 1, :] for h in range(N_HEADS)]
    o_ref[...] = jnp.concatenate(outs, axis=0).T.astype(o_ref.dtype)


def _dsa_kernel(qT_ref, k_ref, vT_ref, mask_ref, bias_ref, o_ref, qsel_ref, s_ref, m_ref, acc_ref,
                *, tq, qpos0):
    tk = ATT_TILE
    jd = lax.shift_right_logical(qpos0 + pl.program_id(1) * tq, int(math.log2(tk)))
    _load_qsel(qT_ref, qsel_ref, tq)
    _init_state(m_ref, acc_ref)

    def k_slab(j):
        r0 = pl.multiple_of(j * tk, tk)
        return lambda g: k_ref[pl.ds(r0, tk), g * 128:(g + 1) * 128]

    def tile(j, near, last=False):
        r0 = pl.multiple_of(j * tk, tk)
        unselected = jnp.where(mask_ref[pl.ds(r0, tk), :].astype(I32) != 0, 0.0,
                               MASKED).astype(BF16)
        addend = ((lambda h: (unselected,)) if near is None
                  else (lambda h: (unselected, bias_ref[near, h])))
        _heads_tile(lambda h: vT_ref[j, h * V_ROWS:(h + 1) * V_ROWS, :],
                    None if last else k_slab(j + 1),
                    qsel_ref, s_ref, m_ref, acc_ref, addend)

    for h in range(N_HEADS):
        _scores(k_slab(0), qsel_ref, s_ref, h)

    n_far = jnp.maximum(jd - 1, 0)

    def far_run(first, count):
        for c in range(count):
            tile(first + c, None)

    def far_octet(u, carry):
        far_run(8 * u, 8)
        return carry

    n_octets = lax.shift_right_logical(n_far, 3)
    lax.fori_loop(0, n_octets, far_octet, 0)
    done = 8 * n_octets
    for count in (4, 2, 1):
        @pl.when((n_far & count) == count)
        def _(count=count, done=done):
            far_run(done, count)
        done = done + (n_far & count)

    @pl.when(jd >= 1)
    def _():
        tile(jd - 1, 0)

    tile(jd, 1, last=True)
    _store_out(o_ref, acc_ref)


def _dsa_attn(qT, k, vT, maskT, bias, *, nb, tq, qpos0):
    tkp = k.shape[1]
    nkt = tkp // ATT_TILE
    nq = qT.shape[1] // nb // tq
    return pl.pallas_call(
        functools.partial(_dsa_kernel, tq=tq, qpos0=qpos0),
        out_shape=jax.ShapeDtypeStruct((nb * nq * tq, WIDTH), BF16),
        grid=(nb, nq),
        in_specs=[pl.BlockSpec((WIDTH, tq), lambda b, i: (0, b * nq + i)),
                  pl.BlockSpec((None, tkp, WIDTH), lambda b, i: (b, 0, 0),
                               pipeline_mode=pl.Buffered(1)),
                  pl.BlockSpec((nkt, N_HEADS * V_ROWS, ATT_TILE), lambda b, i: (b, 0, 0),
                               pipeline_mode=pl.Buffered(1)),
                  pl.BlockSpec((None, tkp, tq), lambda b, i: (b, 0, i)),
                  _const_spec(bias.shape)],
        out_specs=pl.BlockSpec((tq, WIDTH), lambda b, i: (b * nq + i, 0)),
        scratch_shapes=_attn_scratch(tq),
        compiler_params=_cparams(("parallel", "parallel")),
        name="dsa_attn",
    )(qT, k, vT, maskT, bias)


_BAND_TILES = BAND_CHUNKS * CHUNK // ATT_TILE + 1


def _band_kernel(qT_ref, *refs, tq, off):
    k_refs = refs[:_BAND_TILES]
    v_refs = refs[_BAND_TILES:2 * _BAND_TILES]
    bias_ref, o_ref, qsel_ref, s_ref, m_ref, acc_ref = refs[2 * _BAND_TILES:]
    _load_qsel(qT_ref, qsel_ref, tq)
    _init_state(m_ref, acc_ref)
    k_slab = lambda w: (lambda g: k_refs[w][:, g * 128:(g + 1) * 128])
    for h in range(N_HEADS):
        _scores(k_slab(0), qsel_ref, s_ref, h)
    for w in range(_BAND_TILES):
        entry = jnp.where(pl.program_id(1) + off + w >= 0, w, _BAND_TILES)
        _heads_tile(lambda h: v_refs[w][0, h * V_ROWS:(h + 1) * V_ROWS, :],
                    None if w == _BAND_TILES - 1 else k_slab(w + 1),
                    qsel_ref, s_ref, m_ref, acc_ref, lambda h: (bias_ref[entry, h],))
    _store_out(o_ref, acc_ref)


def _band_attn(qT, k, vT, bias, *, nb, tq, off):
    nkt = k.shape[1] // ATT_TILE
    nq = qT.shape[1] // nb // tq
    tile_of = lambda i, w: jnp.maximum(i + off + w, 0)
    k_specs = [pl.BlockSpec((None, ATT_TILE, WIDTH), lambda b, i, w=w: (b, tile_of(i, w), 0))
               for w in range(_BAND_TILES)]
    v_specs = [pl.BlockSpec((1, N_HEADS * V_ROWS, ATT_TILE),
                            lambda b, i, w=w: (b * nkt + tile_of(i, w), 0, 0))
               for w in range(_BAND_TILES)]
    return pl.pallas_call(
        functools.partial(_band_kernel, tq=tq, off=off),
        out_shape=jax.ShapeDtypeStruct((nb * nq * tq, WIDTH), BF16),
        grid=(nb, nq),
        in_specs=[pl.BlockSpec((WIDTH, tq), lambda b, i: (0, b * nq + i))] + k_specs + v_specs
                 + [_const_spec(bias.shape)],
        out_specs=pl.BlockSpec((tq, WIDTH), lambda b, i: (b * nq + i, 0)),
        scratch_shapes=_attn_scratch(tq),
        compiler_params=_cparams(("parallel", "parallel")),
        name="band_attn",
    )(qT, *([k] * _BAND_TILES), *([vT] * _BAND_TILES), bias)


def _mix_ln_kernel(h_ref, oa_ref, ob_ref, wg_ref, wba_ref, wbb_ref, wo_ref, g_ref, b_ref, o_ref,
                   *, alpha):
    h = h_ref[...]
    hb = h.astype(BF16)
    d = h.shape[1]
    ga = jnp.dot(hb, wg_ref[:, :d], preferred_element_type=F32)
    gb = jnp.dot(hb, wg_ref[:, d:], preferred_element_type=F32)
    ya = jnp.dot(oa_ref[...], wba_ref[...], preferred_element_type=F32)
    yb = jnp.dot(ob_ref[...], wbb_ref[...], preferred_element_type=F32)
    gated = jax.nn.sigmoid(ga) * ya + jax.nn.sigmoid(gb) * yb
    mix = jnp.dot(gated.astype(BF16), wo_ref[...], preferred_element_type=F32)
    o_ref[...] = _layer_norm_rows(alpha * h + mix, g_ref[...], b_ref[...])


def _mix_ln(h, oa, ob, wg, wba, wbb, wo, g, b, alpha):
    n, d = h.shape
    tm = TOKEN_TILE
    row = lambda width: pl.BlockSpec((tm, width), lambda i: (i, 0))
    return pl.pallas_call(
        functools.partial(_mix_ln_kernel, alpha=alpha),
        out_shape=jax.ShapeDtypeStruct((n, d), F32),
        grid=(n // tm,),
        in_specs=[row(d), row(WIDTH), row(WIDTH), _const_spec(wg.shape), _const_spec(wba.shape),
                  _const_spec(wbb.shape), _const_spec(wo.shape), _const_spec((1, d)),
                  _const_spec((1, d))],
        out_specs=row(d),
        compiler_params=_cparams(("parallel",)),
        name="mix_ln",
    )(h, oa, ob, wg, wba, wbb, wo, g.reshape(1, d), b.reshape(1, d))


def _t5_bucket(rel):
    half = T5_BUCKETS // 2
    exact = half // 2
    n = jnp.abs(rel)
    log_ratio = jnp.log(jnp.maximum(n, 1).astype(F32) / exact) / math.log(T5_MAX_DIST / exact)
    large = jnp.minimum(exact + (log_ratio * (half - exact)).astype(I32), half - 1)
    return (rel > 0).astype(I32) * half + jnp.where(n < exact, n, large)


def _rel_line(n_tiles, first, tq):
    period = ATT_TILE + tq
    y = jnp.arange(period, dtype=I32)[None, :]
    c = (jnp.arange(n_tiles, dtype=I32)[:, None] + first) * ATT_TILE
    return jnp.where(y < tq, c - y, c + period - y)


def _toeplitz(v, tq):
    period = v.shape[-1]
    flat = jnp.tile(v, (1,) * (v.ndim - 1) + (ATT_TILE,))[..., :ATT_TILE * (period - 1)]
    return flat.reshape(v.shape[:-1] + (ATT_TILE, period - 1))[..., :tq]


def _dsa_bias_table(t5_bias, tq):
    assert T5_MAX_DIST <= ATT_TILE + 1
    far = t5_bias[:, _t5_bucket(jnp.int32(-(ATT_TILE + 1)))]
    line = t5_bias[:, _t5_bucket(_rel_line(2, -1, tq))] - far[:, None, None]
    return (_toeplitz(jnp.moveaxis(line, 0, 1), tq) * LOG2E).astype(BF16)


def _band_bias_table(rel_bias, tq):
    first = 1 - _BAND_TILES
    ridx = jnp.clip(_rel_line(_BAND_TILES, first, tq), -REL_BACK, CHUNK - 1) + REL_BACK
    bias = _toeplitz(jnp.moveaxis(rel_bias[:, ridx], 0, 1), tq)
    w = jnp.arange(_BAND_TILES, dtype=I32)[:, None, None]
    j = jnp.arange(ATT_TILE, dtype=I32)[None, :, None]
    i = jnp.arange(tq, dtype=I32)[None, None, :]
    dchunk = ((w + first) * ATT_TILE + j) // CHUNK - i // CHUNK
    ok = (dchunk <= 0) & (dchunk >= -BAND_CHUNKS)
    table = jnp.where(ok[:, None], bias * LOG2E, MASKED)
    return jnp.concatenate([table, jnp.full_like(table[:1], MASKED)], axis=0).astype(BF16)


def _regroup_w_in(w_in):
    d, n_in = w_in.shape
    c_ki = 3 * WIDTH + N_HEADS_IDX * IDX_DIM
    c_wi = c_ki + IDX_DIM
    c_qb = c_wi + N_HEADS_IDX
    c_gate = c_qb + 3 * WIDTH
    rows = 128

    def regroup(w_ref, proj_ref, gate_ref):
        w = w_ref[...]
        ki = w[:, c_ki:c_wi]
        wi_pad = jnp.concatenate([w[:, c_wi:c_qb], jnp.zeros((rows, 128 - N_HEADS_IDX), F32)], axis=1)
        proj_ref[...] = jnp.concatenate([w[:, :c_ki], ki, ki, wi_pad, w[:, c_qb:c_gate]],
                                        axis=1).astype(BF16)
        gate_ref[...] = w[:, c_gate:].astype(BF16)

    return pl.pallas_call(
        regroup,
        out_shape=(jax.ShapeDtypeStruct((d, _C_END), BF16),
                   jax.ShapeDtypeStruct((d, n_in - c_gate), BF16)),
        grid=(d // rows,),
        in_specs=[pl.BlockSpec((rows, n_in), lambda i: (i, 0))],
        out_specs=(pl.BlockSpec((rows, _C_END), lambda i: (i, 0)),
                   pl.BlockSpec((rows, n_in - c_gate), lambda i: (i, 0))),
        compiler_params=_cparams(("parallel",)),
        name="regroup_w_in",
    )(w_in)


def _kv_prep_kernel(ck_ref, nk_ref, cv_ref, nv_ref, k_ref, vT_ref, *, dec):
    n_cache_tiles = ck_ref.shape[0] // ATT_TILE

    def fresh(n_ref):
        rows = lax.broadcasted_iota(I32, n_ref.shape, 0)
        new = jnp.where(rows < dec, n_ref[...], 0.0)
        return jnp.concatenate([new, jnp.zeros((ATT_TILE - new.shape[0], WIDTH), F32)], axis=0)

    for t in range(n_cache_tiles + 1):
        rows = slice(t * ATT_TILE, (t + 1) * ATT_TILE)
        k_tile = ck_ref[rows, :] if t < n_cache_tiles else fresh(nk_ref)
        v_tile = cv_ref[rows, :] if t < n_cache_tiles else fresh(nv_ref)
        k_ref[rows, :] = k_tile.astype(BF16)
        vT_ref[t] = _value_slabs(v_tile.T).astype(BF16)


def _kv_prep(cache_k, new_k, cache_v, new_v, dec):
    nb, lc, _ = cache_k.shape
    tq = new_k.shape[1]
    nct = lc // ATT_TILE
    cache_spec = pl.BlockSpec((None, lc, WIDTH), lambda b: (b, 0, 0))
    new_spec = pl.BlockSpec((None, tq, WIDTH), lambda b: (b, 0, 0))
    return pl.pallas_call(
        functools.partial(_kv_prep_kernel, dec=dec),
        out_shape=(jax.ShapeDtypeStruct((nb, lc + ATT_TILE, WIDTH), BF16),
                   jax.ShapeDtypeStruct((nb * (nct + 1), N_HEADS * V_ROWS, ATT_TILE), BF16)),
        grid=(nb,),
        in_specs=[cache_spec, new_spec, cache_spec, new_spec],
        out_specs=(pl.BlockSpec((None, lc + ATT_TILE, WIDTH), lambda b: (b, 0, 0)),
                   pl.BlockSpec((nct + 1, N_HEADS * V_ROWS, ATT_TILE), lambda b: (b, 0, 0))),
        compiler_params=_cparams(("parallel",)),
        name="kv_prep",
    )(cache_k, new_k, cache_v, new_v)


def _pad_keys(x, tkp):
    return jnp.pad(x, ((0, 0), (0, tkp - x.shape[1]), (0, 0)))


def kernel(x_prompt, x_sample, cache_k_a, cache_v_a, cache_kidx_a, cache_k_b, cache_v_b, t5_bias,
           ln1_g, ln1_b, ffn1_wi, ffn1_wo, ln2_g, ln2_b, w_in, rel_bias_b, w_branch_a, w_branch_b,
           w_out, ln3_g, ln3_b, ffn2_wi, ffn2_wo):
    depth = ln1_g.shape[0]
    alpha = (2.0 * depth) ** 0.25
    nbp, seq, d = x_prompt.shape
    nbs, dec, _ = x_sample.shape
    past = cache_k_a.shape[2]
    band = cache_k_b.shape[2]
    keep = min(BAND_CHUNKS * CHUNK, seq)
    tq_p, tq_s = ATT_TILE, 128
    assert seq % TOKEN_TILE == 0 and keep == TOKEN_TILE and dec <= tq_s and past % ATT_TILE == 0
    assert band == BAND_CHUNKS * CHUNK and (nbs * tq_s) % TOKEN_TILE == 0

    yp = x_prompt.reshape(nbp * seq, d)
    ys = jnp.pad(x_sample, ((0, 0), (0, tq_s - dec), (0, 0))).reshape(nbs * tq_s, d)
    dsa_bias_p = _dsa_bias_table(t5_bias, tq_p)
    dsa_bias_s = dsa_bias_p[..., :tq_s]
    n_sel_p = min(TOPK_MAX, seq // 4)
    n_sel_s = min(TOPK_MAX, (past + dec) // 4)
    tk_s = past + ATT_TILE
    assert band + ATT_TILE == _BAND_TILES * ATT_TILE

    st_p, st_s = [], []
    for l in range(depth):
        w_proj, w_gate = _regroup_w_in(w_in[l])
        wi1, wo1 = ffn1_wi[l].astype(BF16), ffn1_wo[l].astype(BF16)
        wi2, wo2 = ffn2_wi[l].astype(BF16), ffn2_wo[l].astype(BF16)
        wba, wbb, wo = (w_branch_a[l].astype(BF16), w_branch_b[l].astype(BF16),
                        w_out[l].astype(BF16))
        band_bias_p = _band_bias_table(rel_bias_b[l], tq_p)
        band_bias_s = band_bias_p[..., :tq_s]

        h = _ffn_ln(yp, wi1, wo1, ln1_g[l], ln1_b[l], alpha)
        (qaT, ka, kab, va, vaT, qiT, ki, kk, wiT, qbT, kbb, vbT, kbl, vbl) = _in_proj(
            h, w_proj, seq // TOKEN_TILE)
        maskT = _idx_mask(qiT, wiT, kk.reshape(nbp, seq, 128), nb=nbp, tq=tq_p, qpos0=0,
                          n_valid=seq, n_sel=n_sel_p)
        oa = _dsa_attn(qaT, kab.reshape(nbp, seq, WIDTH), vaT, maskT, dsa_bias_p,
                       nb=nbp, tq=tq_p, qpos0=0)
        ob = _band_attn(qbT, kbb.reshape(nbp, seq, WIDTH), vbT, band_bias_p,
                        nb=nbp, tq=tq_p, off=1 - _BAND_TILES)
        h2 = _mix_ln(h, oa, ob, w_gate, wba, wbb, wo, ln2_g[l], ln2_b[l], alpha)
        yp = _ffn_ln(h2, wi2, wo2, ln3_g[l], ln3_b[l], alpha)
        st_p.append((ka.reshape(nbp, seq, N_HEADS, HEAD_DIM), va.reshape(nbp, seq, N_HEADS, HEAD_DIM),
                     ki.reshape(nbp, seq, IDX_DIM), kbl.reshape(nbp, keep, N_HEADS, HEAD_DIM),
                     vbl.reshape(nbp, keep, N_HEADS, HEAD_DIM)))

        h = _ffn_ln(ys, wi1, wo1, ln1_g[l], ln1_b[l], alpha)
        (qaT, ka, _, va, _, qiT, ki, _, wiT, qbT, _, _, kbl, vbl) = _in_proj(h, w_proj, 1)
        per_stream = lambda a: a.reshape(nbs, tq_s, -1)
        new = lambda a: per_stream(a)[:, :dec]
        ka_n, va_n, ki_n, kb_n, vb_n = new(ka), new(va), new(ki), new(kbl), new(vbl)
        cache_kb = cache_k_b[l].reshape(nbs, band, WIDTH)
        cache_vb = cache_v_b[l].reshape(nbs, band, WIDTH)
        ki_all = jnp.concatenate([cache_kidx_a[l], ki_n], axis=1)
        kb_all = jnp.concatenate([cache_kb, kb_n], axis=1)
        vb_all = jnp.concatenate([cache_vb, vb_n], axis=1)
        kk_s = _pad_keys(jnp.concatenate([ki_all, ki_all], axis=2).astype(BF16), tk_s)
        maskT = _idx_mask(qiT, wiT, kk_s, nb=nbs, tq=tq_s, qpos0=past, n_valid=past + dec,
                          n_sel=n_sel_s)
        k_s, vT_s = _kv_prep(cache_k_a[l].reshape(nbs, past, WIDTH), per_stream(ka),
                             cache_v_a[l].reshape(nbs, past, WIDTH), per_stream(va), dec)
        oa = _dsa_attn(qaT, k_s, vT_s, maskT, dsa_bias_s, nb=nbs, tq=tq_s, qpos0=past)
        kb_s, vbT_s = _kv_prep(cache_kb, per_stream(kbl), cache_vb, per_stream(vbl), dec)
        ob = _band_attn(qbT, kb_s, vbT_s, band_bias_s, nb=nbs, tq=tq_s, off=0)
        h2 = _mix_ln(h, oa, ob, w_gate, wba, wbb, wo, ln2_g[l], ln2_b[l], alpha)
        ys = _ffn_ln(h2, wi2, wo2, ln3_g[l], ln3_b[l], alpha)
        heads = lambda a: a.reshape(nbs, -1, N_HEADS, HEAD_DIM)
        st_s.append((heads(ka_n), heads(va_n), ki_n, heads(kb_all[:, -band:]),
                     heads(vb_all[:, -band:])))

    y_prompt = yp.reshape(nbp, seq, d)
    y_sample = ys.reshape(nbs, tq_s, d)[:, :dec]
    stack = lambda sts, t: jnp.stack([s[t] for s in sts])
    return (y_prompt, y_sample,
            stack(st_p, 0), stack(st_p, 1), stack(st_p, 2), stack(st_p, 3), stack(st_p, 4),
            stack(st_s, 0), stack(st_s, 1), stack(st_s, 2), stack(st_s, 3), stack(st_s, 4))
```

```python
import functools
import math

import jax
import jax.numpy as jnp
from jax import lax
from jax.experimental import pallas as pl
from jax.experimental.pallas import tpu as pltpu

F32 = jnp.float32
BF16 = jnp.bfloat16
I32 = jnp.int32

CHUNK = 64
CHUNK_SHIFT = 6
HEAD_DIM = 64
N_HEADS = 8
WIDTH = N_HEADS * HEAD_DIM
N_HEADS_IDX = 4
IDX_DIM = 64
TOPK_MAX = 256
IDX_SCALE = (IDX_DIM * N_HEADS_IDX) ** -0.5
BAND_CHUNKS = 8
REL_BACK = 128
T5_BUCKETS = 32
T5_MAX_DIST = 128
LN_EPS = 1e-5
MASKED = -2.0 ** 100
LOG2E = math.log2(math.e)
V_ROWS = HEAD_DIM + 16
INT_MIN = -2 ** 31
LANES = 128
SUBLANES = 8

ATT_TILE = 256
TOKEN_TILE = 512
FF_CHUNK = 256
VMEM_LIMIT_MB = 56


def _cparams(sem):
    return pltpu.CompilerParams(dimension_semantics=sem,
                                vmem_limit_bytes=VMEM_LIMIT_MB * 1024 * 1024)


def _const_spec(shape):
    nd = len(shape)
    return pl.BlockSpec(shape, lambda *_: (0,) * nd, pipeline_mode=pl.Buffered(1))


def _tree_sum(xs):
    return _tree_reduce(lambda a, b: a + b, xs)


def _tree_reduce(fn, xs):
    xs = list(xs)
    while len(xs) > 1:
        nxt = [fn(xs[a], xs[a + 1]) for a in range(0, len(xs) - 1, 2)]
        if len(xs) % 2:
            nxt.append(xs[-1])
        xs = nxt
    return xs[0]


def _layer_norm_rows(y, g, b):
    mu = jnp.mean(y, axis=-1, keepdims=True)
    d = y - mu
    var = jnp.mean(d * d, axis=-1, keepdims=True)
    return d * lax.rsqrt(var + LN_EPS) * g + b


def _ffn_ln_kernel(x_ref, wi_ref, wo_ref, g_ref, b_ref, o_ref, *, alpha, dff):
    x = x_ref[...]
    xb = x.astype(BF16)
    acc = jnp.zeros(x.shape, F32)
    for c in range(dff // FF_CHUNK):
        lo = c * FF_CHUNK
        a = jnp.dot(xb, wi_ref[:, lo:lo + FF_CHUNK], preferred_element_type=F32)
        u = jnp.dot(xb, wi_ref[:, dff + lo:dff + lo + FF_CHUNK], preferred_element_type=F32)
        hm = (a * jax.nn.sigmoid(a)) * u
        acc = acc + jnp.dot(hm.astype(BF16), wo_ref[lo:lo + FF_CHUNK, :],
                            preferred_element_type=F32)
    y = alpha * x + 0.5 * acc
    o_ref[...] = _layer_norm_rows(y, g_ref[...], b_ref[...])


def _ffn_ln(x, wi, wo, g, b, alpha):
    n, d = x.shape
    dff = wo.shape[0]
    tm = TOKEN_TILE
    return pl.pallas_call(
        functools.partial(_ffn_ln_kernel, alpha=alpha, dff=dff),
        out_shape=jax.ShapeDtypeStruct((n, d), F32),
        grid=(n // tm,),
        in_specs=[pl.BlockSpec((tm, d), lambda i: (i, 0)),
                  _const_spec((d, 2 * dff)), _const_spec((dff, d)),
                  _const_spec((1, d)), _const_spec((1, d))],
        out_specs=pl.BlockSpec((tm, d), lambda i: (i, 0)),
        compiler_params=_cparams(("parallel",)),
        name="ffn_ln",
    )(x, wi, wo, g.reshape(1, d), b.reshape(1, d))


_C_QA, _C_KA, _C_VA, _C_QI, _C_KI, _C_WI, _C_QB, _C_KB, _C_VB, _C_END = (
    0, 512, 1024, 1536, 1792, 1920, 2048, 2560, 3072, 3584)


def _value_slabs(zt):
    n = zt.shape[1]
    ones_slab = jnp.where(lax.broadcasted_iota(I32, (V_ROWS - HEAD_DIM, n), 0) == 0, 1.0, 0.0)
    return jnp.concatenate([piece for h in range(N_HEADS)
                            for piece in (zt[h * HEAD_DIM:(h + 1) * HEAD_DIM, :], ones_slab)], axis=0)


def _inproj_kernel(h_ref, w_ref, qaT_ref, ka_ref, kab_ref, va_ref, vaT_ref, qiT_ref, ki_ref,
                   kk_ref, wiT_ref, qbT_ref, kbb_ref, vbT_ref, kbl_ref, vbl_ref, *, per):
    hb = h_ref[...].astype(BF16)

    def proj(c0, c1):
        return jnp.dot(hb, w_ref[:, c0:c1], preferred_element_type=F32)

    def put_tiles(ref, zt):
        aug = _value_slabs(zt)
        for c in range(ref.shape[0]):
            ref[c] = aug[:, c * ATT_TILE:(c + 1) * ATT_TILE].astype(ref.dtype)

    scale = HEAD_DIM ** -0.5 * LOG2E
    qaT_ref[...] = (proj(_C_QA, _C_KA) * scale).T.astype(BF16)
    ka = proj(_C_KA, _C_VA)
    ka_ref[...] = ka
    kab_ref[...] = ka.astype(BF16)
    va = proj(_C_VA, _C_QI)
    va_ref[...] = va
    put_tiles(vaT_ref, va.T)
    qiT_ref[...] = proj(_C_QI, _C_KI).T.astype(BF16)
    kk = proj(_C_KI, _C_WI)
    ki_ref[...] = kk[:, :IDX_DIM]
    kk_ref[...] = kk.astype(BF16)
    wiT_ref[...] = proj(_C_WI, _C_QB).T[:8, :]
    qbT_ref[...] = (proj(_C_QB, _C_KB) * scale).T.astype(BF16)
    kb = proj(_C_KB, _C_VB)
    kbb_ref[...] = kb.astype(BF16)
    vb = proj(_C_VB, _C_END)
    put_tiles(vbT_ref, vb.T)

    @pl.when(pl.program_id(0) % per == per - 1)
    def _():
        kbl_ref[...] = kb
        vbl_ref[...] = vb


def _in_proj(h, w, per):
    n, d = h.shape
    tm = TOKEN_TILE
    nt = n // tm
    sub = tm // ATT_TILE
    row = lambda width: pl.BlockSpec((tm, width), lambda i: (i, 0))
    col = lambda rows: pl.BlockSpec((rows, tm), lambda i: (0, i))
    til = pl.BlockSpec((sub, N_HEADS * V_ROWS, ATT_TILE), lambda i: (i, 0, 0))
    last = pl.BlockSpec((tm, WIDTH), lambda i: (i // per, 0))
    sds = jax.ShapeDtypeStruct
    out_shape = (
        sds((WIDTH, n), BF16),
        sds((n, WIDTH), F32), sds((n, WIDTH), BF16),
        sds((n, WIDTH), F32), sds((n // ATT_TILE, N_HEADS * V_ROWS, ATT_TILE), BF16),
        sds((N_HEADS_IDX * IDX_DIM, n), BF16),
        sds((n, IDX_DIM), F32), sds((n, LANES), BF16),
        sds((8, n), F32),
        sds((WIDTH, n), BF16),
        sds((n, WIDTH), BF16),
        sds((n // ATT_TILE, N_HEADS * V_ROWS, ATT_TILE), BF16),
        sds((nt // per * tm, WIDTH), F32), sds((nt // per * tm, WIDTH), F32),
    )
    out_specs = (col(WIDTH), row(WIDTH), row(WIDTH), row(WIDTH), til,
                 col(N_HEADS_IDX * IDX_DIM), row(IDX_DIM), row(LANES), col(SUBLANES),
                 col(WIDTH), row(WIDTH), til, last, last)
    return pl.pallas_call(
        functools.partial(_inproj_kernel, per=per),
        out_shape=out_shape,
        grid=(nt,),
        in_specs=[pl.BlockSpec((tm, d), lambda i: (i, 0)), _const_spec((d, _C_END))],
        out_specs=out_specs,
        compiler_params=_cparams(("arbitrary",)),
        name="in_proj",
    )(h, w)


def _bit_planes(words):
    a = list(reversed(words))
    j, m = 16, 0x0000FFFF
    while j:
        k = 0
        while k < 32:
            t = (a[k] ^ lax.shift_right_logical(a[k + j], j)) & m
            a[k] = a[k] ^ t
            a[k + j] = a[k + j] ^ lax.shift_left(t, j)
            k = (k + j + 1) & ~j
        j >>= 1
        m = (m ^ (m << j)) & 0xFFFFFFFF
    return a


def _idx_mask_kernel(qiT_ref, wiT_ref, kk_ref, mask_ref, st_ref, p_ref, c_ref, d_ref, incl_ref,
                     *, tq, qpos0, n_valid, n_sel):
    tk = ATT_TILE
    assert tk == 32 * 8
    nkt_total = st_ref.shape[0] // tk

    @pl.when((pl.program_id(0) == 0) & (pl.program_id(1) == 0))
    def _():
        p_ref[...] = jnp.zeros(p_ref.shape, I32)

    q0 = qpos0 + pl.program_id(1) * tq
    n_keys = jnp.minimum(q0 + tq, n_valid)
    nt = jnp.minimum(lax.shift_right_logical(n_keys + tk - 1, int(math.log2(tk))), nkt_total)

    lane = lax.broadcasted_iota(I32, (8, tq), 1)
    q_chunk = lax.shift_right_logical(q0 + lane, CHUNK_SHIFT)
    n_adm = jnp.minimum((q_chunk + 1) * CHUNK, n_valid)
    k_row = jnp.minimum(n_adm, n_sel)

    zeros = jnp.zeros((LANES - IDX_DIM, tq), BF16)
    qh = [jnp.concatenate([qiT_ref[h * IDX_DIM:(h + 1) * IDX_DIM, :], zeros], axis=0)
          for h in range(N_HEADS_IDX)]
    scale_exact = math.frexp(IDX_SCALE)[0] == 0.5
    wh = [wiT_ref[h:h + 1, :] * IDX_SCALE if scale_exact else wiT_ref[h:h + 1, :]
          for h in range(N_HEADS_IDX)]

    def dots(j, h):
        r0 = pl.multiple_of(j * tk, tk)
        d_ref[h] = jnp.dot(kk_ref[pl.ds(r0, tk), :], qh[h], preferred_element_type=F32)

    def score_tile(j, last):
        r0 = pl.multiple_of(j * tk, tk)
        sc = None
        for h in range(N_HEADS_IDX):
            d = d_ref[h]
            if not last:
                dots(j + 1, h)
            term = wh[h] * jnp.maximum(d, 0.0)
            sc = term if sc is None else sc + term
        if not scale_exact:
            sc = sc * IDX_SCALE
        bits = lax.bitcast_convert_type(sc, I32)
        sign = lax.shift_right_arithmetic(bits, 31)
        key = (bits ^ (sign & 0x7FFFFFFF)) - sign
        if last:
            kpos = r0 + lax.broadcasted_iota(I32, (tk, tq), 0)
            key = jnp.where(kpos < n_adm[0:1, :], key, INT_MIN)
        st_ref[pl.ds(r0, tk), :] = key
        planes = _bit_planes([key[v * 8:(v + 1) * 8, :] for v in range(32)])
        p_ref[0, j] = planes[0] ^ -1
        for i in range(1, 32):
            p_ref[i, j] = planes[i]

    for h in range(N_HEADS_IDX):
        dots(0, h)

    def score_run(first, count):
        for c in range(count):
            score_tile(first + c, False)

    def score_quad(u, carry):
        score_run(4 * u, 4)
        return carry

    n_open = nt - 1
    n_quads = lax.shift_right_logical(n_open, 2)
    lax.fori_loop(0, n_quads, score_quad, 0)
    done = 4 * n_quads
    for count in (2, 1):
        @pl.when((n_open & count) == count)
        def _(count=count, done=done):
            score_run(done, count)
        done = done + (n_open & count)

    score_tile(nt - 1, True)

    group = incl_ref.shape[0]
    n_groups = nt if group == 1 else lax.shift_right_logical(nt + 3, 2)
    for g in range(group - 1):
        @pl.when(nt + g < n_groups * group)
        def _(g=g):
            r0 = pl.multiple_of((nt + g) * tk, tk)
            st_ref[pl.ds(r0, tk), :] = jnp.full((tk, tq), INT_MIN, I32)

    limit = n_adm - (nt - 1) * tk
    n_low = jnp.clip(lax.shift_right_arithmetic(limit - lax.broadcasted_iota(I32, (8, tq), 0) + 7, 3),
                     0, 32)
    last_word = jnp.where(n_low >= 32, -1, lax.shift_left(1, jnp.minimum(n_low, 31)) - 1)
    for jj in range(nkt_total):
        c_ref[jj] = jnp.where(jj < nt - 1, -1, jnp.where(jj == nt - 1, last_word, 0))

    def radix_select(n_scan):
        def radix_pass(i, carry):
            want, t = carry
            n_set = _tree_sum([lax.population_count(c_ref[jj] & p_ref[i, jj]) for jj in range(n_scan)])
            n_set = jnp.broadcast_to(jnp.sum(n_set, axis=0, keepdims=True), (8, tq))
            keep_set = n_set >= want
            flip = jnp.where(keep_set, 0, -1)
            for jj in range(n_scan):
                c_ref[jj] = c_ref[jj] & (p_ref[i, jj] ^ flip)
            return (jnp.where(keep_set, want, want - n_set),
                    t | jnp.where(keep_set, lax.shift_left(jnp.int32(1), 31 - i), 0))

        return lax.fori_loop(0, 32, radix_pass, (k_row, jnp.zeros((8, tq), I32)))

    if nkt_total >= 16 and nkt_total % 2 == 0:
        want, t_bits = lax.cond(nt <= nkt_total // 2, lambda: radix_select(nkt_total // 2),
                                lambda: radix_select(nkt_total))
    else:
        want, t_bits = radix_select(nkt_total)
    t_row = (t_bits ^ INT_MIN)[0:1, :]
    ties_wanted = want[0:1, :].astype(F32)

    tri = jnp.where(lax.broadcasted_iota(I32, (tk, tk), 0) >= lax.broadcasted_iota(I32, (tk, tk), 1),
                    1.0, 0.0).astype(BF16)

    def tie_prefix(j, g):
        r0 = pl.multiple_of(j * tk, tk)
        eq = jnp.where(st_ref[pl.ds(r0, tk), :] == t_row, 1.0, 0.0)
        incl_ref[g] = jnp.dot(tri, eq.astype(BF16), preferred_element_type=F32)

    def final_group(u, seen, last):
        for g in range(group):
            j = u * group + g
            r0 = pl.multiple_of(j * tk, tk)
            incl = incl_ref[g]
            if not last:
                tie_prefix(j + group, g)
            rank = seen + incl
            bar = t_row + jnp.where(rank <= ties_wanted, 0, 1)
            sel = jnp.where(st_ref[pl.ds(r0, tk), :] >= bar, 1, 0)
            mask_ref[pl.ds(r0, tk), :] = sel.astype(jnp.int8)
            seen = seen + incl[tk - 1:tk, :]
        return seen

    for g in range(group):
        tie_prefix(g, g)
    seen = lax.fori_loop(0, n_groups - 1, lambda u, seen: final_group(u, seen, False),
                         jnp.zeros((1, tq), F32))
    final_group(n_groups - 1, seen, True)

    def zero_tile(j, carry):
        r0 = pl.multiple_of(j * tk, tk)
        mask_ref[pl.ds(r0, tk), :] = jnp.zeros((tk, tq), jnp.int8)
        return carry

    lax.fori_loop(n_groups * group, nkt_total, zero_tile, 0)


def _idx_mask(qiT, wiT, kk, *, nb, tq, qpos0, n_valid, n_sel):
    tkp = kk.shape[1]
    nq = qiT.shape[1] // nb // tq
    return pl.pallas_call(
        functools.partial(_idx_mask_kernel, tq=tq, qpos0=qpos0, n_valid=n_valid, n_sel=n_sel),
        out_shape=jax.ShapeDtypeStruct((nb, tkp, nq * tq), jnp.int8),
        grid=(nb, nq),
        in_specs=[pl.BlockSpec((N_HEADS_IDX * IDX_DIM, tq), lambda b, i: (0, b * nq + i)),
                  pl.BlockSpec((8, tq), lambda b, i: (0, b * nq + i)),
                  pl.BlockSpec((None, tkp, LANES), lambda b, i: (b, 0, 0))],
        out_specs=pl.BlockSpec((None, tkp, tq), lambda b, i: (b, 0, i)),
        scratch_shapes=[pltpu.VMEM((tkp, tq), I32),
                        pltpu.VMEM((32, tkp // ATT_TILE, 8, tq), I32),
                        pltpu.VMEM((tkp // ATT_TILE, 8, tq), I32),
                        pltpu.VMEM((N_HEADS_IDX, ATT_TILE, tq), F32),
                        pltpu.VMEM((4 if (tkp // ATT_TILE) % 4 == 0 else 1, ATT_TILE, tq), F32)],
        compiler_params=_cparams(("arbitrary", "arbitrary")),
        name="idx_mask",
    )(qiT, wiT, kk)


def _load_qsel(qT_ref, qsel_ref, tq):
    zeros = jnp.zeros((HEAD_DIM, tq), BF16)
    for h in range(N_HEADS):
        blk = qT_ref[h * HEAD_DIM:(h + 1) * HEAD_DIM, :]
        qsel_ref[h] = jnp.concatenate([blk, zeros] if h % 2 == 0 else [zeros, blk], axis=0)


def _init_state(m_ref, acc_ref):
    m_ref[...] = jnp.full(m_ref.shape, MASKED, F32)
    acc_ref[...] = jnp.zeros(acc_ref.shape, F32)


def _scores(k_slab, qsel_ref, s_ref, h):
    s_ref[h] = jnp.dot(k_slab(h // 2), qsel_ref[h], preferred_element_type=F32)


def _heads_tile(v_rows, next_k_slab, qsel_ref, s_ref, m_ref, acc_ref, addend):
    tq = s_ref.shape[2]
    slabs = s_ref.shape[1] // 16
    for h in range(N_HEADS):
        s = s_ref[h].astype(BF16)
        for a in addend(h):
            s = s + a
        if next_k_slab is not None:
            _scores(next_k_slab, qsel_ref, s_ref, h)
        m_old = m_ref[h:h + 1, :]
        top = _tree_reduce(jnp.maximum, [s[r * 16:(r + 1) * 16, :] for r in range(slabs)])
        m_new = jnp.maximum(m_old, jnp.max(top.astype(F32), axis=0, keepdims=True))
        alpha = jnp.exp2(m_old - m_new)
        m16 = jnp.broadcast_to(m_new, (16, tq)).astype(BF16)
        p = jnp.concatenate([jnp.exp2(s[r * 16:(r + 1) * 16, :] - m16) for r in range(slabs)], axis=0)
        m_ref[h:h + 1, :] = m_new
        pv = jnp.dot(v_rows(h), p, preferred_element_type=F32)
        rows = slice(h * V_ROWS, (h + 1) * V_ROWS)
        acc_ref[rows, :] = alpha * acc_ref[rows, :] + pv


def _attn_scratch(tq):
    return [pltpu.VMEM((N_HEADS, LANES, tq), BF16), pltpu.VMEM((N_HEADS, ATT_TILE, tq), F32),
            pltpu.VMEM((N_HEADS, tq), F32), pltpu.VMEM((N_HEADS * V_ROWS, tq), F32)]


def _store_out(o_ref, acc_ref):
    outs = [acc_ref[h * V_ROWS:h * V_ROWS + HEAD_DIM, :]
            / acc_ref[h * V_ROWS + HEAD_DIM:h * V_ROWS + HEAD_DIM + 1, :] for h in range(N_HEADS)]
    o_ref[...] = jnp.concatenate(outs, axis=0).T.astype(o_ref.dtype)


def _dsa_kernel(qT_ref, k_ref, vT_ref, mask_ref, bias_ref, o_ref, qsel_ref, s_ref, m_ref, acc_ref,
                *, tq, qpos0):
    tk = ATT_TILE
    jd = lax.shift_right_logical(qpos0 + pl.program_id(1) * tq, int(math.log2(tk)))
    _load_qsel(qT_ref, qsel_ref, tq)
    _init_state(m_ref, acc_ref)

    def k_slab(j):
        r0 = pl.multiple_of(j * tk, tk)
        return lambda g: k_ref[pl.ds(r0, tk), g * LANES:(g + 1) * LANES]

    def tile(j, near, last=False):
        r0 = pl.multiple_of(j * tk, tk)
        unselected = jnp.where(mask_ref[pl.ds(r0, tk), :].astype(I32) != 0, 0.0,
                               MASKED).astype(BF16)
        addend = ((lambda h: (unselected,)) if near is None
                  else (lambda h: (unselected, bias_ref[near, h])))
        _heads_tile(lambda h: vT_ref[j, h * V_ROWS:(h + 1) * V_ROWS, :],
                    None if last else k_slab(j + 1),
                    qsel_ref, s_ref, m_ref, acc_ref, addend)

    for h in range(N_HEADS):
        _scores(k_slab(0), qsel_ref, s_ref, h)

    n_far = jnp.maximum(jd - 1, 0)

    def far_run(first, count):
        for c in range(count):
            tile(first + c, None)

    def far_octet(u, carry):
        far_run(8 * u, 8)
        return carry

    n_octets = lax.shift_right_logical(n_far, 3)
    lax.fori_loop(0, n_octets, far_octet, 0)
    done = 8 * n_octets
    for count in (4, 2, 1):
        @pl.when((n_far & count) == count)
        def _(count=count, done=done):
            far_run(done, count)
        done = done + (n_far & count)

    @pl.when(jd >= 1)
    def _():
        tile(jd - 1, 0)

    tile(jd, 1, last=True)
    _store_out(o_ref, acc_ref)


def _dsa_attn(qT, k, vT, maskT, bias, *, nb, tq, qpos0):
    tkp = k.shape[1]
    nkt = tkp // ATT_TILE
    nq = qT.shape[1] // nb // tq
    return pl.pallas_call(
        functools.partial(_dsa_kernel, tq=tq, qpos0=qpos0),
        out_shape=jax.ShapeDtypeStruct((nb * nq * tq, WIDTH), BF16),
        grid=(nb, nq),
        in_specs=[pl.BlockSpec((WIDTH, tq), lambda b, i: (0, b * nq + i)),
                  pl.BlockSpec((None, tkp, WIDTH), lambda b, i: (b, 0, 0),
                               pipeline_mode=pl.Buffered(1)),
                  pl.BlockSpec((nkt, N_HEADS * V_ROWS, ATT_TILE), lambda b, i: (b, 0, 0),
                               pipeline_mode=pl.Buffered(1)),
                  pl.BlockSpec((None, tkp, tq), lambda b, i: (b, 0, i)),
                  _const_spec(bias.shape)],
        out_specs=pl.BlockSpec((tq, WIDTH), lambda b, i: (b * nq + i, 0)),
        scratch_shapes=_attn_scratch(tq),
        compiler_params=_cparams(("parallel", "parallel")),
        name="dsa_attn",
    )(qT, k, vT, maskT, bias)


_BAND_TILES = BAND_CHUNKS * CHUNK // ATT_TILE + 1


def _band_kernel(qT_ref, *refs, tq, off):
    k_refs = refs[:_BAND_TILES]
    v_refs = refs[_BAND_TILES:2 * _BAND_TILES]
    bias_ref, o_ref, qsel_ref, s_ref, m_ref, acc_ref = refs[2 * _BAND_TILES:]
    _load_qsel(qT_ref, qsel_ref, tq)
    _init_state(m_ref, acc_ref)
    k_slab = lambda w: (lambda g: k_refs[w][:, g * LANES:(g + 1) * LANES])
    for h in range(N_HEADS):
        _scores(k_slab(0), qsel_ref, s_ref, h)
    for w in range(_BAND_TILES):
        entry = jnp.where(pl.program_id(1) + off + w >= 0, w, _BAND_TILES)
        _heads_tile(lambda h: v_refs[w][0, h * V_ROWS:(h + 1) * V_ROWS, :],
                    None if w == _BAND_TILES - 1 else k_slab(w + 1),
                    qsel_ref, s_ref, m_ref, acc_ref, lambda h: (bias_ref[entry, h],))
    _store_out(o_ref, acc_ref)


def _band_attn(qT, k, vT, bias, *, nb, tq, off):
    nkt = k.shape[1] // ATT_TILE
    nq = qT.shape[1] // nb // tq
    tile_of = lambda i, w: jnp.maximum(i + off + w, 0)
    k_specs = [pl.BlockSpec((None, ATT_TILE, WIDTH), lambda b, i, w=w: (b, tile_of(i, w), 0))
               for w in range(_BAND_TILES)]
    v_specs = [pl.BlockSpec((1, N_HEADS * V_ROWS, ATT_TILE),
                            lambda b, i, w=w: (b * nkt + tile_of(i, w), 0, 0))
               for w in range(_BAND_TILES)]
    return pl.pallas_call(
        functools.partial(_band_kernel, tq=tq, off=off),
        out_shape=jax.ShapeDtypeStruct((nb * nq * tq, WIDTH), BF16),
        grid=(nb, nq),
        in_specs=[pl.BlockSpec((WIDTH, tq), lambda b, i: (0, b * nq + i))] + k_specs + v_specs
                 + [_const_spec(bias.shape)],
        out_specs=pl.BlockSpec((tq, WIDTH), lambda b, i: (b * nq + i, 0)),
        scratch_shapes=_attn_scratch(tq),
        compiler_params=_cparams(("parallel", "parallel")),
        name="band_attn",
    )(qT, *([k] * _BAND_TILES), *([vT] * _BAND_TILES), bias)


def _mix_ln_kernel(h_ref, oa_ref, ob_ref, wg_ref, wba_ref, wbb_ref, wo_ref, g_ref, b_ref, o_ref,
                   *, alpha):
    h = h_ref[...]
    hb = h.astype(BF16)
    d = h.shape[1]
    ga = jnp.dot(hb, wg_ref[:, :d], preferred_element_type=F32)
    gb = jnp.dot(hb, wg_ref[:, d:], preferred_element_type=F32)
    ya = jnp.dot(oa_ref[...], wba_ref[...], preferred_element_type=F32)
    yb = jnp.dot(ob_ref[...], wbb_ref[...], preferred_element_type=F32)
    gated = jax.nn.sigmoid(ga) * ya + jax.nn.sigmoid(gb) * yb
    mix = jnp.dot(gated.astype(BF16), wo_ref[...], preferred_element_type=F32)
    o_ref[...] = _layer_norm_rows(alpha * h + mix, g_ref[...], b_ref[...])


def _mix_ln(h, oa, ob, wg, wba, wbb, wo, g, b, alpha):
    n, d = h.shape
    tm = TOKEN_TILE
    row = lambda width: pl.BlockSpec((tm, width), lambda i: (i, 0))
    return pl.pallas_call(
        functools.partial(_mix_ln_kernel, alpha=alpha),
        out_shape=jax.ShapeDtypeStruct((n, d), F32),
        grid=(n // tm,),
        in_specs=[row(d), row(WIDTH), row(WIDTH), _const_spec(wg.shape), _const_spec(wba.shape),
                  _const_spec(wbb.shape), _const_spec(wo.shape), _const_spec((1, d)),
                  _const_spec((1, d))],
        out_specs=row(d),
        compiler_params=_cparams(("parallel",)),
        name="mix_ln",
    )(h, oa, ob, wg, wba, wbb, wo, g.reshape(1, d), b.reshape(1, d))


def _t5_bucket(rel):
    half = T5_BUCKETS // 2
    exact = half // 2
    n = jnp.abs(rel)
    log_ratio = jnp.log(jnp.maximum(n, 1).astype(F32) / exact) / math.log(T5_MAX_DIST / exact)
    large = jnp.minimum(exact + (log_ratio * (half - exact)).astype(I32), half - 1)
    return (rel > 0).astype(I32) * half + jnp.where(n < exact, n, large)


def _rel_line(n_tiles, first, tq):
    period = ATT_TILE + tq
    y = jnp.arange(period, dtype=I32)[None, :]
    c = (jnp.arange(n_tiles, dtype=I32)[:, None] + first) * ATT_TILE
    return jnp.where(y < tq, c - y, c + period - y)


def _toeplitz(v, tq):
    period = v.shape[-1]
    flat = jnp.tile(v, (1,) * (v.ndim - 1) + (ATT_TILE,))[..., :ATT_TILE * (period - 1)]
    return flat.reshape(v.shape[:-1] + (ATT_TILE, period - 1))[..., :tq]


def _dsa_bias_table(t5_bias, tq):
    assert T5_MAX_DIST <= ATT_TILE + 1
    far = t5_bias[:, _t5_bucket(jnp.int32(-(ATT_TILE + 1)))]
    line = t5_bias[:, _t5_bucket(_rel_line(2, -1, tq))] - far[:, None, None]
    return (_toeplitz(jnp.moveaxis(line, 0, 1), tq) * LOG2E).astype(BF16)


def _band_bias_table(rel_bias, tq):
    first = 1 - _BAND_TILES
    ridx = jnp.clip(_rel_line(_BAND_TILES, first, tq), -REL_BACK, CHUNK - 1) + REL_BACK
    bias = _toeplitz(jnp.moveaxis(rel_bias[:, ridx], 0, 1), tq)
    w = jnp.arange(_BAND_TILES, dtype=I32)[:, None, None]
    j = jnp.arange(ATT_TILE, dtype=I32)[None, :, None]
    i = jnp.arange(tq, dtype=I32)[None, None, :]
    dchunk = ((w + first) * ATT_TILE + j) // CHUNK - i // CHUNK
    ok = (dchunk <= 0) & (dchunk >= -BAND_CHUNKS)
    table = jnp.where(ok[:, None], bias * LOG2E, MASKED)
    return jnp.concatenate([table, jnp.full_like(table[:1], MASKED)], axis=0).astype(BF16)


def _regroup_w_in(w_in):
    d, n_in = w_in.shape
    c_ki = 3 * WIDTH + N_HEADS_IDX * IDX_DIM
    c_wi = c_ki + IDX_DIM
    c_qb = c_wi + N_HEADS_IDX
    c_gate = c_qb + 3 * WIDTH
    rows = LANES

    def regroup(w_ref, proj_ref, gate_ref):
        w = w_ref[...]
        ki = w[:, c_ki:c_wi]
        wi_pad = jnp.concatenate([w[:, c_wi:c_qb], jnp.zeros((rows, LANES - N_HEADS_IDX), F32)], axis=1)
        proj_ref[...] = jnp.concatenate([w[:, :c_ki], ki, ki, wi_pad, w[:, c_qb:c_gate]],
                                        axis=1).astype(BF16)
        gate_ref[...] = w[:, c_gate:].astype(BF16)

    return pl.pallas_call(
        regroup,
        out_shape=(jax.ShapeDtypeStruct((d, _C_END), BF16),
                   jax.ShapeDtypeStruct((d, n_in - c_gate), BF16)),
        grid=(d // rows,),
        in_specs=[pl.BlockSpec((rows, n_in), lambda i: (i, 0))],
        out_specs=(pl.BlockSpec((rows, _C_END), lambda i: (i, 0)),
                   pl.BlockSpec((rows, n_in - c_gate), lambda i: (i, 0))),
        compiler_params=_cparams(("parallel",)),
        name="regroup_w_in",
    )(w_in)


def _kv_prep_kernel(ck_ref, nk_ref, cv_ref, nv_ref, k_ref, vT_ref, *, dec):
    n_cache_tiles = ck_ref.shape[0] // ATT_TILE

    def fresh(n_ref):
        rows = lax.broadcasted_iota(I32, n_ref.shape, 0)
        new = jnp.where(rows < dec, n_ref[...], 0.0)
        return jnp.concatenate([new, jnp.zeros((ATT_TILE - new.shape[0], WIDTH), F32)], axis=0)

    for t in range(n_cache_tiles + 1):
        rows = slice(t * ATT_TILE, (t + 1) * ATT_TILE)
        k_tile = ck_ref[rows, :] if t < n_cache_tiles else fresh(nk_ref)
        v_tile = cv_ref[rows, :] if t < n_cache_tiles else fresh(nv_ref)
        k_ref[rows, :] = k_tile.astype(BF16)
        vT_ref[t] = _value_slabs(v_tile.T).astype(BF16)


def _kv_prep(cache_k, new_k, cache_v, new_v, dec):
    nb, lc, _ = cache_k.shape
    tq = new_k.shape[1]
    nct = lc // ATT_TILE
    cache_spec = pl.BlockSpec((None, lc, WIDTH), lambda b: (b, 0, 0))
    new_spec = pl.BlockSpec((None, tq, WIDTH), lambda b: (b, 0, 0))
    return pl.pallas_call(
        functools.partial(_kv_prep_kernel, dec=dec),
        out_shape=(jax.ShapeDtypeStruct((nb, lc + ATT_TILE, WIDTH), BF16),
                   jax.ShapeDtypeStruct((nb * (nct + 1), N_HEADS * V_ROWS, ATT_TILE), BF16)),
        grid=(nb,),
        in_specs=[cache_spec, new_spec, cache_spec, new_spec],
        out_specs=(pl.BlockSpec((None, lc + ATT_TILE, WIDTH), lambda b: (b, 0, 0)),
                   pl.BlockSpec((nct + 1, N_HEADS * V_ROWS, ATT_TILE), lambda b: (b, 0, 0))),
        compiler_params=_cparams(("parallel",)),
        name="kv_prep",
    )(cache_k, new_k, cache_v, new_v)


def _pad_keys(x, tkp):
    return jnp.pad(x, ((0, 0), (0, tkp - x.shape[1]), (0, 0)))


def kernel(x_prompt, x_sample, cache_k_a, cache_v_a, cache_kidx_a, cache_k_b, cache_v_b, t5_bias,
           ln1_g, ln1_b, ffn1_wi, ffn1_wo, ln2_g, ln2_b, w_in, rel_bias_b, w_branch_a, w_branch_b,
           w_out, ln3_g, ln3_b, ffn2_wi, ffn2_wo):
    depth = ln1_g.shape[0]
    alpha = (2.0 * depth) ** 0.25
    nbp, seq, d = x_prompt.shape
    nbs, dec, _ = x_sample.shape
    past = cache_k_a.shape[2]
    band = cache_k_b.shape[2]
    keep = min(BAND_CHUNKS * CHUNK, seq)
    tq_p, tq_s = ATT_TILE, LANES
    assert seq % TOKEN_TILE == 0 and keep == TOKEN_TILE and dec <= tq_s and past % ATT_TILE == 0
    assert band == BAND_CHUNKS * CHUNK and (nbs * tq_s) % TOKEN_TILE == 0

    yp = x_prompt.reshape(nbp * seq, d)
    ys = jnp.pad(x_sample, ((0, 0), (0, tq_s - dec), (0, 0))).reshape(nbs * tq_s, d)
    dsa_bias_p = _dsa_bias_table(t5_bias, tq_p)
    dsa_bias_s = dsa_bias_p[..., :tq_s]
    n_sel_p = min(TOPK_MAX, seq // 4)
    n_sel_s = min(TOPK_MAX, (past + dec) // 4)
    tk_s = past + ATT_TILE
    assert band + ATT_TILE == _BAND_TILES * ATT_TILE

    st_p, st_s = [], []
    for l in range(depth):
        w_proj, w_gate = _regroup_w_in(w_in[l])
        wi1, wo1 = ffn1_wi[l].astype(BF16), ffn1_wo[l].astype(BF16)
        wi2, wo2 = ffn2_wi[l].astype(BF16), ffn2_wo[l].astype(BF16)
        wba, wbb, wo = (w_branch_a[l].astype(BF16), w_branch_b[l].astype(BF16),
                        w_out[l].astype(BF16))
        band_bias_p = _band_bias_table(rel_bias_b[l], tq_p)
        band_bias_s = band_bias_p[..., :tq_s]

        h = _ffn_ln(yp, wi1, wo1, ln1_g[l], ln1_b[l], alpha)
        (qaT, ka, kab, va, vaT, qiT, ki, kk, wiT, qbT, kbb, vbT, kbl, vbl) = _in_proj(
            h, w_proj, seq // TOKEN_TILE)
        maskT = _idx_mask(qiT, wiT, kk.reshape(nbp, seq, LANES), nb=nbp, tq=tq_p, qpos0=0,
                          n_valid=seq, n_sel=n_sel_p)
        oa = _dsa_attn(qaT, kab.reshape(nbp, seq, WIDTH), vaT, maskT, dsa_bias_p,
                       nb=nbp, tq=tq_p, qpos0=0)
        ob = _band_attn(qbT, kbb.reshape(nbp, seq, WIDTH), vbT, band_bias_p,
                        nb=nbp, tq=tq_p, off=1 - _BAND_TILES)
        h2 = _mix_ln(h, oa, ob, w_gate, wba, wbb, wo, ln2_g[l], ln2_b[l], alpha)
        yp = _ffn_ln(h2, wi2, wo2, ln3_g[l], ln3_b[l], alpha)
        st_p.append((ka.reshape(nbp, seq, N_HEADS, HEAD_DIM), va.reshape(nbp, seq, N_HEADS, HEAD_DIM),
                     ki.reshape(nbp, seq, IDX_DIM), kbl.reshape(nbp, keep, N_HEADS, HEAD_DIM),
                     vbl.reshape(nbp, keep, N_HEADS, HEAD_DIM)))

        h = _ffn_ln(ys, wi1, wo1, ln1_g[l], ln1_b[l], alpha)
        (qaT, ka, _, va, _, qiT, ki, _, wiT, qbT, _, _, kbl, vbl) = _in_proj(h, w_proj, 1)
        per_stream = lambda a: a.reshape(nbs, tq_s, -1)
        new = lambda a: per_stream(a)[:, :dec]
        ka_n, va_n, ki_n, kb_n, vb_n = new(ka), new(va), new(ki), new(kbl), new(vbl)
        cache_kb = cache_k_b[l].reshape(nbs, band, WIDTH)
        cache_vb = cache_v_b[l].reshape(nbs, band, WIDTH)
        ki_all = jnp.concatenate([cache_kidx_a[l], ki_n], axis=1)
        kb_all = jnp.concatenate([cache_kb, kb_n], axis=1)
        vb_all = jnp.concatenate([cache_vb, vb_n], axis=1)
        kk_s = _pad_keys(jnp.concatenate([ki_all, ki_all], axis=2).astype(BF16), tk_s)
        maskT = _idx_mask(qiT, wiT, kk_s, nb=nbs, tq=tq_s, qpos0=past, n_valid=past + dec,
                          n_sel=n_sel_s)
        k_s, vT_s = _kv_prep(cache_k_a[l].reshape(nbs, past, WIDTH), per_stream(ka),
                             cache_v_a[l].reshape(nbs, past, WIDTH), per_stream(va), dec)
        oa = _dsa_attn(qaT, k_s, vT_s, maskT, dsa_bias_s, nb=nbs, tq=tq_s, qpos0=past)
        kb_s, vbT_s = _kv_prep(cache_kb, per_stream(kbl), cache_vb, per_stream(vbl), dec)
        ob = _band_attn(qbT, kb_s, vbT_s, band_bias_s, nb=nbs, tq=tq_s, off=0)
        h2 = _mix_ln(h, oa, ob, w_gate, wba, wbb, wo, ln2_g[l], ln2_b[l], alpha)
        ys = _ffn_ln(h2, wi2, wo2, ln3_g[l], ln3_b[l], alpha)
        heads = lambda a: a.reshape(nbs, -1, N_HEADS, HEAD_DIM)
        st_s.append((heads(ka_n), heads(va_n), ki_n, heads(kb_all[:, -band:]),
                     heads(vb_all[:, -band:])))

    y_prompt = yp.reshape(nbp, seq, d)
    y_sample = ys.reshape(nbs, tq_s, d)[:, :dec]
    stack = lambda sts, t: jnp.stack([s[t] for s in sts])
    return (y_prompt, y_sample,
            stack(st_p, 0), stack(st_p, 1), stack(st_p, 2), stack(st_p, 3), stack(st_p, 4),
            stack(st_s, 0), stack(st_s, 1), stack(st_s, 2), stack(st_s, 3), stack(st_s, 4))
```

```python
import functools
import math

import jax
import jax.numpy as jnp
from jax import lax
from jax.experimental import pallas as pl
from jax.experimental.pallas import tpu as pltpu

F32 = jnp.float32
BF16 = jnp.bfloat16
I32 = jnp.int32

CHUNK = 64
CHUNK_SHIFT = 6
HEAD_DIM = 64
N_HEADS = 8
WIDTH = N_HEADS * HEAD_DIM
N_HEADS_IDX = 4
IDX_DIM = 64
TOPK_MAX = 256
IDX_SCALE = (IDX_DIM * N_HEADS_IDX) ** -0.5
BAND_CHUNKS = 8
REL_BACK = 128
T5_BUCKETS = 32
T5_MAX_DIST = 128
LN_EPS = 1e-5
MASKED = -2.0 ** 100
LOG2E = math.log2(math.e)
V_ROWS = HEAD_DIM + 16
INT_MIN = -2 ** 31
LANES = 128
SUBLANES = 8

ATT_TILE = 256
TOKEN_TILE = 512
FF_CHUNK = 256
VMEM_LIMIT_MB = 56


def _cparams(sem):
    return pltpu.CompilerParams(dimension_semantics=sem,
                                vmem_limit_bytes=VMEM_LIMIT_MB * 1024 * 1024)


def _const_spec(shape):
    nd = len(shape)
    return pl.BlockSpec(shape, lambda *_: (0,) * nd, pipeline_mode=pl.Buffered(1))


def _tree_sum(xs):
    return _tree_reduce(lambda a, b: a + b, xs)


def _tree_reduce(fn, xs):
    xs = list(xs)
    while len(xs) > 1:
        nxt = [fn(xs[a], xs[a + 1]) for a in range(0, len(xs) - 1, 2)]
        if len(xs) % 2:
            nxt.append(xs[-1])
        xs = nxt
    return xs[0]


def _layer_norm_rows(y, g, b):
    mu = jnp.mean(y, axis=-1, keepdims=True)
    d = y - mu
    var = jnp.mean(d * d, axis=-1, keepdims=True)
    return d * lax.rsqrt(var + LN_EPS) * g + b


def _ffn_ln_kernel(x_ref, wi_ref, wo_ref, g_ref, b_ref, o_ref, *, alpha, dff):
    x = x_ref[...]
    xb = x.astype(BF16)
    acc = jnp.zeros(x.shape, F32)
    for c in range(dff // FF_CHUNK):
        lo = c * FF_CHUNK
        a = jnp.dot(xb, wi_ref[:, lo:lo + FF_CHUNK], preferred_element_type=F32)
        u = jnp.dot(xb, wi_ref[:, dff + lo:dff + lo + FF_CHUNK], preferred_element_type=F32)
        hm = (a * jax.nn.sigmoid(a)) * u
        acc = acc + jnp.dot(hm.astype(BF16), wo_ref[lo:lo + FF_CHUNK, :],
                            preferred_element_type=F32)
    y = alpha * x + 0.5 * acc
    o_ref[...] = _layer_norm_rows(y, g_ref[...], b_ref[...])


def _ffn_ln(x, wi, wo, g, b, alpha):
    n, d = x.shape
    dff = wo.shape[0]
    tm = TOKEN_TILE
    return pl.pallas_call(
        functools.partial(_ffn_ln_kernel, alpha=alpha, dff=dff),
        out_shape=jax.ShapeDtypeStruct((n, d), F32),
        grid=(n // tm,),
        in_specs=[pl.BlockSpec((tm, d), lambda i: (i, 0)),
                  _const_spec((d, 2 * dff)), _const_spec((dff, d)),
                  _const_spec((1, d)), _const_spec((1, d))],
        out_specs=pl.BlockSpec((tm, d), lambda i: (i, 0)),
        compiler_params=_cparams(("parallel",)),
        name="ffn_ln",
    )(x, wi, wo, g.reshape(1, d), b.reshape(1, d))


_C_QA, _C_KA, _C_VA, _C_QI, _C_KI, _C_WI, _C_QB, _C_KB, _C_VB, _C_END = (
    0, 512, 1024, 1536, 1792, 1920, 2048, 2560, 3072, 3584)


def _value_slabs(zt):
    n = zt.shape[1]
    ones_slab = jnp.where(lax.broadcasted_iota(I32, (V_ROWS - HEAD_DIM, n), 0) == 0, 1.0, 0.0)
    return jnp.concatenate([piece for h in range(N_HEADS)
                            for piece in (zt[h * HEAD_DIM:(h + 1) * HEAD_DIM, :], ones_slab)], axis=0)


def _inproj_kernel(h_ref, w_ref, qaT_ref, ka_ref, kab_ref, va_ref, vaT_ref, qiT_ref, ki_ref,
                   kk_ref, wiT_ref, qbT_ref, kbb_ref, vbT_ref, kbl_ref, vbl_ref, *, per):
    hb = h_ref[...].astype(BF16)

    def proj(c0, c1):
        return jnp.dot(hb, w_ref[:, c0:c1], preferred_element_type=F32)

    def put_tiles(ref, zt):
        aug = _value_slabs(zt)
        for c in range(ref.shape[0]):
            ref[c] = aug[:, c * ATT_TILE:(c + 1) * ATT_TILE].astype(ref.dtype)

    scale = HEAD_DIM ** -0.5 * LOG2E
    qaT_ref[...] = (proj(_C_QA, _C_KA) * scale).T.astype(BF16)
    ka = proj(_C_KA, _C_VA)
    ka_ref[...] = pltpu.einshape("m(hd)->mhd", ka, h=N_HEADS)
    kab_ref[...] = ka.astype(BF16)
    va = proj(_C_VA, _C_QI)
    va_ref[...] = pltpu.einshape("m(hd)->mhd", va, h=N_HEADS)
    put_tiles(vaT_ref, va.T)
    qiT_ref[...] = proj(_C_QI, _C_KI).T.astype(BF16)
    kk = proj(_C_KI, _C_WI)
    ki_ref[...] = kk[:, :IDX_DIM]
    kk_ref[...] = kk.astype(BF16)
    wiT_ref[...] = proj(_C_WI, _C_QB).T[:8, :]
    qbT_ref[...] = (proj(_C_QB, _C_KB) * scale).T.astype(BF16)
    kb = proj(_C_KB, _C_VB)
    kbb_ref[...] = kb.astype(BF16)
    vb = proj(_C_VB, _C_END)
    put_tiles(vbT_ref, vb.T)

    @pl.when(pl.program_id(0) % per == per - 1)
    def _():
        kbl_ref[...] = kb
        vbl_ref[...] = vb


def _in_proj(h, w, per):
    n, d = h.shape
    tm = TOKEN_TILE
    nt = n // tm
    sub = tm // ATT_TILE
    row = lambda width: pl.BlockSpec((tm, width), lambda i: (i, 0))
    col = lambda rows: pl.BlockSpec((rows, tm), lambda i: (0, i))
    til = pl.BlockSpec((sub, N_HEADS * V_ROWS, ATT_TILE), lambda i: (i, 0, 0))
    last = pl.BlockSpec((tm, WIDTH), lambda i: (i // per, 0))
    hds = pl.BlockSpec((tm, N_HEADS, HEAD_DIM), lambda i: (i, 0, 0))
    sds = jax.ShapeDtypeStruct
    out_shape = (
        sds((WIDTH, n), BF16),
        sds((n, N_HEADS, HEAD_DIM), F32), sds((n, WIDTH), BF16),
        sds((n, N_HEADS, HEAD_DIM), F32),
        sds((n // ATT_TILE, N_HEADS * V_ROWS, ATT_TILE), BF16),
        sds((N_HEADS_IDX * IDX_DIM, n), BF16),
        sds((n, IDX_DIM), F32), sds((n, LANES), BF16),
        sds((8, n), F32),
        sds((WIDTH, n), BF16),
        sds((n, WIDTH), BF16),
        sds((n // ATT_TILE, N_HEADS * V_ROWS, ATT_TILE), BF16),
        sds((nt // per * tm, WIDTH), F32), sds((nt // per * tm, WIDTH), F32),
    )
    out_specs = (col(WIDTH), hds, row(WIDTH), hds, til,
                 col(N_HEADS_IDX * IDX_DIM), row(IDX_DIM), row(LANES), col(SUBLANES),
                 col(WIDTH), row(WIDTH), til, last, last)
    return pl.pallas_call(
        functools.partial(_inproj_kernel, per=per),
        out_shape=out_shape,
        grid=(nt,),
        in_specs=[pl.BlockSpec((tm, d), lambda i: (i, 0)), _const_spec((d, _C_END))],
        out_specs=out_specs,
        compiler_params=_cparams(("arbitrary",)),
        name="in_proj",
    )(h, w)


def _bit_planes(words):
    a = list(reversed(words))
    j, m = 16, 0x0000FFFF
    while j:
        k = 0
        while k < 32:
            t = (a[k] ^ lax.shift_right_logical(a[k + j], j)) & m
            a[k] = a[k] ^ t
            a[k + j] = a[k + j] ^ lax.shift_left(t, j)
            k = (k + j + 1) & ~j
        j >>= 1
        m = (m ^ (m << j)) & 0xFFFFFFFF
    return a


def _idx_mask_kernel(qiT_ref, wiT_ref, kk_ref, mask_ref, st_ref, p_ref, c_ref, d_ref, incl_ref,
                     *, tq, qpos0, n_valid, n_sel):
    tk = ATT_TILE
    assert tk == 32 * 8
    nkt_total = st_ref.shape[0] // tk

    @pl.when((pl.program_id(0) == 0) & (pl.program_id(1) == 0))
    def _():
        p_ref[...] = jnp.zeros(p_ref.shape, I32)

    q0 = qpos0 + pl.program_id(1) * tq
    n_keys = jnp.minimum(q0 + tq, n_valid)
    nt = jnp.minimum(lax.shift_right_logical(n_keys + tk - 1, int(math.log2(tk))), nkt_total)

    lane = lax.broadcasted_iota(I32, (8, tq), 1)
    q_chunk = lax.shift_right_logical(q0 + lane, CHUNK_SHIFT)
    n_adm = jnp.minimum((q_chunk + 1) * CHUNK, n_valid)
    k_row = jnp.minimum(n_adm, n_sel)

    zeros = jnp.zeros((LANES - IDX_DIM, tq), BF16)
    qh = [jnp.concatenate([qiT_ref[h * IDX_DIM:(h + 1) * IDX_DIM, :], zeros], axis=0)
          for h in range(N_HEADS_IDX)]
    scale_exact = math.frexp(IDX_SCALE)[0] == 0.5
    wh = [wiT_ref[h:h + 1, :] * IDX_SCALE if scale_exact else wiT_ref[h:h + 1, :]
          for h in range(N_HEADS_IDX)]

    def dots(j, h):
        r0 = pl.multiple_of(j * tk, tk)
        d_ref[h] = jnp.dot(kk_ref[pl.ds(r0, tk), :], qh[h], preferred_element_type=F32)

    def score_tile(j, last):
        r0 = pl.multiple_of(j * tk, tk)
        sc = None
        for h in range(N_HEADS_IDX):
            d = d_ref[h]
            if not last:
                dots(j + 1, h)
            term = wh[h] * jnp.maximum(d, 0.0)
            sc = term if sc is None else sc + term
        if not scale_exact:
            sc = sc * IDX_SCALE
        bits = lax.bitcast_convert_type(sc, I32)
        sign = lax.shift_right_arithmetic(bits, 31)
        key = (bits ^ (sign & 0x7FFFFFFF)) - sign
        if last:
            kpos = r0 + lax.broadcasted_iota(I32, (tk, tq), 0)
            key = jnp.where(kpos < n_adm[0:1, :], key, INT_MIN)
        st_ref[pl.ds(r0, tk), :] = key
        planes = _bit_planes([key[v * 8:(v + 1) * 8, :] for v in range(32)])
        p_ref[0, j] = planes[0] ^ -1
        for i in range(1, 32):
            p_ref[i, j] = planes[i]

    for h in range(N_HEADS_IDX):
        dots(0, h)

    def score_run(first, count):
        for c in range(count):
            score_tile(first + c, False)

    def score_quad(u, carry):
        score_run(4 * u, 4)
        return carry

    n_open = nt - 1
    n_quads = lax.shift_right_logical(n_open, 2)
    lax.fori_loop(0, n_quads, score_quad, 0)
    done = 4 * n_quads
    for count in (2, 1):
        @pl.when((n_open & count) == count)
        def _(count=count, done=done):
            score_run(done, count)
        done = done + (n_open & count)

    score_tile(nt - 1, True)

    group = incl_ref.shape[0]
    n_groups = nt if group == 1 else lax.shift_right_logical(nt + 3, 2)
    for g in range(group - 1):
        @pl.when(nt + g < n_groups * group)
        def _(g=g):
            r0 = pl.multiple_of((nt + g) * tk, tk)
            st_ref[pl.ds(r0, tk), :] = jnp.full((tk, tq), INT_MIN, I32)

    limit = n_adm - (nt - 1) * tk
    n_low = jnp.clip(lax.shift_right_arithmetic(limit - lax.broadcasted_iota(I32, (8, tq), 0) + 7, 3),
                     0, 32)
    last_word = jnp.where(n_low >= 32, -1, lax.shift_left(1, jnp.minimum(n_low, 31)) - 1)
    for jj in range(nkt_total):
        c_ref[jj] = jnp.where(jj < nt - 1, -1, jnp.where(jj == nt - 1, last_word, 0))

    def radix_select(n_scan):
        def radix_pass(i, carry):
            want, t = carry
            n_set = _tree_sum([lax.population_count(c_ref[jj] & p_ref[i, jj]) for jj in range(n_scan)])
            n_set = jnp.broadcast_to(jnp.sum(n_set, axis=0, keepdims=True), (8, tq))
            keep_set = n_set >= want
            flip = jnp.where(keep_set, 0, -1)
            for jj in range(n_scan):
                c_ref[jj] = c_ref[jj] & (p_ref[i, jj] ^ flip)
            return (jnp.where(keep_set, want, want - n_set),
                    t | jnp.where(keep_set, lax.shift_left(jnp.int32(1), 31 - i), 0))

        return lax.fori_loop(0, 32, radix_pass, (k_row, jnp.zeros((8, tq), I32)))

    if nkt_total >= 16 and nkt_total % 2 == 0:
        want, t_bits = lax.cond(nt <= nkt_total // 2, lambda: radix_select(nkt_total // 2),
                                lambda: radix_select(nkt_total))
    else:
        want, t_bits = radix_select(nkt_total)
    t_row = (t_bits ^ INT_MIN)[0:1, :]
    ties_wanted = want[0:1, :].astype(F32)

    tri = jnp.where(lax.broadcasted_iota(I32, (tk, tk), 0) >= lax.broadcasted_iota(I32, (tk, tk), 1),
                    1.0, 0.0).astype(BF16)

    def tie_prefix(j, g):
        r0 = pl.multiple_of(j * tk, tk)
        eq = jnp.where(st_ref[pl.ds(r0, tk), :] == t_row, 1.0, 0.0)
        incl_ref[g] = jnp.dot(tri, eq.astype(BF16), preferred_element_type=F32)

    def final_group(u, seen, last):
        for g in range(group):
            j = u * group + g
            r0 = pl.multiple_of(j * tk, tk)
            incl = incl_ref[g]
            if not last:
                tie_prefix(j + group, g)
            rank = seen + incl
            bar = t_row + jnp.where(rank <= ties_wanted, 0, 1)
            sel = jnp.where(st_ref[pl.ds(r0, tk), :] >= bar, 1, 0)
            mask_ref[pl.ds(r0, tk), :] = sel.astype(jnp.int8)
            seen = seen + incl[tk - 1:tk, :]
        return seen

    for g in range(group):
        tie_prefix(g, g)
    seen = lax.fori_loop(0, n_groups - 1, lambda u, seen: final_group(u, seen, False),
                         jnp.zeros((1, tq), F32))
    final_group(n_groups - 1, seen, True)

    def zero_tile(j, carry):
        r0 = pl.multiple_of(j * tk, tk)
        mask_ref[pl.ds(r0, tk), :] = jnp.zeros((tk, tq), jnp.int8)
        return carry

    lax.fori_loop(n_groups * group, nkt_total, zero_tile, 0)


def _idx_mask(qiT, wiT, kk, *, nb, tq, qpos0, n_valid, n_sel):
    tkp = kk.shape[1]
    nq = qiT.shape[1] // nb // tq
    return pl.pallas_call(
        functools.partial(_idx_mask_kernel, tq=tq, qpos0=qpos0, n_valid=n_valid, n_sel=n_sel),
        out_shape=jax.ShapeDtypeStruct((nb, tkp, nq * tq), jnp.int8),
        grid=(nb, nq),
        in_specs=[pl.BlockSpec((N_HEADS_IDX * IDX_DIM, tq), lambda b, i: (0, b * nq + i)),
                  pl.BlockSpec((8, tq), lambda b, i: (0, b * nq + i)),
                  pl.BlockSpec((None, tkp, LANES), lambda b, i: (b, 0, 0))],
        out_specs=pl.BlockSpec((None, tkp, tq), lambda b, i: (b, 0, i)),
        scratch_shapes=[pltpu.VMEM((tkp, tq), I32),
                        pltpu.VMEM((32, tkp // ATT_TILE, 8, tq), I32),
                        pltpu.VMEM((tkp // ATT_TILE, 8, tq), I32),
                        pltpu.VMEM((N_HEADS_IDX, ATT_TILE, tq), F32),
                        pltpu.VMEM((4 if (tkp // ATT_TILE) % 4 == 0 else 1, ATT_TILE, tq), F32)],
        compiler_params=_cparams(("arbitrary", "arbitrary")),
        name="idx_mask",
    )(qiT, wiT, kk)


def _load_qsel(qT_ref, qsel_ref, tq):
    zeros = jnp.zeros((HEAD_DIM, tq), BF16)
    for h in range(N_HEADS):
        blk = qT_ref[h * HEAD_DIM:(h + 1) * HEAD_DIM, :]
        qsel_ref[h] = jnp.concatenate([blk, zeros] if h % 2 == 0 else [zeros, blk], axis=0)


def _init_state(m_ref, acc_ref):
    m_ref[...] = jnp.full(m_ref.shape, MASKED, F32)
    acc_ref[...] = jnp.zeros(acc_ref.shape, F32)


def _scores(k_slab, qsel_ref, s_ref, h):
    s_ref[h] = jnp.dot(k_slab(h // 2), qsel_ref[h], preferred_element_type=F32)


def _heads_tile(v_rows, next_k_slab, qsel_ref, s_ref, m_ref, acc_ref, addend):
    tq = s_ref.shape[2]
    slabs = s_ref.shape[1] // 16
    for h in range(N_HEADS):
        s = s_ref[h].astype(BF16)
        for a in addend(h):
            s = s + a
        if next_k_slab is not None:
            _scores(next_k_slab, qsel_ref, s_ref, h)
        m_old = m_ref[h:h + 1, :]
        top = _tree_reduce(jnp.maximum, [s[r * 16:(r + 1) * 16, :] for r in range(slabs)])
        m_new = jnp.maximum(m_old, jnp.max(top.astype(F32), axis=0, keepdims=True))
        alpha = jnp.exp2(m_old - m_new)
        m16 = jnp.broadcast_to(m_new, (16, tq)).astype(BF16)
        p = jnp.concatenate([jnp.exp2(s[r * 16:(r + 1) * 16, :] - m16) for r in range(slabs)], axis=0)
        m_ref[h:h + 1, :] = m_new
        pv = jnp.dot(v_rows(h), p, preferred_element_type=F32)
        rows = slice(h * V_ROWS, (h + 1) * V_ROWS)
        acc_ref[rows, :] = alpha * acc_ref[rows, :] + pv


def _attn_scratch(tq):
    return [pltpu.VMEM((N_HEADS, LANES, tq), BF16), pltpu.VMEM((N_HEADS, ATT_TILE, tq), F32),
            pltpu.VMEM((N_HEADS, tq), F32), pltpu.VMEM((N_HEADS * V_ROWS, tq), F32)]


def _store_out(o_ref, acc_ref):
    outs = [acc_ref[h * V_ROWS:h * V_ROWS + HEAD_DIM, :]
            / acc_ref[h * V_ROWS + HEAD_DIM:h * V_ROWS + HEAD_DIM + 1, :] for h in range(N_HEADS)]
    o_ref[...] = jnp.concatenate(outs, axis=0).T.astype(o_ref.dtype)


def _dsa_kernel(qT_ref, k_ref, vT_ref, mask_ref, bias_ref, o_ref, qsel_ref, s_ref, m_ref, acc_ref,
                *, tq, qpos0):
    tk = ATT_TILE
    jd = lax.shift_right_logical(qpos0 + pl.program_id(1) * tq, int(math.log2(tk)))
    _load_qsel(qT_ref, qsel_ref, tq)
    _init_state(m_ref, acc_ref)

    def k_slab(j):
        r0 = pl.multiple_of(j * tk, tk)
        return lambda g: k_ref[pl.ds(r0, tk), g * LANES:(g + 1) * LANES]

    def tile(j, near, last=False):
        r0 = pl.multiple_of(j * tk, tk)
        unselected = jnp.where(mask_ref[pl.ds(r0, tk), :].astype(I32) != 0, 0.0,
                               MASKED).astype(BF16)
        addend = ((lambda h: (unselected,)) if near is None
                  else (lambda h: (unselected, bias_ref[near, h])))
        _heads_tile(lambda h: vT_ref[j, h * V_ROWS:(h + 1) * V_ROWS, :],
                    None if last else k_slab(j + 1),
                    qsel_ref, s_ref, m_ref, acc_ref, addend)

    for h in range(N_HEADS):
        _scores(k_slab(0), qsel_ref, s_ref, h)

    n_far = jnp.maximum(jd - 1, 0)

    def far_run(first, count):
        for c in range(count):
            tile(first + c, None)

    def far_octet(u, carry):
        far_run(8 * u, 8)
        return carry

    n_octets = lax.shift_right_logical(n_far, 3)
    lax.fori_loop(0, n_octets, far_octet, 0)
    done = 8 * n_octets
    for count in (4, 2, 1):
        @pl.when((n_far & count) == count)
        def _(count=count, done=done):
            far_run(done, count)
        done = done + (n_far & count)

    @pl.when(jd >= 1)
    def _():
        tile(jd - 1, 0)

    tile(jd, 1, last=True)
    _store_out(o_ref, acc_ref)


def _dsa_attn(qT, k, vT, maskT, bias, *, nb, tq, qpos0):
    tkp = k.shape[1]
    nkt = tkp // ATT_TILE
    nq = qT.shape[1] // nb // tq
    return pl.pallas_call(
        functools.partial(_dsa_kernel, tq=tq, qpos0=qpos0),
        out_shape=jax.ShapeDtypeStruct((nb * nq * tq, WIDTH), BF16),
        grid=(nb, nq),
        in_specs=[pl.BlockSpec((WIDTH, tq), lambda b, i: (0, b * nq + i)),
                  pl.BlockSpec((None, tkp, WIDTH), lambda b, i: (b, 0, 0),
                               pipeline_mode=pl.Buffered(1)),
                  pl.BlockSpec((nkt, N_HEADS * V_ROWS, ATT_TILE), lambda b, i: (b, 0, 0),
                               pipeline_mode=pl.Buffered(1)),
                  pl.BlockSpec((None, tkp, tq), lambda b, i: (b, 0, i)),
                  _const_spec(bias.shape)],
        out_specs=pl.BlockSpec((tq, WIDTH), lambda b, i: (b * nq + i, 0)),
        scratch_shapes=_attn_scratch(tq),
        compiler_params=_cparams(("parallel", "parallel")),
        name="dsa_attn",
    )(qT, k, vT, maskT, bias)


_BAND_TILES = BAND_CHUNKS * CHUNK // ATT_TILE + 1


def _band_kernel(qT_ref, *refs, tq, off):
    k_refs = refs[:_BAND_TILES]
    v_refs = refs[_BAND_TILES:2 * _BAND_TILES]
    bias_ref, o_ref, qsel_ref, s_ref, m_ref, acc_ref = refs[2 * _BAND_TILES:]
    _load_qsel(qT_ref, qsel_ref, tq)
    _init_state(m_ref, acc_ref)
    k_slab = lambda w: (lambda g: k_refs[w][:, g * LANES:(g + 1) * LANES])
    for h in range(N_HEADS):
        _scores(k_slab(0), qsel_ref, s_ref, h)
    for w in range(_BAND_TILES):
        entry = jnp.where(pl.program_id(1) + off + w >= 0, w, _BAND_TILES)
        _heads_tile(lambda h: v_refs[w][0, h * V_ROWS:(h + 1) * V_ROWS, :],
                    None if w == _BAND_TILES - 1 else k_slab(w + 1),
                    qsel_ref, s_ref, m_ref, acc_ref, lambda h: (bias_ref[entry, h],))
    _store_out(o_ref, acc_ref)


def _band_attn(qT, k, vT, bias, *, nb, tq, off):
    nkt = k.shape[1] // ATT_TILE
    nq = qT.shape[1] // nb // tq
    tile_of = lambda i, w: jnp.maximum(i + off + w, 0)
    k_specs = [pl.BlockSpec((None, ATT_TILE, WIDTH), lambda b, i, w=w: (b, tile_of(i, w), 0))
               for w in range(_BAND_TILES)]
    v_specs = [pl.BlockSpec((1, N_HEADS * V_ROWS, ATT_TILE),
                            lambda b, i, w=w: (b * nkt + tile_of(i, w), 0, 0))
               for w in range(_BAND_TILES)]
    return pl.pallas_call(
        functools.partial(_band_kernel, tq=tq, off=off),
        out_shape=jax.ShapeDtypeStruct((nb * nq * tq, WIDTH), BF16),
        grid=(nb, nq),
        in_specs=[pl.BlockSpec((WIDTH, tq), lambda b, i: (0, b * nq + i))] + k_specs + v_specs
                 + [_const_spec(bias.shape)],
        out_specs=pl.BlockSpec((tq, WIDTH), lambda b, i: (b * nq + i, 0)),
        scratch_shapes=_attn_scratch(tq),
        compiler_params=_cparams(("parallel", "parallel")),
        name="band_attn",
    )(qT, *([k] * _BAND_TILES), *([vT] * _BAND_TILES), bias)


def _mix_ln_kernel(h_ref, oa_ref, ob_ref, wg_ref, wba_ref, wbb_ref, wo_ref, g_ref, b_ref, o_ref,
                   *, alpha):
    h = h_ref[...]
    hb = h.astype(BF16)
    d = h.shape[1]
    ga = jnp.dot(hb, wg_ref[:, :d], preferred_element_type=F32)
    gb = jnp.dot(hb, wg_ref[:, d:], preferred_element_type=F32)
    ya = jnp.dot(oa_ref[...], wba_ref[...], preferred_element_type=F32)
    yb = jnp.dot(ob_ref[...], wbb_ref[...], preferred_element_type=F32)
    gated = jax.nn.sigmoid(ga) * ya + jax.nn.sigmoid(gb) * yb
    mix = jnp.dot(gated.astype(BF16), wo_ref[...], preferred_element_type=F32)
    o_ref[...] = _layer_norm_rows(alpha * h + mix, g_ref[...], b_ref[...])


def _mix_ln(h, oa, ob, wg, wba, wbb, wo, g, b, alpha):
    n, d = h.shape
    tm = TOKEN_TILE
    row = lambda width: pl.BlockSpec((tm, width), lambda i: (i, 0))
    return pl.pallas_call(
        functools.partial(_mix_ln_kernel, alpha=alpha),
        out_shape=jax.ShapeDtypeStruct((n, d), F32),
        grid=(n // tm,),
        in_specs=[row(d), row(WIDTH), row(WIDTH), _const_spec(wg.shape), _const_spec(wba.shape),
                  _const_spec(wbb.shape), _const_spec(wo.shape), _const_spec((1, d)),
                  _const_spec((1, d))],
        out_specs=row(d),
        compiler_params=_cparams(("parallel",)),
        name="mix_ln",
    )(h, oa, ob, wg, wba, wbb, wo, g.reshape(1, d), b.reshape(1, d))


def _t5_bucket(rel):
    half = T5_BUCKETS // 2
    exact = half // 2
    n = jnp.abs(rel)
    log_ratio = jnp.log(jnp.maximum(n, 1).astype(F32) / exact) / math.log(T5_MAX_DIST / exact)
    large = jnp.minimum(exact + (log_ratio * (half - exact)).astype(I32), half - 1)
    return (rel > 0).astype(I32) * half + jnp.where(n < exact, n, large)


def _rel_line(n_tiles, first, tq):
    period = ATT_TILE + tq
    y = jnp.arange(period, dtype=I32)[None, :]
    c = (jnp.arange(n_tiles, dtype=I32)[:, None] + first) * ATT_TILE
    return jnp.where(y < tq, c - y, c + period - y)


def _toeplitz(v, tq):
    period = v.shape[-1]
    flat = jnp.tile(v, (1,) * (v.ndim - 1) + (ATT_TILE,))[..., :ATT_TILE * (period - 1)]
    return flat.reshape(v.shape[:-1] + (ATT_TILE, period - 1))[..., :tq]


def _dsa_bias_table(t5_bias, tq):
    assert T5_MAX_DIST <= ATT_TILE + 1
    far = t5_bias[:, _t5_bucket(jnp.int32(-(ATT_TILE + 1)))]
    line = t5_bias[:, _t5_bucket(_rel_line(2, -1, tq))] - far[:, None, None]
    return (_toeplitz(jnp.moveaxis(line, 0, 1), tq) * LOG2E).astype(BF16)


def _band_bias_table(rel_bias, tq):
    first = 1 - _BAND_TILES
    ridx = jnp.clip(_rel_line(_BAND_TILES, first, tq), -REL_BACK, CHUNK - 1) + REL_BACK
    bias = _toeplitz(jnp.moveaxis(rel_bias[:, ridx], 0, 1), tq)
    w = jnp.arange(_BAND_TILES, dtype=I32)[:, None, None]
    j = jnp.arange(ATT_TILE, dtype=I32)[None, :, None]
    i = jnp.arange(tq, dtype=I32)[None, None, :]
    dchunk = ((w + first) * ATT_TILE + j) // CHUNK - i // CHUNK
    ok = (dchunk <= 0) & (dchunk >= -BAND_CHUNKS)
    table = jnp.where(ok[:, None], bias * LOG2E, MASKED)
    return jnp.concatenate([table, jnp.full_like(table[:1], MASKED)], axis=0).astype(BF16)


def _regroup_w_in(w_in):
    d, n_in = w_in.shape
    c_ki = 3 * WIDTH + N_HEADS_IDX * IDX_DIM
    c_wi = c_ki + IDX_DIM
    c_qb = c_wi + N_HEADS_IDX
    c_gate = c_qb + 3 * WIDTH
    rows = LANES

    def regroup(w_ref, proj_ref, gate_ref):
        w = w_ref[...]
        ki = w[:, c_ki:c_wi]
        wi_pad = jnp.concatenate([w[:, c_wi:c_qb], jnp.zeros((rows, LANES - N_HEADS_IDX), F32)], axis=1)
        proj_ref[...] = jnp.concatenate([w[:, :c_ki], ki, ki, wi_pad, w[:, c_qb:c_gate]],
                                        axis=1).astype(BF16)
        gate_ref[...] = w[:, c_gate:].astype(BF16)

    return pl.pallas_call(
        regroup,
        out_shape=(jax.ShapeDtypeStruct((d, _C_END), BF16),
                   jax.ShapeDtypeStruct((d, n_in - c_gate), BF16)),
        grid=(d // rows,),
        in_specs=[pl.BlockSpec((rows, n_in), lambda i: (i, 0))],
        out_specs=(pl.BlockSpec((rows, _C_END), lambda i: (i, 0)),
                   pl.BlockSpec((rows, n_in - c_gate), lambda i: (i, 0))),
        compiler_params=_cparams(("parallel",)),
        name="regroup_w_in",
    )(w_in)


def _kv_prep_kernel(ck_ref, nk_ref, cv_ref, nv_ref, k_ref, vT_ref, *, dec):
    n_cache_tiles = ck_ref.shape[0] // ATT_TILE

    def fresh(n_ref):
        rows = lax.broadcasted_iota(I32, n_ref.shape, 0)
        new = jnp.where(rows < dec, n_ref[...], 0.0)
        return jnp.concatenate([new, jnp.zeros((ATT_TILE - new.shape[0], WIDTH), F32)], axis=0)

    for t in range(n_cache_tiles + 1):
        rows = slice(t * ATT_TILE, (t + 1) * ATT_TILE)
        k_tile = ck_ref[rows, :] if t < n_cache_tiles else fresh(nk_ref)
        v_tile = cv_ref[rows, :] if t < n_cache_tiles else fresh(nv_ref)
        k_ref[rows, :] = k_tile.astype(BF16)
        vT_ref[t] = _value_slabs(v_tile.T).astype(BF16)


def _kv_prep(cache_k, new_k, cache_v, new_v, dec):
    nb, lc, _ = cache_k.shape
    tq = new_k.shape[1]
    nct = lc // ATT_TILE
    cache_spec = pl.BlockSpec((None, lc, WIDTH), lambda b: (b, 0, 0))
    new_spec = pl.BlockSpec((None, tq, WIDTH), lambda b: (b, 0, 0))
    return pl.pallas_call(
        functools.partial(_kv_prep_kernel, dec=dec),
        out_shape=(jax.ShapeDtypeStruct((nb, lc + ATT_TILE, WIDTH), BF16),
                   jax.ShapeDtypeStruct((nb * (nct + 1), N_HEADS * V_ROWS, ATT_TILE), BF16)),
        grid=(nb,),
        in_specs=[cache_spec, new_spec, cache_spec, new_spec],
        out_specs=(pl.BlockSpec((None, lc + ATT_TILE, WIDTH), lambda b: (b, 0, 0)),
                   pl.BlockSpec((nct + 1, N_HEADS * V_ROWS, ATT_TILE), lambda b: (b, 0, 0))),
        compiler_params=_cparams(("parallel",)),
        name="kv_prep",
    )(cache_k, new_k, cache_v, new_v)


def _pad_keys(x, tkp):
    return jnp.pad(x, ((0, 0), (0, tkp - x.shape[1]), (0, 0)))


def kernel(x_prompt, x_sample, cache_k_a, cache_v_a, cache_kidx_a, cache_k_b, cache_v_b, t5_bias,
           ln1_g, ln1_b, ffn1_wi, ffn1_wo, ln2_g, ln2_b, w_in, rel_bias_b, w_branch_a, w_branch_b,
           w_out, ln3_g, ln3_b, ffn2_wi, ffn2_wo):
    depth = ln1_g.shape[0]
    alpha = (2.0 * depth) ** 0.25
    nbp, seq, d = x_prompt.shape
    nbs, dec, _ = x_sample.shape
    past = cache_k_a.shape[2]
    band = cache_k_b.shape[2]
    keep = min(BAND_CHUNKS * CHUNK, seq)
    tq_p, tq_s = ATT_TILE, LANES
    assert seq % TOKEN_TILE == 0 and keep == TOKEN_TILE and dec <= tq_s and past % ATT_TILE == 0
    assert band == BAND_CHUNKS * CHUNK and (nbs * tq_s) % TOKEN_TILE == 0

    yp = x_prompt.reshape(nbp * seq, d)
    ys = jnp.pad(x_sample, ((0, 0), (0, tq_s - dec), (0, 0))).reshape(nbs * tq_s, d)
    dsa_bias_p = _dsa_bias_table(t5_bias, tq_p)
    dsa_bias_s = dsa_bias_p[..., :tq_s]
    n_sel_p = min(TOPK_MAX, seq // 4)
    n_sel_s = min(TOPK_MAX, (past + dec) // 4)
    tk_s = past + ATT_TILE
    assert band + ATT_TILE == _BAND_TILES * ATT_TILE

    st_p, st_s = [], []
    for l in range(depth):
        w_proj, w_gate = _regroup_w_in(w_in[l])
        wi1, wo1 = ffn1_wi[l].astype(BF16), ffn1_wo[l].astype(BF16)
        wi2, wo2 = ffn2_wi[l].astype(BF16), ffn2_wo[l].astype(BF16)
        wba, wbb, wo = (w_branch_a[l].astype(BF16), w_branch_b[l].astype(BF16),
                        w_out[l].astype(BF16))
        band_bias_p = _band_bias_table(rel_bias_b[l], tq_p)
        band_bias_s = band_bias_p[..., :tq_s]

        h = _ffn_ln(yp, wi1, wo1, ln1_g[l], ln1_b[l], alpha)
        (qaT, ka, kab, va, vaT, qiT, ki, kk, wiT, qbT, kbb, vbT, kbl, vbl) = _in_proj(
            h, w_proj, seq // TOKEN_TILE)
        maskT = _idx_mask(qiT, wiT, kk.reshape(nbp, seq, LANES), nb=nbp, tq=tq_p, qpos0=0,
                          n_valid=seq, n_sel=n_sel_p)
        oa = _dsa_attn(qaT, kab.reshape(nbp, seq, WIDTH), vaT, maskT, dsa_bias_p,
                       nb=nbp, tq=tq_p, qpos0=0)
        ob = _band_attn(qbT, kbb.reshape(nbp, seq, WIDTH), vbT, band_bias_p,
                        nb=nbp, tq=tq_p, off=1 - _BAND_TILES)
        h2 = _mix_ln(h, oa, ob, w_gate, wba, wbb, wo, ln2_g[l], ln2_b[l], alpha)
        yp = _ffn_ln(h2, wi2, wo2, ln3_g[l], ln3_b[l], alpha)
        st_p.append((ka.reshape(nbp, seq, N_HEADS, HEAD_DIM), va.reshape(nbp, seq, N_HEADS, HEAD_DIM),
                     ki.reshape(nbp, seq, IDX_DIM), kbl.reshape(nbp, keep, N_HEADS, HEAD_DIM),
                     vbl.reshape(nbp, keep, N_HEADS, HEAD_DIM)))

        h = _ffn_ln(ys, wi1, wo1, ln1_g[l], ln1_b[l], alpha)
        (qaT, ka, _, va, _, qiT, ki, _, wiT, qbT, _, _, kbl, vbl) = _in_proj(h, w_proj, 1)
        per_stream = lambda a: a.reshape(nbs, tq_s, -1)
        new = lambda a: per_stream(a)[:, :dec]
        ka_n, va_n, ki_n, kb_n, vb_n = new(ka), new(va), new(ki), new(kbl), new(vbl)
        cache_kb = cache_k_b[l].reshape(nbs, band, WIDTH)
        cache_vb = cache_v_b[l].reshape(nbs, band, WIDTH)
        ki_all = jnp.concatenate([cache_kidx_a[l], ki_n], axis=1)
        kb_all = jnp.concatenate([cache_kb, kb_n], axis=1)
        vb_all = jnp.concatenate([cache_vb, vb_n], axis=1)
        kk_s = _pad_keys(jnp.concatenate([ki_all, ki_all], axis=2).astype(BF16), tk_s)
        maskT = _idx_mask(qiT, wiT, kk_s, nb=nbs, tq=tq_s, qpos0=past, n_valid=past + dec,
                          n_sel=n_sel_s)
        k_s, vT_s = _kv_prep(cache_k_a[l].reshape(nbs, past, WIDTH), per_stream(ka),
                             cache_v_a[l].reshape(nbs, past, WIDTH), per_stream(va), dec)
        oa = _dsa_attn(qaT, k_s, vT_s, maskT, dsa_bias_s, nb=nbs, tq=tq_s, qpos0=past)
        kb_s, vbT_s = _kv_prep(cache_kb, per_stream(kbl), cache_vb, per_stream(vbl), dec)
        ob = _band_attn(qbT, kb_s, vbT_s, band_bias_s, nb=nbs, tq=tq_s, off=0)
        h2 = _mix_ln(h, oa, ob, w_gate, wba, wbb, wo, ln2_g[l], ln2_b[l], alpha)
        ys = _ffn_ln(h2, wi2, wo2, ln3_g[l], ln3_b[l], alpha)
        heads = lambda a: a.reshape(nbs, -1, N_HEADS, HEAD_DIM)
        st_s.append((heads(ka_n), heads(va_n), ki_n, heads(kb_all[:, -band:]),
                     heads(vb_all[:, -band:])))

    y_prompt = yp.reshape(nbp, seq, d)
    y_sample = ys.reshape(nbs, tq_s, d)[:, :dec]
    stack = lambda sts, t: jnp.stack([s[t] for s in sts])
    return (y_prompt, y_sample,
            stack(st_p, 0), stack(st_p, 1), stack(st_p, 2), stack(st_p, 3), stack(st_p, 4),
            stack(st_s, 0), stack(st_s, 1), stack(st_s, 2), stack(st_s, 3), stack(st_s, 4))
```

```python
import functools
import math

import jax
import jax.numpy as jnp
from jax import lax
from jax.experimental import pallas as pl
from jax.experimental.pallas import tpu as pltpu

F32 = jnp.float32
BF16 = jnp.bfloat16
I32 = jnp.int32

CHUNK = 64
CHUNK_SHIFT = 6
HEAD_DIM = 64
N_HEADS = 8
WIDTH = N_HEADS * HEAD_DIM
N_HEADS_IDX = 4
IDX_DIM = 64
TOPK_MAX = 256
IDX_SCALE = (IDX_DIM * N_HEADS_IDX) ** -0.5
BAND_CHUNKS = 8
REL_BACK = 128
T5_BUCKETS = 32
T5_MAX_DIST = 128
LN_EPS = 1e-5
MASKED = -2.0 ** 100
LOG2E = math.log2(math.e)
V_ROWS = HEAD_DIM + 16
INT_MIN = -2 ** 31
LANES = 128
SUBLANES = 8

ATT_TILE = 256
TOKEN_TILE = 512
FF_CHUNK = 256
VMEM_LIMIT_MB = 56


def _cparams(sem):
    return pltpu.CompilerParams(dimension_semantics=sem,
                                vmem_limit_bytes=VMEM_LIMIT_MB * 1024 * 1024)


def _const_spec(shape):
    nd = len(shape)
    return pl.BlockSpec(shape, lambda *_: (0,) * nd, pipeline_mode=pl.Buffered(1))


def _tree_sum(xs):
    return _tree_reduce(lambda a, b: a + b, xs)


def _tree_reduce(fn, xs):
    xs = list(xs)
    while len(xs) > 1:
        nxt = [fn(xs[a], xs[a + 1]) for a in range(0, len(xs) - 1, 2)]
        if len(xs) % 2:
            nxt.append(xs[-1])
        xs = nxt
    return xs[0]


def _layer_norm_rows(y, g, b):
    mu = jnp.mean(y, axis=-1, keepdims=True)
    d = y - mu
    var = jnp.mean(d * d, axis=-1, keepdims=True)
    return d * lax.rsqrt(var + LN_EPS) * g + b


def _ffn_ln_kernel(x_ref, wi_ref, wo_ref, g_ref, b_ref, o_ref, *, alpha, dff):
    x = x_ref[...]
    xb = x.astype(BF16)
    acc = jnp.zeros(x.shape, F32)
    for c in range(dff // FF_CHUNK):
        lo = c * FF_CHUNK
        a = jnp.dot(xb, wi_ref[:, lo:lo + FF_CHUNK], preferred_element_type=F32)
        u = jnp.dot(xb, wi_ref[:, dff + lo:dff + lo + FF_CHUNK], preferred_element_type=F32)
        hm = (a * jax.nn.sigmoid(a)) * u
        acc = acc + jnp.dot(hm.astype(BF16), wo_ref[lo:lo + FF_CHUNK, :],
                            preferred_element_type=F32)
    y = alpha * x + 0.5 * acc
    o_ref[...] = _layer_norm_rows(y, g_ref[...], b_ref[...])


def _ffn_ln(x, wi, wo, g, b, alpha):
    n, d = x.shape
    dff = wo.shape[0]
    tm = TOKEN_TILE
    return pl.pallas_call(
        functools.partial(_ffn_ln_kernel, alpha=alpha, dff=dff),
        out_shape=jax.ShapeDtypeStruct((n, d), F32),
        grid=(n // tm,),
        in_specs=[pl.BlockSpec((tm, d), lambda i: (i, 0)),
                  _const_spec((d, 2 * dff)), _const_spec((dff, d)),
                  _const_spec((1, d)), _const_spec((1, d))],
        out_specs=pl.BlockSpec((tm, d), lambda i: (i, 0)),
        compiler_params=_cparams(("parallel",)),
        name="ffn_ln",
    )(x, wi, wo, g.reshape(1, d), b.reshape(1, d))


_C_QA, _C_KA, _C_VA, _C_QI, _C_KI, _C_WI, _C_QB, _C_KB, _C_VB, _C_END = (
    0, 512, 1024, 1536, 1792, 1920, 2048, 2560, 3072, 3584)


def _value_slabs(zt):
    n = zt.shape[1]
    ones_slab = jnp.where(lax.broadcasted_iota(I32, (V_ROWS - HEAD_DIM, n), 0) == 0, 1.0, 0.0)
    return jnp.concatenate([piece for h in range(N_HEADS)
                            for piece in (zt[h * HEAD_DIM:(h + 1) * HEAD_DIM, :], ones_slab)], axis=0)


def _inproj_kernel(h_ref, w_ref, qaT_ref, ka_ref, kab_ref, va_ref, vaT_ref, qiT_ref, ki_ref,
                   kk_ref, wiT_ref, qbT_ref, kbb_ref, vbT_ref, kbl_ref, vbl_ref, *, per):
    hb = h_ref[...].astype(BF16)

    def proj(c0, c1):
        return jnp.dot(hb, w_ref[:, c0:c1], preferred_element_type=F32)

    def put_tiles(ref, zt):
        aug = _value_slabs(zt)
        for c in range(ref.shape[0]):
            ref[c] = aug[:, c * ATT_TILE:(c + 1) * ATT_TILE].astype(ref.dtype)

    scale = HEAD_DIM ** -0.5 * LOG2E
    qaT_ref[...] = (proj(_C_QA, _C_KA) * scale).T.astype(BF16)
    ka = proj(_C_KA, _C_VA)
    ka_ref[...] = pltpu.einshape("m(hd)->mhd", ka, h=N_HEADS)
    kab_ref[...] = ka.astype(BF16)
    va = proj(_C_VA, _C_QI)
    va_ref[...] = pltpu.einshape("m(hd)->mhd", va, h=N_HEADS)
    put_tiles(vaT_ref, va.T)
    qiT_ref[...] = proj(_C_QI, _C_KI).T.astype(BF16)
    kk = proj(_C_KI, _C_WI)
    ki_ref[...] = kk[:, :IDX_DIM]
    kk_ref[...] = kk.astype(BF16)
    wiT_ref[...] = proj(_C_WI, _C_QB).T[:8, :]
    qbT_ref[...] = (proj(_C_QB, _C_KB) * scale).T.astype(BF16)
    kb = proj(_C_KB, _C_VB)
    kbb_ref[...] = kb.astype(BF16)
    vb = proj(_C_VB, _C_END)
    put_tiles(vbT_ref, vb.T)

    @pl.when(pl.program_id(0) % per == per - 1)
    def _():
        kbl_ref[...] = kb
        vbl_ref[...] = vb


def _in_proj(h, w, per):
    n, d = h.shape
    tm = TOKEN_TILE
    nt = n // tm
    sub = tm // ATT_TILE
    row = lambda width: pl.BlockSpec((tm, width), lambda i: (i, 0))
    col = lambda rows: pl.BlockSpec((rows, tm), lambda i: (0, i))
    til = pl.BlockSpec((sub, N_HEADS * V_ROWS, ATT_TILE), lambda i: (i, 0, 0))
    last = pl.BlockSpec((tm, WIDTH), lambda i: (i // per, 0))
    hds = pl.BlockSpec((tm, N_HEADS, HEAD_DIM), lambda i: (i, 0, 0))
    sds = jax.ShapeDtypeStruct
    out_shape = (
        sds((WIDTH, n), BF16),
        sds((n, N_HEADS, HEAD_DIM), F32), sds((n, WIDTH), BF16),
        sds((n, N_HEADS, HEAD_DIM), F32),
        sds((n // ATT_TILE, N_HEADS * V_ROWS, ATT_TILE), BF16),
        sds((N_HEADS_IDX * IDX_DIM, n), BF16),
        sds((n, IDX_DIM), F32), sds((n, LANES), BF16),
        sds((8, n), F32),
        sds((WIDTH, n), BF16),
        sds((n, WIDTH), BF16),
        sds((n // ATT_TILE, N_HEADS * V_ROWS, ATT_TILE), BF16),
        sds((nt // per * tm, WIDTH), F32), sds((nt // per * tm, WIDTH), F32),
    )
    out_specs = (col(WIDTH), hds, row(WIDTH), hds, til,
                 col(N_HEADS_IDX * IDX_DIM), row(IDX_DIM), row(LANES), col(SUBLANES),
                 col(WIDTH), row(WIDTH), til, last, last)
    return pl.pallas_call(
        functools.partial(_inproj_kernel, per=per),
        out_shape=out_shape,
        grid=(nt,),
        in_specs=[pl.BlockSpec((tm, d), lambda i: (i, 0)), _const_spec((d, _C_END))],
        out_specs=out_specs,
        compiler_params=_cparams(("arbitrary",)),
        name="in_proj",
    )(h, w)


def _bit_planes(words):
    a = list(reversed(words))
    j, m = 16, 0x0000FFFF
    while j:
        k = 0
        while k < 32:
            t = (a[k] ^ lax.shift_right_logical(a[k + j], j)) & m
            a[k] = a[k] ^ t
            a[k + j] = a[k + j] ^ lax.shift_left(t, j)
            k = (k + j + 1) & ~j
        j >>= 1
        m = (m ^ (m << j)) & 0xFFFFFFFF
    return a


def _idx_mask_kernel(qiT_ref, wiT_ref, kk_ref, mask_ref, st_ref, p_ref, c_ref, d_ref, incl_ref,
                     *, tq, qpos0, n_valid, n_sel):
    tk = ATT_TILE
    assert tk == 32 * 8
    nkt_total = st_ref.shape[0] // tk

    @pl.when((pl.program_id(0) == 0) & (pl.program_id(1) == 0))
    def _():
        p_ref[...] = jnp.zeros(p_ref.shape, I32)

    q0 = qpos0 + pl.program_id(1) * tq
    n_keys = jnp.minimum(q0 + tq, n_valid)
    nt = jnp.minimum(lax.shift_right_logical(n_keys + tk - 1, int(math.log2(tk))), nkt_total)

    lane = lax.broadcasted_iota(I32, (8, tq), 1)
    q_chunk = lax.shift_right_logical(q0 + lane, CHUNK_SHIFT)
    n_adm = jnp.minimum((q_chunk + 1) * CHUNK, n_valid)
    k_row = jnp.minimum(n_adm, n_sel)

    zeros = jnp.zeros((LANES - IDX_DIM, tq), BF16)
    qh = [jnp.concatenate([qiT_ref[h * IDX_DIM:(h + 1) * IDX_DIM, :], zeros], axis=0)
          for h in range(N_HEADS_IDX)]
    scale_exact = math.frexp(IDX_SCALE)[0] == 0.5
    wh = [wiT_ref[h:h + 1, :] * IDX_SCALE if scale_exact else wiT_ref[h:h + 1, :]
          for h in range(N_HEADS_IDX)]

    def dots(j, h):
        r0 = pl.multiple_of(j * tk, tk)
        d_ref[h] = jnp.dot(kk_ref[pl.ds(r0, tk), :], qh[h], preferred_element_type=F32)

    def score_tile(j, last):
        r0 = pl.multiple_of(j * tk, tk)
        sc = None
        for h in range(N_HEADS_IDX):
            d = d_ref[h]
            if not last:
                dots(j + 1, h)
            term = wh[h] * jnp.maximum(d, 0.0)
            sc = term if sc is None else sc + term
        if not scale_exact:
            sc = sc * IDX_SCALE
        bits = lax.bitcast_convert_type(sc, I32)
        sign = lax.shift_right_arithmetic(bits, 31)
        key = (bits ^ (sign & 0x7FFFFFFF)) - sign
        if last:
            kpos = r0 + lax.broadcasted_iota(I32, (tk, tq), 0)
            key = jnp.where(kpos < n_adm[0:1, :], key, INT_MIN)
        st_ref[pl.ds(r0, tk), :] = key
        planes = _bit_planes([key[v * 8:(v + 1) * 8, :] for v in range(32)])
        p_ref[0, j] = planes[0] ^ -1
        for i in range(1, 32):
            p_ref[i, j] = planes[i]

    for h in range(N_HEADS_IDX):
        dots(0, h)

    def score_run(first, count):
        for c in range(count):
            score_tile(first + c, False)

    def score_quad(u, carry):
        score_run(4 * u, 4)
        return carry

    n_open = nt - 1
    n_quads = lax.shift_right_logical(n_open, 2)
    lax.fori_loop(0, n_quads, score_quad, 0)
    done = 4 * n_quads
    for count in (2, 1):
        @pl.when((n_open & count) == count)
        def _(count=count, done=done):
            score_run(done, count)
        done = done + (n_open & count)

    score_tile(nt - 1, True)

    group = incl_ref.shape[0]
    n_groups = nt if group == 1 else lax.shift_right_logical(nt + 3, 2)
    for g in range(group - 1):
        @pl.when(nt + g < n_groups * group)
        def _(g=g):
            r0 = pl.multiple_of((nt + g) * tk, tk)
            st_ref[pl.ds(r0, tk), :] = jnp.full((tk, tq), INT_MIN, I32)

    limit = n_adm - (nt - 1) * tk
    n_low = jnp.clip(lax.shift_right_arithmetic(limit - lax.broadcasted_iota(I32, (8, tq), 0) + 7, 3),
                     0, 32)
    last_word = jnp.where(n_low >= 32, -1, lax.shift_left(1, jnp.minimum(n_low, 31)) - 1)
    for jj in range(nkt_total):
        c_ref[jj] = jnp.where(jj < nt - 1, -1, jnp.where(jj == nt - 1, last_word, 0))

    def radix_select(n_scan):
        def radix_pass(i, carry):
            want, t = carry
            n_set = _tree_sum([lax.population_count(c_ref[jj] & p_ref[i, jj]) for jj in range(n_scan)])
            n_set = jnp.broadcast_to(jnp.sum(n_set, axis=0, keepdims=True), (8, tq))
            keep_set = n_set >= want
            flip = jnp.where(keep_set, 0, -1)
            for jj in range(n_scan):
                c_ref[jj] = c_ref[jj] & (p_ref[i, jj] ^ flip)
            return (jnp.where(keep_set, want, want - n_set),
                    t | jnp.where(keep_set, lax.shift_left(jnp.int32(1), 31 - i), 0))

        return lax.fori_loop(0, 32, radix_pass, (k_row, jnp.zeros((8, tq), I32)))

    quarter = nkt_total // 4
    if nkt_total >= 16 and quarter * 4 == nkt_total and quarter & (quarter - 1) == 0:
        reach = lax.shift_right_logical(nt - 1, int(math.log2(quarter)))
        want, t_bits = lax.switch(reach, [functools.partial(radix_select, quarter * (c + 1))
                                          for c in range(4)])
    else:
        want, t_bits = radix_select(nkt_total)
    t_row = (t_bits ^ INT_MIN)[0:1, :]
    ties_wanted = want[0:1, :].astype(F32)

    tri = jnp.where(lax.broadcasted_iota(I32, (tk, tk), 0) >= lax.broadcasted_iota(I32, (tk, tk), 1),
                    1.0, 0.0).astype(BF16)

    def tie_prefix(j, g):
        r0 = pl.multiple_of(j * tk, tk)
        eq = jnp.where(st_ref[pl.ds(r0, tk), :] == t_row, 1.0, 0.0)
        incl_ref[g] = jnp.dot(tri, eq.astype(BF16), preferred_element_type=F32)

    def final_group(u, seen, last):
        for g in range(group):
            j = u * group + g
            r0 = pl.multiple_of(j * tk, tk)
            incl = incl_ref[g]
            if not last:
                tie_prefix(j + group, g)
            rank = seen + incl
            bar = t_row + jnp.where(rank <= ties_wanted, 0, 1)
            sel = jnp.where(st_ref[pl.ds(r0, tk), :] >= bar, 1, 0)
            mask_ref[pl.ds(r0, tk), :] = sel.astype(jnp.int8)
            seen = seen + incl[tk - 1:tk, :]
        return seen

    for g in range(group):
        tie_prefix(g, g)
    seen = lax.fori_loop(0, n_groups - 1, lambda u, seen: final_group(u, seen, False),
                         jnp.zeros((1, tq), F32))
    final_group(n_groups - 1, seen, True)

    def zero_tile(j, carry):
        r0 = pl.multiple_of(j * tk, tk)
        mask_ref[pl.ds(r0, tk), :] = jnp.zeros((tk, tq), jnp.int8)
        return carry

    lax.fori_loop(n_groups * group, nkt_total, zero_tile, 0)


def _idx_mask(qiT, wiT, kk, *, nb, tq, qpos0, n_valid, n_sel):
    tkp = kk.shape[1]
    nq = qiT.shape[1] // nb // tq
    return pl.pallas_call(
        functools.partial(_idx_mask_kernel, tq=tq, qpos0=qpos0, n_valid=n_valid, n_sel=n_sel),
        out_shape=jax.ShapeDtypeStruct((nb, tkp, nq * tq), jnp.int8),
        grid=(nb, nq),
        in_specs=[pl.BlockSpec((N_HEADS_IDX * IDX_DIM, tq), lambda b, i: (0, b * nq + i)),
                  pl.BlockSpec((8, tq), lambda b, i: (0, b * nq + i)),
                  pl.BlockSpec((None, tkp, LANES), lambda b, i: (b, 0, 0))],
        out_specs=pl.BlockSpec((None, tkp, tq), lambda b, i: (b, 0, i)),
        scratch_shapes=[pltpu.VMEM((tkp, tq), I32),
                        pltpu.VMEM((32, tkp // ATT_TILE, 8, tq), I32),
                        pltpu.VMEM((tkp // ATT_TILE, 8, tq), I32),
                        pltpu.VMEM((N_HEADS_IDX, ATT_TILE, tq), F32),
                        pltpu.VMEM((4 if (tkp // ATT_TILE) % 4 == 0 else 1, ATT_TILE, tq), F32)],
        compiler_params=_cparams(("arbitrary", "arbitrary")),
        name="idx_mask",
    )(qiT, wiT, kk)


def _load_qsel(qT_ref, qsel_ref, tq):
    zeros = jnp.zeros((HEAD_DIM, tq), BF16)
    for h in range(N_HEADS):
        blk = qT_ref[h * HEAD_DIM:(h + 1) * HEAD_DIM, :]
        qsel_ref[h] = jnp.concatenate([blk, zeros] if h % 2 == 0 else [zeros, blk], axis=0)


def _init_state(m_ref, acc_ref):
    m_ref[...] = jnp.full(m_ref.shape, MASKED, F32)
    acc_ref[...] = jnp.zeros(acc_ref.shape, F32)


def _scores(k_slab, qsel_ref, s_ref, h):
    s_ref[h] = jnp.dot(k_slab(h // 2), qsel_ref[h], preferred_element_type=F32)


def _heads_tile(v_rows, next_k_slab, qsel_ref, s_ref, m_ref, acc_ref, addend):
    tq = s_ref.shape[2]
    slabs = s_ref.shape[1] // 16
    for h in range(N_HEADS):
        s = s_ref[h].astype(BF16)
        for a in addend(h):
            s = s + a
        if next_k_slab is not None:
            _scores(next_k_slab, qsel_ref, s_ref, h)
        m_old = m_ref[h:h + 1, :]
        top = _tree_reduce(jnp.maximum, [s[r * 16:(r + 1) * 16, :] for r in range(slabs)])
        m_new = jnp.maximum(m_old, jnp.max(top.astype(F32), axis=0, keepdims=True))
        alpha = jnp.exp2(m_old - m_new)
        m16 = jnp.broadcast_to(m_new, (16, tq)).astype(BF16)
        p = jnp.concatenate([jnp.exp2(s[r * 16:(r + 1) * 16, :] - m16) for r in range(slabs)], axis=0)
        m_ref[h:h + 1, :] = m_new
        pv = jnp.dot(v_rows(h), p, preferred_element_type=F32)
        rows = slice(h * V_ROWS, (h + 1) * V_ROWS)
        acc_ref[rows, :] = alpha * acc_ref[rows, :] + pv


def _attn_scratch(tq):
    return [pltpu.VMEM((N_HEADS, LANES, tq), BF16), pltpu.VMEM((N_HEADS, ATT_TILE, tq), F32),
            pltpu.VMEM((N_HEADS, tq), F32), pltpu.VMEM((N_HEADS * V_ROWS, tq), F32)]


def _store_out(o_ref, acc_ref):
    outs = [acc_ref[h * V_ROWS:h * V_ROWS + HEAD_DIM, :]
            / acc_ref[h * V_ROWS + HEAD_DIM:h * V_ROWS + HEAD_DIM + 1, :] for h in range(N_HEADS)]
    o_ref[...] = jnp.concatenate(outs, axis=0).T.astype(o_ref.dtype)


def _dsa_kernel(qT_ref, k_ref, vT_ref, mask_ref, bias_ref, o_ref, qsel_ref, s_ref, m_ref, acc_ref,
                *, tq, qpos0):
    tk = ATT_TILE
    jd = lax.shift_right_logical(qpos0 + pl.program_id(1) * tq, int(math.log2(tk)))
    _load_qsel(qT_ref, qsel_ref, tq)
    _init_state(m_ref, acc_ref)

    def k_slab(j):
        r0 = pl.multiple_of(j * tk, tk)
        return lambda g: k_ref[pl.ds(r0, tk), g * LANES:(g + 1) * LANES]

    def tile(j, near, last=False):
        r0 = pl.multiple_of(j * tk, tk)
        unselected = jnp.where(mask_ref[pl.ds(r0, tk), :].astype(I32) != 0, 0.0,
                               MASKED).astype(BF16)
        addend = ((lambda h: (unselected,)) if near is None
                  else (lambda h: (unselected, bias_ref[near, h])))
        _heads_tile(lambda h: vT_ref[j, h * V_ROWS:(h + 1) * V_ROWS, :],
                    None if last else k_slab(j + 1),
                    qsel_ref, s_ref, m_ref, acc_ref, addend)

    for h in range(N_HEADS):
        _scores(k_slab(0), qsel_ref, s_ref, h)

    n_far = jnp.maximum(jd - 1, 0)

    def far_run(first, count):
        for c in range(count):
            tile(first + c, None)

    def far_octet(u, carry):
        far_run(8 * u, 8)
        return carry

    n_octets = lax.shift_right_logical(n_far, 3)
    lax.fori_loop(0, n_octets, far_octet, 0)
    done = 8 * n_octets
    for count in (4, 2, 1):
        @pl.when((n_far & count) == count)
        def _(count=count, done=done):
            far_run(done, count)
        done = done + (n_far & count)

    @pl.when(jd >= 1)
    def _():
        tile(jd - 1, 0)

    tile(jd, 1, last=True)
    _store_out(o_ref, acc_ref)


def _dsa_attn(qT, k, vT, maskT, bias, *, nb, tq, qpos0):
    tkp = k.shape[1]
    nkt = tkp // ATT_TILE
    nq = qT.shape[1] // nb // tq
    return pl.pallas_call(
        functools.partial(_dsa_kernel, tq=tq, qpos0=qpos0),
        out_shape=jax.ShapeDtypeStruct((nb * nq * tq, WIDTH), BF16),
        grid=(nb, nq),
        in_specs=[pl.BlockSpec((WIDTH, tq), lambda b, i: (0, b * nq + i)),
                  pl.BlockSpec((None, tkp, WIDTH), lambda b, i: (b, 0, 0),
                               pipeline_mode=pl.Buffered(1)),
                  pl.BlockSpec((nkt, N_HEADS * V_ROWS, ATT_TILE), lambda b, i: (b, 0, 0),
                               pipeline_mode=pl.Buffered(1)),
                  pl.BlockSpec((None, tkp, tq), lambda b, i: (b, 0, i)),
                  _const_spec(bias.shape)],
        out_specs=pl.BlockSpec((tq, WIDTH), lambda b, i: (b * nq + i, 0)),
        scratch_shapes=_attn_scratch(tq),
        compiler_params=_cparams(("parallel", "parallel")),
        name="dsa_attn",
    )(qT, k, vT, maskT, bias)


_BAND_TILES = BAND_CHUNKS * CHUNK // ATT_TILE + 1


def _band_kernel(qT_ref, *refs, tq, off):
    k_refs = refs[:_BAND_TILES]
    v_refs = refs[_BAND_TILES:2 * _BAND_TILES]
    bias_ref, o_ref, qsel_ref, s_ref, m_ref, acc_ref = refs[2 * _BAND_TILES:]
    _load_qsel(qT_ref, qsel_ref, tq)
    _init_state(m_ref, acc_ref)
    k_slab = lambda w: (lambda g: k_refs[w][:, g * LANES:(g + 1) * LANES])
    for h in range(N_HEADS):
        _scores(k_slab(0), qsel_ref, s_ref, h)
    for w in range(_BAND_TILES):
        entry = jnp.where(pl.program_id(1) + off + w >= 0, w, _BAND_TILES)
        _heads_tile(lambda h: v_refs[w][0, h * V_ROWS:(h + 1) * V_ROWS, :],
                    None if w == _BAND_TILES - 1 else k_slab(w + 1),
                    qsel_ref, s_ref, m_ref, acc_ref, lambda h: (bias_ref[entry, h],))
    _store_out(o_ref, acc_ref)


def _band_attn(qT, k, vT, bias, *, nb, tq, off):
    nkt = k.shape[1] // ATT_TILE
    nq = qT.shape[1] // nb // tq
    tile_of = lambda i, w: jnp.maximum(i + off + w, 0)
    k_specs = [pl.BlockSpec((None, ATT_TILE, WIDTH), lambda b, i, w=w: (b, tile_of(i, w), 0))
               for w in range(_BAND_TILES)]
    v_specs = [pl.BlockSpec((1, N_HEADS * V_ROWS, ATT_TILE),
                            lambda b, i, w=w: (b * nkt + tile_of(i, w), 0, 0))
               for w in range(_BAND_TILES)]
    return pl.pallas_call(
        functools.partial(_band_kernel, tq=tq, off=off),
        out_shape=jax.ShapeDtypeStruct((nb * nq * tq, WIDTH), BF16),
        grid=(nb, nq),
        in_specs=[pl.BlockSpec((WIDTH, tq), lambda b, i: (0, b * nq + i))] + k_specs + v_specs
                 + [_const_spec(bias.shape)],
        out_specs=pl.BlockSpec((tq, WIDTH), lambda b, i: (b * nq + i, 0)),
        scratch_shapes=_attn_scratch(tq),
        compiler_params=_cparams(("parallel", "parallel")),
        name="band_attn",
    )(qT, *([k] * _BAND_TILES), *([vT] * _BAND_TILES), bias)


def _mix_ln_kernel(h_ref, oa_ref, ob_ref, wg_ref, wba_ref, wbb_ref, wo_ref, g_ref, b_ref, o_ref,
                   *, alpha):
    h = h_ref[...]
    hb = h.astype(BF16)
    d = h.shape[1]
    ga = jnp.dot(hb, wg_ref[:, :d], preferred_element_type=F32)
    gb = jnp.dot(hb, wg_ref[:, d:], preferred_element_type=F32)
    ya = jnp.dot(oa_ref[...], wba_ref[...], preferred_element_type=F32)
    yb = jnp.dot(ob_ref[...], wbb_ref[...], preferred_element_type=F32)
    gated = jax.nn.sigmoid(ga) * ya + jax.nn.sigmoid(gb) * yb
    mix = jnp.dot(gated.astype(BF16), wo_ref[...], preferred_element_type=F32)
    o_ref[...] = _layer_norm_rows(alpha * h + mix, g_ref[...], b_ref[...])


def _mix_ln(h, oa, ob, wg, wba, wbb, wo, g, b, alpha):
    n, d = h.shape
    tm = TOKEN_TILE
    row = lambda width: pl.BlockSpec((tm, width), lambda i: (i, 0))
    return pl.pallas_call(
        functools.partial(_mix_ln_kernel, alpha=alpha),
        out_shape=jax.ShapeDtypeStruct((n, d), F32),
        grid=(n // tm,),
        in_specs=[row(d), row(WIDTH), row(WIDTH), _const_spec(wg.shape), _const_spec(wba.shape),
                  _const_spec(wbb.shape), _const_spec(wo.shape), _const_spec((1, d)),
                  _const_spec((1, d))],
        out_specs=row(d),
        compiler_params=_cparams(("parallel",)),
        name="mix_ln",
    )(h, oa, ob, wg, wba, wbb, wo, g.reshape(1, d), b.reshape(1, d))


def _t5_bucket(rel):
    half = T5_BUCKETS // 2
    exact = half // 2
    n = jnp.abs(rel)
    log_ratio = jnp.log(jnp.maximum(n, 1).astype(F32) / exact) / math.log(T5_MAX_DIST / exact)
    large = jnp.minimum(exact + (log_ratio * (half - exact)).astype(I32), half - 1)
    return (rel > 0).astype(I32) * half + jnp.where(n < exact, n, large)


def _rel_line(n_tiles, first, tq):
    period = ATT_TILE + tq
    y = jnp.arange(period, dtype=I32)[None, :]
    c = (jnp.arange(n_tiles, dtype=I32)[:, None] + first) * ATT_TILE
    return jnp.where(y < tq, c - y, c + period - y)


def _toeplitz(v, tq):
    period = v.shape[-1]
    flat = jnp.tile(v, (1,) * (v.ndim - 1) + (ATT_TILE,))[..., :ATT_TILE * (period - 1)]
    return flat.reshape(v.shape[:-1] + (ATT_TILE, period - 1))[..., :tq]


def _dsa_bias_table(t5_bias, tq):
    assert T5_MAX_DIST <= ATT_TILE + 1
    far = t5_bias[:, _t5_bucket(jnp.int32(-(ATT_TILE + 1)))]
    line = t5_bias[:, _t5_bucket(_rel_line(2, -1, tq))] - far[:, None, None]
    return (_toeplitz(jnp.moveaxis(line, 0, 1), tq) * LOG2E).astype(BF16)


def _band_bias_table(rel_bias, tq):
    first = 1 - _BAND_TILES
    ridx = jnp.clip(_rel_line(_BAND_TILES, first, tq), -REL_BACK, CHUNK - 1) + REL_BACK
    bias = _toeplitz(jnp.moveaxis(rel_bias[:, ridx], 0, 1), tq)
    w = jnp.arange(_BAND_TILES, dtype=I32)[:, None, None]
    j = jnp.arange(ATT_TILE, dtype=I32)[None, :, None]
    i = jnp.arange(tq, dtype=I32)[None, None, :]
    dchunk = ((w + first) * ATT_TILE + j) // CHUNK - i // CHUNK
    ok = (dchunk <= 0) & (dchunk >= -BAND_CHUNKS)
    table = jnp.where(ok[:, None], bias * LOG2E, MASKED)
    return jnp.concatenate([table, jnp.full_like(table[:1], MASKED)], axis=0).astype(BF16)


def _regroup_w_in(w_in):
    d, n_in = w_in.shape
    c_ki = 3 * WIDTH + N_HEADS_IDX * IDX_DIM
    c_wi = c_ki + IDX_DIM
    c_qb = c_wi + N_HEADS_IDX
    c_gate = c_qb + 3 * WIDTH
    rows = LANES

    def regroup(w_ref, proj_ref, gate_ref):
        w = w_ref[...]
        ki = w[:, c_ki:c_wi]
        wi_pad = jnp.concatenate([w[:, c_wi:c_qb], jnp.zeros((rows, LANES - N_HEADS_IDX), F32)], axis=1)
        proj_ref[...] = jnp.concatenate([w[:, :c_ki], ki, ki, wi_pad, w[:, c_qb:c_gate]],
                                        axis=1).astype(BF16)
        gate_ref[...] = w[:, c_gate:].astype(BF16)

    return pl.pallas_call(
        regroup,
        out_shape=(jax.ShapeDtypeStruct((d, _C_END), BF16),
                   jax.ShapeDtypeStruct((d, n_in - c_gate), BF16)),
        grid=(d // rows,),
        in_specs=[pl.BlockSpec((rows, n_in), lambda i: (i, 0))],
        out_specs=(pl.BlockSpec((rows, _C_END), lambda i: (i, 0)),
                   pl.BlockSpec((rows, n_in - c_gate), lambda i: (i, 0))),
        compiler_params=_cparams(("parallel",)),
        name="regroup_w_in",
    )(w_in)


def _kv_prep_kernel(ck_ref, nk_ref, cv_ref, nv_ref, k_ref, vT_ref, *, dec):
    n_cache_tiles = ck_ref.shape[0] // ATT_TILE

    def fresh(n_ref):
        rows = lax.broadcasted_iota(I32, n_ref.shape, 0)
        new = jnp.where(rows < dec, n_ref[...], 0.0)
        return jnp.concatenate([new, jnp.zeros((ATT_TILE - new.shape[0], WIDTH), F32)], axis=0)

    for t in range(n_cache_tiles + 1):
        rows = slice(t * ATT_TILE, (t + 1) * ATT_TILE)
        k_tile = ck_ref[rows, :] if t < n_cache_tiles else fresh(nk_ref)
        v_tile = cv_ref[rows, :] if t < n_cache_tiles else fresh(nv_ref)
        k_ref[rows, :] = k_tile.astype(BF16)
        vT_ref[t] = _value_slabs(v_tile.T).astype(BF16)


def _kv_prep(cache_k, new_k, cache_v, new_v, dec):
    nb, lc, _ = cache_k.shape
    tq = new_k.shape[1]
    nct = lc // ATT_TILE
    cache_spec = pl.BlockSpec((None, lc, WIDTH), lambda b: (b, 0, 0))
    new_spec = pl.BlockSpec((None, tq, WIDTH), lambda b: (b, 0, 0))
    return pl.pallas_call(
        functools.partial(_kv_prep_kernel, dec=dec),
        out_shape=(jax.ShapeDtypeStruct((nb, lc + ATT_TILE, WIDTH), BF16),
                   jax.ShapeDtypeStruct((nb * (nct + 1), N_HEADS * V_ROWS, ATT_TILE), BF16)),
        grid=(nb,),
        in_specs=[cache_spec, new_spec, cache_spec, new_spec],
        out_specs=(pl.BlockSpec((None, lc + ATT_TILE, WIDTH), lambda b: (b, 0, 0)),
                   pl.BlockSpec((nct + 1, N_HEADS * V_ROWS, ATT_TILE), lambda b: (b, 0, 0))),
        compiler_params=_cparams(("parallel",)),
        name="kv_prep",
    )(cache_k, new_k, cache_v, new_v)


def _pad_keys(x, tkp):
    return jnp.pad(x, ((0, 0), (0, tkp - x.shape[1]), (0, 0)))


def kernel(x_prompt, x_sample, cache_k_a, cache_v_a, cache_kidx_a, cache_k_b, cache_v_b, t5_bias,
           ln1_g, ln1_b, ffn1_wi, ffn1_wo, ln2_g, ln2_b, w_in, rel_bias_b, w_branch_a, w_branch_b,
           w_out, ln3_g, ln3_b, ffn2_wi, ffn2_wo):
    depth = ln1_g.shape[0]
    alpha = (2.0 * depth) ** 0.25
    nbp, seq, d = x_prompt.shape
    nbs, dec, _ = x_sample.shape
    past = cache_k_a.shape[2]
    band = cache_k_b.shape[2]
    keep = min(BAND_CHUNKS * CHUNK, seq)
    tq_p, tq_s = ATT_TILE, LANES
    assert seq % TOKEN_TILE == 0 and keep == TOKEN_TILE and dec <= tq_s and past % ATT_TILE == 0
    assert band == BAND_CHUNKS * CHUNK and (nbs * tq_s) % TOKEN_TILE == 0

    yp = x_prompt.reshape(nbp * seq, d)
    ys = jnp.pad(x_sample, ((0, 0), (0, tq_s - dec), (0, 0))).reshape(nbs * tq_s, d)
    dsa_bias_p = _dsa_bias_table(t5_bias, tq_p)
    dsa_bias_s = dsa_bias_p[..., :tq_s]
    n_sel_p = min(TOPK_MAX, seq // 4)
    n_sel_s = min(TOPK_MAX, (past + dec) // 4)
    tk_s = past + ATT_TILE
    assert band + ATT_TILE == _BAND_TILES * ATT_TILE

    st_p, st_s = [], []
    for l in range(depth):
        w_proj, w_gate = _regroup_w_in(w_in[l])
        wi1, wo1 = ffn1_wi[l].astype(BF16), ffn1_wo[l].astype(BF16)
        wi2, wo2 = ffn2_wi[l].astype(BF16), ffn2_wo[l].astype(BF16)
        wba, wbb, wo = (w_branch_a[l].astype(BF16), w_branch_b[l].astype(BF16),
                        w_out[l].astype(BF16))
        band_bias_p = _band_bias_table(rel_bias_b[l], tq_p)
        band_bias_s = band_bias_p[..., :tq_s]

        h = _ffn_ln(yp, wi1, wo1, ln1_g[l], ln1_b[l], alpha)
        (qaT, ka, kab, va, vaT, qiT, ki, kk, wiT, qbT, kbb, vbT, kbl, vbl) = _in_proj(
            h, w_proj, seq // TOKEN_TILE)
        maskT = _idx_mask(qiT, wiT, kk.reshape(nbp, seq, LANES), nb=nbp, tq=tq_p, qpos0=0,
                          n_valid=seq, n_sel=n_sel_p)
        oa = _dsa_attn(qaT, kab.reshape(nbp, seq, WIDTH), vaT, maskT, dsa_bias_p,
                       nb=nbp, tq=tq_p, qpos0=0)
        ob = _band_attn(qbT, kbb.reshape(nbp, seq, WIDTH), vbT, band_bias_p,
                        nb=nbp, tq=tq_p, off=1 - _BAND_TILES)
        h2 = _mix_ln(h, oa, ob, w_gate, wba, wbb, wo, ln2_g[l], ln2_b[l], alpha)
        yp = _ffn_ln(h2, wi2, wo2, ln3_g[l], ln3_b[l], alpha)
        st_p.append((ka.reshape(nbp, seq, N_HEADS, HEAD_DIM), va.reshape(nbp, seq, N_HEADS, HEAD_DIM),
                     ki.reshape(nbp, seq, IDX_DIM), kbl.reshape(nbp, keep, N_HEADS, HEAD_DIM),
                     vbl.reshape(nbp, keep, N_HEADS, HEAD_DIM)))

        h = _ffn_ln(ys, wi1, wo1, ln1_g[l], ln1_b[l], alpha)
        (qaT, ka, _, va, _, qiT, ki, _, wiT, qbT, _, _, kbl, vbl) = _in_proj(h, w_proj, 1)
        per_stream = lambda a: a.reshape(nbs, tq_s, -1)
        new = lambda a: per_stream(a)[:, :dec]
        ka_n, va_n, ki_n, kb_n, vb_n = new(ka), new(va), new(ki), new(kbl), new(vbl)
        cache_kb = cache_k_b[l].reshape(nbs, band, WIDTH)
        cache_vb = cache_v_b[l].reshape(nbs, band, WIDTH)
        ki_all = jnp.concatenate([cache_kidx_a[l], ki_n], axis=1)
        kb_all = jnp.concatenate([cache_kb, kb_n], axis=1)
        vb_all = jnp.concatenate([cache_vb, vb_n], axis=1)
        kk_s = _pad_keys(jnp.concatenate([ki_all, ki_all], axis=2).astype(BF16), tk_s)
        maskT = _idx_mask(qiT, wiT, kk_s, nb=nbs, tq=tq_s, qpos0=past, n_valid=past + dec,
                          n_sel=n_sel_s)
        k_s, vT_s = _kv_prep(cache_k_a[l].reshape(nbs, past, WIDTH), per_stream(ka),
                             cache_v_a[l].reshape(nbs, past, WIDTH), per_stream(va), dec)
        oa = _dsa_attn(qaT, k_s, vT_s, maskT, dsa_bias_s, nb=nbs, tq=tq_s, qpos0=past)
        kb_s, vbT_s = _kv_prep(cache_kb, per_stream(kbl), cache_vb, per_stream(vbl), dec)
        ob = _band_attn(qbT, kb_s, vbT_s, band_bias_s, nb=nbs, tq=tq_s, off=0)
        h2 = _mix_ln(h, oa, ob, w_gate, wba, wbb, wo, ln2_g[l], ln2_b[l], alpha)
        ys = _ffn_ln(h2, wi2, wo2, ln3_g[l], ln3_b[l], alpha)
        heads = lambda a: a.reshape(nbs, -1, N_HEADS, HEAD_DIM)
        st_s.append((heads(ka_n), heads(va_n), ki_n, heads(kb_all[:, -band:]),
                     heads(vb_all[:, -band:])))

    y_prompt = yp.reshape(nbp, seq, d)
    y_sample = ys.reshape(nbs, tq_s, d)[:, :dec]
    stack = lambda sts, t: jnp.stack([s[t] for s in sts])
    return (y_prompt, y_sample,
            stack(st_p, 0), stack(st_p, 1), stack(st_p, 2), stack(st_p, 3), stack(st_p, 4),
            stack(st_s, 0), stack(st_s, 1), stack(st_s, 2), stack(st_s, 3), stack(st_s, 4))
```

```python
import functools
import math

import jax
import jax.numpy as jnp
from jax import lax
from jax.experimental import pallas as pl
from jax.experimental.pallas import tpu as pltpu

F32 = jnp.float32
BF16 = jnp.bfloat16
I32 = jnp.int32

CHUNK = 64
CHUNK_SHIFT = 6
HEAD_DIM = 64
N_HEADS = 8
WIDTH = N_HEADS * HEAD_DIM
N_HEADS_IDX = 4
IDX_DIM = 64
TOPK_MAX = 256
IDX_SCALE = (IDX_DIM * N_HEADS_IDX) ** -0.5
BAND_CHUNKS = 8
REL_BACK = 128
T5_BUCKETS = 32
T5_MAX_DIST = 128
LN_EPS = 1e-5
MASKED = -2.0 ** 100
LOG2E = math.log2(math.e)
V_ROWS = HEAD_DIM + 16
INT_MIN = -2 ** 31
LANES = 128
SUBLANES = 8

ATT_TILE = 256
TOKEN_TILE = 512
FF_CHUNK = 256
VMEM_LIMIT_MB = 56


def _cparams(sem):
    return pltpu.CompilerParams(dimension_semantics=sem,
                                vmem_limit_bytes=VMEM_LIMIT_MB * 1024 * 1024)


def _const_spec(shape):
    nd = len(shape)
    return pl.BlockSpec(shape, lambda *_: (0,) * nd, pipeline_mode=pl.Buffered(1))


def _tree_sum(xs):
    return _tree_reduce(lambda a, b: a + b, xs)


def _tree_reduce(fn, xs):
    xs = list(xs)
    while len(xs) > 1:
        nxt = [fn(xs[a], xs[a + 1]) for a in range(0, len(xs) - 1, 2)]
        if len(xs) % 2:
            nxt.append(xs[-1])
        xs = nxt
    return xs[0]


def _layer_norm_rows(y, g, b):
    mu = jnp.mean(y, axis=-1, keepdims=True)
    d = y - mu
    var = jnp.mean(d * d, axis=-1, keepdims=True)
    return d * lax.rsqrt(var + LN_EPS) * g + b


def _ffn_ln_kernel(x_ref, wi_ref, wo_ref, g_ref, b_ref, o_ref, *, alpha, dff):
    x = x_ref[...]
    xb = x.astype(BF16)
    acc = jnp.zeros(x.shape, F32)
    for c in range(dff // FF_CHUNK):
        lo = c * FF_CHUNK
        a = jnp.dot(xb, wi_ref[:, lo:lo + FF_CHUNK], preferred_element_type=F32)
        u = jnp.dot(xb, wi_ref[:, dff + lo:dff + lo + FF_CHUNK], preferred_element_type=F32)
        hm = (a * jax.nn.sigmoid(a)) * u
        acc = acc + jnp.dot(hm.astype(BF16), wo_ref[lo:lo + FF_CHUNK, :],
                            preferred_element_type=F32)
    y = alpha * x + 0.5 * acc
    o_ref[...] = _layer_norm_rows(y, g_ref[...], b_ref[...])


def _ffn_ln(x, wi, wo, g, b, alpha):
    n, d = x.shape
    dff = wo.shape[0]
    tm = TOKEN_TILE
    return pl.pallas_call(
        functools.partial(_ffn_ln_kernel, alpha=alpha, dff=dff),
        out_shape=jax.ShapeDtypeStruct((n, d), F32),
        grid=(n // tm,),
        in_specs=[pl.BlockSpec((tm, d), lambda i: (i, 0)),
                  _const_spec((d, 2 * dff)), _const_spec((dff, d)),
                  _const_spec((1, d)), _const_spec((1, d))],
        out_specs=pl.BlockSpec((tm, d), lambda i: (i, 0)),
        compiler_params=_cparams(("parallel",)),
        name="ffn_ln",
    )(x, wi, wo, g.reshape(1, d), b.reshape(1, d))


_C_QA, _C_KA, _C_VA, _C_QI, _C_KI, _C_WI, _C_QB, _C_KB, _C_VB, _C_END = (
    0, 512, 1024, 1536, 1792, 1920, 2048, 2560, 3072, 3584)


def _value_slabs(zt):
    n = zt.shape[1]
    ones_slab = jnp.where(lax.broadcasted_iota(I32, (V_ROWS - HEAD_DIM, n), 0) == 0, 1.0, 0.0)
    return jnp.concatenate([piece for h in range(N_HEADS)
                            for piece in (zt[h * HEAD_DIM:(h + 1) * HEAD_DIM, :], ones_slab)], axis=0)


def _inproj_kernel(h_ref, w_ref, qaT_ref, ka_ref, kab_ref, va_ref, vaT_ref, qiT_ref, ki_ref,
                   kk_ref, wiT_ref, qbT_ref, kbb_ref, vbT_ref, kbl_ref, vbl_ref, *, per):
    hb = h_ref[...].astype(BF16)

    def proj(c0, c1):
        return jnp.dot(hb, w_ref[:, c0:c1], preferred_element_type=F32)

    def put_tiles(ref, zt):
        aug = _value_slabs(zt)
        for c in range(ref.shape[0]):
            ref[c] = aug[:, c * ATT_TILE:(c + 1) * ATT_TILE].astype(ref.dtype)

    scale = HEAD_DIM ** -0.5 * LOG2E
    qaT_ref[...] = (proj(_C_QA, _C_KA) * scale).T.astype(BF16)
    ka = proj(_C_KA, _C_VA)
    ka_ref[...] = pltpu.einshape("m(hd)->mhd", ka, h=N_HEADS)
    kab_ref[...] = ka.astype(BF16)
    va = proj(_C_VA, _C_QI)
    va_ref[...] = pltpu.einshape("m(hd)->mhd", va, h=N_HEADS)
    put_tiles(vaT_ref, va.T)
    qiT_ref[...] = proj(_C_QI, _C_KI).T.astype(BF16)
    kk = proj(_C_KI, _C_WI)
    ki_ref[...] = kk[:, :IDX_DIM]
    kk_ref[...] = kk.astype(BF16)
    wiT_ref[...] = proj(_C_WI, _C_QB).T[:8, :]
    qbT_ref[...] = (proj(_C_QB, _C_KB) * scale).T.astype(BF16)
    kb = proj(_C_KB, _C_VB)
    kbb_ref[...] = kb.astype(BF16)
    vb = proj(_C_VB, _C_END)
    put_tiles(vbT_ref, vb.T)

    @pl.when(pl.program_id(0) % per == per - 1)
    def _():
        kbl_ref[...] = kb
        vbl_ref[...] = vb


def _in_proj(h, w, per):
    n, d = h.shape
    tm = TOKEN_TILE
    nt = n // tm
    sub = tm // ATT_TILE
    row = lambda width: pl.BlockSpec((tm, width), lambda i: (i, 0))
    col = lambda rows: pl.BlockSpec((rows, tm), lambda i: (0, i))
    til = pl.BlockSpec((sub, N_HEADS * V_ROWS, ATT_TILE), lambda i: (i, 0, 0))
    last = pl.BlockSpec((tm, WIDTH), lambda i: (i // per, 0))
    hds = pl.BlockSpec((tm, N_HEADS, HEAD_DIM), lambda i: (i, 0, 0))
    sds = jax.ShapeDtypeStruct
    out_shape = (
        sds((WIDTH, n), BF16),
        sds((n, N_HEADS, HEAD_DIM), F32), sds((n, WIDTH), BF16),
        sds((n, N_HEADS, HEAD_DIM), F32),
        sds((n // ATT_TILE, N_HEADS * V_ROWS, ATT_TILE), BF16),
        sds((N_HEADS_IDX * IDX_DIM, n), BF16),
        sds((n, IDX_DIM), F32), sds((n, LANES), BF16),
        sds((8, n), F32),
        sds((WIDTH, n), BF16),
        sds((n, WIDTH), BF16),
        sds((n // ATT_TILE, N_HEADS * V_ROWS, ATT_TILE), BF16),
        sds((nt // per * tm, WIDTH), F32), sds((nt // per * tm, WIDTH), F32),
    )
    out_specs = (col(WIDTH), hds, row(WIDTH), hds, til,
                 col(N_HEADS_IDX * IDX_DIM), row(IDX_DIM), row(LANES), col(SUBLANES),
                 col(WIDTH), row(WIDTH), til, last, last)
    return pl.pallas_call(
        functools.partial(_inproj_kernel, per=per),
        out_shape=out_shape,
        grid=(nt,),
        in_specs=[pl.BlockSpec((tm, d), lambda i: (i, 0)), _const_spec((d, _C_END))],
        out_specs=out_specs,
        compiler_params=_cparams(("arbitrary",)),
        name="in_proj",
    )(h, w)


def _bit_planes(words):
    a = list(reversed(words))
    j, m = 16, 0x0000FFFF
    while j:
        k = 0
        while k < 32:
            t = (a[k] ^ lax.shift_right_logical(a[k + j], j)) & m
            a[k] = a[k] ^ t
            a[k + j] = a[k + j] ^ lax.shift_left(t, j)
            k = (k + j + 1) & ~j
        j >>= 1
        m = (m ^ (m << j)) & 0xFFFFFFFF
    return a


def _idx_mask_kernel(qiT_ref, wiT_ref, kk_ref, mask_ref, st_ref, p_ref, c_ref, d_ref, incl_ref,
                     *, tq, qpos0, n_valid, n_sel):
    tk = ATT_TILE
    assert tk == 32 * 8
    nkt_total = st_ref.shape[0] // tk

    @pl.when((pl.program_id(0) == 0) & (pl.program_id(1) == 0))
    def _():
        p_ref[...] = jnp.zeros(p_ref.shape, I32)

    q0 = qpos0 + pl.program_id(1) * tq
    n_keys = jnp.minimum(q0 + tq, n_valid)
    nt = jnp.minimum(lax.shift_right_logical(n_keys + tk - 1, int(math.log2(tk))), nkt_total)

    lane = lax.broadcasted_iota(I32, (8, tq), 1)
    q_chunk = lax.shift_right_logical(q0 + lane, CHUNK_SHIFT)
    n_adm = jnp.minimum((q_chunk + 1) * CHUNK, n_valid)
    k_row = jnp.minimum(n_adm, n_sel)

    zeros = jnp.zeros((LANES - IDX_DIM, tq), BF16)
    qh = [jnp.concatenate([qiT_ref[h * IDX_DIM:(h + 1) * IDX_DIM, :], zeros], axis=0)
          for h in range(N_HEADS_IDX)]
    scale_exact = math.frexp(IDX_SCALE)[0] == 0.5
    wh = [wiT_ref[h:h + 1, :] * IDX_SCALE if scale_exact else wiT_ref[h:h + 1, :]
          for h in range(N_HEADS_IDX)]

    def dots(j, h):
        r0 = pl.multiple_of(j * tk, tk)
        d_ref[h] = jnp.dot(kk_ref[pl.ds(r0, tk), :], qh[h], preferred_element_type=F32)

    def score_tile(j, last):
        r0 = pl.multiple_of(j * tk, tk)
        sc = None
        for h in range(N_HEADS_IDX):
            d = d_ref[h]
            if not last:
                dots(j + 1, h)
            term = wh[h] * jnp.maximum(d, 0.0)
            sc = term if sc is None else sc + term
        if not scale_exact:
            sc = sc * IDX_SCALE
        bits = lax.bitcast_convert_type(sc, I32)
        sign = lax.shift_right_arithmetic(bits, 31)
        key = (bits ^ (sign & 0x7FFFFFFF)) - sign
        if last:
            kpos = r0 + lax.broadcasted_iota(I32, (tk, tq), 0)
            key = jnp.where(kpos < n_adm[0:1, :], key, INT_MIN)
        st_ref[pl.ds(r0, tk), :] = key
        planes = _bit_planes([key[v * 8:(v + 1) * 8, :] for v in range(32)])
        p_ref[0, j] = planes[0] ^ -1
        for i in range(1, 32):
            p_ref[i, j] = planes[i]

    for h in range(N_HEADS_IDX):
        dots(0, h)

    def score_run(first, count):
        for c in range(count):
            score_tile(first + c, False)

    def score_quad(u, carry):
        score_run(4 * u, 4)
        return carry

    n_open = nt - 1
    n_quads = lax.shift_right_logical(n_open, 2)
    lax.fori_loop(0, n_quads, score_quad, 0)
    done = 4 * n_quads
    for count in (2, 1):
        @pl.when((n_open & count) == count)
        def _(count=count, done=done):
            score_run(done, count)
        done = done + (n_open & count)

    score_tile(nt - 1, True)

    group = incl_ref.shape[0]
    n_groups = nt if group == 1 else lax.shift_right_logical(nt + 3, 2)
    for g in range(group - 1):
        @pl.when(nt + g < n_groups * group)
        def _(g=g):
            r0 = pl.multiple_of((nt + g) * tk, tk)
            st_ref[pl.ds(r0, tk), :] = jnp.full((tk, tq), INT_MIN, I32)

    limit = n_adm - (nt - 1) * tk
    n_low = jnp.clip(lax.shift_right_arithmetic(limit - lax.broadcasted_iota(I32, (8, tq), 0) + 7, 3),
                     0, 32)
    last_word = jnp.where(n_low >= 32, -1, lax.shift_left(1, jnp.minimum(n_low, 31)) - 1)
    for jj in range(nkt_total):
        c_ref[jj] = jnp.where(jj < nt - 1, -1, jnp.where(jj == nt - 1, last_word, 0))

    def radix_select(n_scan):
        def radix_pass(i, carry):
            want, t = carry
            n_set = _tree_sum([lax.population_count(c_ref[jj] & p_ref[i, jj]) for jj in range(n_scan)])
            n_set = jnp.broadcast_to(jnp.sum(n_set, axis=0, keepdims=True), (8, tq))
            keep_set = n_set >= want
            flip = jnp.where(keep_set, 0, -1)
            for jj in range(n_scan):
                c_ref[jj] = c_ref[jj] & (p_ref[i, jj] ^ flip)
            return (jnp.where(keep_set, want, want - n_set),
                    t | jnp.where(keep_set, lax.shift_left(jnp.int32(1), 31 - i), 0))

        return lax.fori_loop(0, 32, radix_pass, (k_row, jnp.zeros((8, tq), I32)))

    quarter = nkt_total // 4
    if nkt_total >= 16 and quarter * 4 == nkt_total and quarter & (quarter - 1) == 0:
        reach = lax.shift_right_logical(nt - 1, int(math.log2(quarter)))
        want, t_bits = lax.switch(reach, [functools.partial(radix_select, quarter * (c + 1))
                                          for c in range(4)])
    else:
        want, t_bits = radix_select(nkt_total)
    t_row = (t_bits ^ INT_MIN)[0:1, :]
    ties_wanted = want[0:1, :].astype(F32)

    tri = jnp.where(lax.broadcasted_iota(I32, (tk, tk), 0) >= lax.broadcasted_iota(I32, (tk, tk), 1),
                    1.0, 0.0).astype(BF16)

    def tie_prefix(j, g):
        r0 = pl.multiple_of(j * tk, tk)
        eq = jnp.where(st_ref[pl.ds(r0, tk), :] == t_row, 1.0, 0.0)
        incl_ref[g] = jnp.dot(tri, eq.astype(BF16), preferred_element_type=F32)

    def final_group(u, seen, last):
        for g in range(group):
            j = u * group + g
            r0 = pl.multiple_of(j * tk, tk)
            incl = incl_ref[g]
            if not last:
                tie_prefix(j + group, g)
            rank = seen + incl
            bar = t_row + jnp.where(rank <= ties_wanted, 0, 1)
            sel = jnp.where(st_ref[pl.ds(r0, tk), :] >= bar, 1, 0)
            mask_ref[pl.ds(r0, tk), :] = sel.astype(jnp.int8)
            seen = seen + incl[tk - 1:tk, :]
        return seen

    for g in range(group):
        tie_prefix(g, g)
    seen = lax.fori_loop(0, n_groups - 1, lambda u, seen: final_group(u, seen, False),
                         jnp.zeros((1, tq), F32))
    final_group(n_groups - 1, seen, True)

    def zero_tile(j, carry):
        r0 = pl.multiple_of(j * tk, tk)
        mask_ref[pl.ds(r0, tk), :] = jnp.zeros((tk, tq), jnp.int8)
        return carry

    lax.fori_loop(n_groups * group, nkt_total, zero_tile, 0)


def _idx_mask(qiT, wiT, kk, *, nb, tq, qpos0, n_valid, n_sel):
    tkp = kk.shape[1]
    nq = qiT.shape[1] // nb // tq
    return pl.pallas_call(
        functools.partial(_idx_mask_kernel, tq=tq, qpos0=qpos0, n_valid=n_valid, n_sel=n_sel),
        out_shape=jax.ShapeDtypeStruct((nb, tkp, nq * tq), jnp.int8),
        grid=(nb, nq),
        in_specs=[pl.BlockSpec((N_HEADS_IDX * IDX_DIM, tq), lambda b, i: (0, b * nq + i)),
                  pl.BlockSpec((8, tq), lambda b, i: (0, b * nq + i)),
                  pl.BlockSpec((None, tkp, LANES), lambda b, i: (b, 0, 0))],
        out_specs=pl.BlockSpec((None, tkp, tq), lambda b, i: (b, 0, i)),
        scratch_shapes=[pltpu.VMEM((tkp, tq), I32),
                        pltpu.VMEM((32, tkp // ATT_TILE, 8, tq), I32),
                        pltpu.VMEM((tkp // ATT_TILE, 8, tq), I32),
                        pltpu.VMEM((N_HEADS_IDX, ATT_TILE, tq), F32),
                        pltpu.VMEM((4 if (tkp // ATT_TILE) % 4 == 0 else 1, ATT_TILE, tq), F32)],
        compiler_params=_cparams(("arbitrary", "arbitrary")),
        name="idx_mask",
    )(qiT, wiT, kk)


def _load_qsel(qT_ref, qsel_ref, tq):
    zeros = jnp.zeros((HEAD_DIM, tq), BF16)
    for h in range(N_HEADS):
        blk = qT_ref[h * HEAD_DIM:(h + 1) * HEAD_DIM, :]
        qsel_ref[h] = jnp.concatenate([blk, zeros] if h % 2 == 0 else [zeros, blk], axis=0)


def _init_state(m_ref, acc_ref):
    m_ref[...] = jnp.full(m_ref.shape, MASKED, F32)
    acc_ref[...] = jnp.zeros(acc_ref.shape, F32)


def _scores(k_slab, qsel_ref, s_ref, h):
    s_ref[h] = jnp.dot(k_slab(h // 2), qsel_ref[h], preferred_element_type=F32)


def _heads_tile(v_rows, next_k_slab, qsel_ref, s_ref, m_ref, acc_ref, addend):
    tq = s_ref.shape[2]
    slabs = s_ref.shape[1] // 16
    for h in range(N_HEADS):
        s = s_ref[h].astype(BF16)
        for a in addend(h):
            s = s + a
        if next_k_slab is not None:
            _scores(next_k_slab, qsel_ref, s_ref, h)
        m_old = m_ref[h:h + 1, :]
        top = _tree_reduce(jnp.maximum, [s[r * 16:(r + 1) * 16, :] for r in range(slabs)])
        m_new = jnp.maximum(m_old, jnp.max(top.astype(F32), axis=0, keepdims=True))
        alpha = jnp.exp2(m_old - m_new)
        m16 = jnp.broadcast_to(m_new, (16, tq)).astype(BF16)
        p = jnp.concatenate([jnp.exp2(s[r * 16:(r + 1) * 16, :] - m16) for r in range(slabs)], axis=0)
        m_ref[h:h + 1, :] = m_new
        pv = jnp.dot(v_rows(h), p, preferred_element_type=F32)
        rows = slice(h * V_ROWS, (h + 1) * V_ROWS)
        acc_ref[rows, :] = alpha * acc_ref[rows, :] + pv


def _attn_scratch(tq):
    return [pltpu.VMEM((N_HEADS, LANES, tq), BF16), pltpu.VMEM((N_HEADS, ATT_TILE, tq), F32),
            pltpu.VMEM((N_HEADS, tq), F32), pltpu.VMEM((N_HEADS * V_ROWS, tq), F32)]


def _store_out(o_ref, acc_ref):
    outs = [acc_ref[h * V_ROWS:h * V_ROWS + HEAD_DIM, :]
            / acc_ref[h * V_ROWS + HEAD_DIM:h * V_ROWS + HEAD_DIM + 1, :] for h in range(N_HEADS)]
    o_ref[...] = jnp.concatenate(outs, axis=0).T.astype(o_ref.dtype)


def _dsa_kernel(qT_ref, k_ref, vT_ref, mask_ref, bias_ref, o_ref, qsel_ref, s_ref, m_ref, acc_ref,
                *, tq, qpos0):
    tk = ATT_TILE
    jd = lax.shift_right_logical(qpos0 + pl.program_id(1) * tq, int(math.log2(tk)))
    _load_qsel(qT_ref, qsel_ref, tq)
    _init_state(m_ref, acc_ref)

    def k_slab(j):
        r0 = pl.multiple_of(j * tk, tk)
        return lambda g: k_ref[pl.ds(r0, tk), g * LANES:(g + 1) * LANES]

    def tile(j, near, last=False):
        r0 = pl.multiple_of(j * tk, tk)
        unselected = jnp.where(mask_ref[pl.ds(r0, tk), :].astype(I32) != 0, 0.0,
                               MASKED).astype(BF16)
        addend = ((lambda h: (unselected,)) if near is None
                  else (lambda h: (unselected, bias_ref[near, h])))
        _heads_tile(lambda h: vT_ref[j, h * V_ROWS:(h + 1) * V_ROWS, :],
                    None if last else k_slab(j + 1),
                    qsel_ref, s_ref, m_ref, acc_ref, addend)

    for h in range(N_HEADS):
        _scores(k_slab(0), qsel_ref, s_ref, h)

    n_far = jnp.maximum(jd - 1, 0)

    def far_run(first, count):
        for c in range(count):
            tile(first + c, None)

    def far_octet(u, carry):
        far_run(8 * u, 8)
        return carry

    n_octets = lax.shift_right_logical(n_far, 3)
    lax.fori_loop(0, n_octets, far_octet, 0)
    done = 8 * n_octets
    for count in (4, 2, 1):
        @pl.when((n_far & count) == count)
        def _(count=count, done=done):
            far_run(done, count)
        done = done + (n_far & count)

    @pl.when(jd >= 1)
    def _():
        tile(jd - 1, 0)

    tile(jd, 1, last=True)
    _store_out(o_ref, acc_ref)


def _dsa_attn(qT, k, vT, maskT, bias, *, nb, tq, qpos0):
    tkp = k.shape[1]
    nkt = tkp // ATT_TILE
    nq = qT.shape[1] // nb // tq
    return pl.pallas_call(
        functools.partial(_dsa_kernel, tq=tq, qpos0=qpos0),
        out_shape=jax.ShapeDtypeStruct((nb * nq * tq, WIDTH), BF16),
        grid=(nb, nq),
        in_specs=[pl.BlockSpec((WIDTH, tq), lambda b, i: (0, b * nq + i)),
                  pl.BlockSpec((None, tkp, WIDTH), lambda b, i: (b, 0, 0),
                               pipeline_mode=pl.Buffered(1)),
                  pl.BlockSpec((nkt, N_HEADS * V_ROWS, ATT_TILE), lambda b, i: (b, 0, 0),
                               pipeline_mode=pl.Buffered(1)),
                  pl.BlockSpec((None, tkp, tq), lambda b, i: (b, 0, i)),
                  _const_spec(bias.shape)],
        out_specs=pl.BlockSpec((tq, WIDTH), lambda b, i: (b * nq + i, 0)),
        scratch_shapes=_attn_scratch(tq),
        compiler_params=_cparams(("parallel", "parallel")),
        name="dsa_attn",
    )(qT, k, vT, maskT, bias)


_BAND_TILES = BAND_CHUNKS * CHUNK // ATT_TILE + 1


def _band_kernel(qT_ref, *refs, tq, off):
    k_refs = refs[:_BAND_TILES]
    v_refs = refs[_BAND_TILES:2 * _BAND_TILES]
    bias_ref, o_ref, qsel_ref, s_ref, m_ref, acc_ref = refs[2 * _BAND_TILES:]
    _load_qsel(qT_ref, qsel_ref, tq)
    _init_state(m_ref, acc_ref)
    k_slab = lambda w: (lambda g: k_refs[w][:, g * LANES:(g + 1) * LANES])
    for h in range(N_HEADS):
        _scores(k_slab(0), qsel_ref, s_ref, h)
    for w in range(_BAND_TILES):
        entry = jnp.where(pl.program_id(1) + off + w >= 0, w, _BAND_TILES)
        _heads_tile(lambda h: v_refs[w][0, h * V_ROWS:(h + 1) * V_ROWS, :],
                    None if w == _BAND_TILES - 1 else k_slab(w + 1),
                    qsel_ref, s_ref, m_ref, acc_ref, lambda h: (bias_ref[entry, h],))
    _store_out(o_ref, acc_ref)


def _band_attn(qT, k, vT, bias, *, nb, tq, off):
    nkt = k.shape[1] // ATT_TILE
    nq = qT.shape[1] // nb // tq
    tile_of = lambda i, w: jnp.maximum(i + off + w, 0)
    k_specs = [pl.BlockSpec((None, ATT_TILE, WIDTH), lambda b, i, w=w: (b, tile_of(i, w), 0))
               for w in range(_BAND_TILES)]
    v_specs = [pl.BlockSpec((1, N_HEADS * V_ROWS, ATT_TILE),
                            lambda b, i, w=w: (b * nkt + tile_of(i, w), 0, 0))
               for w in range(_BAND_TILES)]
    return pl.pallas_call(
        functools.partial(_band_kernel, tq=tq, off=off),
        out_shape=jax.ShapeDtypeStruct((nb * nq * tq, WIDTH), BF16),
        grid=(nb, nq),
        in_specs=[pl.BlockSpec((WIDTH, tq), lambda b, i: (0, b * nq + i))] + k_specs + v_specs
                 + [_const_spec(bias.shape)],
        out_specs=pl.BlockSpec((tq, WIDTH), lambda b, i: (b * nq + i, 0)),
        scratch_shapes=_attn_scratch(tq),
        compiler_params=_cparams(("parallel", "parallel")),
        name="band_attn",
    )(qT, *([k] * _BAND_TILES), *([vT] * _BAND_TILES), bias)


def _mix_ln_kernel(h_ref, oa_ref, ob_ref, wg_ref, wba_ref, wbb_ref, wo_ref, g_ref, b_ref, o_ref,
                   *, alpha):
    d = h_ref.shape[1]
    half = h_ref.shape[0] // 2
    for r in range(2):
        rows = slice(r * half, (r + 1) * half)
        h = h_ref[rows, :]
        hb = h.astype(BF16)
        ga = jnp.dot(hb, wg_ref[:, :d], preferred_element_type=F32)
        gb = jnp.dot(hb, wg_ref[:, d:], preferred_element_type=F32)
        ya = jnp.dot(oa_ref[rows, :], wba_ref[...], preferred_element_type=F32)
        yb = jnp.dot(ob_ref[rows, :], wbb_ref[...], preferred_element_type=F32)
        gated = jax.nn.sigmoid(ga) * ya + jax.nn.sigmoid(gb) * yb
        mix = jnp.dot(gated.astype(BF16), wo_ref[...], preferred_element_type=F32)
        o_ref[rows, :] = _layer_norm_rows(alpha * h + mix, g_ref[...], b_ref[...])


def _mix_ln(h, oa, ob, wg, wba, wbb, wo, g, b, alpha):
    n, d = h.shape
    tm = TOKEN_TILE
    row = lambda width: pl.BlockSpec((tm, width), lambda i: (i, 0))
    return pl.pallas_call(
        functools.partial(_mix_ln_kernel, alpha=alpha),
        out_shape=jax.ShapeDtypeStruct((n, d), F32),
        grid=(n // tm,),
        in_specs=[row(d), row(WIDTH), row(WIDTH), _const_spec(wg.shape), _const_spec(wba.shape),
                  _const_spec(wbb.shape), _const_spec(wo.shape), _const_spec((1, d)),
                  _const_spec((1, d))],
        out_specs=row(d),
        compiler_params=_cparams(("parallel",)),
        name="mix_ln",
    )(h, oa, ob, wg, wba, wbb, wo, g.reshape(1, d), b.reshape(1, d))


def _t5_bucket(rel):
    half = T5_BUCKETS // 2
    exact = half // 2
    n = jnp.abs(rel)
    log_ratio = jnp.log(jnp.maximum(n, 1).astype(F32) / exact) / math.log(T5_MAX_DIST / exact)
    large = jnp.minimum(exact + (log_ratio * (half - exact)).astype(I32), half - 1)
    return (rel > 0).astype(I32) * half + jnp.where(n < exact, n, large)


def _rel_line(n_tiles, first, tq):
    period = ATT_TILE + tq
    y = jnp.arange(period, dtype=I32)[None, :]
    c = (jnp.arange(n_tiles, dtype=I32)[:, None] + first) * ATT_TILE
    return jnp.where(y < tq, c - y, c + period - y)


def _toeplitz(v, tq):
    period = v.shape[-1]
    flat = jnp.tile(v, (1,) * (v.ndim - 1) + (ATT_TILE,))[..., :ATT_TILE * (period - 1)]
    return flat.reshape(v.shape[:-1] + (ATT_TILE, period - 1))[..., :tq]


def _dsa_bias_table(t5_bias, tq):
    assert T5_MAX_DIST <= ATT_TILE + 1
    far = t5_bias[:, _t5_bucket(jnp.int32(-(ATT_TILE + 1)))]
    line = t5_bias[:, _t5_bucket(_rel_line(2, -1, tq))] - far[:, None, None]
    return (_toeplitz(jnp.moveaxis(line, 0, 1), tq) * LOG2E).astype(BF16)


def _band_bias_table(rel_bias, tq):
    first = 1 - _BAND_TILES
    ridx = jnp.clip(_rel_line(_BAND_TILES, first, tq), -REL_BACK, CHUNK - 1) + REL_BACK
    bias = _toeplitz(jnp.moveaxis(rel_bias[:, ridx], 0, 1), tq)
    w = jnp.arange(_BAND_TILES, dtype=I32)[:, None, None]
    j = jnp.arange(ATT_TILE, dtype=I32)[None, :, None]
    i = jnp.arange(tq, dtype=I32)[None, None, :]
    dchunk = ((w + first) * ATT_TILE + j) // CHUNK - i // CHUNK
    ok = (dchunk <= 0) & (dchunk >= -BAND_CHUNKS)
    table = jnp.where(ok[:, None], bias * LOG2E, MASKED)
    return jnp.concatenate([table, jnp.full_like(table[:1], MASKED)], axis=0).astype(BF16)


def _regroup_w_in(w_in):
    d, n_in = w_in.shape
    c_ki = 3 * WIDTH + N_HEADS_IDX * IDX_DIM
    c_wi = c_ki + IDX_DIM
    c_qb = c_wi + N_HEADS_IDX
    c_gate = c_qb + 3 * WIDTH
    rows = LANES

    def regroup(w_ref, proj_ref, gate_ref):
        w = w_ref[...]
        ki = w[:, c_ki:c_wi]
        wi_pad = jnp.concatenate([w[:, c_wi:c_qb], jnp.zeros((rows, LANES - N_HEADS_IDX), F32)], axis=1)
        proj_ref[...] = jnp.concatenate([w[:, :c_ki], ki, ki, wi_pad, w[:, c_qb:c_gate]],
                                        axis=1).astype(BF16)
        gate_ref[...] = w[:, c_gate:].astype(BF16)

    return pl.pallas_call(
        regroup,
        out_shape=(jax.ShapeDtypeStruct((d, _C_END), BF16),
                   jax.ShapeDtypeStruct((d, n_in - c_gate), BF16)),
        grid=(d // rows,),
        in_specs=[pl.BlockSpec((rows, n_in), lambda i: (i, 0))],
        out_specs=(pl.BlockSpec((rows, _C_END), lambda i: (i, 0)),
                   pl.BlockSpec((rows, n_in - c_gate), lambda i: (i, 0))),
        compiler_params=_cparams(("parallel",)),
        name="regroup_w_in",
    )(w_in)


def _kv_prep_kernel(ck_ref, nk_ref, cv_ref, nv_ref, k_ref, vT_ref, *, dec):
    n_cache_tiles = ck_ref.shape[0] // ATT_TILE

    def fresh(n_ref):
        rows = lax.broadcasted_iota(I32, n_ref.shape, 0)
        new = jnp.where(rows < dec, n_ref[...], 0.0)
        return jnp.concatenate([new, jnp.zeros((ATT_TILE - new.shape[0], WIDTH), F32)], axis=0)

    for t in range(n_cache_tiles + 1):
        rows = slice(t * ATT_TILE, (t + 1) * ATT_TILE)
        k_tile = ck_ref[rows, :] if t < n_cache_tiles else fresh(nk_ref)
        v_tile = cv_ref[rows, :] if t < n_cache_tiles else fresh(nv_ref)
        k_ref[rows, :] = k_tile.astype(BF16)
        vT_ref[t] = _value_slabs(v_tile.T).astype(BF16)


def _kv_prep(cache_k, new_k, cache_v, new_v, dec):
    nb, lc, _ = cache_k.shape
    tq = new_k.shape[1]
    nct = lc // ATT_TILE
    cache_spec = pl.BlockSpec((None, lc, WIDTH), lambda b: (b, 0, 0))
    new_spec = pl.BlockSpec((None, tq, WIDTH), lambda b: (b, 0, 0))
    return pl.pallas_call(
        functools.partial(_kv_prep_kernel, dec=dec),
        out_shape=(jax.ShapeDtypeStruct((nb, lc + ATT_TILE, WIDTH), BF16),
                   jax.ShapeDtypeStruct((nb * (nct + 1), N_HEADS * V_ROWS, ATT_TILE), BF16)),
        grid=(nb,),
        in_specs=[cache_spec, new_spec, cache_spec, new_spec],
        out_specs=(pl.BlockSpec((None, lc + ATT_TILE, WIDTH), lambda b: (b, 0, 0)),
                   pl.BlockSpec((nct + 1, N_HEADS * V_ROWS, ATT_TILE), lambda b: (b, 0, 0))),
        compiler_params=_cparams(("parallel",)),
        name="kv_prep",
    )(cache_k, new_k, cache_v, new_v)


def _pad_keys(x, tkp):
    return jnp.pad(x, ((0, 0), (0, tkp - x.shape[1]), (0, 0)))


def kernel(x_prompt, x_sample, cache_k_a, cache_v_a, cache_kidx_a, cache_k_b, cache_v_b, t5_bias,
           ln1_g, ln1_b, ffn1_wi, ffn1_wo, ln2_g, ln2_b, w_in, rel_bias_b, w_branch_a, w_branch_b,
           w_out, ln3_g, ln3_b, ffn2_wi, ffn2_wo):
    depth = ln1_g.shape[0]
    alpha = (2.0 * depth) ** 0.25
    nbp, seq, d = x_prompt.shape
    nbs, dec, _ = x_sample.shape
    past = cache_k_a.shape[2]
    band = cache_k_b.shape[2]
    keep = min(BAND_CHUNKS * CHUNK, seq)
    tq_p, tq_s = ATT_TILE, LANES
    assert seq % TOKEN_TILE == 0 and keep == TOKEN_TILE and dec <= tq_s and past % ATT_TILE == 0
    assert band == BAND_CHUNKS * CHUNK and (nbs * tq_s) % TOKEN_TILE == 0

    yp = x_prompt.reshape(nbp * seq, d)
    ys = jnp.pad(x_sample, ((0, 0), (0, tq_s - dec), (0, 0))).reshape(nbs * tq_s, d)
    dsa_bias_p = _dsa_bias_table(t5_bias, tq_p)
    dsa_bias_s = dsa_bias_p[..., :tq_s]
    n_sel_p = min(TOPK_MAX, seq // 4)
    n_sel_s = min(TOPK_MAX, (past + dec) // 4)
    tk_s = past + ATT_TILE
    assert band + ATT_TILE == _BAND_TILES * ATT_TILE

    st_p, st_s = [], []
    for l in range(depth):
        w_proj, w_gate = _regroup_w_in(w_in[l])
        wi1, wo1 = ffn1_wi[l].astype(BF16), ffn1_wo[l].astype(BF16)
        wi2, wo2 = ffn2_wi[l].astype(BF16), ffn2_wo[l].astype(BF16)
        wba, wbb, wo = (w_branch_a[l].astype(BF16), w_branch_b[l].astype(BF16),
                        w_out[l].astype(BF16))
        band_bias_p = _band_bias_table(rel_bias_b[l], tq_p)
        band_bias_s = band_bias_p[..., :tq_s]

        h = _ffn_ln(yp, wi1, wo1, ln1_g[l], ln1_b[l], alpha)
        (qaT, ka, kab, va, vaT, qiT, ki, kk, wiT, qbT, kbb, vbT, kbl, vbl) = _in_proj(
            h, w_proj, seq // TOKEN_TILE)
        maskT = _idx_mask(qiT, wiT, kk.reshape(nbp, seq, LANES), nb=nbp, tq=tq_p, qpos0=0,
                          n_valid=seq, n_sel=n_sel_p)
        oa = _dsa_attn(qaT, kab.reshape(nbp, seq, WIDTH), vaT, maskT, dsa_bias_p,
                       nb=nbp, tq=tq_p, qpos0=0)
        ob = _band_attn(qbT, kbb.reshape(nbp, seq, WIDTH), vbT, band_bias_p,
                        nb=nbp, tq=tq_p, off=1 - _BAND_TILES)
        h2 = _mix_ln(h, oa, ob, w_gate, wba, wbb, wo, ln2_g[l], ln2_b[l], alpha)
        yp = _ffn_ln(h2, wi2, wo2, ln3_g[l], ln3_b[l], alpha)
        st_p.append((ka.reshape(nbp, seq, N_HEADS, HEAD_DIM), va.reshape(nbp, seq, N_HEADS, HEAD_DIM),
                     ki.reshape(nbp, seq, IDX_DIM), kbl.reshape(nbp, keep, N_HEADS, HEAD_DIM),
                     vbl.reshape(nbp, keep, N_HEADS, HEAD_DIM)))

        h = _ffn_ln(ys, wi1, wo1, ln1_g[l], ln1_b[l], alpha)
        (qaT, ka, _, va, _, qiT, ki, _, wiT, qbT, _, _, kbl, vbl) = _in_proj(h, w_proj, 1)
        per_stream = lambda a: a.reshape(nbs, tq_s, -1)
        new = lambda a: per_stream(a)[:, :dec]
        ka_n, va_n, ki_n, kb_n, vb_n = new(ka), new(va), new(ki), new(kbl), new(vbl)
        cache_kb = cache_k_b[l].reshape(nbs, band, WIDTH)
        cache_vb = cache_v_b[l].reshape(nbs, band, WIDTH)
        ki_all = jnp.concatenate([cache_kidx_a[l], ki_n], axis=1)
        kb_all = jnp.concatenate([cache_kb, kb_n], axis=1)
        vb_all = jnp.concatenate([cache_vb, vb_n], axis=1)
        kk_s = _pad_keys(jnp.concatenate([ki_all, ki_all], axis=2).astype(BF16), tk_s)
        maskT = _idx_mask(qiT, wiT, kk_s, nb=nbs, tq=tq_s, qpos0=past, n_valid=past + dec,
                          n_sel=n_sel_s)
        k_s, vT_s = _kv_prep(cache_k_a[l].reshape(nbs, past, WIDTH), per_stream(ka),
                             cache_v_a[l].reshape(nbs, past, WIDTH), per_stream(va), dec)
        oa = _dsa_attn(qaT, k_s, vT_s, maskT, dsa_bias_s, nb=nbs, tq=tq_s, qpos0=past)
        kb_s, vbT_s = _kv_prep(cache_kb, per_stream(kbl), cache_vb, per_stream(vbl), dec)
        ob = _band_attn(qbT, kb_s, vbT_s, band_bias_s, nb=nbs, tq=tq_s, off=0)
        h2 = _mix_ln(h, oa, ob, w_gate, wba, wbb, wo, ln2_g[l], ln2_b[l], alpha)
        ys = _ffn_ln(h2, wi2, wo2, ln3_g[l], ln3_b[l], alpha)
        heads = lambda a: a.reshape(nbs, -1, N_HEADS, HEAD_DIM)
        st_s.append((heads(ka_n), heads(va_n), ki_n, heads(kb_all[:, -band:]),
                     heads(vb_all[:, -band:])))

    y_prompt = yp.reshape(nbp, seq, d)
    y_sample = ys.reshape(nbs, tq_s, d)[:, :dec]
    stack = lambda sts, t: jnp.stack([s[t] for s in sts])
    return (y_prompt, y_sample,
            stack(st_p, 0), stack(st_p, 1), stack(st_p, 2), stack(st_p, 3), stack(st_p, 4),
            stack(st_s, 0), stack(st_s, 1), stack(st_s, 2), stack(st_s, 3), stack(st_s, 4))
```
